```python
import math
import jax, jax.numpy as jnp
from jax import lax
import numpy as np

D_MODEL = 2048
BATCH = 4
SEQ = 4096
DEPTH = 1

ATTN_HEADS = 16
ATTN_HEAD_DIM = 128
ATTN_W = ATTN_HEADS * ATTN_HEAD_DIM
SSM_HEADS = 32
SSM_HEAD_DIM = 64
SSM_W = SSM_HEADS * SSM_HEAD_DIM
MIX_W = ATTN_W + SSM_W
SSM_GROUPS = 8
SSM_HEADS_PER_GROUP = SSM_HEADS // SSM_GROUPS
SSM_STATE = 128
CONV_WIDTH = 5
CONV_DIM = SSM_W + 2 * SSM_GROUPS * SSM_STATE
CHUNK = 128
IN_W = 3 * ATTN_W + SSM_W + CONV_DIM + 2 * SSM_HEADS
DILATED_PATTERNS = ((128, 1), (512, 4), (2048, 16))
BLK = 64
NUM_BUCKETS = 32
MAX_DISTANCE = 1024
NEG_INF = -1e30
D_FF = 4 * D_MODEL
EPS = 1e-6

kernel_name = "hybrid_dilated_attn_ssd_block"


def rms_norm(x, g):
    xf = x.astype(jnp.float32)
    y = xf * lax.rsqrt(jnp.mean(xf * xf, axis=-1, keepdims=True) + EPS)
    return (y * g.astype(jnp.float32)).astype(x.dtype)


def t5_bucket(rel):
    nb = NUM_BUCKETS // 2
    max_exact = nb // 2
    ret = (rel > 0).astype(jnp.int32) * nb
    n = jnp.abs(rel)
    nf = jnp.maximum(n, 1).astype(jnp.float32)
    large = max_exact + (jnp.log(nf / max_exact) / math.log(MAX_DISTANCE / max_exact)
                         * (nb - max_exact)).astype(jnp.int32)
    large = jnp.minimum(large, nb - 1)
    return ret + jnp.where(n < max_exact, n, large)


def dilated_window_attention(q, k, v, rel_bias, window, dilation):
    b, S, H, e = q.shape
    half = window // (2 * dilation)
    L = S // dilation
    nb = -(-L // BLK)
    Lp = nb * BLK

    def to_sub(t):
        t = t.reshape(b, L, dilation, H, e).transpose(0, 2, 3, 1, 4)
        return jnp.pad(t, ((0, 0), (0, 0), (0, 0), (0, Lp - L), (0, 0)))

    def windows(t):
        tp = jnp.pad(to_sub(t), ((0, 0), (0, 0), (0, 0), (BLK, BLK), (0, 0)))
        tp = tp.reshape(b, dilation, H, nb + 2, BLK, e)
        return jnp.concatenate([tp[:, :, :, :-2], tp[:, :, :, 1:-1], tp[:, :, :, 2:]], axis=-2)

    qs = to_sub(q).reshape(b, dilation, H, nb, BLK, e)
    kw, vw = windows(k), windows(v)

    s_idx = jnp.arange(BLK)[:, None]
    t_idx = jnp.arange(3 * BLK)[None, :]
    rel = t_idx - BLK - s_idx
    bias = rel_bias[t5_bucket(rel * dilation)].transpose(2, 0, 1)
    key_pos = jnp.arange(nb)[:, None] * BLK + t_idx - BLK
    valid = (jnp.abs(rel) <= half)[None] & ((key_pos >= 0) & (key_pos < L))[:, None, :]

    scale = 1.0 / math.sqrt(e)
    logits = jnp.einsum('bdhnqe,bdhnke->bdhnqk', qs, kw).astype(jnp.float32) * scale
    logits = logits + bias[:, None].astype(jnp.float32)
    logits = jnp.where(valid, logits, NEG_INF)
    m = jnp.max(logits, axis=-1, keepdims=True)
    p = jnp.exp(logits - m)
    den = jnp.sum(p, axis=-1)
    o = jnp.einsum('bdhnqk,bdhnke->bdhnqe', p, vw.astype(jnp.float32)) / den[..., None]
    lse = m[..., 0] + jnp.log(den)

    o = o.reshape(b, dilation, H, Lp, e)[:, :, :, :L].transpose(0, 3, 1, 2, 4).reshape(b, S, H, e)
    lse = lse.reshape(b, dilation, H, Lp)[:, :, :, :L].transpose(0, 3, 1, 2).reshape(b, S, H)
    return o, lse


def segsum(a):
    T = a.shape[-1]
    cs = jnp.cumsum(a, axis=-1)
    diff = cs[..., :, None] - cs[..., None, :]
    return jnp.where(jnp.tril(jnp.ones((T, T), dtype=bool)), diff, -jnp.inf)


def ssd_chunked(x, dt, A, Bm, Cm):
    b, L, g, h, p = x.shape
    n = Bm.shape[-1]
    nc = L // CHUNK
    xc = (x.astype(jnp.float32) * dt[..., None]).reshape(b, nc, CHUNK, g, h, p)
    Bc = Bm.reshape(b, nc, CHUNK, g, n)
    Cc = Cm.reshape(b, nc, CHUNK, g, n)
    ac = (dt * A).reshape(b, nc, CHUNK, g, h).transpose(0, 3, 4, 1, 2)
    a_cum = jnp.cumsum(ac, axis=-1)

    Lmat = jnp.exp(segsum(ac))
    cb = jnp.einsum('bclgn,bcsgn->bgcls', Cc, Bc)
    y_diag = jnp.einsum('bgcls,bghcls,bcsghp->bclghp', cb, Lmat, xc)

    decay_states = jnp.exp(a_cum[..., -1:] - a_cum)
    states = jnp.einsum('bclgn,bghcl,bclghp->bcghpn', Bc, decay_states, xc)
    states = jnp.concatenate([jnp.zeros_like(states[:, :1]), states], axis=1)

    a_last = jnp.pad(a_cum[..., -1], ((0, 0), (0, 0), (0, 0), (1, 0)))
    chunk_decay = jnp.exp(segsum(a_last))
    states = jnp.einsum('bghzc,bcghpn->bzghpn', chunk_decay, states)[:, :-1]

    y_off = jnp.einsum('bclgn,bcghpn,bghcl->bclghp', Cc, states, jnp.exp(a_cum))
    return (y_diag + y_off).reshape(b, L, g, h, p).astype(x.dtype)


def ssd_mixer(z, xbc, dt_raw, conv_w, conv_b, dt_bias, a_log, d_skip, norm_g):
    b, S, _ = xbc.shape
    pad = CONV_WIDTH // 2
    xbc = lax.conv_general_dilated(xbc, conv_w[:, None, :], window_strides=(1,),
                                   padding=[(pad, pad)], dimension_numbers=('NWC', 'WIO', 'NWC'),
                                   feature_group_count=CONV_DIM) + conv_b
    xbc = jax.nn.silu(xbc)
    gn = SSM_GROUPS * SSM_STATE
    xs = xbc[..., :SSM_W].reshape(b, S, SSM_GROUPS, SSM_HEADS_PER_GROUP, SSM_HEAD_DIM)
    Bm = xbc[..., SSM_W:SSM_W + gn].reshape(b, S, SSM_GROUPS, SSM_STATE)
    Cm = xbc[..., SSM_W + gn:].reshape(b, S, SSM_GROUPS, SSM_STATE)

    dt = jax.nn.softplus(dt_raw.astype(jnp.float32).reshape(b, S, 2, SSM_GROUPS, SSM_HEADS_PER_GROUP)
                         + dt_bias.astype(jnp.float32).reshape(2, SSM_GROUPS, SSM_HEADS_PER_GROUP))
    A = -jnp.exp(a_log.astype(jnp.float32)).reshape(2, SSM_GROUPS, SSM_HEADS_PER_GROUP)

    flip = lambda t: t[:, ::-1]
    y_fwd = ssd_chunked(xs, dt[:, :, 0], A[0], Bm, Cm)
    y_bwd = flip(ssd_chunked(flip(xs), flip(dt[:, :, 1]), A[1], flip(Bm), flip(Cm)))
    y = y_fwd + y_bwd + xs * d_skip.reshape(SSM_GROUPS, SSM_HEADS_PER_GROUP)[..., None]

    y = y.reshape(b, S, SSM_W) * jax.nn.silu(z)
    y = rms_norm(y.reshape(b, S, SSM_GROUPS, SSM_W // SSM_GROUPS),
                 norm_g.reshape(SSM_GROUPS, SSM_W // SSM_GROUPS))
    return y.reshape(b, S, SSM_W)


def setup_inputs(seed: int = 0) -> dict:
    key = jax.random.key(seed)
    ks = jax.random.split(key, 20)
    f32 = jnp.float32
    nrm = lambda k, shape, s: jax.random.normal(k, shape, f32) * s
    dt0 = jnp.exp(jax.random.uniform(ks[8], (DEPTH, 2, SSM_HEADS), f32,
                                     math.log(1e-3), math.log(1e-1)))
    return {
        "x": nrm(ks[0], (BATCH, SEQ, D_MODEL), 1.0),
        "norm_mix_g": 1.0 + nrm(ks[1], (DEPTH, D_MODEL), 0.02),
        "w_in": nrm(ks[2], (DEPTH, D_MODEL, IN_W), D_MODEL ** -0.5),
        "q_norm_g": 1.0 + nrm(ks[3], (DEPTH, ATTN_HEAD_DIM), 0.02),
        "k_norm_g": 1.0 + nrm(ks[4], (DEPTH, ATTN_HEAD_DIM), 0.02),
        "rel_bias": nrm(ks[5], (NUM_BUCKETS, ATTN_HEADS), 0.5),
        "conv_w": nrm(ks[6], (DEPTH, CONV_WIDTH, CONV_DIM), CONV_WIDTH ** -0.5),
        "conv_b": nrm(ks[7], (DEPTH, CONV_DIM), 0.01),
        "dt_bias": dt0 + jnp.log(-jnp.expm1(-dt0)),
        "a_log": jnp.log(jax.random.uniform(ks[9], (DEPTH, 2, SSM_HEADS), f32, 1.0, 16.0)),
        "d_skip": 1.0 + nrm(ks[10], (DEPTH, SSM_HEADS), 0.1),
        "ssd_norm_g": 1.0 + nrm(ks[11], (DEPTH, SSM_W), 0.02),
        "w_out": nrm(ks[12], (DEPTH, MIX_W, D_MODEL), MIX_W ** -0.5),
        "norm_mlp_g": 1.0 + nrm(ks[13], (DEPTH, D_MODEL), 0.02),
        "w_up": nrm(ks[14], (DEPTH, D_MODEL, D_FF), D_MODEL ** -0.5),
        "w_down": nrm(ks[15], (DEPTH, D_FF, D_MODEL), D_FF ** -0.5),
    }


def reference(x, norm_mix_g, w_in, q_norm_g, k_norm_g, rel_bias, conv_w, conv_b, dt_bias,
              a_log, d_skip, ssd_norm_g, w_out, norm_mlp_g, w_up, w_down):
    b, S, _ = x.shape
    splits = np.cumsum([ATTN_W, ATTN_W, ATTN_W, SSM_W, CONV_DIM]).tolist()
    for layer in range(DEPTH):
        h = rms_norm(x, norm_mix_g[layer])
        u = h @ w_in[layer]
        q, k, v, z, xbc, dt_raw = jnp.split(u, splits, axis=-1)

        heads = lambda t: t.reshape(b, S, ATTN_HEADS, ATTN_HEAD_DIM)
        q = rms_norm(heads(q), q_norm_g[layer])
        k = rms_norm(heads(k), k_norm_g[layer])
        v = heads(v)
        outs, lses = [], []
        for window, dilation in DILATED_PATTERNS:
            o, lse = dilated_window_attention(q, k, v, rel_bias, window, dilation)
            outs.append(o)
            lses.append(lse)
        wts = jax.nn.softmax(jnp.stack(lses, axis=0), axis=0)
        attn = jnp.sum(wts[..., None] * jnp.stack(outs, axis=0), axis=0)
        attn = attn.astype(x.dtype).reshape(b, S, ATTN_W)

        ssd = ssd_mixer(z, xbc, dt_raw, conv_w[layer], conv_b[layer], dt_bias[layer],
                        a_log[layer], d_skip[layer], ssd_norm_g[layer])

        x = x + jnp.concatenate([attn, ssd], axis=-1) @ w_out[layer]

        hm = rms_norm(x, norm_mlp_g[layer]) @ w_up[layer]
        x = x + jnp.square(jax.nn.relu(hm)) @ w_down[layer]
    return x
```

```python
import functools
import math

import jax
import jax.numpy as jnp
from jax import lax
from jax.experimental import pallas as pl
from jax.experimental.pallas import tpu as pltpu

D_MODEL = 2048
ATTN_HEADS = 16
HEAD_DIM = 128
ATTN_W = ATTN_HEADS * HEAD_DIM
SSM_HEADS = 32
SSM_HEAD_DIM = 64
SSM_W = SSM_HEADS * SSM_HEAD_DIM
SSM_GROUPS = 8
HEADS_PER_GROUP = SSM_HEADS // SSM_GROUPS
GROUP_W = SSM_W // SSM_GROUPS
SSM_STATE = 128
CONV_WIDTH = 5
D_FF = 4 * D_MODEL
DILATIONS = (1, 4, 16)
HALF_WINDOW = 64
NUM_BUCKETS = 32
MAX_DISTANCE = 1024
NEG_INF = -1e30
EPS = 1e-6

LANES = 128
Q_SUB = 128
K_WIN = 256
SSD_CHUNK = 128
VMEM_LIMIT = 56 * 1024 * 1024

F32 = jnp.float32
BF16 = jnp.bfloat16


def _params(*sem):
    return pltpu.CompilerParams(dimension_semantics=sem, vmem_limit_bytes=VMEM_LIMIT)


def _sigmoid(x):
    return 1.0 / (1.0 + jnp.exp(-x))


def _softplus(x):
    return jnp.maximum(x, 0.0) + jnp.log1p(jnp.exp(-jnp.abs(x)))


def _rmsnorm_kernel(x_ref, g_ref, o_ref):
    x = x_ref[...]
    ms = jnp.mean(x * x, axis=-1, keepdims=True)
    o_ref[...] = (x * lax.rsqrt(ms + EPS) * g_ref[...]).astype(o_ref.dtype)


def _rmsnorm(x2d, g, tm=512):
    n, d = x2d.shape
    return pl.pallas_call(
        _rmsnorm_kernel,
        grid=(n // tm,),
        in_specs=[pl.BlockSpec((tm, d), lambda i: (i, 0)),
                  pl.BlockSpec((1, d), lambda i: (0, 0))],
        out_specs=pl.BlockSpec((tm, d), lambda i: (i, 0)),
        out_shape=jax.ShapeDtypeStruct((n, d), BF16),
        compiler_params=_params("parallel"),
        name="rmsnorm",
    )(x2d, g.reshape(1, d))


def _proj_qkv_kernel(a_ref, w_ref, g_ref, o_ref, *, heads_per_tile, normed_tiles):
    acc = jnp.dot(a_ref[...], w_ref[...], preferred_element_type=F32)
    j = pl.program_id(1)

    @pl.when(j < normed_tiles)
    def _():
        for h in range(heads_per_tile):
            a = acc[:, h * HEAD_DIM:(h + 1) * HEAD_DIM]
            ms = jnp.mean(a * a, axis=-1, keepdims=True)
            g = g_ref[:, h * HEAD_DIM:(h + 1) * HEAD_DIM]
            o_ref[h] = (a * lax.rsqrt(ms + EPS) * g).astype(o_ref.dtype)

    @pl.when(j >= normed_tiles)
    def _():
        for h in range(heads_per_tile):
            o_ref[h] = acc[:, h * HEAD_DIM:(h + 1) * HEAD_DIM].astype(o_ref.dtype)


def _proj_split_kernel(a_ref, w_ref, o_ref, *, width):
    acc = jnp.dot(a_ref[...], w_ref[...], preferred_element_type=F32)
    for c in range(o_ref.shape[0]):
        o_ref[c] = acc[:, c * width:(c + 1) * width].astype(o_ref.dtype)


def _proj_plain_kernel(a_ref, w_ref, o_ref):
    o_ref[...] = jnp.dot(a_ref[...], w_ref[...], preferred_element_type=F32).astype(o_ref.dtype)


def _proj_qkv(h, w, gains, tm=1024, tn=1024):
    n, k = h.shape
    m = w.shape[1]
    hpt = tn // HEAD_DIM
    return pl.pallas_call(
        functools.partial(_proj_qkv_kernel, heads_per_tile=hpt, normed_tiles=2 * ATTN_W // tn),
        grid=(n // tm, m // tn),
        in_specs=[pl.BlockSpec((tm, k), lambda i, j: (i, 0)),
                  pl.BlockSpec((k, tn), lambda i, j: (0, j)),
                  pl.BlockSpec((1, tn), lambda i, j: (0, j))],
        out_specs=pl.BlockSpec((hpt, tm, HEAD_DIM), lambda i, j: (j, i, 0)),
        out_shape=jax.ShapeDtypeStruct((m // HEAD_DIM, n, HEAD_DIM), BF16),
        compiler_params=_params("parallel", "arbitrary"),
        name="proj_qkv",
    )(h, w, gains)


def _proj_split(h, w, width, name, tm=1024, tn=1024):
    n, k = h.shape
    m = w.shape[1]
    cpt = tn // width
    return pl.pallas_call(
        functools.partial(_proj_split_kernel, width=width),
        grid=(n // tm, m // tn),
        in_specs=[pl.BlockSpec((tm, k), lambda i, j: (i, 0)),
                  pl.BlockSpec((k, tn), lambda i, j: (0, j))],
        out_specs=pl.BlockSpec((cpt, tm, width), lambda i, j: (j, i, 0)),
        out_shape=jax.ShapeDtypeStruct((m // width, n, width), BF16),
        compiler_params=_params("parallel", "arbitrary"),
        name=name,
    )(h, w)


def _proj_plain(h, w, name, tm=1024):
    n, k = h.shape
    m = w.shape[1]
    return pl.pallas_call(
        _proj_plain_kernel,
        grid=(n // tm,),
        in_specs=[pl.BlockSpec((tm, k), lambda i: (i, 0)),
                  pl.BlockSpec((k, m), lambda i: (0, 0))],
        out_specs=pl.BlockSpec((tm, m), lambda i: (i, 0)),
        out_shape=jax.ShapeDtypeStruct((n, m), F32),
        compiler_params=_params("parallel"),
        name=name,
    )(h, w)


def _t5_bucket(rel):
    nb = NUM_BUCKETS // 2
    max_exact = nb // 2
    ret = (rel > 0).astype(jnp.int32) * nb
    n = jnp.abs(rel)
    nf = jnp.maximum(n, 1).astype(jnp.float32)
    large = max_exact + (jnp.log(nf / max_exact) / math.log(MAX_DISTANCE / max_exact)
                         * (nb - max_exact)).astype(jnp.int32)
    large = jnp.minimum(large, nb - 1)
    return ret + jnp.where(n < max_exact, n, large)


def _bias_tiles(rel_bias):
    qi = jnp.arange(Q_SUB)[:, None]
    kj = jnp.arange(K_WIN)[None, :]
    tiles = []
    for d in DILATIONS:
        per_place = []
        for off in (0, -HALF_WINDOW, -2 * HALF_WINDOW):
            rel = kj + off - qi
            valid = jnp.abs(rel) <= HALF_WINDOW
            b = rel_bias[_t5_bucket(rel * d)]
            per_place.append(jnp.where(valid[..., None], b.astype(F32), NEG_INF))
        tiles.append(jnp.stack(per_place, axis=0))
    t = jnp.stack(tiles, axis=0)
    return t.transpose(4, 0, 1, 2, 3)


def _attn_kernel(q1, k1, v1, q4, k4, v4, q16, k16, v16, bias_ref, o_ref,
                 o1_s, l1_s, o4_s, l4_s, o16_s, l16_s, *, seq):
    ones = jnp.ones((K_WIN, LANES), BF16)

    def sub_tile(q_ref, k_ref, v_ref, lane0, length, s0, pi):
        w0 = pl.multiple_of(jnp.clip(s0 - HALF_WINDOW, 0, length - K_WIN), HALF_WINDOW)
        place = jnp.where(s0 == 0, 0, jnp.where(s0 == length - Q_SUB, 2, 1))
        q = q_ref[pl.ds(s0, Q_SUB), lane0:lane0 + LANES]
        k = k_ref[pl.ds(w0, K_WIN), lane0:lane0 + LANES]
        v = v_ref[pl.ds(w0, K_WIN), lane0:lane0 + LANES]
        s = lax.dot_general(q, k, (((1,), (1,)), ((), ())), preferred_element_type=F32)
        s = s + bias_ref[pi, place]
        m = jnp.max(s, axis=-1, keepdims=True)
        p = jnp.exp(s - m).astype(BF16)
        pv = jnp.dot(p, jnp.concatenate([v, ones], axis=1), preferred_element_type=F32)
        den = pv[:, LANES:]
        return pv[:, :LANES] / den, m + jnp.log(den)

    for pi, (d, o_s, l_s, refs) in enumerate(((1, o1_s, l1_s, (q1, k1, v1)),
                                              (4, o4_s, l4_s, (q4, k4, v4)),
                                              (16, o16_s, l16_s, (q16, k16, v16)))):
        length = seq // d
        for r in range(d):
            def body(jj, carry, r=r, d=d, o_s=o_s, l_s=l_s, refs=refs, length=length, pi=pi):
                s0 = pl.multiple_of(jj * Q_SUB, Q_SUB)
                o, lse = sub_tile(*refs, r * LANES, length, s0, pi)
                if d == 1:
                    o_s[pl.ds(s0, Q_SUB), :] = o
                    l_s[pl.ds(s0, Q_SUB), :] = lse
                else:
                    o_s[r, pl.ds(s0, Q_SUB), :] = o
                    l_s[r, pl.ds(s0, Q_SUB), :] = lse
                return carry
            lax.fori_loop(0, length // Q_SUB, body, 0)

    rows = 64
    n16 = seq // 16
    for r in range(16):
        a, r4 = r // 4, r % 4
        for pc in range(n16 // rows):
            l0 = pc * rows
            oa = o1_s[pl.ds(r + 16 * l0, rows, stride=16), :]
            la = l1_s[pl.ds(r + 16 * l0, rows, stride=16), :]
            ob = o4_s[r4, pl.ds(a + 4 * l0, rows, stride=4), :]
            lb = l4_s[r4, pl.ds(a + 4 * l0, rows, stride=4), :]
            oc = o16_s[r, pl.ds(l0, rows), :]
            lc = l16_s[r, pl.ds(l0, rows), :]
            mx = jnp.maximum(jnp.maximum(la, lb), lc)
            ea, eb, ec = jnp.exp(la - mx), jnp.exp(lb - mx), jnp.exp(lc - mx)
            out = (ea * oa + eb * ob + ec * oc) / (ea + eb + ec)
            o_ref[pl.ds(l0, rows), r * LANES:(r + 1) * LANES] = out.astype(o_ref.dtype)


def _attention(qkv, bias, batch, seq):
    H = ATTN_HEADS
    views = [qkv.reshape(3 * H, batch, seq // d, d * HEAD_DIM) for d in DILATIONS]
    in_specs, args = [], []
    for d, view in zip(DILATIONS, views):
        for part in range(3):
            in_specs.append(pl.BlockSpec((None, None, seq // d, d * HEAD_DIM),
                                         lambda h, b, part=part: (part * H + h, b, 0, 0)))
            args.append(view)
    in_specs.append(pl.BlockSpec((None, 3, 3, Q_SUB, K_WIN), lambda h, b: (h, 0, 0, 0, 0)))
    args.append(bias)
    out = pl.pallas_call(
        functools.partial(_attn_kernel, seq=seq),
        grid=(H, batch),
        in_specs=in_specs,
        out_specs=pl.BlockSpec((None, None, seq // 16, 16 * HEAD_DIM), lambda h, b: (h, b, 0, 0)),
        out_shape=jax.ShapeDtypeStruct((H, batch, seq // 16, 16 * HEAD_DIM), BF16),
        scratch_shapes=[pltpu.VMEM((seq, LANES), F32), pltpu.VMEM((seq, LANES), F32),
                        pltpu.VMEM((4, seq // 4, LANES), F32), pltpu.VMEM((4, seq // 4, LANES), F32),
                        pltpu.VMEM((16, seq // 16, LANES), F32), pltpu.VMEM((16, seq // 16, LANES), F32)],
        compiler_params=_params("parallel", "parallel"),
        name="dilated_attention",
    )(*args)
    return out.reshape(H, batch * seq, HEAD_DIM)


def _ssd_kernel(z_ref, x_ref, b_ref, c_ref, dtc_ref, dtr_ref, cwx_ref, cwb_ref, cwc_ref,
                cbx_ref, cbb_ref, cbc_ref, dbc_ref, dbr_ref, alc_ref, alr_ref, dsk_ref, ng_ref,
                o_ref, pad, xc, bc, cc, yf, yb, sf, sb, *, seq):
    T = SSD_CHUNK
    nc = seq // T
    rb = 128
    hi = lax.Precision.HIGHEST

    def conv(src_ref, cw_ref, cb_ref, dst_ref, width):
        pad[0:8, :] = jnp.zeros((8, GROUP_W), F32)
        pad[seq + 8:seq + 16, :] = jnp.zeros((8, GROUP_W), F32)

        def fill(i, carry):
            t0 = pl.multiple_of(i * rb, rb)
            pad[pl.ds(t0 + 8, rb), 0:width] = src_ref[pl.ds(t0, rb), :].astype(F32)
            return carry
        lax.fori_loop(0, seq // rb, fill, 0)

        def taps(i, carry):
            t0 = pl.multiple_of(i * rb, rb)
            acc = jnp.broadcast_to(cb_ref[...], (rb, width))
            blk = pad[pl.ds(t0, rb + 16), 0:width]
            for w in range(CONV_WIDTH):
                lo = 8 - CONV_WIDTH // 2 + w
                acc = acc + blk[lo:lo + rb] * cw_ref[w:w + 1, :]
            dst_ref[pl.ds(t0, rb), :] = (acc * _sigmoid(acc)).astype(dst_ref.dtype)
            return carry
        lax.fori_loop(0, seq // rb, taps, 0)

    conv(x_ref, cwx_ref, cbx_ref, xc, GROUP_W)
    conv(b_ref, cwb_ref, cbb_ref, bc, SSM_STATE)
    conv(c_ref, cwc_ref, cbc_ref, cc, SSM_STATE)

    row = lax.broadcasted_iota(jnp.int32, (T, T), 0)
    col = lax.broadcasted_iota(jnp.int32, (T, T), 1)
    tril = (col <= row).astype(F32)
    triu = (row <= col).astype(F32)
    lower = col < row
    upper = col > row
    nh = HEADS_PER_GROUP
    expand = (lax.broadcasted_iota(jnp.int32, (2 * nh, 2 * GROUP_W), 0)
              == lax.broadcasted_iota(jnp.int32, (2 * nh, 2 * GROUP_W), 1) // SSM_HEAD_DIM).astype(F32)
    exp_f = expand[:, :GROUP_W]
    exp_b = expand[:, GROUP_W:]
    lane_head = lax.broadcasted_iota(jnp.int32, (T, GROUP_W), 1) // SSM_HEAD_DIM
    a_col = -jnp.exp(alc_ref[...])
    a_row = -jnp.exp(alr_ref[...])

    sf[...] = jnp.zeros_like(sf)
    sb[...] = jnp.zeros_like(sb)

    def col_terms(t0):
        dt_c = _softplus(dtc_ref[pl.ds(t0, T), :] + dbc_ref[...])
        a_c = dt_c * a_col
        cum_c = jnp.dot(tril, a_c, precision=hi, preferred_element_type=F32)
        return dt_c, a_c, cum_c

    def step(i, carry):
        t0 = pl.multiple_of(i * T, T)
        dt_c, a_c, cum_c = col_terms(t0)
        dt_r = _softplus(dtr_ref[i] + dbr_ref[...])
        a_r = dt_r * a_row
        cum_r = jnp.dot(a_r, triu, precision=hi, preferred_element_type=F32)
        exc_c = cum_c - a_c
        exc_r = cum_r - a_r
        x = xc[pl.ds(t0, T), :]
        xb = x.astype(BF16)
        bk = bc[pl.ds(t0, T), :]
        ck = cc[pl.ds(t0, T), :]
        cb = lax.dot_general(ck, bk, (((1,), (1,)), ((), ())), preferred_element_type=F32)
        y = x * dsk_ref[...]
        for h in range(nh):
            hb = nh + h
            arg = jnp.where(lower, cum_c[:, h:h + 1] - cum_r[h:h + 1, :],
                            exc_r[hb:hb + 1, :] - exc_c[:, hb:hb + 1])
            wgt = jnp.where(lower, dt_r[h:h + 1, :],
                            jnp.where(upper, dt_r[hb:hb + 1, :], dt_r[h:h + 1, :] + dt_r[hb:hb + 1, :]))
            mat = (cb * jnp.exp(arg) * wgt).astype(BF16)
            yh = jnp.dot(mat, xb, preferred_element_type=F32)
            y = jnp.where(lane_head == h, y + yh, y)
        cumx = jnp.dot(cum_c, exp_f, precision=hi, preferred_element_type=F32)
        dtx = jnp.dot(dt_c, exp_f, precision=hi, preferred_element_type=F32)
        last = cumx[T - 1:T, :]
        s_prev = sf[...]
        y = y + jnp.exp(cumx) * jnp.dot(ck, s_prev.astype(BF16), preferred_element_type=F32)
        xs = (x * dtx * jnp.exp(last - cumx)).astype(BF16)
        sf[...] = jnp.exp(last) * s_prev + lax.dot_general(
            bk, xs, (((0,), (0,)), ((), ())), preferred_element_type=F32)
        yf[pl.ds(t0, T), :] = y

        t2 = pl.multiple_of((nc - 1 - i) * T, T)
        dt_c2, a_c2, cum_c2 = col_terms(t2)
        cumx2 = jnp.dot(cum_c2, exp_b, precision=hi, preferred_element_type=F32)
        excx2 = cumx2 - jnp.dot(a_c2, exp_b, precision=hi, preferred_element_type=F32)
        dtx2 = jnp.dot(dt_c2, exp_b, precision=hi, preferred_element_type=F32)
        last2 = cumx2[T - 1:T, :]
        x2 = xc[pl.ds(t2, T), :]
        bk2 = bc[pl.ds(t2, T), :]
        ck2 = cc[pl.ds(t2, T), :]
        s_prev2 = sb[...]
        yb[pl.ds(t2, T), :] = jnp.exp(last2 - excx2) * jnp.dot(
            ck2, s_prev2.astype(BF16), preferred_element_type=F32)
        xs2 = (x2 * dtx2 * jnp.exp(excx2)).astype(BF16)
        sb[...] = jnp.exp(last2) * s_prev2 + lax.dot_general(
            bk2, xs2, (((0,), (0,)), ((), ())), preferred_element_type=F32)
        return carry

    lax.fori_loop(0, nc, step, 0)

    def finish(i, carry):
        t0 = pl.multiple_of(i * T, T)
        zz = z_ref[pl.ds(t0, T), :].astype(F32)
        y = (yf[pl.ds(t0, T), :] + yb[pl.ds(t0, T), :]) * (zz * _sigmoid(zz))
        ms = jnp.mean(y * y, axis=-1, keepdims=True)
        o_ref[pl.ds(t0, T), :] = (y * lax.rsqrt(ms + EPS) * ng_ref[...]).astype(o_ref.dtype)
        return carry
    lax.fori_loop(0, nc, finish, 0)


def _ssd(zx, bcm, dt_col, dt_row, cwx, cwb, cwc, cbx, cbb, cbc, db_col, db_row, al_col, al_row,
         dskip, ng, batch, seq):
    G = SSM_GROUPS
    nc = seq // SSD_CHUNK
    zx4 = zx.reshape(2 * G, batch, seq, GROUP_W)
    bc4 = bcm.reshape(2 * G, batch, seq, SSM_STATE)
    nd = 2 * HEADS_PER_GROUP

    def per_group(shape):
        return pl.BlockSpec((None,) + shape, lambda b, g: (g,) + (0,) * len(shape))

    in_specs = [
        pl.BlockSpec((None, None, seq, GROUP_W), lambda b, g: (g, b, 0, 0)),
        pl.BlockSpec((None, None, seq, GROUP_W), lambda b, g: (G + g, b, 0, 0)),
        pl.BlockSpec((None, None, seq, SSM_STATE), lambda b, g: (g, b, 0, 0)),
        pl.BlockSpec((None, None, seq, SSM_STATE), lambda b, g: (G + g, b, 0, 0)),
        pl.BlockSpec((None, None, seq, nd), lambda b, g: (b, g, 0, 0)),
        pl.BlockSpec((None, None, nc, nd, SSD_CHUNK), lambda b, g: (b, g, 0, 0, 0)),
        per_group((CONV_WIDTH, GROUP_W)), per_group((CONV_WIDTH, SSM_STATE)),
        per_group((CONV_WIDTH, SSM_STATE)),
        per_group((1, GROUP_W)), per_group((1, SSM_STATE)), per_group((1, SSM_STATE)),
        per_group((1, nd)), per_group((nd, 1)), per_group((1, nd)), per_group((nd, 1)),
        per_group((1, GROUP_W)), per_group((1, GROUP_W)),
    ]
    out = pl.pallas_call(
        functools.partial(_ssd_kernel, seq=seq),
        grid=(batch, G),
        in_specs=in_specs,
        out_specs=pl.BlockSpec((None, None, seq, GROUP_W), lambda b, g: (g, b, 0, 0)),
        out_shape=jax.ShapeDtypeStruct((G, batch, seq, GROUP_W), BF16),
        scratch_shapes=[pltpu.VMEM((seq + 16, GROUP_W), F32),
                        pltpu.VMEM((seq, GROUP_W), F32),
                        pltpu.VMEM((seq, SSM_STATE), BF16),
                        pltpu.VMEM((seq, SSM_STATE), BF16),
                        pltpu.VMEM((seq, GROUP_W), F32),
                        pltpu.VMEM((seq, GROUP_W), F32),
                        pltpu.VMEM((SSM_STATE, GROUP_W), F32),
                        pltpu.VMEM((SSM_STATE, GROUP_W), F32)],
        compiler_params=_params("parallel", "parallel"),
        name="ssd",
    )(zx4, zx4, bc4, bc4, dt_col, dt_row, cwx, cwb, cwc, cbx, cbb, cbc,
      db_col, db_row, al_col, al_row, dskip, ng)
    return out.reshape(G, batch * seq, GROUP_W)


def _out_proj_kernel(attn_ref, ssd_ref, w_ref, x_ref, o_ref, lhs):
    @pl.when(pl.program_id(1) == 0)
    def _():
        for h in range(ATTN_HEADS):
            lhs[:, h * HEAD_DIM:(h + 1) * HEAD_DIM] = attn_ref[h]
        for g in range(SSM_GROUPS):
            lhs[:, ATTN_W + g * GROUP_W:ATTN_W + (g + 1) * GROUP_W] = ssd_ref[g]
    o_ref[...] = x_ref[...] + jnp.dot(lhs[...], w_ref[...], preferred_element_type=F32)


def _out_proj(attn, ssd, w, x2d, tm=512, tn=1024):
    n, d = x2d.shape
    kk = w.shape[0]
    return pl.pallas_call(
        _out_proj_kernel,
        grid=(n // tm, d // tn),
        in_specs=[pl.BlockSpec((ATTN_HEADS, tm, HEAD_DIM), lambda i, j: (0, i, 0)),
                  pl.BlockSpec((SSM_GROUPS, tm, GROUP_W), lambda i, j: (0, i, 0)),
                  pl.BlockSpec((kk, tn), lambda i, j: (0, j)),
                  pl.BlockSpec((tm, tn), lambda i, j: (i, j))],
        out_specs=pl.BlockSpec((tm, tn), lambda i, j: (i, j)),
        out_shape=jax.ShapeDtypeStruct((n, d), F32),
        scratch_shapes=[pltpu.VMEM((tm, kk), BF16)],
        compiler_params=_params("parallel", "arbitrary"),
        name="out_proj",
    )(attn, ssd, w, x2d)


def _mlp_kernel(x_ref, g_ref, wu_ref, wd_ref, o_ref, hm):
    @pl.when(pl.program_id(1) == 0)
    def _():
        x = x_ref[...]
        ms = jnp.mean(x * x, axis=-1, keepdims=True)
        hm[...] = (x * lax.rsqrt(ms + EPS) * g_ref[...]).astype(hm.dtype)
        o_ref[...] = x
    u = jnp.maximum(jnp.dot(hm[...], wu_ref[...], preferred_element_type=F32), 0.0)
    o_ref[...] += jnp.dot((u * u).astype(BF16), wd_ref[...], preferred_element_type=F32)


def _mlp(x2d, g, wu, wd, tm=512, tf=1024):
    n, d = x2d.shape
    f = wu.shape[1]
    return pl.pallas_call(
        _mlp_kernel,
        grid=(n // tm, f // tf),
        in_specs=[pl.BlockSpec((tm, d), lambda i, j: (i, 0)),
                  pl.BlockSpec((1, d), lambda i, j: (0, 0)),
                  pl.BlockSpec((d, tf), lambda i, j: (0, j)),
                  pl.BlockSpec((tf, d), lambda i, j: (j, 0))],
        out_specs=pl.BlockSpec((tm, d), lambda i, j: (i, 0)),
        out_shape=jax.ShapeDtypeStruct((n, d), F32),
        scratch_shapes=[pltpu.VMEM((tm, d), BF16)],
        compiler_params=_params("parallel", "arbitrary"),
        name="mlp",
    )(x2d, g.reshape(1, d), wu, wd)


def kernel(x, norm_mix_g, w_in, q_norm_g, k_norm_g, rel_bias, conv_w, conv_b, dt_bias, a_log,
           d_skip, ssd_norm_g, w_out, norm_mlp_g, w_up, w_down):
    batch, seq, _ = x.shape
    n = batch * seq
    G, nh = SSM_GROUPS, HEADS_PER_GROUP
    nc = seq // SSD_CHUNK
    o_z = 3 * ATTN_W
    o_bc = o_z + 2 * SSM_W
    o_dt = o_bc + 2 * G * SSM_STATE
    x2d = x.reshape(n, D_MODEL)
    bias = _bias_tiles(rel_bias)

    for layer in range(w_in.shape[0]):
        wi = w_in[layer]
        h = _rmsnorm(x2d, norm_mix_g[layer])

        scale = 1.0 / math.sqrt(HEAD_DIM)
        gains = jnp.concatenate([jnp.tile(q_norm_g[layer].astype(F32) * scale, ATTN_HEADS),
                                 jnp.tile(k_norm_g[layer].astype(F32), ATTN_HEADS),
                                 jnp.ones((ATTN_W,), F32)]).reshape(1, 3 * ATTN_W)
        qkv = _proj_qkv(h, wi[:, :o_z].astype(BF16), gains)
        zx = _proj_split(h, wi[:, o_z:o_bc].astype(BF16), GROUP_W, "proj_zx")
        bcm = _proj_split(h, wi[:, o_bc:o_dt].astype(BF16), SSM_STATE, "proj_bc")
        dt_raw = _proj_plain(h, wi[:, o_dt:].astype(BF16), "proj_dt")

        attn = _attention(qkv, bias, batch, seq)

        dt5 = dt_raw.reshape(batch, seq, 2, G, nh)
        dt_col = dt5.transpose(0, 3, 1, 2, 4).reshape(batch, G, seq, 2 * nh)
        dt_row = (dt5.reshape(batch, nc, SSD_CHUNK, 2, G, nh)
                  .transpose(0, 4, 1, 3, 5, 2).reshape(batch, G, nc, 2 * nh, SSD_CHUNK))
        per_dir = lambda t: t.astype(F32).reshape(2, G, nh).transpose(1, 0, 2).reshape(G, 2 * nh)
        db, al = per_dir(dt_bias[layer]), per_dir(a_log[layer])
        cw, cbias = conv_w[layer].astype(F32), conv_b[layer].astype(F32)
        gn = G * SSM_STATE
        grp = lambda t, width: t.reshape(t.shape[0], G, width).transpose(1, 0, 2)
        ssd = _ssd(
            zx, bcm, dt_col, dt_row,
            grp(cw[:, :SSM_W], GROUP_W), grp(cw[:, SSM_W:SSM_W + gn], SSM_STATE),
            grp(cw[:, SSM_W + gn:], SSM_STATE),
            grp(cbias[None, :SSM_W], GROUP_W), grp(cbias[None, SSM_W:SSM_W + gn], SSM_STATE),
            grp(cbias[None, SSM_W + gn:], SSM_STATE),
            db.reshape(G, 1, 2 * nh), db.reshape(G, 2 * nh, 1),
            al.reshape(G, 1, 2 * nh), al.reshape(G, 2 * nh, 1),
            jnp.repeat(d_skip[layer].astype(F32), SSM_HEAD_DIM).reshape(G, 1, GROUP_W),
            ssd_norm_g[layer].astype(F32).reshape(G, 1, GROUP_W),
            batch, seq)

        x2d = _out_proj(attn, ssd, w_out[layer].astype(BF16), x2d)
        x2d = _mlp(x2d, norm_mlp_g[layer], w_up[layer].astype(BF16), w_down[layer].astype(BF16))
    return x2d.reshape(batch, seq, D_MODEL)
```

```python
import functools
import math

import jax
import jax.numpy as jnp
from jax import lax
from jax.experimental import pallas as pl
from jax.experimental.pallas import tpu as pltpu

D_MODEL = 2048
ATTN_HEADS = 16
HEAD_DIM = 128
ATTN_W = ATTN_HEADS * HEAD_DIM
SSM_HEADS = 32
SSM_HEAD_DIM = 64
SSM_W = SSM_HEADS * SSM_HEAD_DIM
SSM_GROUPS = 8
HEADS_PER_GROUP = SSM_HEADS // SSM_GROUPS
GROUP_W = SSM_W // SSM_GROUPS
SSM_STATE = 128
CONV_WIDTH = 5
D_FF = 4 * D_MODEL
DILATIONS = (1, 4, 16)
HALF_WINDOW = 64
NUM_BUCKETS = 32
MAX_DISTANCE = 1024
NEG_INF = -1e30
EPS = 1e-6

LANES = 128
Q_SUB = 128
K_WIN = 256
SSD_CHUNK = 128
VMEM_LIMIT = 56 * 1024 * 1024

F32 = jnp.float32
BF16 = jnp.bfloat16


def _params(*sem):
    return pltpu.CompilerParams(dimension_semantics=sem, vmem_limit_bytes=VMEM_LIMIT)


def _sigmoid(x):
    return 1.0 / (1.0 + jnp.exp(-x))


def _softplus(x):
    return jnp.maximum(x, 0.0) + jnp.log1p(jnp.exp(-jnp.abs(x)))


def _rmsnorm_kernel(x_ref, g_ref, o_ref):
    x = x_ref[...]
    ms = jnp.mean(x * x, axis=-1, keepdims=True)
    o_ref[...] = (x * lax.rsqrt(ms + EPS) * g_ref[...]).astype(o_ref.dtype)


def _rmsnorm(x2d, g, tm=512):
    n, d = x2d.shape
    return pl.pallas_call(
        _rmsnorm_kernel,
        grid=(n // tm,),
        in_specs=[pl.BlockSpec((tm, d), lambda i: (i, 0)),
                  pl.BlockSpec((1, d), lambda i: (0, 0))],
        out_specs=pl.BlockSpec((tm, d), lambda i: (i, 0)),
        out_shape=jax.ShapeDtypeStruct((n, d), BF16),
        compiler_params=_params("parallel"),
        name="rmsnorm",
    )(x2d, g.reshape(1, d))


def _proj_qkv_kernel(a_ref, w_ref, g_ref, o_ref, *, heads_per_tile, normed_tiles):
    acc = jnp.dot(a_ref[...], w_ref[...], preferred_element_type=F32)
    j = pl.program_id(1)

    @pl.when(j < normed_tiles)
    def _():
        for h in range(heads_per_tile):
            a = acc[:, h * HEAD_DIM:(h + 1) * HEAD_DIM]
            ms = jnp.mean(a * a, axis=-1, keepdims=True)
            g = g_ref[:, h * HEAD_DIM:(h + 1) * HEAD_DIM]
            o_ref[h] = (a * lax.rsqrt(ms + EPS) * g).astype(o_ref.dtype)

    @pl.when(j >= normed_tiles)
    def _():
        for h in range(heads_per_tile):
            o_ref[h] = acc[:, h * HEAD_DIM:(h + 1) * HEAD_DIM].astype(o_ref.dtype)


def _proj_split_kernel(a_ref, w_ref, o_ref, *, width):
    acc = jnp.dot(a_ref[...], w_ref[...], preferred_element_type=F32)
    for c in range(o_ref.shape[0]):
        o_ref[c] = acc[:, c * width:(c + 1) * width].astype(o_ref.dtype)


def _proj_plain_kernel(a_ref, w_ref, o_ref):
    o_ref[...] = jnp.dot(a_ref[...], w_ref[...], preferred_element_type=F32).astype(o_ref.dtype)


def _proj_qkv(h, w, gains, tm=1024, tn=1024):
    n, k = h.shape
    m = w.shape[1]
    hpt = tn // HEAD_DIM
    return pl.pallas_call(
        functools.partial(_proj_qkv_kernel, heads_per_tile=hpt, normed_tiles=2 * ATTN_W // tn),
        grid=(n // tm, m // tn),
        in_specs=[pl.BlockSpec((tm, k), lambda i, j: (i, 0)),
                  pl.BlockSpec((k, tn), lambda i, j: (0, j)),
                  pl.BlockSpec((1, tn), lambda i, j: (0, j))],
        out_specs=pl.BlockSpec((hpt, tm, HEAD_DIM), lambda i, j: (j, i, 0)),
        out_shape=jax.ShapeDtypeStruct((m // HEAD_DIM, n, HEAD_DIM), F32),
        compiler_params=_params("parallel", "arbitrary"),
        name="proj_qkv",
    )(h, w, gains)


def _proj_split(h, w, width, name, tm=1024, tn=1024):
    n, k = h.shape
    m = w.shape[1]
    cpt = tn // width
    return pl.pallas_call(
        functools.partial(_proj_split_kernel, width=width),
        grid=(n // tm, m // tn),
        in_specs=[pl.BlockSpec((tm, k), lambda i, j: (i, 0)),
                  pl.BlockSpec((k, tn), lambda i, j: (0, j))],
        out_specs=pl.BlockSpec((cpt, tm, width), lambda i, j: (j, i, 0)),
        out_shape=jax.ShapeDtypeStruct((m // width, n, width), BF16),
        compiler_params=_params("parallel", "arbitrary"),
        name=name,
    )(h, w)


def _proj_plain(h, w, name, tm=1024):
    n, k = h.shape
    m = w.shape[1]
    return pl.pallas_call(
        _proj_plain_kernel,
        grid=(n // tm,),
        in_specs=[pl.BlockSpec((tm, k), lambda i: (i, 0)),
                  pl.BlockSpec((k, m), lambda i: (0, 0))],
        out_specs=pl.BlockSpec((tm, m), lambda i: (i, 0)),
        out_shape=jax.ShapeDtypeStruct((n, m), F32),
        compiler_params=_params("parallel"),
        name=name,
    )(h, w)


def _t5_bucket(rel):
    nb = NUM_BUCKETS // 2
    max_exact = nb // 2
    ret = (rel > 0).astype(jnp.int32) * nb
    n = jnp.abs(rel)
    nf = jnp.maximum(n, 1).astype(jnp.float32)
    large = max_exact + (jnp.log(nf / max_exact) / math.log(MAX_DISTANCE / max_exact)
                         * (nb - max_exact)).astype(jnp.int32)
    large = jnp.minimum(large, nb - 1)
    return ret + jnp.where(n < max_exact, n, large)


def _bias_rows(rel_bias):
    period = K_WIN + Q_SUB
    m = jnp.arange(period)
    delta = jnp.where(m < K_WIN, m, m - period)
    rows = []
    for d in DILATIONS:
        for off in (0, -HALF_WINDOW, -2 * HALF_WINDOW):
            rel = delta + off
            valid = jnp.abs(rel) <= HALF_WINDOW
            b = rel_bias[_t5_bucket(rel * d)].astype(F32)
            rows.append(jnp.where(valid[:, None], b, NEG_INF))
    return jnp.stack(rows, axis=0).transpose(2, 0, 1)


ATTN_UNROLL = 16


def _attn_kernel(q_ref, k_ref, v_ref, brow_ref, o_ref, bias_s, tmp_s, xq4, xt4,
                 q16, k1, k4, k16, v1, v4, v16, acc_s, max_s, den_s, *, seq):
    period = K_WIN + Q_SUB
    n4, n16 = seq // 4, seq // 16

    @pl.when(pl.program_id(1) == 0)
    def _():
        for idx in range(9):
            row = jnp.broadcast_to(brow_ref[idx:idx + 1, :], (Q_SUB, period))
            tile = pltpu.roll(row, 0, 1, stride=1, stride_axis=0)
            d = DILATIONS[idx // 3]
            if d == 16:
                bias_s[idx] = tile[:, :K_WIN]
                continue
            for half in range(K_WIN // LANES):
                tmp_s[half] = tile[:, half * LANES:(half + 1) * LANES]
            groups = 16 // d
            for half in range(K_WIN // LANES):
                for g in range(groups):
                    n = Q_SUB // groups
                    bias_s[idx, g * n:(g + 1) * n, half * LANES:(half + 1) * LANES] = (
                        tmp_s[half, pl.ds(g, n, stride=groups), :])

    cp = 256

    def split4(src, dst):
        for r4 in range(4):
            def body(c, carry, r4=r4):
                t0 = pl.multiple_of(c * cp, cp)
                dst[r4, pl.ds(t0, cp), :] = src[pl.ds(r4 + 4 * t0, cp, stride=4), :]
                return carry
            lax.fori_loop(0, n4 // cp, body, 0)

    def split16(src4, dst16, dst4=None):
        for r4 in range(4):
            for a in range(4):
                dst16[4 * a + r4] = src4[r4, pl.ds(a, n16, stride=4), :].astype(BF16)
            if dst4 is not None:
                def body(c, carry, r4=r4):
                    t0 = pl.multiple_of(c * cp, cp)
                    dst4[r4, pl.ds(t0, cp), :] = src4[r4, pl.ds(t0, cp), :].astype(BF16)
                    return carry
                lax.fori_loop(0, n4 // cp, body, 0)

    def cast(src, dst):
        def body(c, carry):
            t0 = pl.multiple_of(c * cp, cp)
            dst[pl.ds(t0, cp), :] = src[pl.ds(t0, cp), :].astype(BF16)
            return carry
        lax.fori_loop(0, seq // cp, body, 0)

    split4(q_ref, xq4)
    split16(xq4, q16)
    for src, d1, d4, d16 in ((k_ref, k1, k4, k16), (v_ref, v1, v4, v16)):
        cast(src, d1)
        split4(src, xt4)
        split16(xt4, d16, d4)

    ones = jnp.ones((K_WIN, LANES), BF16)

    def sub_tile(q, k_s, v_s, length, s0, pi):
        w0 = pl.multiple_of(jnp.clip(s0 - HALF_WINDOW, 0, length - K_WIN), HALF_WINDOW)
        place = jnp.where(s0 == 0, 0, jnp.where(s0 == length - Q_SUB, 2, 1))
        k = k_s[pl.ds(w0, K_WIN), :]
        v = v_s[pl.ds(w0, K_WIN), :]
        s = lax.dot_general(q, k, (((1,), (1,)), ((), ())), preferred_element_type=F32)
        s = s + bias_s[3 * pi + place]
        m = jnp.max(s, axis=-1, keepdims=True)
        p = jnp.exp(s - m).astype(BF16)
        pv = jnp.dot(p, jnp.concatenate([v, ones], axis=1), preferred_element_type=F32)
        return pv[:, :LANES], jnp.broadcast_to(m, (Q_SUB, LANES)), pv[:, LANES:]

    def store(pi, res, slab_of_group, l0, n):
        for g in range(Q_SUB // n):
            for val, dst in zip(res, (acc_s, max_s, den_s)):
                dst[pi, slab_of_group(g), pl.ds(l0, n), :] = val[g * n:(g + 1) * n]

    U = ATTN_UNROLL

    def body1(it, carry):
        res = []
        for u in range(U):
            l0 = pl.multiple_of((it * U + u) * 8, 8)
            q = jnp.concatenate([xq4[r % 4, pl.ds(4 * l0 + r // 4, 8, stride=4), :] for r in range(16)],
                                axis=0).astype(BF16)
            res.append((sub_tile(q, k1, v1, seq, pl.multiple_of(16 * l0, Q_SUB), 0), l0))
        for r3, l0 in res:
            store(0, r3, lambda g: g, l0, 8)
        return carry
    lax.fori_loop(0, seq // Q_SUB // U, body1, 0)

    def body4(it, carry):
        res = []
        for u in range(U // 4):
            l0 = pl.multiple_of((it * (U // 4) + u) * 32, 32)
            for r4 in range(4):
                q = jnp.concatenate([q16[4 * a + r4, pl.ds(l0, 32), :] for a in range(4)], axis=0)
                res.append((sub_tile(q, k4.at[r4], v4.at[r4], n4, pl.multiple_of(4 * l0, Q_SUB), 1),
                            r4, l0))
        for r3, r4, l0 in res:
            store(1, r3, lambda g, r4=r4: 4 * g + r4, l0, 32)
        return carry
    lax.fori_loop(0, n4 // Q_SUB // (U // 4), body4, 0)

    def body16(it, carry):
        g0 = (it // (n16 // Q_SUB)) * U
        l0 = pl.multiple_of((it % (n16 // Q_SUB)) * Q_SUB, Q_SUB)
        res = [sub_tile(q16[g0 + u, pl.ds(l0, Q_SUB), :], k16.at[g0 + u], v16.at[g0 + u], n16, l0, 2)
               for u in range(U)]
        for u, r3 in enumerate(res):
            store(2, r3, lambda g, u=u: g0 + u, l0, Q_SUB)
        return carry
    lax.fori_loop(0, (16 // U) * (n16 // Q_SUB), body16, 0)

    rows = 64
    for r in range(16):
        for pc in range(n16 // rows):
            idx = (r, pl.ds(pc * rows, rows))
            ma, mb, mc = max_s[(0,) + idx], max_s[(1,) + idx], max_s[(2,) + idx]
            mx = jnp.maximum(jnp.maximum(ma, mb), mc)
            ea, eb, ec = jnp.exp(ma - mx), jnp.exp(mb - mx), jnp.exp(mc - mx)
            num = ea * acc_s[(0,) + idx] + eb * acc_s[(1,) + idx] + ec * acc_s[(2,) + idx]
            den = ea * den_s[(0,) + idx] + eb * den_s[(1,) + idx] + ec * den_s[(2,) + idx]
            o_ref[pl.ds(pc * rows, rows), r * LANES:(r + 1) * LANES] = (num / den).astype(o_ref.dtype)


def _attention(qkv, brow, batch, seq):
    H = ATTN_HEADS
    n4, n16 = seq // 4, seq // 16
    view = qkv.reshape(3 * H, batch, seq, HEAD_DIM)
    in_specs = [pl.BlockSpec((None, None, seq, HEAD_DIM), lambda h, b, part=part: (part * H + h, b, 0, 0))
                for part in range(3)]
    in_specs.append(pl.BlockSpec((None, 9, K_WIN + Q_SUB), lambda h, b: (h, 0, 0)))
    kv_slabs = [pltpu.VMEM((seq, LANES), BF16), pltpu.VMEM((4, n4, LANES), BF16),
                pltpu.VMEM((16, n16, LANES), BF16)]
    out = pl.pallas_call(
        functools.partial(_attn_kernel, seq=seq),
        grid=(H, batch),
        in_specs=in_specs,
        out_specs=pl.BlockSpec((None, None, n16, 16 * HEAD_DIM), lambda h, b: (h, b, 0, 0)),
        out_shape=jax.ShapeDtypeStruct((H, batch, n16, 16 * HEAD_DIM), BF16),
        scratch_shapes=[pltpu.VMEM((9, Q_SUB, K_WIN), F32),
                        pltpu.VMEM((K_WIN // LANES, Q_SUB, LANES), F32),
                        pltpu.VMEM((4, n4, LANES), F32),
                        pltpu.VMEM((4, n4, LANES), F32),
                        pltpu.VMEM((16, n16, LANES), BF16)]
                       + kv_slabs + kv_slabs
                       + [pltpu.VMEM((3, 16, n16, LANES), F32)] * 3,
        compiler_params=_params("parallel", "arbitrary"),
        name="dilated_attention",
    )(view, view, view, brow)
    return out.reshape(H, batch * seq, HEAD_DIM)


def _ssd_kernel(z_ref, x_ref, b_ref, c_ref, dtc_ref, dtr_ref, cwx_ref, cwb_ref, cwc_ref,
                cbx_ref, cbb_ref, cbc_ref, dbc_ref, dbr_ref, alc_ref, alr_ref, dsk_ref, ng_ref,
                o_ref, pad, xc, bc, cc, yf, yb, sf, sb, *, seq):
    T = SSD_CHUNK
    nc = seq // T
    rb = 128
    hi = lax.Precision.HIGHEST

    def conv(src_ref, cw_ref, cb_ref, dst_ref, width):
        pad[0:8, :] = jnp.zeros((8, GROUP_W), F32)
        pad[seq + 8:seq + 16, :] = jnp.zeros((8, GROUP_W), F32)

        def fill(i, carry):
            t0 = pl.multiple_of(i * rb, rb)
            pad[pl.ds(t0 + 8, rb), 0:width] = src_ref[pl.ds(t0, rb), :].astype(F32)
            return carry
        lax.fori_loop(0, seq // rb, fill, 0)

        def taps(i, carry):
            t0 = pl.multiple_of(i * rb, rb)
            acc = jnp.broadcast_to(cb_ref[...], (rb, width))
            blk = pad[pl.ds(t0, rb + 16), 0:width]
            for w in range(CONV_WIDTH):
                lo = 8 - CONV_WIDTH // 2 + w
                acc = acc + blk[lo:lo + rb] * cw_ref[w:w + 1, :]
            dst_ref[pl.ds(t0, rb), :] = (acc * _sigmoid(acc)).astype(dst_ref.dtype)
            return carry
        lax.fori_loop(0, seq // rb, taps, 0)

    conv(x_ref, cwx_ref, cbx_ref, xc, GROUP_W)
    conv(b_ref, cwb_ref, cbb_ref, bc, SSM_STATE)
    conv(c_ref, cwc_ref, cbc_ref, cc, SSM_STATE)

    row = lax.broadcasted_iota(jnp.int32, (T, T), 0)
    col = lax.broadcasted_iota(jnp.int32, (T, T), 1)
    tril = (col <= row).astype(F32)
    triu = (row <= col).astype(F32)
    lower = col < row
    upper = col > row
    nh = HEADS_PER_GROUP
    expand = (lax.broadcasted_iota(jnp.int32, (2 * nh, 2 * GROUP_W), 0)
              == lax.broadcasted_iota(jnp.int32, (2 * nh, 2 * GROUP_W), 1) // SSM_HEAD_DIM).astype(F32)
    exp_f = expand[:, :GROUP_W]
    exp_b = expand[:, GROUP_W:]
    lane_head = lax.broadcasted_iota(jnp.int32, (T, GROUP_W), 1) // SSM_HEAD_DIM
    a_col = -jnp.exp(alc_ref[...])
    a_row = -jnp.exp(alr_ref[...])

    sf[...] = jnp.zeros_like(sf)
    sb[...] = jnp.zeros_like(sb)

    def col_terms(t0):
        dt_c = _softplus(dtc_ref[pl.ds(t0, T), :] + dbc_ref[...])
        a_c = dt_c * a_col
        cum_c = jnp.dot(tril, a_c, precision=hi, preferred_element_type=F32)
        return dt_c, a_c, cum_c

    def step(i, carry):
        t0 = pl.multiple_of(i * T, T)
        dt_c, a_c, cum_c = col_terms(t0)
        dt_r = _softplus(dtr_ref[i] + dbr_ref[...])
        a_r = dt_r * a_row
        cum_r = jnp.dot(a_r, triu, precision=hi, preferred_element_type=F32)
        exc_c = cum_c - a_c
        exc_r = cum_r - a_r
        x = xc[pl.ds(t0, T), :]
        xb = x.astype(BF16)
        bk = bc[pl.ds(t0, T), :]
        ck = cc[pl.ds(t0, T), :]
        cb = lax.dot_general(ck, bk, (((1,), (1,)), ((), ())), preferred_element_type=F32)
        y = x * dsk_ref[...]
        for h in range(nh):
            hb = nh + h
            arg = jnp.where(lower, cum_c[:, h:h + 1] - cum_r[h:h + 1, :],
                            exc_r[hb:hb + 1, :] - exc_c[:, hb:hb + 1])
            wgt = jnp.where(lower, dt_r[h:h + 1, :],
                            jnp.where(upper, dt_r[hb:hb + 1, :], dt_r[h:h + 1, :] + dt_r[hb:hb + 1, :]))
            mat = (cb * jnp.exp(arg) * wgt).astype(BF16)
            yh = jnp.dot(mat, xb, preferred_element_type=F32)
            y = jnp.where(lane_head == h, y + yh, y)
        cumx = jnp.dot(cum_c, exp_f, precision=hi, preferred_element_type=F32)
        dtx = jnp.dot(dt_c, exp_f, precision=hi, preferred_element_type=F32)
        last = cumx[T - 1:T, :]
        s_prev = sf[...]
        y = y + jnp.exp(cumx) * jnp.dot(ck, s_prev.astype(BF16), preferred_element_type=F32)
        xs = (x * dtx * jnp.exp(last - cumx)).astype(BF16)
        sf[...] = jnp.exp(last) * s_prev + lax.dot_general(
            bk, xs, (((0,), (0,)), ((), ())), preferred_element_type=F32)
        yf[pl.ds(t0, T), :] = y

        t2 = pl.multiple_of((nc - 1 - i) * T, T)
        dt_c2, a_c2, cum_c2 = col_terms(t2)
        cumx2 = jnp.dot(cum_c2, exp_b, precision=hi, preferred_element_type=F32)
        excx2 = cumx2 - jnp.dot(a_c2, exp_b, precision=hi, preferred_element_type=F32)
        dtx2 = jnp.dot(dt_c2, exp_b, precision=hi, preferred_element_type=F32)
        last2 = cumx2[T - 1:T, :]
        x2 = xc[pl.ds(t2, T), :]
        bk2 = bc[pl.ds(t2, T), :]
        ck2 = cc[pl.ds(t2, T), :]
        s_prev2 = sb[...]
        yb[pl.ds(t2, T), :] = jnp.exp(last2 - excx2) * jnp.dot(
            ck2, s_prev2.astype(BF16), preferred_element_type=F32)
        xs2 = (x2 * dtx2 * jnp.exp(excx2)).astype(BF16)
        sb[...] = jnp.exp(last2) * s_prev2 + lax.dot_general(
            bk2, xs2, (((0,), (0,)), ((), ())), preferred_element_type=F32)
        return carry

    lax.fori_loop(0, nc, step, 0)

    def finish(i, carry):
        t0 = pl.multiple_of(i * T, T)
        zz = z_ref[pl.ds(t0, T), :].astype(F32)
        y = (yf[pl.ds(t0, T), :] + yb[pl.ds(t0, T), :]) * (zz * _sigmoid(zz))
        ms = jnp.mean(y * y, axis=-1, keepdims=True)
        o_ref[pl.ds(t0, T), :] = (y * lax.rsqrt(ms + EPS) * ng_ref[...]).astype(o_ref.dtype)
        return carry
    lax.fori_loop(0, nc, finish, 0)


def _ssd(zx, bcm, dt_col, dt_row, cwx, cwb, cwc, cbx, cbb, cbc, db_col, db_row, al_col, al_row,
         dskip, ng, batch, seq):
    G = SSM_GROUPS
    nc = seq // SSD_CHUNK
    zx4 = zx.reshape(2 * G, batch, seq, GROUP_W)
    bc4 = bcm.reshape(2 * G, batch, seq, SSM_STATE)
    nd = 2 * HEADS_PER_GROUP

    def per_group(shape):
        return pl.BlockSpec((None,) + shape, lambda b, g: (g,) + (0,) * len(shape))

    in_specs = [
        pl.BlockSpec((None, None, seq, GROUP_W), lambda b, g: (g, b, 0, 0)),
        pl.BlockSpec((None, None, seq, GROUP_W), lambda b, g: (G + g, b, 0, 0)),
        pl.BlockSpec((None, None, seq, SSM_STATE), lambda b, g: (g, b, 0, 0)),
        pl.BlockSpec((None, None, seq, SSM_STATE), lambda b, g: (G + g, b, 0, 0)),
        pl.BlockSpec((None, None, seq, nd), lambda b, g: (b, g, 0, 0)),
        pl.BlockSpec((None, None, nc, nd, SSD_CHUNK), lambda b, g: (b, g, 0, 0, 0)),
        per_group((CONV_WIDTH, GROUP_W)), per_group((CONV_WIDTH, SSM_STATE)),
        per_group((CONV_WIDTH, SSM_STATE)),
        per_group((1, GROUP_W)), per_group((1, SSM_STATE)), per_group((1, SSM_STATE)),
        per_group((1, nd)), per_group((nd, 1)), per_group((1, nd)), per_group((nd, 1)),
        per_group((1, GROUP_W)), per_group((1, GROUP_W)),
    ]
    out = pl.pallas_call(
        functools.partial(_ssd_kernel, seq=seq),
        grid=(batch, G),
        in_specs=in_specs,
        out_specs=pl.BlockSpec((None, None, seq, GROUP_W), lambda b, g: (g, b, 0, 0)),
        out_shape=jax.ShapeDtypeStruct((G, batch, seq, GROUP_W), BF16),
        scratch_shapes=[pltpu.VMEM((seq + 16, GROUP_W), F32),
                        pltpu.VMEM((seq, GROUP_W), F32),
                        pltpu.VMEM((seq, SSM_STATE), BF16),
                        pltpu.VMEM((seq, SSM_STATE), BF16),
                        pltpu.VMEM((seq, GROUP_W), F32),
                        pltpu.VMEM((seq, GROUP_W), F32),
                        pltpu.VMEM((SSM_STATE, GROUP_W), F32),
                        pltpu.VMEM((SSM_STATE, GROUP_W), F32)],
        compiler_params=_params("parallel", "parallel"),
        name="ssd",
    )(zx4, zx4, bc4, bc4, dt_col, dt_row, cwx, cwb, cwc, cbx, cbb, cbc,
      db_col, db_row, al_col, al_row, dskip, ng)
    return out.reshape(G, batch * seq, GROUP_W)


def _out_proj_kernel(attn_ref, ssd_ref, w_ref, x_ref, o_ref, lhs):
    @pl.when(pl.program_id(1) == 0)
    def _():
        for h in range(ATTN_HEADS):
            lhs[:, h * HEAD_DIM:(h + 1) * HEAD_DIM] = attn_ref[h]
        for g in range(SSM_GROUPS):
            lhs[:, ATTN_W + g * GROUP_W:ATTN_W + (g + 1) * GROUP_W] = ssd_ref[g]
    o_ref[...] = x_ref[...] + jnp.dot(lhs[...], w_ref[...], preferred_element_type=F32)


def _out_proj(attn, ssd, w, x2d, tm=512, tn=1024):
    n, d = x2d.shape
    kk = w.shape[0]
    return pl.pallas_call(
        _out_proj_kernel,
        grid=(n // tm, d // tn),
        in_specs=[pl.BlockSpec((ATTN_HEADS, tm, HEAD_DIM), lambda i, j: (0, i, 0)),
                  pl.BlockSpec((SSM_GROUPS, tm, GROUP_W), lambda i, j: (0, i, 0)),
                  pl.BlockSpec((kk, tn), lambda i, j: (0, j)),
                  pl.BlockSpec((tm, tn), lambda i, j: (i, j))],
        out_specs=pl.BlockSpec((tm, tn), lambda i, j: (i, j)),
        out_shape=jax.ShapeDtypeStruct((n, d), F32),
        scratch_shapes=[pltpu.VMEM((tm, kk), BF16)],
        compiler_params=_params("parallel", "arbitrary"),
        name="out_proj",
    )(attn, ssd, w, x2d)


def _mlp_kernel(x_ref, g_ref, wu_ref, wd_ref, o_ref, hm):
    @pl.when(pl.program_id(1) == 0)
    def _():
        x = x_ref[...]
        ms = jnp.mean(x * x, axis=-1, keepdims=True)
        hm[...] = (x * lax.rsqrt(ms + EPS) * g_ref[...]).astype(hm.dtype)
        o_ref[...] = x
    u = jnp.maximum(jnp.dot(hm[...], wu_ref[...], preferred_element_type=F32), 0.0)
    o_ref[...] += jnp.dot((u * u).astype(BF16), wd_ref[...], preferred_element_type=F32)


def _mlp(x2d, g, wu, wd, tm=512, tf=1024):
    n, d = x2d.shape
    f = wu.shape[1]
    return pl.pallas_call(
        _mlp_kernel,
        grid=(n // tm, f // tf),
        in_specs=[pl.BlockSpec((tm, d), lambda i, j: (i, 0)),
                  pl.BlockSpec((1, d), lambda i, j: (0, 0)),
                  pl.BlockSpec((d, tf), lambda i, j: (0, j)),
                  pl.BlockSpec((tf, d), lambda i, j: (j, 0))],
        out_specs=pl.BlockSpec((tm, d), lambda i, j: (i, 0)),
        out_shape=jax.ShapeDtypeStruct((n, d), F32),
        scratch_shapes=[pltpu.VMEM((tm, d), BF16)],
        compiler_params=_params("parallel", "arbitrary"),
        name="mlp",
    )(x2d, g.reshape(1, d), wu, wd)


def kernel(x, norm_mix_g, w_in, q_norm_g, k_norm_g, rel_bias, conv_w, conv_b, dt_bias, a_log,
           d_skip, ssd_norm_g, w_out, norm_mlp_g, w_up, w_down):
    batch, seq, _ = x.shape
    n = batch * seq
    G, nh = SSM_GROUPS, HEADS_PER_GROUP
    nc = seq // SSD_CHUNK
    o_z = 3 * ATTN_W
    o_bc = o_z + 2 * SSM_W
    o_dt = o_bc + 2 * G * SSM_STATE
    x2d = x.reshape(n, D_MODEL)
    brow = _bias_rows(rel_bias)

    for layer in range(w_in.shape[0]):
        wi = w_in[layer]
        h = _rmsnorm(x2d, norm_mix_g[layer])

        scale = 1.0 / math.sqrt(HEAD_DIM)
        gains = jnp.concatenate([jnp.tile(q_norm_g[layer].astype(F32) * scale, ATTN_HEADS),
                                 jnp.tile(k_norm_g[layer].astype(F32), ATTN_HEADS),
                                 jnp.ones((ATTN_W,), F32)]).reshape(1, 3 * ATTN_W)
        qkv = _proj_qkv(h, wi[:, :o_z].astype(BF16), gains)
        zx = _proj_split(h, wi[:, o_z:o_bc].astype(BF16), GROUP_W, "proj_zx")
        bcm = _proj_split(h, wi[:, o_bc:o_dt].astype(BF16), SSM_STATE, "proj_bc")
        dt_raw = _proj_plain(h, wi[:, o_dt:].astype(BF16), "proj_dt")

        attn = _attention(qkv, brow, batch, seq)

        dt5 = dt_raw.reshape(batch, seq, 2, G, nh)
        dt_col = dt5.transpose(0, 3, 1, 2, 4).reshape(batch, G, seq, 2 * nh)
        dt_row = (dt5.reshape(batch, nc, SSD_CHUNK, 2, G, nh)
                  .transpose(0, 4, 1, 3, 5, 2).reshape(batch, G, nc, 2 * nh, SSD_CHUNK))
        per_dir = lambda t: t.astype(F32).reshape(2, G, nh).transpose(1, 0, 2).reshape(G, 2 * nh)
        db, al = per_dir(dt_bias[layer]), per_dir(a_log[layer])
        cw, cbias = conv_w[layer].astype(F32), conv_b[layer].astype(F32)
        gn = G * SSM_STATE
        grp = lambda t, width: t.reshape(t.shape[0], G, width).transpose(1, 0, 2)
        ssd = _ssd(
            zx, bcm, dt_col, dt_row,
            grp(cw[:, :SSM_W], GROUP_W), grp(cw[:, SSM_W:SSM_W + gn], SSM_STATE),
            grp(cw[:, SSM_W + gn:], SSM_STATE),
            grp(cbias[None, :SSM_W], GROUP_W), grp(cbias[None, SSM_W:SSM_W + gn], SSM_STATE),
            grp(cbias[None, SSM_W + gn:], SSM_STATE),
            db.reshape(G, 1, 2 * nh), db.reshape(G, 2 * nh, 1),
            al.reshape(G, 1, 2 * nh), al.reshape(G, 2 * nh, 1),
            jnp.repeat(d_skip[layer].astype(F32), SSM_HEAD_DIM).reshape(G, 1, GROUP_W),
            ssd_norm_g[layer].astype(F32).reshape(G, 1, GROUP_W),
            batch, seq)

        x2d = _out_proj(attn, ssd, w_out[layer].astype(BF16), x2d)
        x2d = _mlp(x2d, norm_mlp_g[layer], w_up[layer].astype(BF16), w_down[layer].astype(BF16))
    return x2d.reshape(batch, seq, D_MODEL)
```

```python
import functools
import math

import jax
import jax.numpy as jnp
from jax import lax
from jax.experimental import pallas as pl
from jax.experimental.pallas import tpu as pltpu

D_MODEL = 2048
ATTN_HEADS = 16
HEAD_DIM = 128
ATTN_W = ATTN_HEADS * HEAD_DIM
SSM_HEADS = 32
SSM_HEAD_DIM = 64
SSM_W = SSM_HEADS * SSM_HEAD_DIM
SSM_GROUPS = 8
HEADS_PER_GROUP = SSM_HEADS // SSM_GROUPS
GROUP_W = SSM_W // SSM_GROUPS
SSM_STATE = 128
CONV_WIDTH = 5
D_FF = 4 * D_MODEL
DILATIONS = (1, 4, 16)
HALF_WINDOW = 64
NUM_BUCKETS = 32
MAX_DISTANCE = 1024
NEG_INF = -1e30
EPS = 1e-6

LANES = 128
Q_SUB = 128
K_WIN = 256
SSD_CHUNK = 128
VMEM_LIMIT = 56 * 1024 * 1024

F32 = jnp.float32
BF16 = jnp.bfloat16


def _params(*sem):
    return pltpu.CompilerParams(dimension_semantics=sem, vmem_limit_bytes=VMEM_LIMIT)


def _sigmoid(x):
    return 1.0 / (1.0 + jnp.exp(-x))


def _softplus(x):
    return jnp.maximum(x, 0.0) + jnp.log1p(jnp.exp(-jnp.abs(x)))


def _rmsnorm_kernel(x_ref, g_ref, o_ref):
    x = x_ref[...]
    ms = jnp.mean(x * x, axis=-1, keepdims=True)
    o_ref[...] = (x * lax.rsqrt(ms + EPS) * g_ref[...]).astype(o_ref.dtype)


def _rmsnorm(x2d, g, tm=512):
    n, d = x2d.shape
    return pl.pallas_call(
        _rmsnorm_kernel,
        grid=(n // tm,),
        in_specs=[pl.BlockSpec((tm, d), lambda i: (i, 0)),
                  pl.BlockSpec((1, d), lambda i: (0, 0))],
        out_specs=pl.BlockSpec((tm, d), lambda i: (i, 0)),
        out_shape=jax.ShapeDtypeStruct((n, d), BF16),
        compiler_params=_params("parallel"),
        name="rmsnorm",
    )(x2d, g.reshape(1, d))


def _proj_qkv_kernel(a_ref, w_ref, g_ref, o_ref, *, heads_per_tile, normed_tiles):
    acc = jnp.dot(a_ref[...], w_ref[...], preferred_element_type=F32)
    j = pl.program_id(1)

    @pl.when(j < normed_tiles)
    def _():
        for h in range(heads_per_tile):
            a = acc[:, h * HEAD_DIM:(h + 1) * HEAD_DIM]
            ms = jnp.mean(a * a, axis=-1, keepdims=True)
            g = g_ref[:, h * HEAD_DIM:(h + 1) * HEAD_DIM]
            o_ref[h] = (a * lax.rsqrt(ms + EPS) * g).astype(o_ref.dtype)

    @pl.when(j >= normed_tiles)
    def _():
        for h in range(heads_per_tile):
            o_ref[h] = acc[:, h * HEAD_DIM:(h + 1) * HEAD_DIM].astype(o_ref.dtype)


def _proj_split_kernel(a_ref, w_ref, o_ref, *, width):
    acc = jnp.dot(a_ref[...], w_ref[...], preferred_element_type=F32)
    for c in range(o_ref.shape[0]):
        o_ref[c] = acc[:, c * width:(c + 1) * width].astype(o_ref.dtype)


def _proj_plain_kernel(a_ref, w_ref, o_ref):
    o_ref[...] = jnp.dot(a_ref[...], w_ref[...], preferred_element_type=F32).astype(o_ref.dtype)


def _proj_qkv(h, w, gains, tm=1024, tn=1024):
    n, k = h.shape
    m = 3 * ATTN_W
    hpt = tn // HEAD_DIM
    return pl.pallas_call(
        functools.partial(_proj_qkv_kernel, heads_per_tile=hpt, normed_tiles=2 * ATTN_W // tn),
        grid=(n // tm, m // tn),
        in_specs=[pl.BlockSpec((tm, k), lambda i, j: (i, 0)),
                  pl.BlockSpec((k, tn), lambda i, j: (0, j)),
                  pl.BlockSpec((1, tn), lambda i, j: (0, j))],
        out_specs=pl.BlockSpec((hpt, tm, HEAD_DIM), lambda i, j: (j, i, 0)),
        out_shape=jax.ShapeDtypeStruct((m // HEAD_DIM, n, HEAD_DIM), F32),
        compiler_params=_params("parallel", "arbitrary"),
        name="proj_qkv",
    )(h, w, gains)


def _proj_split(h, w, col0, m, width, name, tm=1024, tn=1024):
    n, k = h.shape
    cpt = tn // width
    j0 = col0 // tn
    return pl.pallas_call(
        functools.partial(_proj_split_kernel, width=width),
        grid=(n // tm, m // tn),
        in_specs=[pl.BlockSpec((tm, k), lambda i, j: (i, 0)),
                  pl.BlockSpec((k, tn), lambda i, j: (0, j0 + j))],
        out_specs=pl.BlockSpec((cpt, tm, width), lambda i, j: (j, i, 0)),
        out_shape=jax.ShapeDtypeStruct((m // width, n, width), BF16),
        compiler_params=_params("parallel", "arbitrary"),
        name=name,
    )(h, w)


def _proj_plain(h, w, col0, m, name, tm=1024):
    n, k = h.shape
    j0 = col0 // m
    return pl.pallas_call(
        _proj_plain_kernel,
        grid=(n // tm,),
        in_specs=[pl.BlockSpec((tm, k), lambda i: (i, 0)),
                  pl.BlockSpec((k, m), lambda i: (0, j0))],
        out_specs=pl.BlockSpec((tm, m), lambda i: (i, 0)),
        out_shape=jax.ShapeDtypeStruct((n, m), F32),
        compiler_params=_params("parallel"),
        name=name,
    )(h, w)


def _t5_bucket(rel):
    nb = NUM_BUCKETS // 2
    max_exact = nb // 2
    ret = (rel > 0).astype(jnp.int32) * nb
    n = jnp.abs(rel)
    nf = jnp.maximum(n, 1).astype(jnp.float32)
    large = max_exact + (jnp.log(nf / max_exact) / math.log(MAX_DISTANCE / max_exact)
                         * (nb - max_exact)).astype(jnp.int32)
    large = jnp.minimum(large, nb - 1)
    return ret + jnp.where(n < max_exact, n, large)


def _bias_rows(rel_bias):
    period = K_WIN + Q_SUB
    m = jnp.arange(period)
    delta = jnp.where(m < K_WIN, m, m - period)
    rows = []
    for d in DILATIONS:
        for off in (0, -HALF_WINDOW, -2 * HALF_WINDOW):
            rel = delta + off
            valid = jnp.abs(rel) <= HALF_WINDOW
            b = rel_bias[_t5_bucket(rel * d)].astype(F32)
            rows.append(jnp.where(valid[:, None], b, NEG_INF))
    return jnp.stack(rows, axis=0).transpose(2, 0, 1)


ATTN_UNROLL = 16


def _attn_kernel(q_ref, k_ref, v_ref, brow_ref, o_ref, bias_s, tmp_s, xq4, xt4,
                 q16, k1, k4, k16, v1, v4, v16, acc_s, max_s, den_s, *, seq):
    period = K_WIN + Q_SUB
    n4, n16 = seq // 4, seq // 16

    @pl.when(pl.program_id(1) == 0)
    def _():
        for idx in range(9):
            row = jnp.broadcast_to(brow_ref[idx:idx + 1, :], (Q_SUB, period))
            tile = pltpu.roll(row, 0, 1, stride=1, stride_axis=0)
            d = DILATIONS[idx // 3]
            if d == 16:
                bias_s[idx] = tile[:, :K_WIN]
                continue
            for half in range(K_WIN // LANES):
                tmp_s[half] = tile[:, half * LANES:(half + 1) * LANES]
            groups = 16 // d
            for half in range(K_WIN // LANES):
                for g in range(groups):
                    n = Q_SUB // groups
                    bias_s[idx, g * n:(g + 1) * n, half * LANES:(half + 1) * LANES] = (
                        tmp_s[half, pl.ds(g, n, stride=groups), :])

    cp = 256

    def split4(src, dst):
        for r4 in range(4):
            def body(c, carry, r4=r4):
                t0 = pl.multiple_of(c * cp, cp)
                dst[r4, pl.ds(t0, cp), :] = src[pl.ds(r4 + 4 * t0, cp, stride=4), :]
                return carry
            lax.fori_loop(0, n4 // cp, body, 0)

    def split16(src4, dst16, dst4=None):
        for r4 in range(4):
            for a in range(4):
                dst16[4 * a + r4] = src4[r4, pl.ds(a, n16, stride=4), :].astype(BF16)
            if dst4 is not None:
                def body(c, carry, r4=r4):
                    t0 = pl.multiple_of(c * cp, cp)
                    dst4[r4, pl.ds(t0, cp), :] = src4[r4, pl.ds(t0, cp), :].astype(BF16)
                    return carry
                lax.fori_loop(0, n4 // cp, body, 0)

    def cast(src, dst):
        def body(c, carry):
            t0 = pl.multiple_of(c * cp, cp)
            dst[pl.ds(t0, cp), :] = src[pl.ds(t0, cp), :].astype(BF16)
            return carry
        lax.fori_loop(0, seq // cp, body, 0)

    split4(q_ref, xq4)
    split16(xq4, q16)
    for src, d1, d4, d16 in ((k_ref, k1, k4, k16), (v_ref, v1, v4, v16)):
        cast(src, d1)
        split4(src, xt4)
        split16(xt4, d16, d4)

    ones = jnp.ones((K_WIN, LANES), BF16)

    def sub_tile(q, k_s, v_s, length, s0, pi):
        w0 = pl.multiple_of(jnp.clip(s0 - HALF_WINDOW, 0, length - K_WIN), HALF_WINDOW)
        place = jnp.where(s0 == 0, 0, jnp.where(s0 == length - Q_SUB, 2, 1))
        k = k_s[pl.ds(w0, K_WIN), :]
        v = v_s[pl.ds(w0, K_WIN), :]
        s = lax.dot_general(q, k, (((1,), (1,)), ((), ())), preferred_element_type=F32)
        s = s + bias_s[3 * pi + place]
        m = jnp.max(s, axis=-1, keepdims=True)
        p = jnp.exp(s - m).astype(BF16)
        pv = jnp.dot(p, jnp.concatenate([v, ones], axis=1), preferred_element_type=F32)
        return pv[:, :LANES], jnp.broadcast_to(m, (Q_SUB, LANES)), pv[:, LANES:]

    def store(pi, res, slab_of_group, l0, n):
        for g in range(Q_SUB // n):
            for val, dst in zip(res, (acc_s, max_s, den_s)):
                dst[pi, slab_of_group(g), pl.ds(l0, n), :] = val[g * n:(g + 1) * n]

    U = ATTN_UNROLL

    def body1(it, carry):
        res = []
        for u in range(U):
            l0 = pl.multiple_of((it * U + u) * 8, 8)
            q = jnp.concatenate([xq4[r % 4, pl.ds(4 * l0 + r // 4, 8, stride=4), :] for r in range(16)],
                                axis=0).astype(BF16)
            res.append((sub_tile(q, k1, v1, seq, pl.multiple_of(16 * l0, Q_SUB), 0), l0))
        for r3, l0 in res:
            store(0, r3, lambda g: g, l0, 8)
        return carry
    lax.fori_loop(0, seq // Q_SUB // U, body1, 0)

    def body4(it, carry):
        res = []
        for u in range(U // 4):
            l0 = pl.multiple_of((it * (U // 4) + u) * 32, 32)
            for r4 in range(4):
                q = jnp.concatenate([q16[4 * a + r4, pl.ds(l0, 32), :] for a in range(4)], axis=0)
                res.append((sub_tile(q, k4.at[r4], v4.at[r4], n4, pl.multiple_of(4 * l0, Q_SUB), 1),
                            r4, l0))
        for r3, r4, l0 in res:
            store(1, r3, lambda g, r4=r4: 4 * g + r4, l0, 32)
        return carry
    lax.fori_loop(0, n4 // Q_SUB // (U // 4), body4, 0)

    def body16(it, carry):
        g0 = (it // (n16 // Q_SUB)) * U
        l0 = pl.multiple_of((it % (n16 // Q_SUB)) * Q_SUB, Q_SUB)
        res = [sub_tile(q16[g0 + u, pl.ds(l0, Q_SUB), :], k16.at[g0 + u], v16.at[g0 + u], n16, l0, 2)
               for u in range(U)]
        for u, r3 in enumerate(res):
            store(2, r3, lambda g, u=u: g0 + u, l0, Q_SUB)
        return carry
    lax.fori_loop(0, (16 // U) * (n16 // Q_SUB), body16, 0)

    rows = 64
    for r in range(16):
        for pc in range(n16 // rows):
            idx = (r, pl.ds(pc * rows, rows))
            ma, mb, mc = max_s[(0,) + idx], max_s[(1,) + idx], max_s[(2,) + idx]
            mx = jnp.maximum(jnp.maximum(ma, mb), mc)
            ea, eb, ec = jnp.exp(ma - mx), jnp.exp(mb - mx), jnp.exp(mc - mx)
            num = ea * acc_s[(0,) + idx] + eb * acc_s[(1,) + idx] + ec * acc_s[(2,) + idx]
            den = ea * den_s[(0,) + idx] + eb * den_s[(1,) + idx] + ec * den_s[(2,) + idx]
            o_ref[pl.ds(pc * rows, rows), r * LANES:(r + 1) * LANES] = (num / den).astype(o_ref.dtype)


def _attention(qkv, brow, batch, seq):
    H = ATTN_HEADS
    n4, n16 = seq // 4, seq // 16
    view = qkv.reshape(3 * H, batch, seq, HEAD_DIM)
    in_specs = [pl.BlockSpec((None, None, seq, HEAD_DIM), lambda h, b, part=part: (part * H + h, b, 0, 0))
                for part in range(3)]
    in_specs.append(pl.BlockSpec((None, 9, K_WIN + Q_SUB), lambda h, b: (h, 0, 0)))
    kv_slabs = [pltpu.VMEM((seq, LANES), BF16), pltpu.VMEM((4, n4, LANES), BF16),
                pltpu.VMEM((16, n16, LANES), BF16)]
    out = pl.pallas_call(
        functools.partial(_attn_kernel, seq=seq),
        grid=(H, batch),
        in_specs=in_specs,
        out_specs=pl.BlockSpec((None, None, n16, 16 * HEAD_DIM), lambda h, b: (h, b, 0, 0)),
        out_shape=jax.ShapeDtypeStruct((H, batch, n16, 16 * HEAD_DIM), BF16),
        scratch_shapes=[pltpu.VMEM((9, Q_SUB, K_WIN), F32),
                        pltpu.VMEM((K_WIN // LANES, Q_SUB, LANES), F32),
                        pltpu.VMEM((4, n4, LANES), F32),
                        pltpu.VMEM((4, n4, LANES), F32),
                        pltpu.VMEM((16, n16, LANES), BF16)]
                       + kv_slabs + kv_slabs
                       + [pltpu.VMEM((3, 16, n16, LANES), F32)] * 3,
        compiler_params=_params("parallel", "arbitrary"),
        name="dilated_attention",
    )(view, view, view, brow)
    return out.reshape(H, batch * seq, HEAD_DIM)


def _ssd_kernel(z_ref, x_ref, b_ref, c_ref, dtr_ref, cwx_ref, cwb_ref, cwc_ref,
                cbx_ref, cbb_ref, cbc_ref, dbr_ref, alr_ref, dsk_ref, ng_ref,
                o_ref, pad, xc, bc, cc, arg_s, dt_s, rows_s, cols_s, sbs, sf, sb, *, seq):
    T = SSD_CHUNK
    nc = seq // T
    rb = 128
    hi = lax.Precision.HIGHEST

    def conv(src_ref, cw_ref, cb_ref, dst_ref, width):
        pad[0:8, :] = jnp.zeros((8, GROUP_W), F32)
        pad[seq + 8:seq + 16, :] = jnp.zeros((8, GROUP_W), F32)

        def fill(i, carry):
            t0 = pl.multiple_of(i * rb, rb)
            pad[pl.ds(t0 + 8, rb), 0:width] = src_ref[pl.ds(t0, rb), :].astype(F32)
            return carry
        lax.fori_loop(0, seq // rb, fill, 0)

        def taps(i, carry):
            t0 = pl.multiple_of(i * rb, rb)
            acc = jnp.broadcast_to(cb_ref[...], (rb, width))
            blk = pad[pl.ds(t0, rb + 16), 0:width]
            for w in range(CONV_WIDTH):
                lo = 8 - CONV_WIDTH // 2 + w
                acc = acc + blk[lo:lo + rb] * cw_ref[w:w + 1, :]
            dst_ref[pl.ds(t0, rb), :] = (acc * _sigmoid(acc)).astype(dst_ref.dtype)
            return carry
        lax.fori_loop(0, seq // rb, taps, 0)

    conv(x_ref, cwx_ref, cbx_ref, xc, GROUP_W)
    conv(b_ref, cwb_ref, cbb_ref, bc, SSM_STATE)
    conv(c_ref, cwc_ref, cbc_ref, cc, SSM_STATE)

    nh = HEADS_PER_GROUP
    nd = 2 * nh
    row = lax.broadcasted_iota(jnp.int32, (T, T), 0)
    col = lax.broadcasted_iota(jnp.int32, (T, T), 1)
    triu = (row <= col).astype(F32)
    lower = col < row
    upper = col > row
    lane_head = lax.broadcasted_iota(jnp.int32, (T, GROUP_W), 1) // SSM_HEAD_DIM

    dt = _softplus(dtr_ref[...] + dbr_ref[...])
    a = dt * (-jnp.exp(alr_ref[...]))
    cum = jnp.dot(a.reshape(nc * nd, T), triu, precision=hi,
                  preferred_element_type=F32).reshape(nc, nd, T)
    last = cum[:, :, T - 1:T]
    exc = cum - a
    fwd = lax.broadcasted_iota(jnp.int32, (nc, nd, T), 1) < nh
    arg_s[...] = jnp.where(fwd, cum, exc)
    dt_s[...] = dt
    rows_s[:, 0:nd, :] = arg_s[...]
    rows_s[:, nd:2 * nd, :] = jnp.exp(jnp.where(fwd, cum, last - exc))
    rows_s[:, 2 * nd:3 * nd, :] = dt * jnp.exp(jnp.where(fwd, last - cum, exc))
    rows_s[:, 3 * nd:, :] = jnp.zeros((nc, LANES - 3 * nd, T), F32)

    def expand(cols, first):
        n = cols.shape[0]
        low = lax.broadcasted_iota(jnp.int32, (n, LANES), 1) < SSM_HEAD_DIM
        halves = []
        for j in range(GROUP_W // LANES):
            c0 = jnp.broadcast_to(cols[:, first + 2 * j:first + 2 * j + 1], (n, LANES))
            c1 = jnp.broadcast_to(cols[:, first + 2 * j + 1:first + 2 * j + 2], (n, LANES))
            halves.append(jnp.where(low, c0, c1))
        return jnp.concatenate(halves, axis=1)

    tn_dims = (((0,), (0,)), ((), ()))

    sb[...] = jnp.zeros_like(sb)

    def sweep_back(i, carry):
        ci = nc - 1 - i
        t0 = pl.multiple_of(ci * T, T)
        ct = rows_s[ci].T
        cols_s[ci] = ct
        s_prev = sb[...]
        sbs[ci] = s_prev.astype(BF16)
        xs = (xc[pl.ds(t0, T), :] * expand(ct, 2 * nd + nh)).astype(BF16)
        sb[...] = expand(ct[0:1, :], nd + nh) * s_prev + lax.dot_general(
            bc[pl.ds(t0, T), :], xs, tn_dims, preferred_element_type=F32)
        return carry
    lax.fori_loop(0, nc, sweep_back, 0)

    sf[...] = jnp.zeros_like(sf)

    def sweep_fwd(ci, carry):
        t0 = pl.multiple_of(ci * T, T)
        ct = cols_s[ci]
        arg_r = arg_s[ci]
        dt_r = dt_s[ci]
        x = xc[pl.ds(t0, T), :]
        bk = bc[pl.ds(t0, T), :]
        ck = cc[pl.ds(t0, T), :]
        ckf = ck.astype(F32)
        cb = lax.dot_general(ck, bk, (((1,), (1,)), ((), ())), preferred_element_type=F32)
        lhs = []
        for h in range(nh):
            hb = nh + h
            arg = jnp.where(lower, ct[:, h:h + 1] - arg_r[h:h + 1, :], arg_r[hb:hb + 1, :] - ct[:, hb:hb + 1])
            wgt = jnp.where(lower, dt_r[h:h + 1, :],
                            jnp.where(upper, dt_r[hb:hb + 1, :], dt_r[h:h + 1, :] + dt_r[hb:hb + 1, :]))
            lhs.append(jnp.concatenate(
                [(cb * jnp.exp(arg) * wgt).astype(BF16),
                 (ckf * ct[:, nd + h:nd + h + 1]).astype(BF16),
                 (ckf * ct[:, nd + hb:nd + hb + 1]).astype(BF16)], axis=1))
        s_prev = sf[...]
        rhs = jnp.concatenate([x.astype(BF16), s_prev.astype(BF16), sbs[ci]], axis=0)
        y4 = jnp.dot(jnp.concatenate(lhs, axis=0), rhs, preferred_element_type=F32)
        y = x * dsk_ref[...]
        for h in range(nh):
            y = jnp.where(lane_head == h, y + y4[h * T:(h + 1) * T], y)
        xs = (x * expand(ct, 2 * nd)).astype(BF16)
        sf[...] = expand(ct[T - 1:T, :], nd) * s_prev + lax.dot_general(
            bk, xs, tn_dims, preferred_element_type=F32)
        zz = z_ref[pl.ds(t0, T), :].astype(F32)
        y = y * (zz * _sigmoid(zz))
        ms = jnp.mean(y * y, axis=-1, keepdims=True)
        o_ref[pl.ds(t0, T), :] = (y * lax.rsqrt(ms + EPS) * ng_ref[...]).astype(o_ref.dtype)
        return carry
    lax.fori_loop(0, nc, sweep_fwd, 0)


def _ssd(zx, bcm, dt_row, cwx, cwb, cwc, cbx, cbb, cbc, db_row, al_row, dskip, ng, batch, seq):
    G = SSM_GROUPS
    nc = seq // SSD_CHUNK
    zx4 = zx.reshape(2 * G, batch, seq, GROUP_W)
    bc4 = bcm.reshape(2 * G, batch, seq, SSM_STATE)
    nd = 2 * HEADS_PER_GROUP

    def per_group(shape):
        return pl.BlockSpec((None,) + shape, lambda b, g: (g,) + (0,) * len(shape))

    in_specs = [
        pl.BlockSpec((None, None, seq, GROUP_W), lambda b, g: (g, b, 0, 0)),
        pl.BlockSpec((None, None, seq, GROUP_W), lambda b, g: (G + g, b, 0, 0)),
        pl.BlockSpec((None, None, seq, SSM_STATE), lambda b, g: (g, b, 0, 0)),
        pl.BlockSpec((None, None, seq, SSM_STATE), lambda b, g: (G + g, b, 0, 0)),
        pl.BlockSpec((None, None, nc, nd, SSD_CHUNK), lambda b, g: (b, g, 0, 0, 0)),
        per_group((CONV_WIDTH, GROUP_W)), per_group((CONV_WIDTH, SSM_STATE)),
        per_group((CONV_WIDTH, SSM_STATE)),
        per_group((1, GROUP_W)), per_group((1, SSM_STATE)), per_group((1, SSM_STATE)),
        per_group((nd, 1)), per_group((nd, 1)),
        per_group((1, GROUP_W)), per_group((1, GROUP_W)),
    ]
    out = pl.pallas_call(
        functools.partial(_ssd_kernel, seq=seq),
        grid=(batch, G),
        in_specs=in_specs,
        out_specs=pl.BlockSpec((None, None, seq, GROUP_W), lambda b, g: (g, b, 0, 0)),
        out_shape=jax.ShapeDtypeStruct((G, batch, seq, GROUP_W), BF16),
        scratch_shapes=[pltpu.VMEM((seq + 16, GROUP_W), F32),
                        pltpu.VMEM((seq, GROUP_W), F32),
                        pltpu.VMEM((seq, SSM_STATE), BF16),
                        pltpu.VMEM((seq, SSM_STATE), BF16),
                        pltpu.VMEM((nc, nd, SSD_CHUNK), F32),
                        pltpu.VMEM((nc, nd, SSD_CHUNK), F32),
                        pltpu.VMEM((nc, LANES, SSD_CHUNK), F32),
                        pltpu.VMEM((nc, SSD_CHUNK, LANES), F32),
                        pltpu.VMEM((nc, SSM_STATE, GROUP_W), BF16),
                        pltpu.VMEM((SSM_STATE, GROUP_W), F32),
                        pltpu.VMEM((SSM_STATE, GROUP_W), F32)],
        compiler_params=_params("parallel", "parallel"),
        name="ssd",
    )(zx4, zx4, bc4, bc4, dt_row, cwx, cwb, cwc, cbx, cbb, cbc, db_row, al_row, dskip, ng)
    return out.reshape(G, batch * seq, GROUP_W)


def _out_proj_kernel(attn_ref, ssd_ref, w_ref, x_ref, o_ref, lhs):
    @pl.when(pl.program_id(1) == 0)
    def _():
        for h in range(ATTN_HEADS):
            lhs[:, h * HEAD_DIM:(h + 1) * HEAD_DIM] = attn_ref[h]
        for g in range(SSM_GROUPS):
            lhs[:, ATTN_W + g * GROUP_W:ATTN_W + (g + 1) * GROUP_W] = ssd_ref[g]
    o_ref[...] = x_ref[...] + jnp.dot(lhs[...], w_ref[...], preferred_element_type=F32)


def _out_proj(attn, ssd, w, x2d, tm=512, tn=1024):
    n, d = x2d.shape
    kk = w.shape[0]
    return pl.pallas_call(
        _out_proj_kernel,
        grid=(n // tm, d // tn),
        in_specs=[pl.BlockSpec((ATTN_HEADS, tm, HEAD_DIM), lambda i, j: (0, i, 0)),
                  pl.BlockSpec((SSM_GROUPS, tm, GROUP_W), lambda i, j: (0, i, 0)),
                  pl.BlockSpec((kk, tn), lambda i, j: (0, j)),
                  pl.BlockSpec((tm, tn), lambda i, j: (i, j))],
        out_specs=pl.BlockSpec((tm, tn), lambda i, j: (i, j)),
        out_shape=jax.ShapeDtypeStruct((n, d), F32),
        scratch_shapes=[pltpu.VMEM((tm, kk), BF16)],
        compiler_params=_params("parallel", "arbitrary"),
        name="out_proj",
    )(attn, ssd, w, x2d)


def _mlp_kernel(x_ref, g_ref, wu_ref, wd_ref, o_ref, hm):
    @pl.when(pl.program_id(1) == 0)
    def _():
        x = x_ref[...]
        ms = jnp.mean(x * x, axis=-1, keepdims=True)
        hm[...] = (x * lax.rsqrt(ms + EPS) * g_ref[...]).astype(hm.dtype)
        o_ref[...] = x
    u = jnp.maximum(jnp.dot(hm[...], wu_ref[...], preferred_element_type=F32), 0.0)
    o_ref[...] += jnp.dot((u * u).astype(BF16), wd_ref[...], preferred_element_type=F32)


def _mlp(x2d, g, wu, wd, tm=512, tf=1024):
    n, d = x2d.shape
    f = wu.shape[1]
    return pl.pallas_call(
        _mlp_kernel,
        grid=(n // tm, f // tf),
        in_specs=[pl.BlockSpec((tm, d), lambda i, j: (i, 0)),
                  pl.BlockSpec((1, d), lambda i, j: (0, 0)),
                  pl.BlockSpec((d, tf), lambda i, j: (0, j)),
                  pl.BlockSpec((tf, d), lambda i, j: (j, 0))],
        out_specs=pl.BlockSpec((tm, d), lambda i, j: (i, 0)),
        out_shape=jax.ShapeDtypeStruct((n, d), F32),
        scratch_shapes=[pltpu.VMEM((tm, d), BF16)],
        compiler_params=_params("parallel", "arbitrary"),
        name="mlp",
    )(x2d, g.reshape(1, d), wu, wd)


def kernel(x, norm_mix_g, w_in, q_norm_g, k_norm_g, rel_bias, conv_w, conv_b, dt_bias, a_log,
           d_skip, ssd_norm_g, w_out, norm_mlp_g, w_up, w_down):
    batch, seq, _ = x.shape
    n = batch * seq
    G, nh = SSM_GROUPS, HEADS_PER_GROUP
    nc = seq // SSD_CHUNK
    o_z = 3 * ATTN_W
    o_bc = o_z + 2 * SSM_W
    o_dt = o_bc + 2 * G * SSM_STATE
    x2d = x.reshape(n, D_MODEL)
    brow = _bias_rows(rel_bias)

    for layer in range(w_in.shape[0]):
        wi = w_in[layer].astype(BF16)
        h = _rmsnorm(x2d, norm_mix_g[layer])

        scale = 1.0 / math.sqrt(HEAD_DIM)
        gains = jnp.concatenate([jnp.tile(q_norm_g[layer].astype(F32) * scale, ATTN_HEADS),
                                 jnp.tile(k_norm_g[layer].astype(F32), ATTN_HEADS),
                                 jnp.ones((ATTN_W,), F32)]).reshape(1, 3 * ATTN_W)
        qkv = _proj_qkv(h, wi, gains)
        zx = _proj_split(h, wi, o_z, o_bc - o_z, GROUP_W, "proj_zx")
        bcm = _proj_split(h, wi, o_bc, o_dt - o_bc, SSM_STATE, "proj_bc")
        dt_raw = _proj_plain(h, w_in[layer][:, o_dt:].astype(BF16), 0, 2 * SSM_HEADS, "proj_dt")

        attn = _attention(qkv, brow, batch, seq)

        dt_row = (dt_raw.reshape(batch, nc, SSD_CHUNK, 2, G, nh)
                  .transpose(0, 4, 1, 3, 5, 2).reshape(batch, G, nc, 2 * nh, SSD_CHUNK))
        per_dir = lambda t: t.astype(F32).reshape(2, G, nh).transpose(1, 0, 2).reshape(G, 2 * nh)
        db, al = per_dir(dt_bias[layer]), per_dir(a_log[layer])
        cw, cbias = conv_w[layer].astype(F32), conv_b[layer].astype(F32)
        gn = G * SSM_STATE
        grp = lambda t, width: t.reshape(t.shape[0], G, width).transpose(1, 0, 2)
        ssd = _ssd(
            zx, bcm, dt_row,
            grp(cw[:, :SSM_W], GROUP_W), grp(cw[:, SSM_W:SSM_W + gn], SSM_STATE),
            grp(cw[:, SSM_W + gn:], SSM_STATE),
            grp(cbias[None, :SSM_W], GROUP_W), grp(cbias[None, SSM_W:SSM_W + gn], SSM_STATE),
            grp(cbias[None, SSM_W + gn:], SSM_STATE),
            db.reshape(G, 2 * nh, 1), al.reshape(G, 2 * nh, 1),
            jnp.repeat(d_skip[layer].astype(F32), SSM_HEAD_DIM).reshape(G, 1, GROUP_W),
            ssd_norm_g[layer].astype(F32).reshape(G, 1, GROUP_W),
            batch, seq)

        x2d = _out_proj(attn, ssd, w_out[layer].astype(BF16), x2d)
        x2d = _mlp(x2d, norm_mlp_g[layer], w_up[layer].astype(BF16), w_down[layer].astype(BF16))
    return x2d.reshape(batch, seq, D_MODEL)
```

```python
import functools
import math

import jax
import jax.numpy as jnp
from jax import lax
from jax.experimental import pallas as pl
from jax.experimental.pallas import tpu as pltpu

D_MODEL = 2048
ATTN_HEADS = 16
HEAD_DIM = 128
ATTN_W = ATTN_HEADS * HEAD_DIM
SSM_HEADS = 32
SSM_HEAD_DIM = 64
SSM_W = SSM_HEADS * SSM_HEAD_DIM
SSM_GROUPS = 8
HEADS_PER_GROUP = SSM_HEADS // SSM_GROUPS
GROUP_W = SSM_W // SSM_GROUPS
SSM_STATE = 128
CONV_WIDTH = 5
D_FF = 4 * D_MODEL
DILATIONS = (1, 4, 16)
HALF_WINDOW = 64
NUM_BUCKETS = 32
MAX_DISTANCE = 1024
NEG_INF = -1e30
EPS = 1e-6

LANES = 128
Q_SUB = 128
K_WIN = 256
SSD_CHUNK = 128
VMEM_LIMIT = 56 * 1024 * 1024

F32 = jnp.float32
BF16 = jnp.bfloat16


def _params(*sem):
    return pltpu.CompilerParams(dimension_semantics=sem, vmem_limit_bytes=VMEM_LIMIT)


def _sigmoid(x):
    return 1.0 / (1.0 + jnp.exp(-x))


def _softplus(x):
    return jnp.maximum(x, 0.0) + jnp.log1p(jnp.exp(-jnp.abs(x)))


def _rmsnorm_kernel(x_ref, g_ref, o_ref):
    x = x_ref[...]
    ms = jnp.mean(x * x, axis=-1, keepdims=True)
    o_ref[...] = (x * lax.rsqrt(ms + EPS) * g_ref[...]).astype(o_ref.dtype)


def _rmsnorm(x2d, g, tm=512):
    n, d = x2d.shape
    return pl.pallas_call(
        _rmsnorm_kernel,
        grid=(n // tm,),
        in_specs=[pl.BlockSpec((tm, d), lambda i: (i, 0)),
                  pl.BlockSpec((1, d), lambda i: (0, 0))],
        out_specs=pl.BlockSpec((tm, d), lambda i: (i, 0)),
        out_shape=jax.ShapeDtypeStruct((n, d), BF16),
        compiler_params=_params("parallel"),
        name="rmsnorm",
    )(x2d, g.reshape(1, d))


QK_SUB = 256


def _proj_qk_kernel(a_ref, w_ref, g_ref, o_ref, *, heads_per_tile):
    a = a_ref[...]
    per = QK_SUB // HEAD_DIM
    for c in range(heads_per_tile // per):
        acc = jnp.dot(a, w_ref[:, c * QK_SUB:(c + 1) * QK_SUB], preferred_element_type=F32)
        for hh in range(per):
            h = c * per + hh
            s = acc[:, hh * HEAD_DIM:(hh + 1) * HEAD_DIM]
            ms = jnp.mean(s * s, axis=-1, keepdims=True)
            g = g_ref[:, h * HEAD_DIM:(h + 1) * HEAD_DIM]
            o_ref[h] = (s * lax.rsqrt(ms + EPS) * g).astype(o_ref.dtype)


def _proj_split_kernel(a_ref, w_ref, o_ref, *, width):
    acc = jnp.dot(a_ref[...], w_ref[...], preferred_element_type=F32)
    for c in range(o_ref.shape[0]):
        o_ref[c] = acc[:, c * width:(c + 1) * width].astype(o_ref.dtype)


def _proj_plain_kernel(a_ref, w_ref, o_ref):
    o_ref[...] = jnp.dot(a_ref[...], w_ref[...], preferred_element_type=F32).astype(o_ref.dtype)


def _proj_qk(h, w, gains, tm=1024, tn=1024):
    n, k = h.shape
    m = 2 * ATTN_W
    hpt = tn // HEAD_DIM
    return pl.pallas_call(
        functools.partial(_proj_qk_kernel, heads_per_tile=hpt),
        grid=(n // tm, m // tn),
        in_specs=[pl.BlockSpec((tm, k), lambda i, j: (i, 0)),
                  pl.BlockSpec((k, tn), lambda i, j: (0, j)),
                  pl.BlockSpec((1, tn), lambda i, j: (0, j))],
        out_specs=pl.BlockSpec((hpt, tm, HEAD_DIM), lambda i, j: (j, i, 0)),
        out_shape=jax.ShapeDtypeStruct((m // HEAD_DIM, n, HEAD_DIM), F32),
        compiler_params=_params("parallel", "arbitrary"),
        name="proj_qk",
    )(h, w, gains)


def _proj_split(h, w, col0, m, width, name, out_dtype=BF16, tm=1024, tn=1024):
    n, k = h.shape
    cpt = tn // width
    j0 = col0 // tn
    return pl.pallas_call(
        functools.partial(_proj_split_kernel, width=width),
        grid=(n // tm, m // tn),
        in_specs=[pl.BlockSpec((tm, k), lambda i, j: (i, 0)),
                  pl.BlockSpec((k, tn), lambda i, j: (0, j0 + j))],
        out_specs=pl.BlockSpec((cpt, tm, width), lambda i, j: (j, i, 0)),
        out_shape=jax.ShapeDtypeStruct((m // width, n, width), out_dtype),
        compiler_params=_params("parallel", "arbitrary"),
        name=name,
    )(h, w)


def _proj_plain(h, w, col0, m, name, tm=1024):
    n, k = h.shape
    j0 = col0 // m
    return pl.pallas_call(
        _proj_plain_kernel,
        grid=(n // tm,),
        in_specs=[pl.BlockSpec((tm, k), lambda i: (i, 0)),
                  pl.BlockSpec((k, m), lambda i: (0, j0))],
        out_specs=pl.BlockSpec((tm, m), lambda i: (i, 0)),
        out_shape=jax.ShapeDtypeStruct((n, m), F32),
        compiler_params=_params("parallel"),
        name=name,
    )(h, w)


def _t5_bucket(rel):
    nb = NUM_BUCKETS // 2
    max_exact = nb // 2
    ret = (rel > 0).astype(jnp.int32) * nb
    n = jnp.abs(rel)
    nf = jnp.maximum(n, 1).astype(jnp.float32)
    large = max_exact + (jnp.log(nf / max_exact) / math.log(MAX_DISTANCE / max_exact)
                         * (nb - max_exact)).astype(jnp.int32)
    large = jnp.minimum(large, nb - 1)
    return ret + jnp.where(n < max_exact, n, large)


def _bias_rows(rel_bias):
    period = K_WIN + Q_SUB
    m = jnp.arange(period)
    delta = jnp.where(m < K_WIN, m, m - period)
    rows = []
    for d in DILATIONS:
        for off in (0, -HALF_WINDOW, -2 * HALF_WINDOW):
            rel = delta + off
            valid = jnp.abs(rel) <= HALF_WINDOW
            b = rel_bias[_t5_bucket(rel * d)].astype(F32)
            rows.append(jnp.where(valid[:, None], b, NEG_INF))
    return jnp.stack(rows, axis=0).transpose(2, 0, 1)


ATTN_UNROLL = 16


def _attn_kernel(q_ref, k_ref, v_ref, brow_ref, o_ref, bias_s, tmp_s, xq4, xt4,
                 q16, k1, k4, k16, v1, v4, v16, acc_s, max_s, den_s, *, seq):
    period = K_WIN + Q_SUB
    n4, n16 = seq // 4, seq // 16

    @pl.when(pl.program_id(1) == 0)
    def _():
        for idx in range(9):
            row = jnp.broadcast_to(brow_ref[idx:idx + 1, :], (Q_SUB, period))
            tile = pltpu.roll(row, 0, 1, stride=1, stride_axis=0)
            d = DILATIONS[idx // 3]
            if d == 16:
                bias_s[idx] = tile[:, :K_WIN]
                continue
            for half in range(K_WIN // LANES):
                tmp_s[half] = tile[:, half * LANES:(half + 1) * LANES]
            groups = 16 // d
            for half in range(K_WIN // LANES):
                for g in range(groups):
                    n = Q_SUB // groups
                    bias_s[idx, g * n:(g + 1) * n, half * LANES:(half + 1) * LANES] = (
                        tmp_s[half, pl.ds(g, n, stride=groups), :])

    cp = 256

    def split4(src, dst):
        for r4 in range(4):
            def body(c, carry, r4=r4):
                t0 = pl.multiple_of(c * cp, cp)
                dst[r4, pl.ds(t0, cp), :] = src[pl.ds(r4 + 4 * t0, cp, stride=4), :]
                return carry
            lax.fori_loop(0, n4 // cp, body, 0)

    def split16(src4, dst16, dst4=None):
        for r4 in range(4):
            for a in range(4):
                dst16[4 * a + r4] = src4[r4, pl.ds(a, n16, stride=4), :].astype(BF16)
            if dst4 is not None:
                def body(c, carry, r4=r4):
                    t0 = pl.multiple_of(c * cp, cp)
                    dst4[r4, pl.ds(t0, cp), :] = src4[r4, pl.ds(t0, cp), :].astype(BF16)
                    return carry
                lax.fori_loop(0, n4 // cp, body, 0)

    def cast(src, dst):
        def body(c, carry):
            t0 = pl.multiple_of(c * cp, cp)
            dst[pl.ds(t0, cp), :] = src[pl.ds(t0, cp), :].astype(BF16)
            return carry
        lax.fori_loop(0, seq // cp, body, 0)

    split4(q_ref, xq4)
    split16(xq4, q16)
    for src, d1, d4, d16 in ((k_ref, k1, k4, k16), (v_ref, v1, v4, v16)):
        cast(src, d1)
        split4(src, xt4)
        split16(xt4, d16, d4)

    ones = jnp.ones((K_WIN, LANES), BF16)

    def sub_tile(q, k_s, v_s, length, s0, pi):
        w0 = pl.multiple_of(jnp.clip(s0 - HALF_WINDOW, 0, length - K_WIN), HALF_WINDOW)
        place = jnp.where(s0 == 0, 0, jnp.where(s0 == length - Q_SUB, 2, 1))
        k = k_s[pl.ds(w0, K_WIN), :]
        v = v_s[pl.ds(w0, K_WIN), :]
        s = lax.dot_general(q, k, (((1,), (1,)), ((), ())), preferred_element_type=F32)
        s = s + bias_s[3 * pi + place]
        m = jnp.max(s, axis=-1, keepdims=True)
        p = jnp.exp(s - m).astype(BF16)
        pv = jnp.dot(p, jnp.concatenate([v, ones], axis=1), preferred_element_type=F32)
        return pv[:, :LANES], jnp.broadcast_to(m, (Q_SUB, LANES)), pv[:, LANES:]

    def store(pi, res, slab_of_group, l0, n):
        for g in range(Q_SUB // n):
            for val, dst in zip(res, (acc_s, max_s, den_s)):
                dst[pi, slab_of_group(g), pl.ds(l0, n), :] = val[g * n:(g + 1) * n]

    U = ATTN_UNROLL

    def body1(it, carry):
        res = []
        for u in range(U):
            l0 = pl.multiple_of((it * U + u) * 8, 8)
            q = jnp.concatenate([xq4[r % 4, pl.ds(4 * l0 + r // 4, 8, stride=4), :] for r in range(16)],
                                axis=0).astype(BF16)
            res.append((sub_tile(q, k1, v1, seq, pl.multiple_of(16 * l0, Q_SUB), 0), l0))
        for r3, l0 in res:
            store(0, r3, lambda g: g, l0, 8)
        return carry
    lax.fori_loop(0, seq // Q_SUB // U, body1, 0)

    def body4(it, carry):
        res = []
        for u in range(U // 4):
            l0 = pl.multiple_of((it * (U // 4) + u) * 32, 32)
            for r4 in range(4):
                q = jnp.concatenate([q16[4 * a + r4, pl.ds(l0, 32), :] for a in range(4)], axis=0)
                res.append((sub_tile(q, k4.at[r4], v4.at[r4], n4, pl.multiple_of(4 * l0, Q_SUB), 1),
                            r4, l0))
        for r3, r4, l0 in res:
            store(1, r3, lambda g, r4=r4: 4 * g + r4, l0, 32)
        return carry
    lax.fori_loop(0, n4 // Q_SUB // (U // 4), body4, 0)

    def body16(it, carry):
        g0 = (it // (n16 // Q_SUB)) * U
        l0 = pl.multiple_of((it % (n16 // Q_SUB)) * Q_SUB, Q_SUB)
        res = [sub_tile(q16[g0 + u, pl.ds(l0, Q_SUB), :], k16.at[g0 + u], v16.at[g0 + u], n16, l0, 2)
               for u in range(U)]
        for u, r3 in enumerate(res):
            store(2, r3, lambda g, u=u: g0 + u, l0, Q_SUB)
        return carry
    lax.fori_loop(0, (16 // U) * (n16 // Q_SUB), body16, 0)

    rows = 64
    for r in range(16):
        for pc in range(n16 // rows):
            idx = (r, pl.ds(pc * rows, rows))
            ma, mb, mc = max_s[(0,) + idx], max_s[(1,) + idx], max_s[(2,) + idx]
            mx = jnp.maximum(jnp.maximum(ma, mb), mc)
            ea, eb, ec = jnp.exp(ma - mx), jnp.exp(mb - mx), jnp.exp(mc - mx)
            num = ea * acc_s[(0,) + idx] + eb * acc_s[(1,) + idx] + ec * acc_s[(2,) + idx]
            den = ea * den_s[(0,) + idx] + eb * den_s[(1,) + idx] + ec * den_s[(2,) + idx]
            o_ref[pl.ds(r + 16 * pc * rows, rows, stride=16), :] = num / den


def _attention(qk, v, brow, batch, seq):
    H = ATTN_HEADS
    n4, n16 = seq // 4, seq // 16
    qk4 = qk.reshape(2 * H, batch, seq, HEAD_DIM)
    v4 = v.reshape(H, batch, seq, HEAD_DIM)
    in_specs = [pl.BlockSpec((None, None, seq, HEAD_DIM), lambda h, b: (h, b, 0, 0)),
                pl.BlockSpec((None, None, seq, HEAD_DIM), lambda h, b: (H + h, b, 0, 0)),
                pl.BlockSpec((None, None, seq, HEAD_DIM), lambda h, b: (h, b, 0, 0))]
    in_specs.append(pl.BlockSpec((None, 9, K_WIN + Q_SUB), lambda h, b: (h, 0, 0)))
    kv_slabs = [pltpu.VMEM((seq, LANES), BF16), pltpu.VMEM((4, n4, LANES), BF16),
                pltpu.VMEM((16, n16, LANES), BF16)]
    out = pl.pallas_call(
        functools.partial(_attn_kernel, seq=seq),
        grid=(H, batch),
        in_specs=in_specs,
        out_specs=pl.BlockSpec((None, None, seq, HEAD_DIM), lambda h, b: (h, b, 0, 0)),
        out_shape=jax.ShapeDtypeStruct((H, batch, seq, HEAD_DIM), F32),
        scratch_shapes=[pltpu.VMEM((9, Q_SUB, K_WIN), F32),
                        pltpu.VMEM((K_WIN // LANES, Q_SUB, LANES), F32),
                        pltpu.VMEM((4, n4, LANES), F32),
                        pltpu.VMEM((4, n4, LANES), F32),
                        pltpu.VMEM((16, n16, LANES), BF16)]
                       + kv_slabs + kv_slabs
                       + [pltpu.VMEM((3, 16, n16, LANES), F32)] * 3,
        compiler_params=_params("parallel", "arbitrary"),
        name="dilated_attention",
    )(qk4, qk4, v4, brow)
    return out.reshape(H, batch * seq, HEAD_DIM)


def _ssd_kernel(z_ref, x_ref, b_ref, c_ref, dtr_ref, cwx_ref, cwb_ref, cwc_ref,
                cbx_ref, cbb_ref, cbc_ref, dbr_ref, alr_ref, dsk_ref, ng_ref,
                o_ref, pad, xc, bc, cc, arg_s, dt_s, rows_s, cols_s, sbs, sf, sb, *, seq):
    T = SSD_CHUNK
    nc = seq // T
    rb = 128
    hi = lax.Precision.HIGHEST

    def conv(src_ref, cw_ref, cb_ref, dst_ref, width):
        pad[0:8, :] = jnp.zeros((8, GROUP_W), F32)
        pad[seq + 8:seq + 16, :] = jnp.zeros((8, GROUP_W), F32)

        def fill(i, carry):
            t0 = pl.multiple_of(i * rb, rb)
            pad[pl.ds(t0 + 8, rb), 0:width] = src_ref[pl.ds(t0, rb), :].astype(F32)
            return carry
        lax.fori_loop(0, seq // rb, fill, 0)

        def taps(i, carry):
            t0 = pl.multiple_of(i * rb, rb)
            acc = jnp.broadcast_to(cb_ref[...], (rb, width))
            blk = pad[pl.ds(t0, rb + 16), 0:width]
            for w in range(CONV_WIDTH):
                lo = 8 - CONV_WIDTH // 2 + w
                acc = acc + blk[lo:lo + rb] * cw_ref[w:w + 1, :]
            dst_ref[pl.ds(t0, rb), :] = (acc * _sigmoid(acc)).astype(dst_ref.dtype)
            return carry
        lax.fori_loop(0, seq // rb, taps, 0)

    conv(x_ref, cwx_ref, cbx_ref, xc, GROUP_W)
    conv(b_ref, cwb_ref, cbb_ref, bc, SSM_STATE)
    conv(c_ref, cwc_ref, cbc_ref, cc, SSM_STATE)

    nh = HEADS_PER_GROUP
    nd = 2 * nh
    row = lax.broadcasted_iota(jnp.int32, (T, T), 0)
    col = lax.broadcasted_iota(jnp.int32, (T, T), 1)
    triu = (row <= col).astype(F32)
    lower = col < row
    upper = col > row
    lane_head = lax.broadcasted_iota(jnp.int32, (T, GROUP_W), 1) // SSM_HEAD_DIM

    dt = _softplus(dtr_ref[...] + dbr_ref[...])
    a = dt * (-jnp.exp(alr_ref[...]))
    cum = jnp.dot(a.reshape(nc * nd, T), triu, precision=hi,
                  preferred_element_type=F32).reshape(nc, nd, T)
    last = cum[:, :, T - 1:T]
    exc = cum - a
    fwd = lax.broadcasted_iota(jnp.int32, (nc, nd, T), 1) < nh
    arg_s[...] = jnp.where(fwd, cum, exc)
    dt_s[...] = dt
    rows_s[:, 0:nd, :] = arg_s[...]
    rows_s[:, nd:2 * nd, :] = jnp.exp(jnp.where(fwd, cum, last - exc))
    rows_s[:, 2 * nd:3 * nd, :] = dt * jnp.exp(jnp.where(fwd, last - cum, exc))
    rows_s[:, 3 * nd:, :] = jnp.zeros((nc, LANES - 3 * nd, T), F32)

    def expand(cols, first):
        n = cols.shape[0]
        low = lax.broadcasted_iota(jnp.int32, (n, LANES), 1) < SSM_HEAD_DIM
        halves = []
        for j in range(GROUP_W // LANES):
            c0 = jnp.broadcast_to(cols[:, first + 2 * j:first + 2 * j + 1], (n, LANES))
            c1 = jnp.broadcast_to(cols[:, first + 2 * j + 1:first + 2 * j + 2], (n, LANES))
            halves.append(jnp.where(low, c0, c1))
        return jnp.concatenate(halves, axis=1)

    tn_dims = (((0,), (0,)), ((), ()))

    sb[...] = jnp.zeros_like(sb)

    def sweep_back(i, carry):
        ci = nc - 1 - i
        t0 = pl.multiple_of(ci * T, T)
        ct = rows_s[ci].T
        cols_s[ci] = ct
        s_prev = sb[...]
        sbs[ci] = s_prev.astype(BF16)
        xs = (xc[pl.ds(t0, T), :] * expand(ct, 2 * nd + nh)).astype(BF16)
        sb[...] = expand(ct[0:1, :], nd + nh) * s_prev + lax.dot_general(
            bc[pl.ds(t0, T), :], xs, tn_dims, preferred_element_type=F32)
        return carry
    lax.fori_loop(0, nc, sweep_back, 0)

    sf[...] = jnp.zeros_like(sf)

    def sweep_fwd(ci, carry):
        t0 = pl.multiple_of(ci * T, T)
        ct = cols_s[ci]
        arg_r = arg_s[ci]
        dt_r = dt_s[ci]
        x = xc[pl.ds(t0, T), :]
        bk = bc[pl.ds(t0, T), :]
        ck = cc[pl.ds(t0, T), :]
        ckf = ck.astype(F32)
        cb = lax.dot_general(ck, bk, (((1,), (1,)), ((), ())), preferred_element_type=F32)
        lhs = []
        for h in range(nh):
            hb = nh + h
            arg = jnp.where(lower, ct[:, h:h + 1] - arg_r[h:h + 1, :], arg_r[hb:hb + 1, :] - ct[:, hb:hb + 1])
            wgt = jnp.where(lower, dt_r[h:h + 1, :],
                            jnp.where(upper, dt_r[hb:hb + 1, :], dt_r[h:h + 1, :] + dt_r[hb:hb + 1, :]))
            lhs.append(jnp.concatenate(
                [(cb * jnp.exp(arg) * wgt).astype(BF16),
                 (ckf * ct[:, nd + h:nd + h + 1]).astype(BF16),
                 (ckf * ct[:, nd + hb:nd + hb + 1]).astype(BF16)], axis=1))
        s_prev = sf[...]
        rhs = jnp.concatenate([x.astype(BF16), s_prev.astype(BF16), sbs[ci]], axis=0)
        y4 = jnp.dot(jnp.concatenate(lhs, axis=0), rhs, preferred_element_type=F32)
        y = x * dsk_ref[...]
        for h in range(nh):
            y = jnp.where(lane_head == h, y + y4[h * T:(h + 1) * T], y)
        xs = (x * expand(ct, 2 * nd)).astype(BF16)
        sf[...] = expand(ct[T - 1:T, :], nd) * s_prev + lax.dot_general(
            bk, xs, tn_dims, preferred_element_type=F32)
        zz = z_ref[pl.ds(t0, T), :].astype(F32)
        y = y * (zz * _sigmoid(zz))
        ms = jnp.mean(y * y, axis=-1, keepdims=True)
        o_ref[pl.ds(t0, T), :] = (y * lax.rsqrt(ms + EPS) * ng_ref[...]).astype(o_ref.dtype)
        return carry
    lax.fori_loop(0, nc, sweep_fwd, 0)


def _ssd(zx, bcm, dt_row, cwx, cwb, cwc, cbx, cbb, cbc, db_row, al_row, dskip, ng, batch, seq):
    G = SSM_GROUPS
    nc = seq // SSD_CHUNK
    zx4 = zx.reshape(2 * G, batch, seq, GROUP_W)
    bc4 = bcm.reshape(2 * G, batch, seq, SSM_STATE)
    nd = 2 * HEADS_PER_GROUP

    def per_group(shape):
        return pl.BlockSpec((None,) + shape, lambda b, g: (g,) + (0,) * len(shape))

    in_specs = [
        pl.BlockSpec((None, None, seq, GROUP_W), lambda b, g: (g, b, 0, 0)),
        pl.BlockSpec((None, None, seq, GROUP_W), lambda b, g: (G + g, b, 0, 0)),
        pl.BlockSpec((None, None, seq, SSM_STATE), lambda b, g: (g, b, 0, 0)),
        pl.BlockSpec((None, None, seq, SSM_STATE), lambda b, g: (G + g, b, 0, 0)),
        pl.BlockSpec((None, None, nc, nd, SSD_CHUNK), lambda b, g: (b, g, 0, 0, 0)),
        per_group((CONV_WIDTH, GROUP_W)), per_group((CONV_WIDTH, SSM_STATE)),
        per_group((CONV_WIDTH, SSM_STATE)),
        per_group((1, GROUP_W)), per_group((1, SSM_STATE)), per_group((1, SSM_STATE)),
        per_group((nd, 1)), per_group((nd, 1)),
        per_group((1, GROUP_W)), per_group((1, GROUP_W)),
    ]
    out = pl.pallas_call(
        functools.partial(_ssd_kernel, seq=seq),
        grid=(batch, G),
        in_specs=in_specs,
        out_specs=pl.BlockSpec((None, None, seq, GROUP_W), lambda b, g: (g, b, 0, 0)),
        out_shape=jax.ShapeDtypeStruct((G, batch, seq, GROUP_W), BF16),
        scratch_shapes=[pltpu.VMEM((seq + 16, GROUP_W), F32),
                        pltpu.VMEM((seq, GROUP_W), F32),
                        pltpu.VMEM((seq, SSM_STATE), BF16),
                        pltpu.VMEM((seq, SSM_STATE), BF16),
                        pltpu.VMEM((nc, nd, SSD_CHUNK), F32),
                        pltpu.VMEM((nc, nd, SSD_CHUNK), F32),
                        pltpu.VMEM((nc, LANES, SSD_CHUNK), F32),
                        pltpu.VMEM((nc, SSD_CHUNK, LANES), F32),
                        pltpu.VMEM((nc, SSM_STATE, GROUP_W), BF16),
                        pltpu.VMEM((SSM_STATE, GROUP_W), F32),
                        pltpu.VMEM((SSM_STATE, GROUP_W), F32)],
        compiler_params=_params("parallel", "parallel"),
        name="ssd",
    )(zx4, zx4, bc4, bc4, dt_row, cwx, cwb, cwc, cbx, cbb, cbc, db_row, al_row, dskip, ng)
    return out.reshape(G, batch * seq, GROUP_W)


def _out_proj_kernel(attn_ref, ssd_ref, w_ref, x_ref, o_ref, lhs):
    for h in range(ATTN_HEADS):
        lhs[:, h * HEAD_DIM:(h + 1) * HEAD_DIM] = attn_ref[h].astype(BF16)
    for g in range(SSM_GROUPS):
        lhs[:, ATTN_W + g * GROUP_W:ATTN_W + (g + 1) * GROUP_W] = ssd_ref[g]
    o_ref[...] = x_ref[...] + jnp.dot(lhs[...], w_ref[...], preferred_element_type=F32)


def _out_proj(attn, ssd, w, x2d, tm=256):
    n, d = x2d.shape
    kk = w.shape[0]
    return pl.pallas_call(
        _out_proj_kernel,
        grid=(n // tm,),
        in_specs=[pl.BlockSpec((ATTN_HEADS, tm, HEAD_DIM), lambda i: (0, i, 0)),
                  pl.BlockSpec((SSM_GROUPS, tm, GROUP_W), lambda i: (0, i, 0)),
                  pl.BlockSpec((kk, d), lambda i: (0, 0), pipeline_mode=pl.Buffered(1)),
                  pl.BlockSpec((tm, d), lambda i: (i, 0))],
        out_specs=pl.BlockSpec((tm, d), lambda i: (i, 0)),
        out_shape=jax.ShapeDtypeStruct((n, d), F32),
        scratch_shapes=[pltpu.VMEM((tm, kk), BF16)],
        compiler_params=_params("parallel"),
        name="out_proj",
    )(attn, ssd, w, x2d)


def _mlp_kernel(x_ref, g_ref, wu_ref, wd_ref, o_ref, hm):
    @pl.when(pl.program_id(1) == 0)
    def _():
        x = x_ref[...]
        ms = jnp.mean(x * x, axis=-1, keepdims=True)
        hm[...] = (x * lax.rsqrt(ms + EPS) * g_ref[...]).astype(hm.dtype)
        o_ref[...] = x
    u = jnp.maximum(jnp.dot(hm[...], wu_ref[...], preferred_element_type=F32), 0.0)
    o_ref[...] += jnp.dot((u * u).astype(BF16), wd_ref[...], preferred_element_type=F32)


def _mlp(x2d, g, wu, wd, tm=512, tf=1024):
    n, d = x2d.shape
    f = wu.shape[1]
    return pl.pallas_call(
        _mlp_kernel,
        grid=(n // tm, f // tf),
        in_specs=[pl.BlockSpec((tm, d), lambda i, j: (i, 0)),
                  pl.BlockSpec((1, d), lambda i, j: (0, 0)),
                  pl.BlockSpec((d, tf), lambda i, j: (0, j)),
                  pl.BlockSpec((tf, d), lambda i, j: (j, 0))],
        out_specs=pl.BlockSpec((tm, d), lambda i, j: (i, 0)),
        out_shape=jax.ShapeDtypeStruct((n, d), F32),
        scratch_shapes=[pltpu.VMEM((tm, d), BF16)],
        compiler_params=_params("parallel", "arbitrary"),
        name="mlp",
    )(x2d, g.reshape(1, d), wu, wd)


def kernel(x, norm_mix_g, w_in, q_norm_g, k_norm_g, rel_bias, conv_w, conv_b, dt_bias, a_log,
           d_skip, ssd_norm_g, w_out, norm_mlp_g, w_up, w_down):
    batch, seq, _ = x.shape
    n = batch * seq
    G, nh = SSM_GROUPS, HEADS_PER_GROUP
    nc = seq // SSD_CHUNK
    o_z = 3 * ATTN_W
    o_bc = o_z + 2 * SSM_W
    o_dt = o_bc + 2 * G * SSM_STATE
    x2d = x.reshape(n, D_MODEL)
    brow = _bias_rows(rel_bias)

    for layer in range(w_in.shape[0]):
        wi = w_in[layer].astype(BF16)
        h = _rmsnorm(x2d, norm_mix_g[layer])

        scale = 1.0 / math.sqrt(HEAD_DIM)
        gains = jnp.concatenate([jnp.tile(q_norm_g[layer].astype(F32) * scale, ATTN_HEADS),
                                 jnp.tile(k_norm_g[layer].astype(F32), ATTN_HEADS)]).reshape(1, 2 * ATTN_W)
        qk = _proj_qk(h, wi, gains)
        v = _proj_split(h, wi, 2 * ATTN_W, ATTN_W, HEAD_DIM, "proj_v", out_dtype=F32)
        zx = _proj_split(h, wi, o_z, o_bc - o_z, GROUP_W, "proj_zx")
        bcm = _proj_split(h, wi, o_bc, o_dt - o_bc, SSM_STATE, "proj_bc")
        dt_raw = _proj_plain(h, w_in[layer][:, o_dt:].astype(BF16), 0, 2 * SSM_HEADS, "proj_dt")

        attn = _attention(qk, v, brow, batch, seq)

        dt_row = (dt_raw.reshape(batch, nc, SSD_CHUNK, 2, G, nh)
                  .transpose(0, 4, 1, 3, 5, 2).reshape(batch, G, nc, 2 * nh, SSD_CHUNK))
        per_dir = lambda t: t.astype(F32).reshape(2, G, nh).transpose(1, 0, 2).reshape(G, 2 * nh)
        db, al = per_dir(dt_bias[layer]), per_dir(a_log[layer])
        cw, cbias = conv_w[layer].astype(F32), conv_b[layer].astype(F32)
        gn = G * SSM_STATE
        grp = lambda t, width: t.reshape(t.shape[0], G, width).transpose(1, 0, 2)
        ssd = _ssd(
            zx, bcm, dt_row,
            grp(cw[:, :SSM_W], GROUP_W), grp(cw[:, SSM_W:SSM_W + gn], SSM_STATE),
            grp(cw[:, SSM_W + gn:], SSM_STATE),
            grp(cbias[None, :SSM_W], GROUP_W), grp(cbias[None, SSM_W:SSM_W + gn], SSM_STATE),
            grp(cbias[None, SSM_W + gn:], SSM_STATE),
            db.reshape(G, 2 * nh, 1), al.reshape(G, 2 * nh, 1),
            jnp.repeat(d_skip[layer].astype(F32), SSM_HEAD_DIM).reshape(G, 1, GROUP_W),
            ssd_norm_g[layer].astype(F32).reshape(G, 1, GROUP_W),
            batch, seq)

        x2d = _out_proj(attn, ssd, w_out[layer].astype(BF16), x2d)
        x2d = _mlp(x2d, norm_mlp_g[layer], w_up[layer].astype(BF16), w_down[layer].astype(BF16))
    return x2d.reshape(batch, seq, D_MODEL)
```

```python
import functools
import math

import jax
import jax.numpy as jnp
from jax import lax
from jax.experimental import pallas as pl
from jax.experimental.pallas import tpu as pltpu

D_MODEL = 2048
ATTN_HEADS = 16
HEAD_DIM = 128
ATTN_W = ATTN_HEADS * HEAD_DIM
SSM_HEADS = 32
SSM_HEAD_DIM = 64
SSM_W = SSM_HEADS * SSM_HEAD_DIM
SSM_GROUPS = 8
HEADS_PER_GROUP = SSM_HEADS // SSM_GROUPS
GROUP_W = SSM_W // SSM_GROUPS
SSM_STATE = 128
CONV_WIDTH = 5
D_FF = 4 * D_MODEL
DILATIONS = (1, 4, 16)
HALF_WINDOW = 64
NUM_BUCKETS = 32
MAX_DISTANCE = 1024
NEG_INF = -1e30
EPS = 1e-6

LANES = 128
Q_SUB = 128
K_WIN = 256
SSD_CHUNK = 128
VMEM_LIMIT = 56 * 1024 * 1024

F32 = jnp.float32
BF16 = jnp.bfloat16


def _params(*sem):
    return pltpu.CompilerParams(dimension_semantics=sem, vmem_limit_bytes=VMEM_LIMIT)


def _sigmoid(x):
    return 1.0 / (1.0 + jnp.exp(-x))


def _softplus(x):
    return jnp.maximum(x, 0.0) + jnp.log1p(jnp.exp(-jnp.abs(x)))


def _rmsnorm_kernel(x_ref, g_ref, o_ref):
    x = x_ref[...]
    ms = jnp.mean(x * x, axis=-1, keepdims=True)
    o_ref[...] = (x * lax.rsqrt(ms + EPS) * g_ref[...]).astype(o_ref.dtype)


def _rmsnorm(x2d, g, tm=512):
    n, d = x2d.shape
    return pl.pallas_call(
        _rmsnorm_kernel,
        grid=(n // tm,),
        in_specs=[pl.BlockSpec((tm, d), lambda i: (i, 0)),
                  pl.BlockSpec((1, d), lambda i: (0, 0))],
        out_specs=pl.BlockSpec((tm, d), lambda i: (i, 0)),
        out_shape=jax.ShapeDtypeStruct((n, d), BF16),
        compiler_params=_params("parallel"),
        name="rmsnorm",
    )(x2d, g.reshape(1, d))


QK_SUB = 256


def _proj_qk_kernel(a_ref, w_ref, g_ref, o_ref, *, heads_per_tile):
    a = a_ref[...]
    per = QK_SUB // HEAD_DIM
    for c in range(heads_per_tile // per):
        acc = jnp.dot(a, w_ref[:, c * QK_SUB:(c + 1) * QK_SUB], preferred_element_type=F32)
        for hh in range(per):
            h = c * per + hh
            s = acc[:, hh * HEAD_DIM:(hh + 1) * HEAD_DIM]
            ms = jnp.mean(s * s, axis=-1, keepdims=True)
            g = g_ref[:, h * HEAD_DIM:(h + 1) * HEAD_DIM]
            o_ref[h] = (s * lax.rsqrt(ms + EPS) * g).astype(o_ref.dtype)


def _proj_split_kernel(a_ref, w_ref, o_ref, *, width):
    acc = jnp.dot(a_ref[...], w_ref[...], preferred_element_type=F32)
    for c in range(o_ref.shape[0]):
        o_ref[c] = acc[:, c * width:(c + 1) * width].astype(o_ref.dtype)


def _proj_plain_kernel(a_ref, w_ref, o_ref):
    o_ref[...] = jnp.dot(a_ref[...], w_ref[...], preferred_element_type=F32).astype(o_ref.dtype)


def _proj_qk(h, w, gains, tm=1024, tn=1024):
    n, k = h.shape
    m = 2 * ATTN_W
    hpt = tn // HEAD_DIM
    return pl.pallas_call(
        functools.partial(_proj_qk_kernel, heads_per_tile=hpt),
        grid=(n // tm, m // tn),
        in_specs=[pl.BlockSpec((tm, k), lambda i, j: (i, 0)),
                  pl.BlockSpec((k, tn), lambda i, j: (0, j)),
                  pl.BlockSpec((1, tn), lambda i, j: (0, j))],
        out_specs=pl.BlockSpec((hpt, tm, HEAD_DIM), lambda i, j: (j, i, 0)),
        out_shape=jax.ShapeDtypeStruct((m // HEAD_DIM, n, HEAD_DIM), F32),
        compiler_params=_params("parallel", "arbitrary"),
        name="proj_qk",
    )(h, w, gains)


def _proj_split(h, w, col0, m, width, name, out_dtype=BF16, tm=1024, tn=1024):
    n, k = h.shape
    cpt = tn // width
    j0 = col0 // tn
    return pl.pallas_call(
        functools.partial(_proj_split_kernel, width=width),
        grid=(n // tm, m // tn),
        in_specs=[pl.BlockSpec((tm, k), lambda i, j: (i, 0)),
                  pl.BlockSpec((k, tn), lambda i, j: (0, j0 + j))],
        out_specs=pl.BlockSpec((cpt, tm, width), lambda i, j: (j, i, 0)),
        out_shape=jax.ShapeDtypeStruct((m // width, n, width), out_dtype),
        compiler_params=_params("parallel", "arbitrary"),
        name=name,
    )(h, w)


def _proj_plain(h, w, col0, m, name, tm=1024):
    n, k = h.shape
    j0 = col0 // m
    return pl.pallas_call(
        _proj_plain_kernel,
        grid=(n // tm,),
        in_specs=[pl.BlockSpec((tm, k), lambda i: (i, 0)),
                  pl.BlockSpec((k, m), lambda i: (0, j0))],
        out_specs=pl.BlockSpec((tm, m), lambda i: (i, 0)),
        out_shape=jax.ShapeDtypeStruct((n, m), F32),
        compiler_params=_params("parallel"),
        name=name,
    )(h, w)


def _t5_bucket(rel):
    nb = NUM_BUCKETS // 2
    max_exact = nb // 2
    ret = (rel > 0).astype(jnp.int32) * nb
    n = jnp.abs(rel)
    nf = jnp.maximum(n, 1).astype(jnp.float32)
    large = max_exact + (jnp.log(nf / max_exact) / math.log(MAX_DISTANCE / max_exact)
                         * (nb - max_exact)).astype(jnp.int32)
    large = jnp.minimum(large, nb - 1)
    return ret + jnp.where(n < max_exact, n, large)


def _bias_rows(rel_bias):
    period = K_WIN + Q_SUB
    m = jnp.arange(period)
    delta = jnp.where(m < K_WIN, m, m - period)
    rows = []
    for d in DILATIONS:
        for off in (0, -HALF_WINDOW, -2 * HALF_WINDOW):
            rel = delta + off
            valid = jnp.abs(rel) <= HALF_WINDOW
            b = rel_bias[_t5_bucket(rel * d)].astype(F32)
            rows.append(jnp.where(valid[:, None], b, NEG_INF))
    return jnp.stack(rows, axis=0).transpose(2, 0, 1)


ATTN_UNROLL = 16


def _attn_kernel(q_ref, k_ref, v_ref, brow_ref, o_ref, bias_s, tmp_s, xq4, xt4,
                 q16, k1, k4, k16, v1, v4, v16, acc_s, max_s, den_s, *, seq):
    period = K_WIN + Q_SUB
    n4, n16 = seq // 4, seq // 16

    @pl.when(pl.program_id(1) == 0)
    def _():
        for idx in range(9):
            row = jnp.broadcast_to(brow_ref[idx:idx + 1, :], (Q_SUB, period))
            tile = pltpu.roll(row, 0, 1, stride=1, stride_axis=0)
            d = DILATIONS[idx // 3]
            if d == 16:
                bias_s[idx] = tile[:, :K_WIN]
                continue
            for half in range(K_WIN // LANES):
                tmp_s[half] = tile[:, half * LANES:(half + 1) * LANES]
            groups = 16 // d
            for half in range(K_WIN // LANES):
                for g in range(groups):
                    n = Q_SUB // groups
                    bias_s[idx, g * n:(g + 1) * n, half * LANES:(half + 1) * LANES] = (
                        tmp_s[half, pl.ds(g, n, stride=groups), :])

    cp = 256

    def split4(src, dst):
        for r4 in range(4):
            def body(c, carry, r4=r4):
                t0 = pl.multiple_of(c * cp, cp)
                dst[r4, pl.ds(t0, cp), :] = src[pl.ds(r4 + 4 * t0, cp, stride=4), :]
                return carry
            lax.fori_loop(0, n4 // cp, body, 0)

    def split16(src4, dst16, dst4=None):
        for r4 in range(4):
            for a in range(4):
                dst16[4 * a + r4] = src4[r4, pl.ds(a, n16, stride=4), :].astype(BF16)
            if dst4 is not None:
                def body(c, carry, r4=r4):
                    t0 = pl.multiple_of(c * cp, cp)
                    dst4[r4, pl.ds(t0, cp), :] = src4[r4, pl.ds(t0, cp), :].astype(BF16)
                    return carry
                lax.fori_loop(0, n4 // cp, body, 0)

    def cast(src, dst):
        def body(c, carry):
            t0 = pl.multiple_of(c * cp, cp)
            dst[pl.ds(t0, cp), :] = src[pl.ds(t0, cp), :].astype(BF16)
            return carry
        lax.fori_loop(0, seq // cp, body, 0)

    split4(q_ref, xq4)
    split16(xq4, q16)
    for src, d1, d4, d16 in ((k_ref, k1, k4, k16), (v_ref, v1, v4, v16)):
        cast(src, d1)
        split4(src, xt4)
        split16(xt4, d16, d4)

    ones = jnp.ones((K_WIN, LANES), BF16)

    def sub_tile(q, k_s, v_s, length, s0, pi):
        w0 = pl.multiple_of(jnp.clip(s0 - HALF_WINDOW, 0, length - K_WIN), HALF_WINDOW)
        place = jnp.where(s0 == 0, 0, jnp.where(s0 == length - Q_SUB, 2, 1))
        k = k_s[pl.ds(w0, K_WIN), :]
        v = v_s[pl.ds(w0, K_WIN), :]
        s = lax.dot_general(q, k, (((1,), (1,)), ((), ())), preferred_element_type=F32)
        s = s + bias_s[3 * pi + place]
        m = jnp.max(s, axis=-1, keepdims=True)
        p = jnp.exp(s - m).astype(BF16)
        pv = jnp.dot(p, jnp.concatenate([v, ones], axis=1), preferred_element_type=F32)
        return pv[:, :LANES], jnp.broadcast_to(m, (Q_SUB, LANES)), pv[:, LANES:]

    def store(pi, res, slab_of_group, l0, n):
        for g in range(Q_SUB // n):
            for val, dst in zip(res, (acc_s, max_s, den_s)):
                dst[pi, slab_of_group(g), pl.ds(l0, n), :] = val[g * n:(g + 1) * n]

    U = ATTN_UNROLL

    def body1(it, carry):
        res = []
        for u in range(U):
            l0 = pl.multiple_of((it * U + u) * 8, 8)
            q = jnp.concatenate([xq4[r % 4, pl.ds(4 * l0 + r // 4, 8, stride=4), :] for r in range(16)],
                                axis=0).astype(BF16)
            res.append((sub_tile(q, k1, v1, seq, pl.multiple_of(16 * l0, Q_SUB), 0), l0))
        for r3, l0 in res:
            store(0, r3, lambda g: g, l0, 8)
        return carry
    lax.fori_loop(0, seq // Q_SUB // U, body1, 0)

    def body4(it, carry):
        res = []
        for u in range(U // 4):
            l0 = pl.multiple_of((it * (U // 4) + u) * 32, 32)
            for r4 in range(4):
                q = jnp.concatenate([q16[4 * a + r4, pl.ds(l0, 32), :] for a in range(4)], axis=0)
                res.append((sub_tile(q, k4.at[r4], v4.at[r4], n4, pl.multiple_of(4 * l0, Q_SUB), 1),
                            r4, l0))
        for r3, r4, l0 in res:
            store(1, r3, lambda g, r4=r4: 4 * g + r4, l0, 32)
        return carry
    lax.fori_loop(0, n4 // Q_SUB // (U // 4), body4, 0)

    def body16(it, carry):
        g0 = (it // (n16 // Q_SUB)) * U
        l0 = pl.multiple_of((it % (n16 // Q_SUB)) * Q_SUB, Q_SUB)
        res = [sub_tile(q16[g0 + u, pl.ds(l0, Q_SUB), :], k16.at[g0 + u], v16.at[g0 + u], n16, l0, 2)
               for u in range(U)]
        for u, r3 in enumerate(res):
            store(2, r3, lambda g, u=u: g0 + u, l0, Q_SUB)
        return carry
    lax.fori_loop(0, (16 // U) * (n16 // Q_SUB), body16, 0)

    rows = 64
    for r in range(16):
        for pc in range(n16 // rows):
            idx = (r, pl.ds(pc * rows, rows))
            ma, mb, mc = max_s[(0,) + idx], max_s[(1,) + idx], max_s[(2,) + idx]
            mx = jnp.maximum(jnp.maximum(ma, mb), mc)
            ea, eb, ec = jnp.exp(ma - mx), jnp.exp(mb - mx), jnp.exp(mc - mx)
            num = ea * acc_s[(0,) + idx] + eb * acc_s[(1,) + idx] + ec * acc_s[(2,) + idx]
            den = ea * den_s[(0,) + idx] + eb * den_s[(1,) + idx] + ec * den_s[(2,) + idx]
            o_ref[pl.ds(r + 16 * pc * rows, rows, stride=16), :] = num / den


def _attention(qk, v, brow, batch, seq):
    H = ATTN_HEADS
    n4, n16 = seq // 4, seq // 16
    qk4 = qk.reshape(2 * H, batch, seq, HEAD_DIM)
    v4 = v.reshape(H, batch, seq, HEAD_DIM)
    in_specs = [pl.BlockSpec((None, None, seq, HEAD_DIM), lambda h, b: (h, b, 0, 0)),
                pl.BlockSpec((None, None, seq, HEAD_DIM), lambda h, b: (H + h, b, 0, 0)),
                pl.BlockSpec((None, None, seq, HEAD_DIM), lambda h, b: (h, b, 0, 0))]
    in_specs.append(pl.BlockSpec((None, 9, K_WIN + Q_SUB), lambda h, b: (h, 0, 0)))
    kv_slabs = [pltpu.VMEM((seq, LANES), BF16), pltpu.VMEM((4, n4, LANES), BF16),
                pltpu.VMEM((16, n16, LANES), BF16)]
    out = pl.pallas_call(
        functools.partial(_attn_kernel, seq=seq),
        grid=(H, batch),
        in_specs=in_specs,
        out_specs=pl.BlockSpec((None, None, seq, HEAD_DIM), lambda h, b: (h, b, 0, 0)),
        out_shape=jax.ShapeDtypeStruct((H, batch, seq, HEAD_DIM), F32),
        scratch_shapes=[pltpu.VMEM((9, Q_SUB, K_WIN), F32),
                        pltpu.VMEM((K_WIN // LANES, Q_SUB, LANES), F32),
                        pltpu.VMEM((4, n4, LANES), F32),
                        pltpu.VMEM((4, n4, LANES), F32),
                        pltpu.VMEM((16, n16, LANES), BF16)]
                       + kv_slabs + kv_slabs
                       + [pltpu.VMEM((3, 16, n16, LANES), F32)] * 3,
        compiler_params=_params("parallel", "arbitrary"),
        name="dilated_attention",
    )(qk4, qk4, v4, brow)
    return out.reshape(H, batch * seq, HEAD_DIM)


def _ssd_kernel(z_ref, x_ref, b_ref, c_ref, dtr_ref, cwx_ref, cwb_ref, cwc_ref,
                cbx_ref, cbb_ref, cbc_ref, dbr_ref, alr_ref, dsk_ref, ng_ref,
                o_ref, pad_x, pad_b, pad_c, xc, bc, cc, arg_s, dt_s, rows_s, cols_s, sbs, sf, sb, *, seq):
    T = SSD_CHUNK
    nc = seq // T
    hi = lax.Precision.HIGHEST
    convs = ((x_ref, pad_x, cwx_ref, cbx_ref, xc), (b_ref, pad_b, cwb_ref, cbb_ref, bc),
             (c_ref, pad_c, cwc_ref, cbc_ref, cc))

    for src_ref, pad, _, _, _ in convs:
        pad[0:8, :] = jnp.zeros((8, pad.shape[1]), F32)
        pad[seq + 8:seq + 16, :] = jnp.zeros((8, pad.shape[1]), F32)

        def fill(i, carry, src_ref=src_ref, pad=pad):
            t0 = pl.multiple_of(i * T, T)
            pad[pl.ds(t0 + 8, T), :] = src_ref[pl.ds(t0, T), :].astype(F32)
            return carry
        lax.fori_loop(0, nc, fill, 0)

    def conv_chunk(ci):
        t0 = pl.multiple_of(ci * T, T)
        for _, pad, cw_ref, cb_ref, dst_ref in convs:
            acc = jnp.broadcast_to(cb_ref[...], (T, pad.shape[1]))
            blk = pad[pl.ds(t0, T + 16), :]
            for w in range(CONV_WIDTH):
                lo = 8 - CONV_WIDTH // 2 + w
                acc = acc + blk[lo:lo + T] * cw_ref[w:w + 1, :]
            dst_ref[pl.ds(t0, T), :] = (acc * _sigmoid(acc)).astype(dst_ref.dtype)

    nh = HEADS_PER_GROUP
    nd = 2 * nh
    row = lax.broadcasted_iota(jnp.int32, (T, T), 0)
    col = lax.broadcasted_iota(jnp.int32, (T, T), 1)
    triu = (row <= col).astype(F32)
    lower = col < row
    upper = col > row
    lane_head = lax.broadcasted_iota(jnp.int32, (T, GROUP_W), 1) // SSM_HEAD_DIM

    dt = _softplus(dtr_ref[...] + dbr_ref[...])
    a = dt * (-jnp.exp(alr_ref[...]))
    cum = jnp.dot(a.reshape(nc * nd, T), triu, precision=hi,
                  preferred_element_type=F32).reshape(nc, nd, T)
    last = cum[:, :, T - 1:T]
    exc = cum - a
    fwd = lax.broadcasted_iota(jnp.int32, (nc, nd, T), 1) < nh
    arg_s[...] = jnp.where(fwd, cum, exc)
    dt_s[...] = dt
    rows_s[:, 0:nd, :] = arg_s[...]
    rows_s[:, nd:2 * nd, :] = jnp.exp(jnp.where(fwd, cum, last - exc))
    rows_s[:, 2 * nd:3 * nd, :] = dt * jnp.exp(jnp.where(fwd, last - cum, exc))
    rows_s[:, 3 * nd:, :] = jnp.zeros((nc, LANES - 3 * nd, T), F32)

    def expand(cols, first):
        n = cols.shape[0]
        low = lax.broadcasted_iota(jnp.int32, (n, LANES), 1) < SSM_HEAD_DIM
        halves = []
        for j in range(GROUP_W // LANES):
            c0 = jnp.broadcast_to(cols[:, first + 2 * j:first + 2 * j + 1], (n, LANES))
            c1 = jnp.broadcast_to(cols[:, first + 2 * j + 1:first + 2 * j + 2], (n, LANES))
            halves.append(jnp.where(low, c0, c1))
        return jnp.concatenate(halves, axis=1)

    tn_dims = (((0,), (0,)), ((), ()))

    n_dec = nd * LANES
    cid = lax.broadcasted_iota(jnp.int32, (LANES, n_dec + 2 * GROUP_W), 0)
    lid = lax.broadcasted_iota(jnp.int32, (LANES, n_dec + 2 * GROUP_W), 1)
    want = jnp.where(lid < n_dec, nd + lid // LANES, 2 * nd + (lid - n_dec) // SSM_HEAD_DIM)
    spread = jnp.where(cid == want, 1.0, 0.0).astype(BF16)
    spread_f = spread[:, :n_dec + GROUP_W]
    spread_b = spread[:, n_dec + GROUP_W:]

    sb[...] = jnp.zeros_like(sb)

    def state_back(ci):
        t0 = pl.multiple_of(ci * T, T)
        ct = rows_s[ci].T
        cols_s[ci] = ct
        s_prev = sb[...]
        sbs[ci] = s_prev.astype(BF16)
        wx = jnp.dot(ct.astype(BF16), spread_b, preferred_element_type=F32)
        xs = (xc[pl.ds(t0, T), :] * wx).astype(BF16)
        sb[...] = expand(ct[0:1, :], nd + nh) * s_prev + lax.dot_general(
            bc[pl.ds(t0, T), :], xs, tn_dims, preferred_element_type=F32)

    conv_chunk(nc - 1)
    conv_chunk(nc - 2)

    def sweep_back(i, carry):
        k = nc - 1 - 2 * i
        state_back(k + 2)
        state_back(k + 1)
        conv_chunk(k)
        conv_chunk(k - 1)
        return carry
    lax.fori_loop(1, nc // 2, sweep_back, 0)
    state_back(1)
    state_back(0)

    sf[...] = jnp.zeros_like(sf)
    low_half = lax.broadcasted_iota(jnp.int32, (T, LANES), 1) < SSM_HEAD_DIM

    y_s = pad_x

    def finish(ci):
        t0 = pl.multiple_of(ci * T, T)
        zz = z_ref[pl.ds(t0, T), :].astype(F32)
        y = y_s[pl.ds(t0, T), :] * (zz * _sigmoid(zz))
        ms = jnp.mean(y * y, axis=-1, keepdims=True)
        o_ref[pl.ds(t0, T), :] = (y * lax.rsqrt(ms + EPS) * ng_ref[...]).astype(o_ref.dtype)

    def chunk_fwd(ci):
        t0 = pl.multiple_of(ci * T, T)
        ct = cols_s[ci]
        arg_r = arg_s[ci]
        dt_r = dt_s[ci]
        x = xc[pl.ds(t0, T), :]
        bk = bc[pl.ds(t0, T), :]
        ck = cc[pl.ds(t0, T), :]
        ckf = ck.astype(F32)
        cb = lax.dot_general(ck, bk, (((1,), (1,)), ((), ())), preferred_element_type=F32)
        spreadv = jnp.dot(ct.astype(BF16), spread_f, preferred_element_type=F32)
        lhs = []
        for h in range(nh):
            hb = nh + h
            arg = jnp.where(lower, ct[:, h:h + 1] - arg_r[h:h + 1, :], arg_r[hb:hb + 1, :] - ct[:, hb:hb + 1])
            wgt = jnp.where(lower, dt_r[h:h + 1, :],
                            jnp.where(upper, dt_r[hb:hb + 1, :], dt_r[h:h + 1, :] + dt_r[hb:hb + 1, :]))
            lhs.append(jnp.concatenate(
                [(cb * jnp.exp(arg) * wgt).astype(BF16),
                 (ckf * spreadv[:, h * LANES:(h + 1) * LANES]).astype(BF16),
                 (ckf * spreadv[:, hb * LANES:(hb + 1) * LANES]).astype(BF16)], axis=1))
        s_prev = sf[...]
        rhs = jnp.concatenate([x.astype(BF16), s_prev.astype(BF16), sbs[ci]], axis=0)
        y4 = jnp.dot(jnp.concatenate(lhs, axis=0), rhs, preferred_element_type=F32)
        y = x * dsk_ref[...] + jnp.concatenate(
            [jnp.where(low_half, y4[2 * j * T:(2 * j + 1) * T, j * LANES:(j + 1) * LANES],
                       y4[(2 * j + 1) * T:(2 * j + 2) * T, j * LANES:(j + 1) * LANES])
             for j in range(GROUP_W // LANES)], axis=1)
        xs = (x * spreadv[:, n_dec:]).astype(BF16)
        sf[...] = expand(ct[T - 1:T, :], nd) * s_prev + lax.dot_general(
            bk, xs, tn_dims, preferred_element_type=F32)
        y_s[pl.ds(t0, T), :] = y

    chunk_fwd(0)

    def sweep_fwd(i, carry):
        ci = 2 * i + 1
        finish(ci - 1)
        chunk_fwd(ci)
        finish(ci)
        chunk_fwd(ci + 1)
        return carry
    lax.fori_loop(0, (nc - 2) // 2, sweep_fwd, 0)
    finish(nc - 2)
    chunk_fwd(nc - 1)
    finish(nc - 1)


def _ssd(zx, bcm, dt_row, cwx, cwb, cwc, cbx, cbb, cbc, db_row, al_row, dskip, ng, batch, seq):
    G = SSM_GROUPS
    nc = seq // SSD_CHUNK
    zx4 = zx.reshape(2 * G, batch, seq, GROUP_W)
    bc4 = bcm.reshape(2 * G, batch, seq, SSM_STATE)
    nd = 2 * HEADS_PER_GROUP

    def per_group(shape):
        return pl.BlockSpec((None,) + shape, lambda b, g: (g,) + (0,) * len(shape))

    in_specs = [
        pl.BlockSpec((None, None, seq, GROUP_W), lambda b, g: (g, b, 0, 0)),
        pl.BlockSpec((None, None, seq, GROUP_W), lambda b, g: (G + g, b, 0, 0)),
        pl.BlockSpec((None, None, seq, SSM_STATE), lambda b, g: (g, b, 0, 0)),
        pl.BlockSpec((None, None, seq, SSM_STATE), lambda b, g: (G + g, b, 0, 0)),
        pl.BlockSpec((None, None, nc, nd, SSD_CHUNK), lambda b, g: (b, g, 0, 0, 0)),
        per_group((CONV_WIDTH, GROUP_W)), per_group((CONV_WIDTH, SSM_STATE)),
        per_group((CONV_WIDTH, SSM_STATE)),
        per_group((1, GROUP_W)), per_group((1, SSM_STATE)), per_group((1, SSM_STATE)),
        per_group((nd, 1)), per_group((nd, 1)),
        per_group((1, GROUP_W)), per_group((1, GROUP_W)),
    ]
    out = pl.pallas_call(
        functools.partial(_ssd_kernel, seq=seq),
        grid=(batch, G),
        in_specs=in_specs,
        out_specs=pl.BlockSpec((None, None, seq, GROUP_W), lambda b, g: (g, b, 0, 0)),
        out_shape=jax.ShapeDtypeStruct((G, batch, seq, GROUP_W), BF16),
        scratch_shapes=[pltpu.VMEM((seq + 16, GROUP_W), F32),
                        pltpu.VMEM((seq + 16, SSM_STATE), F32),
                        pltpu.VMEM((seq + 16, SSM_STATE), F32),
                        pltpu.VMEM((seq, GROUP_W), F32),
                        pltpu.VMEM((seq, SSM_STATE), BF16),
                        pltpu.VMEM((seq, SSM_STATE), BF16),
                        pltpu.VMEM((nc, nd, SSD_CHUNK), F32),
                        pltpu.VMEM((nc, nd, SSD_CHUNK), F32),
                        pltpu.VMEM((nc, LANES, SSD_CHUNK), F32),
                        pltpu.VMEM((nc, SSD_CHUNK, LANES), F32),
                        pltpu.VMEM((nc, SSM_STATE, GROUP_W), BF16),
                        pltpu.VMEM((SSM_STATE, GROUP_W), F32),
                        pltpu.VMEM((SSM_STATE, GROUP_W), F32)],
        compiler_params=_params("parallel", "parallel"),
        name="ssd",
    )(zx4, zx4, bc4, bc4, dt_row, cwx, cwb, cwc, cbx, cbb, cbc, db_row, al_row, dskip, ng)
    return out.reshape(G, batch * seq, GROUP_W)


def _out_proj_kernel(attn_ref, ssd_ref, w_ref, x_ref, o_ref, lhs):
    for h in range(ATTN_HEADS):
        lhs[:, h * HEAD_DIM:(h + 1) * HEAD_DIM] = attn_ref[h].astype(BF16)
    for g in range(SSM_GROUPS):
        lhs[:, ATTN_W + g * GROUP_W:ATTN_W + (g + 1) * GROUP_W] = ssd_ref[g]
    o_ref[...] = x_ref[...] + jnp.dot(lhs[...], w_ref[...], preferred_element_type=F32)


def _out_proj(attn, ssd, w, x2d, tm=256):
    n, d = x2d.shape
    kk = w.shape[0]
    return pl.pallas_call(
        _out_proj_kernel,
        grid=(n // tm,),
        in_specs=[pl.BlockSpec((ATTN_HEADS, tm, HEAD_DIM), lambda i: (0, i, 0)),
                  pl.BlockSpec((SSM_GROUPS, tm, GROUP_W), lambda i: (0, i, 0)),
                  pl.BlockSpec((kk, d), lambda i: (0, 0), pipeline_mode=pl.Buffered(1)),
                  pl.BlockSpec((tm, d), lambda i: (i, 0))],
        out_specs=pl.BlockSpec((tm, d), lambda i: (i, 0)),
        out_shape=jax.ShapeDtypeStruct((n, d), F32),
        scratch_shapes=[pltpu.VMEM((tm, kk), BF16)],
        compiler_params=_params("parallel"),
        name="out_proj",
    )(attn, ssd, w, x2d)


def _mlp_kernel(x_ref, g_ref, wu_ref, wd_ref, o_ref, hm):
    @pl.when(pl.program_id(1) == 0)
    def _():
        x = x_ref[...]
        ms = jnp.mean(x * x, axis=-1, keepdims=True)
        hm[...] = (x * lax.rsqrt(ms + EPS) * g_ref[...]).astype(hm.dtype)
        o_ref[...] = x
    u = jnp.maximum(jnp.dot(hm[...], wu_ref[...], preferred_element_type=F32), 0.0)
    o_ref[...] += jnp.dot((u * u).astype(BF16), wd_ref[...], preferred_element_type=F32)


def _mlp(x2d, g, wu, wd, tm=512, tf=1024):
    n, d = x2d.shape
    f = wu.shape[1]
    return pl.pallas_call(
        _mlp_kernel,
        grid=(n // tm, f // tf),
        in_specs=[pl.BlockSpec((tm, d), lambda i, j: (i, 0)),
                  pl.BlockSpec((1, d), lambda i, j: (0, 0)),
                  pl.BlockSpec((d, tf), lambda i, j: (0, j)),
                  pl.BlockSpec((tf, d), lambda i, j: (j, 0))],
        out_specs=pl.BlockSpec((tm, d), lambda i, j: (i, 0)),
        out_shape=jax.ShapeDtypeStruct((n, d), F32),
        scratch_shapes=[pltpu.VMEM((tm, d), BF16)],
        compiler_params=_params("parallel", "arbitrary"),
        name="mlp",
    )(x2d, g.reshape(1, d), wu, wd)


def kernel(x, norm_mix_g, w_in, q_norm_g, k_norm_g, rel_bias, conv_w, conv_b, dt_bias, a_log,
           d_skip, ssd_norm_g, w_out, norm_mlp_g, w_up, w_down):
    batch, seq, _ = x.shape
    n = batch * seq
    G, nh = SSM_GROUPS, HEADS_PER_GROUP
    nc = seq // SSD_CHUNK
    o_z = 3 * ATTN_W
    o_bc = o_z + 2 * SSM_W
    o_dt = o_bc + 2 * G * SSM_STATE
    x2d = x.reshape(n, D_MODEL)
    brow = _bias_rows(rel_bias)

    for layer in range(w_in.shape[0]):
        wi = w_in[layer].astype(BF16)
        h = _rmsnorm(x2d, norm_mix_g[layer])

        scale = 1.0 / math.sqrt(HEAD_DIM)
        gains = jnp.concatenate([jnp.tile(q_norm_g[layer].astype(F32) * scale, ATTN_HEADS),
                                 jnp.tile(k_norm_g[layer].astype(F32), ATTN_HEADS)]).reshape(1, 2 * ATTN_W)
        qk = _proj_qk(h, wi, gains)
        v = _proj_split(h, wi, 2 * ATTN_W, ATTN_W, HEAD_DIM, "proj_v", out_dtype=F32)
        zx = _proj_split(h, wi, o_z, o_bc - o_z, GROUP_W, "proj_zx")
        bcm = _proj_split(h, wi, o_bc, o_dt - o_bc, SSM_STATE, "proj_bc")
        dt_raw = _proj_plain(h, w_in[layer][:, o_dt:].astype(BF16), 0, 2 * SSM_HEADS, "proj_dt")

        attn = _attention(qk, v, brow, batch, seq)

        dt_row = (dt_raw.reshape(batch, nc, SSD_CHUNK, 2, G, nh)
                  .transpose(0, 4, 1, 3, 5, 2).reshape(batch, G, nc, 2 * nh, SSD_CHUNK))
        per_dir = lambda t: t.astype(F32).reshape(2, G, nh).transpose(1, 0, 2).reshape(G, 2 * nh)
        db, al = per_dir(dt_bias[layer]), per_dir(a_log[layer])
        cw, cbias = conv_w[layer].astype(F32), conv_b[layer].astype(F32)
        gn = G * SSM_STATE
        grp = lambda t, width: t.reshape(t.shape[0], G, width).transpose(1, 0, 2)
        ssd = _ssd(
            zx, bcm, dt_row,
            grp(cw[:, :SSM_W], GROUP_W), grp(cw[:, SSM_W:SSM_W + gn], SSM_STATE),
            grp(cw[:, SSM_W + gn:], SSM_STATE),
            grp(cbias[None, :SSM_W], GROUP_W), grp(cbias[None, SSM_W:SSM_W + gn], SSM_STATE),
            grp(cbias[None, SSM_W + gn:], SSM_STATE),
            db.reshape(G, 2 * nh, 1), al.reshape(G, 2 * nh, 1),
            jnp.repeat(d_skip[layer].astype(F32), SSM_HEAD_DIM).reshape(G, 1, GROUP_W),
            ssd_norm_g[layer].astype(F32).reshape(G, 1, GROUP_W),
            batch, seq)

        x2d = _out_proj(attn, ssd, w_out[layer].astype(BF16), x2d)
        x2d = _mlp(x2d, norm_mlp_g[layer], w_up[layer].astype(BF16), w_down[layer].astype(BF16))
    return x2d.reshape(batch, seq, D_MODEL)
```

```python
import functools
import math

import jax
import jax.numpy as jnp
from jax import lax
from jax.experimental import pallas as pl
from jax.experimental.pallas import tpu as pltpu

D_MODEL = 2048
ATTN_HEADS = 16
HEAD_DIM = 128
ATTN_W = ATTN_HEADS * HEAD_DIM
SSM_HEADS = 32
SSM_HEAD_DIM = 64
SSM_W = SSM_HEADS * SSM_HEAD_DIM
SSM_GROUPS = 8
HEADS_PER_GROUP = SSM_HEADS // SSM_GROUPS
GROUP_W = SSM_W // SSM_GROUPS
SSM_STATE = 128
CONV_WIDTH = 5
D_FF = 4 * D_MODEL
DILATIONS = (1, 4, 16)
HALF_WINDOW = 64
NUM_BUCKETS = 32
MAX_DISTANCE = 1024
NEG_INF = -1e30
EPS = 1e-6

LANES = 128
Q_SUB = 128
K_WIN = 256
SSD_CHUNK = 128
CONV_HALO = 8
VMEM_LIMIT = 56 * 1024 * 1024

F32 = jnp.float32
BF16 = jnp.bfloat16


def _params(*sem):
    return pltpu.CompilerParams(dimension_semantics=sem, vmem_limit_bytes=VMEM_LIMIT)


def _sigmoid(x):
    return 1.0 / (1.0 + jnp.exp(-x))


def _softplus(x):
    return jnp.maximum(x, 0.0) + jnp.log1p(jnp.exp(-jnp.abs(x)))


def _rmsnorm_kernel(x_ref, g_ref, o_ref):
    x = x_ref[...]
    ms = jnp.mean(x * x, axis=-1, keepdims=True)
    o_ref[...] = (x * lax.rsqrt(ms + EPS) * g_ref[...]).astype(o_ref.dtype)


def _rmsnorm(x2d, g, tm=512):
    n, d = x2d.shape
    return pl.pallas_call(
        _rmsnorm_kernel,
        grid=(n // tm,),
        in_specs=[pl.BlockSpec((tm, d), lambda i: (i, 0)),
                  pl.BlockSpec((1, d), lambda i: (0, 0))],
        out_specs=pl.BlockSpec((tm, d), lambda i: (i, 0)),
        out_shape=jax.ShapeDtypeStruct((n, d), BF16),
        compiler_params=_params("parallel"),
        name="rmsnorm",
    )(x2d, g.reshape(1, d))


QK_SUB = 256


def _proj_qk_kernel(a_ref, w_ref, g_ref, o_ref, *, heads_per_tile):
    a = a_ref[...]
    per = QK_SUB // HEAD_DIM
    for c in range(heads_per_tile // per):
        acc = jnp.dot(a, w_ref[:, c * QK_SUB:(c + 1) * QK_SUB], preferred_element_type=F32)
        for hh in range(per):
            h = c * per + hh
            s = acc[:, hh * HEAD_DIM:(hh + 1) * HEAD_DIM]
            ms = jnp.mean(s * s, axis=-1, keepdims=True)
            g = g_ref[:, h * HEAD_DIM:(h + 1) * HEAD_DIM]
            o_ref[h] = (s * lax.rsqrt(ms + EPS) * g).astype(o_ref.dtype)


def _proj_split_kernel(a_ref, w_ref, o_ref, *, width):
    acc = jnp.dot(a_ref[...], w_ref[...], preferred_element_type=F32)
    for c in range(o_ref.shape[0]):
        o_ref[c] = acc[:, c * width:(c + 1) * width].astype(o_ref.dtype)


def _proj_plain_kernel(a_ref, w_ref, o_ref):
    o_ref[...] = jnp.dot(a_ref[...], w_ref[...], preferred_element_type=F32).astype(o_ref.dtype)


def _proj_qk(h, w, gains, tm=1024, tn=1024):
    n, k = h.shape
    m = 2 * ATTN_W
    hpt = tn // HEAD_DIM
    return pl.pallas_call(
        functools.partial(_proj_qk_kernel, heads_per_tile=hpt),
        grid=(n // tm, m // tn),
        in_specs=[pl.BlockSpec((tm, k), lambda i, j: (i, 0)),
                  pl.BlockSpec((k, tn), lambda i, j: (0, j)),
                  pl.BlockSpec((1, tn), lambda i, j: (0, j))],
        out_specs=pl.BlockSpec((hpt, tm, HEAD_DIM), lambda i, j: (j, i, 0)),
        out_shape=jax.ShapeDtypeStruct((m // HEAD_DIM, n, HEAD_DIM), F32),
        compiler_params=_params("parallel", "arbitrary"),
        name="proj_qk",
    )(h, w, gains)


def _proj_split(h, w, col0, m, width, name, out_dtype=BF16, tm=1024, tn=1024):
    n, k = h.shape
    cpt = tn // width
    j0 = col0 // tn
    return pl.pallas_call(
        functools.partial(_proj_split_kernel, width=width),
        grid=(n // tm, m // tn),
        in_specs=[pl.BlockSpec((tm, k), lambda i, j: (i, 0)),
                  pl.BlockSpec((k, tn), lambda i, j: (0, j0 + j))],
        out_specs=pl.BlockSpec((cpt, tm, width), lambda i, j: (j, i, 0)),
        out_shape=jax.ShapeDtypeStruct((m // width, n, width), out_dtype),
        compiler_params=_params("parallel", "arbitrary"),
        name=name,
    )(h, w)


def _proj_plain(h, w, col0, m, name, tm=1024):
    n, k = h.shape
    j0 = col0 // m
    return pl.pallas_call(
        _proj_plain_kernel,
        grid=(n // tm,),
        in_specs=[pl.BlockSpec((tm, k), lambda i: (i, 0)),
                  pl.BlockSpec((k, m), lambda i: (0, j0))],
        out_specs=pl.BlockSpec((tm, m), lambda i: (i, 0)),
        out_shape=jax.ShapeDtypeStruct((n, m), F32),
        compiler_params=_params("parallel"),
        name=name,
    )(h, w)


def _t5_bucket(rel):
    nb = NUM_BUCKETS // 2
    max_exact = nb // 2
    ret = (rel > 0).astype(jnp.int32) * nb
    n = jnp.abs(rel)
    nf = jnp.maximum(n, 1).astype(jnp.float32)
    large = max_exact + (jnp.log(nf / max_exact) / math.log(MAX_DISTANCE / max_exact)
                         * (nb - max_exact)).astype(jnp.int32)
    large = jnp.minimum(large, nb - 1)
    return ret + jnp.where(n < max_exact, n, large)


def _bias_rows(rel_bias):
    period = K_WIN + Q_SUB
    m = jnp.arange(period)
    delta = jnp.where(m < K_WIN, m, m - period)
    rows = []
    for d in DILATIONS:
        for off in (0, -HALF_WINDOW, -2 * HALF_WINDOW):
            rel = delta + off
            valid = jnp.abs(rel) <= HALF_WINDOW
            b = rel_bias[_t5_bucket(rel * d)].astype(F32)
            rows.append(jnp.where(valid[:, None], b, NEG_INF))
    return jnp.stack(rows, axis=0).transpose(2, 0, 1)


ATTN_UNROLL = 16


def _attn_kernel(q_ref, k_ref, v_ref, brow_ref, o_ref, bias_s, tmp_s, xq4, xt4,
                 q16, k1, k4, k16, v1, v4, v16, acc_s, max_s, den_s, *, seq):
    period = K_WIN + Q_SUB
    n4, n16 = seq // 4, seq // 16

    @pl.when(pl.program_id(1) == 0)
    def _():
        for idx in range(9):
            row = jnp.broadcast_to(brow_ref[idx:idx + 1, :], (Q_SUB, period))
            tile = pltpu.roll(row, 0, 1, stride=1, stride_axis=0)
            d = DILATIONS[idx // 3]
            if d == 16:
                bias_s[idx] = tile[:, :K_WIN]
                continue
            for half in range(K_WIN // LANES):
                tmp_s[half] = tile[:, half * LANES:(half + 1) * LANES]
            groups = 16 // d
            for half in range(K_WIN // LANES):
                for g in range(groups):
                    n = Q_SUB // groups
                    bias_s[idx, g * n:(g + 1) * n, half * LANES:(half + 1) * LANES] = (
                        tmp_s[half, pl.ds(g, n, stride=groups), :])

    cp = 256

    def split4(src, dst):
        for r4 in range(4):
            def body(c, carry, r4=r4):
                t0 = pl.multiple_of(c * cp, cp)
                dst[r4, pl.ds(t0, cp), :] = src[pl.ds(r4 + 4 * t0, cp, stride=4), :]
                return carry
            lax.fori_loop(0, n4 // cp, body, 0)

    def split16(src4, dst16, dst4=None):
        for r4 in range(4):
            for a in range(4):
                dst16[4 * a + r4] = src4[r4, pl.ds(a, n16, stride=4), :].astype(BF16)
            if dst4 is not None:
                def body(c, carry, r4=r4):
                    t0 = pl.multiple_of(c * cp, cp)
                    dst4[r4, pl.ds(t0, cp), :] = src4[r4, pl.ds(t0, cp), :].astype(BF16)
                    return carry
                lax.fori_loop(0, n4 // cp, body, 0)

    def cast(src, dst):
        def body(c, carry):
            t0 = pl.multiple_of(c * cp, cp)
            dst[pl.ds(t0, cp), :] = src[pl.ds(t0, cp), :].astype(BF16)
            return carry
        lax.fori_loop(0, seq // cp, body, 0)

    split4(q_ref, xq4)
    split16(xq4, q16)
    for src, d1, d4, d16 in ((k_ref, k1, k4, k16), (v_ref, v1, v4, v16)):
        cast(src, d1)
        split4(src, xt4)
        split16(xt4, d16, d4)

    ones = jnp.ones((K_WIN, LANES), BF16)

    def sub_tile(q, k_s, v_s, length, s0, pi):
        w0 = pl.multiple_of(jnp.clip(s0 - HALF_WINDOW, 0, length - K_WIN), HALF_WINDOW)
        place = jnp.where(s0 == 0, 0, jnp.where(s0 == length - Q_SUB, 2, 1))
        k = k_s[pl.ds(w0, K_WIN), :]
        v = v_s[pl.ds(w0, K_WIN), :]
        s = lax.dot_general(q, k, (((1,), (1,)), ((), ())), preferred_element_type=F32)
        s = s + bias_s[3 * pi + place]
        m = jnp.max(s, axis=-1, keepdims=True)
        p = jnp.exp(s - m).astype(BF16)
        pv = jnp.dot(p, jnp.concatenate([v, ones], axis=1), preferred_element_type=F32)
        return pv[:, :LANES], jnp.broadcast_to(m, (Q_SUB, LANES)), pv[:, LANES:]

    def store(pi, res, slab_of_group, l0, n):
        for g in range(Q_SUB // n):
            for val, dst in zip(res, (acc_s, max_s, den_s)):
                dst[pi, slab_of_group(g), pl.ds(l0, n), :] = val[g * n:(g + 1) * n]

    U = ATTN_UNROLL

    def body1(it, carry):
        res = []
        for u in range(U):
            l0 = pl.multiple_of((it * U + u) * 8, 8)
            q = jnp.concatenate([xq4[r % 4, pl.ds(4 * l0 + r // 4, 8, stride=4), :] for r in range(16)],
                                axis=0).astype(BF16)
            res.append((sub_tile(q, k1, v1, seq, pl.multiple_of(16 * l0, Q_SUB), 0), l0))
        for r3, l0 in res:
            store(0, r3, lambda g: g, l0, 8)
        return carry
    lax.fori_loop(0, seq // Q_SUB // U, body1, 0)

    def body4(it, carry):
        res = []
        for u in range(U // 4):
            l0 = pl.multiple_of((it * (U // 4) + u) * 32, 32)
            for r4 in range(4):
                q = jnp.concatenate([q16[4 * a + r4, pl.ds(l0, 32), :] for a in range(4)], axis=0)
                res.append((sub_tile(q, k4.at[r4], v4.at[r4], n4, pl.multiple_of(4 * l0, Q_SUB), 1),
                            r4, l0))
        for r3, r4, l0 in res:
            store(1, r3, lambda g, r4=r4: 4 * g + r4, l0, 32)
        return carry
    lax.fori_loop(0, n4 // Q_SUB // (U // 4), body4, 0)

    def body16(it, carry):
        g0 = (it // (n16 // Q_SUB)) * U
        l0 = pl.multiple_of((it % (n16 // Q_SUB)) * Q_SUB, Q_SUB)
        res = [sub_tile(q16[g0 + u, pl.ds(l0, Q_SUB), :], k16.at[g0 + u], v16.at[g0 + u], n16, l0, 2)
               for u in range(U)]
        for u, r3 in enumerate(res):
            store(2, r3, lambda g, u=u: g0 + u, l0, Q_SUB)
        return carry
    lax.fori_loop(0, (16 // U) * (n16 // Q_SUB), body16, 0)

    rows = 64
    for r in range(16):
        for pc in range(n16 // rows):
            idx = (r, pl.ds(pc * rows, rows))
            ma, mb, mc = max_s[(0,) + idx], max_s[(1,) + idx], max_s[(2,) + idx]
            mx = jnp.maximum(jnp.maximum(ma, mb), mc)
            ea, eb, ec = jnp.exp(ma - mx), jnp.exp(mb - mx), jnp.exp(mc - mx)
            num = ea * acc_s[(0,) + idx] + eb * acc_s[(1,) + idx] + ec * acc_s[(2,) + idx]
            den = ea * den_s[(0,) + idx] + eb * den_s[(1,) + idx] + ec * den_s[(2,) + idx]
            o_ref[pl.ds(r + 16 * pc * rows, rows, stride=16), :] = num / den


def _attention(qk, v, brow, batch, seq):
    H = ATTN_HEADS
    n4, n16 = seq // 4, seq // 16
    qk4 = qk.reshape(2 * H, batch, seq, HEAD_DIM)
    v4 = v.reshape(H, batch, seq, HEAD_DIM)
    in_specs = [pl.BlockSpec((None, None, seq, HEAD_DIM), lambda h, b: (h, b, 0, 0)),
                pl.BlockSpec((None, None, seq, HEAD_DIM), lambda h, b: (H + h, b, 0, 0)),
                pl.BlockSpec((None, None, seq, HEAD_DIM), lambda h, b: (h, b, 0, 0))]
    in_specs.append(pl.BlockSpec((None, 9, K_WIN + Q_SUB), lambda h, b: (h, 0, 0)))
    kv_slabs = [pltpu.VMEM((seq, LANES), BF16), pltpu.VMEM((4, n4, LANES), BF16),
                pltpu.VMEM((16, n16, LANES), BF16)]
    out = pl.pallas_call(
        functools.partial(_attn_kernel, seq=seq),
        grid=(H, batch),
        in_specs=in_specs,
        out_specs=pl.BlockSpec((None, None, seq, HEAD_DIM), lambda h, b: (h, b, 0, 0)),
        out_shape=jax.ShapeDtypeStruct((H, batch, seq, HEAD_DIM), F32),
        scratch_shapes=[pltpu.VMEM((9, Q_SUB, K_WIN), F32),
                        pltpu.VMEM((K_WIN // LANES, Q_SUB, LANES), F32),
                        pltpu.VMEM((4, n4, LANES), F32),
                        pltpu.VMEM((4, n4, LANES), F32),
                        pltpu.VMEM((16, n16, LANES), BF16)]
                       + kv_slabs + kv_slabs
                       + [pltpu.VMEM((3, 16, n16, LANES), F32)] * 3,
        compiler_params=_params("parallel", "arbitrary"),
        name="dilated_attention",
    )(qk4, qk4, v4, brow)
    return out.reshape(H, batch * seq, HEAD_DIM)


def _ssd_kernel(z_ref, x_ref, b_ref, c_ref, dtr_ref, cwx_ref, cwb_ref, cwc_ref,
                cbx_ref, cbb_ref, cbc_ref, dbr_ref, alr_ref, dsk_ref, ng_ref,
                o_ref, pad, cv, y_s, arg_s, dt_s, rows_s, cols_s, sbs, sf, sb, *, seq):
    T = SSD_CHUNK
    nc = seq // T
    hi = lax.Precision.HIGHEST
    halo = CONV_HALO
    half = T // 2

    pad[:, 0:halo, :] = jnp.zeros((4, halo, LANES), F32)
    pad[:, seq + halo:seq + 2 * halo, :] = jnp.zeros((4, halo, LANES), F32)

    def fill(i, carry):
        t0 = pl.multiple_of(i * T, T)
        xin = x_ref[pl.ds(t0, T), :].astype(F32)
        pad[0, pl.ds(t0 + halo, T), :] = xin[:, :LANES]
        pad[1, pl.ds(t0 + halo, T), :] = xin[:, LANES:]
        pad[2, pl.ds(t0 + halo, T), :] = b_ref[pl.ds(t0, T), :].astype(F32)
        pad[3, pl.ds(t0 + halo, T), :] = c_ref[pl.ds(t0, T), :].astype(F32)
        return carry
    lax.fori_loop(0, nc, fill, 0)

    cws = (cwx_ref[:, :LANES], cwx_ref[:, LANES:], cwb_ref[...], cwc_ref[...])
    cbs = (cbx_ref[:, :LANES], cbx_ref[:, LANES:], cbb_ref[...], cbc_ref[...])

    def conv_chunk(ci):
        t0 = pl.multiple_of(ci * T, T)
        for s in range(4):
            ev = [pad[s, pl.ds(t0 + halo + 2 * k, half, stride=2), :] for k in (-1, 0, 1)]
            od = [pad[s, pl.ds(t0 + halo + 1 + 2 * k, half, stride=2), :] for k in (-1, 0, 1)]
            w = [cws[s][j:j + 1, :] for j in range(CONV_WIDTH)]
            out_e = cbs[s] + w[0] * ev[0] + w[1] * od[0] + w[2] * ev[1] + w[3] * od[1] + w[4] * ev[2]
            out_o = cbs[s] + w[0] * od[0] + w[1] * ev[1] + w[2] * od[1] + w[3] * ev[2] + w[4] * od[2]
            cv[s, pl.ds(t0, half, stride=2), :] = out_e * _sigmoid(out_e)
            cv[s, pl.ds(t0 + 1, half, stride=2), :] = out_o * _sigmoid(out_o)

    def load_x(t0):
        return jnp.concatenate([cv[0, pl.ds(t0, T), :], cv[1, pl.ds(t0, T), :]], axis=1)

    nh = HEADS_PER_GROUP
    nd = 2 * nh
    row = lax.broadcasted_iota(jnp.int32, (T, T), 0)
    col = lax.broadcasted_iota(jnp.int32, (T, T), 1)
    triu = (row <= col).astype(F32)
    lower = col < row
    upper = col > row
    lane_head = lax.broadcasted_iota(jnp.int32, (T, GROUP_W), 1) // SSM_HEAD_DIM

    dt = _softplus(dtr_ref[...] + dbr_ref[...])
    a = dt * (-jnp.exp(alr_ref[...]))
    cum = jnp.dot(a.reshape(nc * nd, T), triu, precision=hi,
                  preferred_element_type=F32).reshape(nc, nd, T)
    last = cum[:, :, T - 1:T]
    exc = cum - a
    fwd = lax.broadcasted_iota(jnp.int32, (nc, nd, T), 1) < nh
    arg_s[...] = jnp.where(fwd, cum, exc)
    dt_s[...] = dt
    rows_s[:, 0:nd, :] = arg_s[...]
    rows_s[:, nd:2 * nd, :] = jnp.exp(jnp.where(fwd, cum, last - exc))
    rows_s[:, 2 * nd:3 * nd, :] = dt * jnp.exp(jnp.where(fwd, last - cum, exc))
    rows_s[:, 3 * nd:, :] = jnp.zeros((nc, LANES - 3 * nd, T), F32)

    def expand(cols, first):
        n = cols.shape[0]
        low = lax.broadcasted_iota(jnp.int32, (n, LANES), 1) < SSM_HEAD_DIM
        halves = []
        for j in range(GROUP_W // LANES):
            c0 = jnp.broadcast_to(cols[:, first + 2 * j:first + 2 * j + 1], (n, LANES))
            c1 = jnp.broadcast_to(cols[:, first + 2 * j + 1:first + 2 * j + 2], (n, LANES))
            halves.append(jnp.where(low, c0, c1))
        return jnp.concatenate(halves, axis=1)

    tn_dims = (((0,), (0,)), ((), ()))

    n_dec = nd * LANES
    cid = lax.broadcasted_iota(jnp.int32, (LANES, n_dec + 2 * GROUP_W), 0)
    lid = lax.broadcasted_iota(jnp.int32, (LANES, n_dec + 2 * GROUP_W), 1)
    want = jnp.where(lid < n_dec, nd + lid // LANES, 2 * nd + (lid - n_dec) // SSM_HEAD_DIM)
    spread = jnp.where(cid == want, 1.0, 0.0).astype(BF16)
    spread_f = spread[:, :n_dec + GROUP_W]
    spread_b = spread[:, n_dec + GROUP_W:]

    sb[...] = jnp.zeros_like(sb)

    def state_back(ci):
        t0 = pl.multiple_of(ci * T, T)
        ct = rows_s[ci].T
        cols_s[ci] = ct
        s_prev = sb[...]
        sbs[ci] = s_prev.astype(BF16)
        wx = jnp.dot(ct.astype(BF16), spread_b, preferred_element_type=F32)
        xs = (load_x(t0) * wx).astype(BF16)
        sb[...] = expand(ct[0:1, :], nd + nh) * s_prev + lax.dot_general(
            cv[2, pl.ds(t0, T), :].astype(BF16), xs, tn_dims, preferred_element_type=F32)

    conv_chunk(nc - 1)
    conv_chunk(nc - 2)

    def sweep_back(i, carry):
        k = nc - 1 - 2 * i
        state_back(k + 2)
        state_back(k + 1)
        conv_chunk(k)
        conv_chunk(k - 1)
        return carry
    lax.fori_loop(1, nc // 2, sweep_back, 0)
    state_back(1)
    state_back(0)

    sf[...] = jnp.zeros_like(sf)
    low_half = lax.broadcasted_iota(jnp.int32, (T, LANES), 1) < SSM_HEAD_DIM

    def finish(ci):
        t0 = pl.multiple_of(ci * T, T)
        zz = z_ref[pl.ds(t0, T), :].astype(F32)
        y = y_s[pl.ds(t0, T), :] * (zz * _sigmoid(zz))
        ms = jnp.mean(y * y, axis=-1, keepdims=True)
        o_ref[pl.ds(t0, T), :] = (y * lax.rsqrt(ms + EPS) * ng_ref[...]).astype(o_ref.dtype)

    def chunk_fwd(ci):
        t0 = pl.multiple_of(ci * T, T)
        ct = cols_s[ci]
        arg_r = arg_s[ci]
        dt_r = dt_s[ci]
        x = load_x(t0)
        bk = cv[2, pl.ds(t0, T), :].astype(BF16)
        ckf = cv[3, pl.ds(t0, T), :]
        ck = ckf.astype(BF16)
        cb = lax.dot_general(ck, bk, (((1,), (1,)), ((), ())), preferred_element_type=F32)
        spreadv = jnp.dot(ct.astype(BF16), spread_f, preferred_element_type=F32)
        lhs = []
        for h in range(nh):
            hb = nh + h
            arg = jnp.where(lower, ct[:, h:h + 1] - arg_r[h:h + 1, :], arg_r[hb:hb + 1, :] - ct[:, hb:hb + 1])
            wgt = jnp.where(lower, dt_r[h:h + 1, :],
                            jnp.where(upper, dt_r[hb:hb + 1, :], dt_r[h:h + 1, :] + dt_r[hb:hb + 1, :]))
            lhs.append(jnp.concatenate(
                [(cb * jnp.exp(arg) * wgt).astype(BF16),
                 (ckf * spreadv[:, h * LANES:(h + 1) * LANES]).astype(BF16),
                 (ckf * spreadv[:, hb * LANES:(hb + 1) * LANES]).astype(BF16)], axis=1))
        s_prev = sf[...]
        rhs = jnp.concatenate([x.astype(BF16), s_prev.astype(BF16), sbs[ci]], axis=0)
        y4 = jnp.dot(jnp.concatenate(lhs, axis=0), rhs, preferred_element_type=F32)
        y = x * dsk_ref[...] + jnp.concatenate(
            [jnp.where(low_half, y4[2 * j * T:(2 * j + 1) * T, j * LANES:(j + 1) * LANES],
                       y4[(2 * j + 1) * T:(2 * j + 2) * T, j * LANES:(j + 1) * LANES])
             for j in range(GROUP_W // LANES)], axis=1)
        xs = (x * spreadv[:, n_dec:]).astype(BF16)
        sf[...] = expand(ct[T - 1:T, :], nd) * s_prev + lax.dot_general(
            bk, xs, tn_dims, preferred_element_type=F32)
        y_s[pl.ds(t0, T), :] = y

    chunk_fwd(0)

    def sweep_fwd(i, carry):
        ci = 2 * i + 1
        finish(ci - 1)
        chunk_fwd(ci)
        finish(ci)
        chunk_fwd(ci + 1)
        return carry
    lax.fori_loop(0, (nc - 2) // 2, sweep_fwd, 0)
    finish(nc - 2)
    chunk_fwd(nc - 1)
    finish(nc - 1)


def _ssd(zx, bcm, dt_row, cwx, cwb, cwc, cbx, cbb, cbc, db_row, al_row, dskip, ng, batch, seq):
    G = SSM_GROUPS
    nc = seq // SSD_CHUNK
    zx4 = zx.reshape(2 * G, batch, seq, GROUP_W)
    bc4 = bcm.reshape(2 * G, batch, seq, SSM_STATE)
    nd = 2 * HEADS_PER_GROUP

    def per_group(shape):
        return pl.BlockSpec((None,) + shape, lambda b, g: (g,) + (0,) * len(shape))

    in_specs = [
        pl.BlockSpec((None, None, seq, GROUP_W), lambda b, g: (g, b, 0, 0)),
        pl.BlockSpec((None, None, seq, GROUP_W), lambda b, g: (G + g, b, 0, 0)),
        pl.BlockSpec((None, None, seq, SSM_STATE), lambda b, g: (g, b, 0, 0)),
        pl.BlockSpec((None, None, seq, SSM_STATE), lambda b, g: (G + g, b, 0, 0)),
        pl.BlockSpec((None, None, nc, nd, SSD_CHUNK), lambda b, g: (b, g, 0, 0, 0)),
        per_group((CONV_WIDTH, GROUP_W)), per_group((CONV_WIDTH, SSM_STATE)),
        per_group((CONV_WIDTH, SSM_STATE)),
        per_group((1, GROUP_W)), per_group((1, SSM_STATE)), per_group((1, SSM_STATE)),
        per_group((nd, 1)), per_group((nd, 1)),
        per_group((1, GROUP_W)), per_group((1, GROUP_W)),
    ]
    out = pl.pallas_call(
        functools.partial(_ssd_kernel, seq=seq),
        grid=(batch, G),
        in_specs=in_specs,
        out_specs=pl.BlockSpec((None, None, seq, GROUP_W), lambda b, g: (g, b, 0, 0)),
        out_shape=jax.ShapeDtypeStruct((G, batch, seq, GROUP_W), BF16),
        scratch_shapes=[pltpu.VMEM((4, seq + 2 * CONV_HALO, LANES), F32),
                        pltpu.VMEM((4, seq, LANES), F32),
                        pltpu.VMEM((seq, GROUP_W), F32),
                        pltpu.VMEM((nc, nd, SSD_CHUNK), F32),
                        pltpu.VMEM((nc, nd, SSD_CHUNK), F32),
                        pltpu.VMEM((nc, LANES, SSD_CHUNK), F32),
                        pltpu.VMEM((nc, SSD_CHUNK, LANES), F32),
                        pltpu.VMEM((nc, SSM_STATE, GROUP_W), BF16),
                        pltpu.VMEM((SSM_STATE, GROUP_W), F32),
                        pltpu.VMEM((SSM_STATE, GROUP_W), F32)],
        compiler_params=_params("parallel", "parallel"),
        name="ssd",
    )(zx4, zx4, bc4, bc4, dt_row, cwx, cwb, cwc, cbx, cbb, cbc, db_row, al_row, dskip, ng)
    return out.reshape(G, batch * seq, GROUP_W)


def _out_proj_kernel(attn_ref, ssd_ref, w_ref, x_ref, o_ref, lhs):
    for h in range(ATTN_HEADS):
        lhs[:, h * HEAD_DIM:(h + 1) * HEAD_DIM] = attn_ref[h].astype(BF16)
    for g in range(SSM_GROUPS):
        lhs[:, ATTN_W + g * GROUP_W:ATTN_W + (g + 1) * GROUP_W] = ssd_ref[g]
    o_ref[...] = x_ref[...] + jnp.dot(lhs[...], w_ref[...], preferred_element_type=F32)


def _out_proj(attn, ssd, w, x2d, tm=256):
    n, d = x2d.shape
    kk = w.shape[0]
    return pl.pallas_call(
        _out_proj_kernel,
        grid=(n // tm,),
        in_specs=[pl.BlockSpec((ATTN_HEADS, tm, HEAD_DIM), lambda i: (0, i, 0)),
                  pl.BlockSpec((SSM_GROUPS, tm, GROUP_W), lambda i: (0, i, 0)),
                  pl.BlockSpec((kk, d), lambda i: (0, 0), pipeline_mode=pl.Buffered(1)),
                  pl.BlockSpec((tm, d), lambda i: (i, 0))],
        out_specs=pl.BlockSpec((tm, d), lambda i: (i, 0)),
        out_shape=jax.ShapeDtypeStruct((n, d), F32),
        scratch_shapes=[pltpu.VMEM((tm, kk), BF16)],
        compiler_params=_params("parallel"),
        name="out_proj",
    )(attn, ssd, w, x2d)


def _mlp_kernel(x_ref, g_ref, wu_ref, wd_ref, o_ref, hm):
    @pl.when(pl.program_id(1) == 0)
    def _():
        x = x_ref[...]
        ms = jnp.mean(x * x, axis=-1, keepdims=True)
        hm[...] = (x * lax.rsqrt(ms + EPS) * g_ref[...]).astype(hm.dtype)
        o_ref[...] = x
    u = jnp.maximum(jnp.dot(hm[...], wu_ref[...], preferred_element_type=F32), 0.0)
    o_ref[...] += jnp.dot((u * u).astype(BF16), wd_ref[...], preferred_element_type=F32)


def _mlp(x2d, g, wu, wd, tm=512, tf=1024):
    n, d = x2d.shape
    f = wu.shape[1]
    return pl.pallas_call(
        _mlp_kernel,
        grid=(n // tm, f // tf),
        in_specs=[pl.BlockSpec((tm, d), lambda i, j: (i, 0)),
                  pl.BlockSpec((1, d), lambda i, j: (0, 0)),
                  pl.BlockSpec((d, tf), lambda i, j: (0, j)),
                  pl.BlockSpec((tf, d), lambda i, j: (j, 0))],
        out_specs=pl.BlockSpec((tm, d), lambda i, j: (i, 0)),
        out_shape=jax.ShapeDtypeStruct((n, d), F32),
        scratch_shapes=[pltpu.VMEM((tm, d), BF16)],
        compiler_params=_params("parallel", "arbitrary"),
        name="mlp",
    )(x2d, g.reshape(1, d), wu, wd)


def kernel(x, norm_mix_g, w_in, q_norm_g, k_norm_g, rel_bias, conv_w, conv_b, dt_bias, a_log,
           d_skip, ssd_norm_g, w_out, norm_mlp_g, w_up, w_down):
    batch, seq, _ = x.shape
    n = batch * seq
    G, nh = SSM_GROUPS, HEADS_PER_GROUP
    nc = seq // SSD_CHUNK
    o_z = 3 * ATTN_W
    o_bc = o_z + 2 * SSM_W
    o_dt = o_bc + 2 * G * SSM_STATE
    x2d = x.reshape(n, D_MODEL)
    brow = _bias_rows(rel_bias)

    for layer in range(w_in.shape[0]):
        wi = w_in[layer].astype(BF16)
        h = _rmsnorm(x2d, norm_mix_g[layer])

        scale = 1.0 / math.sqrt(HEAD_DIM)
        gains = jnp.concatenate([jnp.tile(q_norm_g[layer].astype(F32) * scale, ATTN_HEADS),
                                 jnp.tile(k_norm_g[layer].astype(F32), ATTN_HEADS)]).reshape(1, 2 * ATTN_W)
        qk = _proj_qk(h, wi, gains)
        v = _proj_split(h, wi, 2 * ATTN_W, ATTN_W, HEAD_DIM, "proj_v", out_dtype=F32)
        zx = _proj_split(h, wi, o_z, o_bc - o_z, GROUP_W, "proj_zx")
        bcm = _proj_split(h, wi, o_bc, o_dt - o_bc, SSM_STATE, "proj_bc")
        dt_raw = _proj_plain(h, w_in[layer][:, o_dt:].astype(BF16), 0, 2 * SSM_HEADS, "proj_dt")

        attn = _attention(qk, v, brow, batch, seq)

        dt_row = (dt_raw.reshape(batch, nc, SSD_CHUNK, 2, G, nh)
                  .transpose(0, 4, 1, 3, 5, 2).reshape(batch, G, nc, 2 * nh, SSD_CHUNK))
        per_dir = lambda t: t.astype(F32).reshape(2, G, nh).transpose(1, 0, 2).reshape(G, 2 * nh)
        db, al = per_dir(dt_bias[layer]), per_dir(a_log[layer])
        cw, cbias = conv_w[layer].astype(F32), conv_b[layer].astype(F32)
        gn = G * SSM_STATE
        grp = lambda t, width: t.reshape(t.shape[0], G, width).transpose(1, 0, 2)
        ssd = _ssd(
            zx, bcm, dt_row,
            grp(cw[:, :SSM_W], GROUP_W), grp(cw[:, SSM_W:SSM_W + gn], SSM_STATE),
            grp(cw[:, SSM_W + gn:], SSM_STATE),
            grp(cbias[None, :SSM_W], GROUP_W), grp(cbias[None, SSM_W:SSM_W + gn], SSM_STATE),
            grp(cbias[None, SSM_W + gn:], SSM_STATE),
            db.reshape(G, 2 * nh, 1), al.reshape(G, 2 * nh, 1),
            jnp.repeat(d_skip[layer].astype(F32), SSM_HEAD_DIM).reshape(G, 1, GROUP_W),
            ssd_norm_g[layer].astype(F32).reshape(G, 1, GROUP_W),
            batch, seq)

        x2d = _out_proj(attn, ssd, w_out[layer].astype(BF16), x2d)
        x2d = _mlp(x2d, norm_mlp_g[layer], w_up[layer].astype(BF16), w_down[layer].astype(BF16))
    return x2d.reshape(batch, seq, D_MODEL)
```

```python
import functools
import math

import jax
import jax.numpy as jnp
from jax import lax
from jax.experimental import pallas as pl
from jax.experimental.pallas import tpu as pltpu

D_MODEL = 2048
ATTN_HEADS = 16
HEAD_DIM = 128
ATTN_W = ATTN_HEADS * HEAD_DIM
SSM_HEADS = 32
SSM_HEAD_DIM = 64
SSM_W = SSM_HEADS * SSM_HEAD_DIM
SSM_GROUPS = 8
HEADS_PER_GROUP = SSM_HEADS // SSM_GROUPS
GROUP_W = SSM_W // SSM_GROUPS
SSM_STATE = 128
CONV_WIDTH = 5
D_FF = 4 * D_MODEL
DILATIONS = (1, 4, 16)
HALF_WINDOW = 64
NUM_BUCKETS = 32
MAX_DISTANCE = 1024
NEG_INF = -1e30
EPS = 1e-6

LANES = 128
Q_SUB = 128
K_WIN = 256
SSD_CHUNK = 128
SSD_UNROLL = 4
SSD_UNROLL_FWD = 2
CONV_HALO = 8
VMEM_LIMIT = 56 * 1024 * 1024

F32 = jnp.float32
BF16 = jnp.bfloat16


def _params(*sem):
    return pltpu.CompilerParams(dimension_semantics=sem, vmem_limit_bytes=VMEM_LIMIT)


def _sigmoid(x):
    return 1.0 / (1.0 + jnp.exp(-x))


def _softplus(x):
    return jnp.maximum(x, 0.0) + jnp.log1p(jnp.exp(-jnp.abs(x)))


def _rmsnorm_kernel(x_ref, g_ref, o_ref):
    x = x_ref[...]
    ms = jnp.mean(x * x, axis=-1, keepdims=True)
    o_ref[...] = (x * lax.rsqrt(ms + EPS) * g_ref[...]).astype(o_ref.dtype)


def _rmsnorm(x2d, g, tm=512):
    n, d = x2d.shape
    return pl.pallas_call(
        _rmsnorm_kernel,
        grid=(n // tm,),
        in_specs=[pl.BlockSpec((tm, d), lambda i: (i, 0)),
                  pl.BlockSpec((1, d), lambda i: (0, 0))],
        out_specs=pl.BlockSpec((tm, d), lambda i: (i, 0)),
        out_shape=jax.ShapeDtypeStruct((n, d), BF16),
        compiler_params=_params("parallel"),
        name="rmsnorm",
    )(x2d, g.reshape(1, d))


QK_SUB = 256


def _cast_weight_tile(w_ref, wb):
    @pl.when(pl.program_id(1) == 0)
    def _():
        wb[...] = w_ref[...].astype(wb.dtype)


def _proj_qk_kernel(a_ref, w_ref, g_ref, o_ref, wb, *, heads_per_tile):
    _cast_weight_tile(w_ref, wb)
    a = a_ref[...]
    per = QK_SUB // HEAD_DIM
    for c in range(heads_per_tile // per):
        acc = jnp.dot(a, wb[:, c * QK_SUB:(c + 1) * QK_SUB], preferred_element_type=F32)
        for hh in range(per):
            h = c * per + hh
            s = acc[:, hh * HEAD_DIM:(hh + 1) * HEAD_DIM]
            ms = jnp.mean(s * s, axis=-1, keepdims=True)
            g = g_ref[:, h * HEAD_DIM:(h + 1) * HEAD_DIM]
            o_ref[h] = (s * lax.rsqrt(ms + EPS) * g).astype(o_ref.dtype)


def _proj_split_kernel(a_ref, w_ref, o_ref, wb, *, width):
    _cast_weight_tile(w_ref, wb)
    acc = jnp.dot(a_ref[...], wb[...], preferred_element_type=F32)
    for c in range(o_ref.shape[0]):
        o_ref[c] = acc[:, c * width:(c + 1) * width].astype(o_ref.dtype)


def _proj_plain_kernel(a_ref, w_ref, o_ref):
    o_ref[...] = jnp.dot(a_ref[...], w_ref[...], preferred_element_type=F32).astype(o_ref.dtype)


def _proj_qk(h, w, gains, tm=1024, tn=1024):
    n, k = h.shape
    m = 2 * ATTN_W
    hpt = tn // HEAD_DIM
    return pl.pallas_call(
        functools.partial(_proj_qk_kernel, heads_per_tile=hpt),
        grid=(m // tn, n // tm),
        in_specs=[pl.BlockSpec((tm, k), lambda j, i: (i, 0)),
                  pl.BlockSpec((k, tn), lambda j, i: (0, j)),
                  pl.BlockSpec((1, tn), lambda j, i: (0, j))],
        out_specs=pl.BlockSpec((hpt, tm, HEAD_DIM), lambda j, i: (j, i, 0)),
        out_shape=jax.ShapeDtypeStruct((m // HEAD_DIM, n, HEAD_DIM), F32),
        scratch_shapes=[pltpu.VMEM((k, tn), BF16)],
        compiler_params=_params("parallel", "arbitrary"),
        name="proj_qk",
    )(h, w, gains)


def _proj_split(h, w, col0, m, width, name, out_dtype=BF16, tm=1024, tn=1024):
    n, k = h.shape
    cpt = tn // width
    j0 = col0 // tn
    return pl.pallas_call(
        functools.partial(_proj_split_kernel, width=width),
        grid=(m // tn, n // tm),
        in_specs=[pl.BlockSpec((tm, k), lambda j, i: (i, 0)),
                  pl.BlockSpec((k, tn), lambda j, i: (0, j0 + j))],
        out_specs=pl.BlockSpec((cpt, tm, width), lambda j, i: (j, i, 0)),
        out_shape=jax.ShapeDtypeStruct((m // width, n, width), out_dtype),
        scratch_shapes=[pltpu.VMEM((k, tn), BF16)],
        compiler_params=_params("parallel", "arbitrary"),
        name=name,
    )(h, w)


def _proj_plain(h, w, col0, m, name, tm=1024):
    n, k = h.shape
    j0 = col0 // m
    return pl.pallas_call(
        _proj_plain_kernel,
        grid=(n // tm,),
        in_specs=[pl.BlockSpec((tm, k), lambda i: (i, 0)),
                  pl.BlockSpec((k, m), lambda i: (0, j0))],
        out_specs=pl.BlockSpec((tm, m), lambda i: (i, 0)),
        out_shape=jax.ShapeDtypeStruct((n, m), F32),
        compiler_params=_params("parallel"),
        name=name,
    )(h, w)


def _t5_bucket(rel):
    nb = NUM_BUCKETS // 2
    max_exact = nb // 2
    ret = (rel > 0).astype(jnp.int32) * nb
    n = jnp.abs(rel)
    nf = jnp.maximum(n, 1).astype(jnp.float32)
    large = max_exact + (jnp.log(nf / max_exact) / math.log(MAX_DISTANCE / max_exact)
                         * (nb - max_exact)).astype(jnp.int32)
    large = jnp.minimum(large, nb - 1)
    return ret + jnp.where(n < max_exact, n, large)


def _bias_rows(rel_bias):
    period = K_WIN + Q_SUB
    m = jnp.arange(period)
    delta = jnp.where(m < K_WIN, m, m - period)
    rows = []
    for d in DILATIONS:
        for off in (0, -HALF_WINDOW, -2 * HALF_WINDOW):
            rel = delta + off
            valid = jnp.abs(rel) <= HALF_WINDOW
            b = rel_bias[_t5_bucket(rel * d)].astype(F32)
            rows.append(jnp.where(valid[:, None], b, NEG_INF))
    return jnp.stack(rows, axis=0).transpose(2, 0, 1)


ATTN_UNROLL = 16


def _attn_kernel(q_ref, k_ref, v_ref, brow_ref, o_ref, bias_s, tmp_s, xq4, xt4,
                 q16, k1, k4, k16, v1, v4, v16, acc_s, max_s, den_s, *, seq):
    period = K_WIN + Q_SUB
    n4, n16 = seq // 4, seq // 16

    @pl.when(pl.program_id(1) == 0)
    def _():
        for idx in range(9):
            row = jnp.broadcast_to(brow_ref[idx:idx + 1, :], (Q_SUB, period))
            tile = pltpu.roll(row, 0, 1, stride=1, stride_axis=0)
            d = DILATIONS[idx // 3]
            if d == 16:
                bias_s[idx] = tile[:, :K_WIN]
                continue
            for half in range(K_WIN // LANES):
                tmp_s[half] = tile[:, half * LANES:(half + 1) * LANES]
            groups = 16 // d
            for half in range(K_WIN // LANES):
                for g in range(groups):
                    n = Q_SUB // groups
                    bias_s[idx, g * n:(g + 1) * n, half * LANES:(half + 1) * LANES] = (
                        tmp_s[half, pl.ds(g, n, stride=groups), :])

    cp = 256

    def split4(src, dst):
        for r4 in range(4):
            def body(c, carry, r4=r4):
                t0 = pl.multiple_of(c * cp, cp)
                dst[r4, pl.ds(t0, cp), :] = src[pl.ds(r4 + 4 * t0, cp, stride=4), :]
                return carry
            lax.fori_loop(0, n4 // cp, body, 0)

    def split16(src4, dst16, dst4=None):
        for r4 in range(4):
            for a in range(4):
                dst16[4 * a + r4] = src4[r4, pl.ds(a, n16, stride=4), :].astype(BF16)
            if dst4 is not None:
                def body(c, carry, r4=r4):
                    t0 = pl.multiple_of(c * cp, cp)
                    dst4[r4, pl.ds(t0, cp), :] = src4[r4, pl.ds(t0, cp), :].astype(BF16)
                    return carry
                lax.fori_loop(0, n4 // cp, body, 0)

    def cast(src, dst):
        def body(c, carry):
            t0 = pl.multiple_of(c * cp, cp)
            dst[pl.ds(t0, cp), :] = src[pl.ds(t0, cp), :].astype(BF16)
            return carry
        lax.fori_loop(0, seq // cp, body, 0)

    split4(q_ref, xq4)
    split16(xq4, q16)
    for src, d1, d4, d16 in ((k_ref, k1, k4, k16), (v_ref, v1, v4, v16)):
        cast(src, d1)
        split4(src, xt4)
        split16(xt4, d16, d4)

    ones = jnp.ones((K_WIN, LANES), BF16)

    def sub_tile(q, k_s, v_s, length, s0, pi):
        w0 = pl.multiple_of(jnp.clip(s0 - HALF_WINDOW, 0, length - K_WIN), HALF_WINDOW)
        place = jnp.where(s0 == 0, 0, jnp.where(s0 == length - Q_SUB, 2, 1))
        k = k_s[pl.ds(w0, K_WIN), :]
        v = v_s[pl.ds(w0, K_WIN), :]
        s = lax.dot_general(q, k, (((1,), (1,)), ((), ())), preferred_element_type=F32)
        s = s + bias_s[3 * pi + place]
        m = jnp.max(s, axis=-1, keepdims=True)
        p = jnp.exp(s - m).astype(BF16)
        pv = jnp.dot(p, jnp.concatenate([v, ones], axis=1), preferred_element_type=F32)
        return pv[:, :LANES], jnp.broadcast_to(m, (Q_SUB, LANES)), pv[:, LANES:]

    def store(pi, res, slab_of_group, l0, n):
        for g in range(Q_SUB // n):
            for val, dst in zip(res, (acc_s, max_s, den_s)):
                dst[pi, slab_of_group(g), pl.ds(l0, n), :] = val[g * n:(g + 1) * n]

    U = ATTN_UNROLL

    def body1(it, carry):
        res = []
        for u in range(U):
            l0 = pl.multiple_of((it * U + u) * 8, 8)
            q = jnp.concatenate([xq4[r % 4, pl.ds(4 * l0 + r // 4, 8, stride=4), :] for r in range(16)],
                                axis=0).astype(BF16)
            res.append((sub_tile(q, k1, v1, seq, pl.multiple_of(16 * l0, Q_SUB), 0), l0))
        for r3, l0 in res:
            store(0, r3, lambda g: g, l0, 8)
        return carry
    lax.fori_loop(0, seq // Q_SUB // U, body1, 0)

    def body4(it, carry):
        res = []
        for u in range(U // 4):
            l0 = pl.multiple_of((it * (U // 4) + u) * 32, 32)
            for r4 in range(4):
                q = jnp.concatenate([q16[4 * a + r4, pl.ds(l0, 32), :] for a in range(4)], axis=0)
                res.append((sub_tile(q, k4.at[r4], v4.at[r4], n4, pl.multiple_of(4 * l0, Q_SUB), 1),
                            r4, l0))
        for r3, r4, l0 in res:
            store(1, r3, lambda g, r4=r4: 4 * g + r4, l0, 32)
        return carry
    lax.fori_loop(0, n4 // Q_SUB // (U // 4), body4, 0)

    def body16(it, carry):
        g0 = (it // (n16 // Q_SUB)) * U
        l0 = pl.multiple_of((it % (n16 // Q_SUB)) * Q_SUB, Q_SUB)
        res = [sub_tile(q16[g0 + u, pl.ds(l0, Q_SUB), :], k16.at[g0 + u], v16.at[g0 + u], n16, l0, 2)
               for u in range(U)]
        for u, r3 in enumerate(res):
            store(2, r3, lambda g, u=u: g0 + u, l0, Q_SUB)
        return carry
    lax.fori_loop(0, (16 // U) * (n16 // Q_SUB), body16, 0)

    rows = 64
    for r in range(16):
        for pc in range(n16 // rows):
            idx = (r, pl.ds(pc * rows, rows))
            ma, mb, mc = max_s[(0,) + idx], max_s[(1,) + idx], max_s[(2,) + idx]
            mx = jnp.maximum(jnp.maximum(ma, mb), mc)
            ea, eb, ec = jnp.exp(ma - mx), jnp.exp(mb - mx), jnp.exp(mc - mx)
            num = ea * acc_s[(0,) + idx] + eb * acc_s[(1,) + idx] + ec * acc_s[(2,) + idx]
            den = ea * den_s[(0,) + idx] + eb * den_s[(1,) + idx] + ec * den_s[(2,) + idx]
            o_ref[pl.ds(r + 16 * pc * rows, rows, stride=16), :] = num / den


def _attention(qk, v, brow, batch, seq):
    H = ATTN_HEADS
    n4, n16 = seq // 4, seq // 16
    qk4 = qk.reshape(2 * H, batch, seq, HEAD_DIM)
    v4 = v.reshape(H, batch, seq, HEAD_DIM)
    in_specs = [pl.BlockSpec((None, None, seq, HEAD_DIM), lambda h, b: (h, b, 0, 0)),
                pl.BlockSpec((None, None, seq, HEAD_DIM), lambda h, b: (H + h, b, 0, 0)),
                pl.BlockSpec((None, None, seq, HEAD_DIM), lambda h, b: (h, b, 0, 0))]
    in_specs.append(pl.BlockSpec((None, 9, K_WIN + Q_SUB), lambda h, b: (h, 0, 0)))
    kv_slabs = [pltpu.VMEM((seq, LANES), BF16), pltpu.VMEM((4, n4, LANES), BF16),
                pltpu.VMEM((16, n16, LANES), BF16)]
    out = pl.pallas_call(
        functools.partial(_attn_kernel, seq=seq),
        grid=(H, batch),
        in_specs=in_specs,
        out_specs=pl.BlockSpec((None, None, seq, HEAD_DIM), lambda h, b: (h, b, 0, 0)),
        out_shape=jax.ShapeDtypeStruct((H, batch, seq, HEAD_DIM), F32),
        scratch_shapes=[pltpu.VMEM((9, Q_SUB, K_WIN), F32),
                        pltpu.VMEM((K_WIN // LANES, Q_SUB, LANES), F32),
                        pltpu.VMEM((4, n4, LANES), F32),
                        pltpu.VMEM((4, n4, LANES), F32),
                        pltpu.VMEM((16, n16, LANES), BF16)]
                       + kv_slabs + kv_slabs
                       + [pltpu.VMEM((3, 16, n16, LANES), F32)] * 3,
        compiler_params=_params("parallel", "arbitrary"),
        name="dilated_attention",
    )(qk4, qk4, v4, brow)
    return out.reshape(H, batch * seq, HEAD_DIM)


def _ssd_kernel(z_ref, x_ref, b_ref, c_ref, dtr_ref, cwx_ref, cwb_ref, cwc_ref,
                cbx_ref, cbb_ref, cbc_ref, dbr_ref, alr_ref, dsk_ref, ng_ref,
                o_ref, pad, cv, y_s, arg_s, dt_s, rows_s, cols_s, sbs, sf, sb, *, seq):
    T = SSD_CHUNK
    nc = seq // T
    hi = lax.Precision.HIGHEST
    halo = CONV_HALO
    half = T // 2

    pad[:, 0:halo, :] = jnp.zeros((4, halo, LANES), F32)
    pad[:, seq + halo:seq + 2 * halo, :] = jnp.zeros((4, halo, LANES), F32)

    def fill(i, carry):
        t0 = pl.multiple_of(i * T, T)
        xin = x_ref[pl.ds(t0, T), :].astype(F32)
        pad[0, pl.ds(t0 + halo, T), :] = xin[:, :LANES]
        pad[1, pl.ds(t0 + halo, T), :] = xin[:, LANES:]
        pad[2, pl.ds(t0 + halo, T), :] = b_ref[pl.ds(t0, T), :].astype(F32)
        pad[3, pl.ds(t0 + halo, T), :] = c_ref[pl.ds(t0, T), :].astype(F32)
        return carry
    lax.fori_loop(0, nc, fill, 0)

    cws = (cwx_ref[:, :LANES], cwx_ref[:, LANES:], cwb_ref[...], cwc_ref[...])
    cbs = (cbx_ref[:, :LANES], cbx_ref[:, LANES:], cbb_ref[...], cbc_ref[...])

    def conv_chunk(ci):
        t0 = pl.multiple_of(ci * T, T)
        for s in range(4):
            ev = [pad[s, pl.ds(t0 + halo + 2 * k, half, stride=2), :] for k in (-1, 0, 1)]
            od = [pad[s, pl.ds(t0 + halo + 1 + 2 * k, half, stride=2), :] for k in (-1, 0, 1)]
            w = [cws[s][j:j + 1, :] for j in range(CONV_WIDTH)]
            out_e = cbs[s] + w[0] * ev[0] + w[1] * od[0] + w[2] * ev[1] + w[3] * od[1] + w[4] * ev[2]
            out_o = cbs[s] + w[0] * od[0] + w[1] * ev[1] + w[2] * od[1] + w[3] * ev[2] + w[4] * od[2]
            cv[s, pl.ds(t0, half, stride=2), :] = out_e * _sigmoid(out_e)
            cv[s, pl.ds(t0 + 1, half, stride=2), :] = out_o * _sigmoid(out_o)

    def load_x(t0):
        return jnp.concatenate([cv[0, pl.ds(t0, T), :], cv[1, pl.ds(t0, T), :]], axis=1)

    nh = HEADS_PER_GROUP
    nd = 2 * nh
    row = lax.broadcasted_iota(jnp.int32, (T, T), 0)
    col = lax.broadcasted_iota(jnp.int32, (T, T), 1)
    triu = (row <= col).astype(F32)
    lower = col < row
    upper = col > row
    lane_head = lax.broadcasted_iota(jnp.int32, (T, GROUP_W), 1) // SSM_HEAD_DIM

    dt = _softplus(dtr_ref[...] + dbr_ref[...])
    a = dt * (-jnp.exp(alr_ref[...]))
    cum = jnp.dot(a.reshape(nc * nd, T), triu, precision=hi,
                  preferred_element_type=F32).reshape(nc, nd, T)
    last = cum[:, :, T - 1:T]
    exc = cum - a
    fwd = lax.broadcasted_iota(jnp.int32, (nc, nd, T), 1) < nh
    arg_s[...] = jnp.where(fwd, cum, exc)
    dt_s[...] = dt
    rows_s[:, 0:nd, :] = arg_s[...]
    rows_s[:, nd:2 * nd, :] = jnp.exp(jnp.where(fwd, cum, last - exc))
    rows_s[:, 2 * nd:3 * nd, :] = dt * jnp.exp(jnp.where(fwd, last - cum, exc))
    rows_s[:, 3 * nd:, :] = jnp.zeros((nc, LANES - 3 * nd, T), F32)

    def expand(cols, first):
        n = cols.shape[0]
        low = lax.broadcasted_iota(jnp.int32, (n, LANES), 1) < SSM_HEAD_DIM
        halves = []
        for j in range(GROUP_W // LANES):
            c0 = jnp.broadcast_to(cols[:, first + 2 * j:first + 2 * j + 1], (n, LANES))
            c1 = jnp.broadcast_to(cols[:, first + 2 * j + 1:first + 2 * j + 2], (n, LANES))
            halves.append(jnp.where(low, c0, c1))
        return jnp.concatenate(halves, axis=1)

    tn_dims = (((0,), (0,)), ((), ()))

    n_dec = nd * LANES
    cid = lax.broadcasted_iota(jnp.int32, (LANES, n_dec + 2 * GROUP_W), 0)
    lid = lax.broadcasted_iota(jnp.int32, (LANES, n_dec + 2 * GROUP_W), 1)
    want = jnp.where(lid < n_dec, nd + lid // LANES, 2 * nd + (lid - n_dec) // SSM_HEAD_DIM)
    spread = jnp.where(cid == want, 1.0, 0.0).astype(BF16)
    spread_f = spread[:, :n_dec + GROUP_W]
    spread_b = spread[:, n_dec + GROUP_W:]

    sb[...] = jnp.zeros_like(sb)

    def state_back(ci):
        t0 = pl.multiple_of(ci * T, T)
        ct = rows_s[ci].T
        cols_s[ci] = ct
        s_prev = sb[...]
        sbs[ci] = s_prev.astype(BF16)
        wx = jnp.dot(ct.astype(BF16), spread_b, preferred_element_type=F32)
        xs = (load_x(t0) * wx).astype(BF16)
        sb[...] = expand(ct[0:1, :], nd + nh) * s_prev + lax.dot_general(
            cv[2, pl.ds(t0, T), :].astype(BF16), xs, tn_dims, preferred_element_type=F32)

    U = SSD_UNROLL
    for u in range(U):
        conv_chunk(nc - 1 - u)

    def sweep_back(i, carry):
        k = nc - 1 - U * i
        for u in range(U):
            state_back(k + U - u)
        for u in range(U):
            conv_chunk(k - u)
        return carry
    lax.fori_loop(1, nc // U, sweep_back, 0)
    for u in range(U):
        state_back(U - 1 - u)

    sf[...] = jnp.zeros_like(sf)
    low_half = lax.broadcasted_iota(jnp.int32, (T, LANES), 1) < SSM_HEAD_DIM

    def finish(ci):
        t0 = pl.multiple_of(ci * T, T)
        zz = z_ref[pl.ds(t0, T), :].astype(F32)
        y = y_s[pl.ds(t0, T), :] * (zz * _sigmoid(zz))
        ms = jnp.mean(y * y, axis=-1, keepdims=True)
        o_ref[pl.ds(t0, T), :] = (y * lax.rsqrt(ms + EPS) * ng_ref[...]).astype(o_ref.dtype)

    def chunk_fwd(ci):
        t0 = pl.multiple_of(ci * T, T)
        ct = cols_s[ci]
        arg_r = arg_s[ci]
        dt_r = dt_s[ci]
        x = load_x(t0)
        bk = cv[2, pl.ds(t0, T), :].astype(BF16)
        ckf = cv[3, pl.ds(t0, T), :]
        ck = ckf.astype(BF16)
        cb = lax.dot_general(ck, bk, (((1,), (1,)), ((), ())), preferred_element_type=F32)
        spreadv = jnp.dot(ct.astype(BF16), spread_f, preferred_element_type=F32)
        lhs = []
        for h in range(nh):
            hb = nh + h
            arg = jnp.where(lower, ct[:, h:h + 1] - arg_r[h:h + 1, :], arg_r[hb:hb + 1, :] - ct[:, hb:hb + 1])
            wgt = jnp.where(lower, dt_r[h:h + 1, :],
                            jnp.where(upper, dt_r[hb:hb + 1, :], dt_r[h:h + 1, :] + dt_r[hb:hb + 1, :]))
            lhs.append(jnp.concatenate(
                [(cb * jnp.exp(arg) * wgt).astype(BF16),
                 (ckf * spreadv[:, h * LANES:(h + 1) * LANES]).astype(BF16),
                 (ckf * spreadv[:, hb * LANES:(hb + 1) * LANES]).astype(BF16)], axis=1))
        s_prev = sf[...]
        rhs = jnp.concatenate([x.astype(BF16), s_prev.astype(BF16), sbs[ci]], axis=0)
        y4 = jnp.dot(jnp.concatenate(lhs, axis=0), rhs, preferred_element_type=F32)
        y = x * dsk_ref[...] + jnp.concatenate(
            [jnp.where(low_half, y4[2 * j * T:(2 * j + 1) * T, j * LANES:(j + 1) * LANES],
                       y4[(2 * j + 1) * T:(2 * j + 2) * T, j * LANES:(j + 1) * LANES])
             for j in range(GROUP_W // LANES)], axis=1)
        xs = (x * spreadv[:, n_dec:]).astype(BF16)
        sf[...] = expand(ct[T - 1:T, :], nd) * s_prev + lax.dot_general(
            bk, xs, tn_dims, preferred_element_type=F32)
        y_s[pl.ds(t0, T), :] = y

    U = SSD_UNROLL_FWD
    for u in range(U):
        chunk_fwd(u)

    def sweep_fwd(i, carry):
        c = U * i
        for u in range(U):
            finish(c - U + u)
        for u in range(U):
            chunk_fwd(c + u)
        return carry
    lax.fori_loop(1, nc // U, sweep_fwd, 0)
    for u in range(U):
        finish(nc - U + u)


def _ssd(zx, bcm, dt_row, cwx, cwb, cwc, cbx, cbb, cbc, db_row, al_row, dskip, ng, batch, seq):
    G = SSM_GROUPS
    nc = seq // SSD_CHUNK
    zx4 = zx.reshape(2 * G, batch, seq, GROUP_W)
    bc4 = bcm.reshape(2 * G, batch, seq, SSM_STATE)
    nd = 2 * HEADS_PER_GROUP

    def per_group(shape):
        return pl.BlockSpec((None,) + shape, lambda b, g: (g,) + (0,) * len(shape))

    in_specs = [
        pl.BlockSpec((None, None, seq, GROUP_W), lambda b, g: (g, b, 0, 0)),
        pl.BlockSpec((None, None, seq, GROUP_W), lambda b, g: (G + g, b, 0, 0)),
        pl.BlockSpec((None, None, seq, SSM_STATE), lambda b, g: (g, b, 0, 0)),
        pl.BlockSpec((None, None, seq, SSM_STATE), lambda b, g: (G + g, b, 0, 0)),
        pl.BlockSpec((None, None, nc, nd, SSD_CHUNK), lambda b, g: (b, g, 0, 0, 0)),
        per_group((CONV_WIDTH, GROUP_W)), per_group((CONV_WIDTH, SSM_STATE)),
        per_group((CONV_WIDTH, SSM_STATE)),
        per_group((1, GROUP_W)), per_group((1, SSM_STATE)), per_group((1, SSM_STATE)),
        per_group((nd, 1)), per_group((nd, 1)),
        per_group((1, GROUP_W)), per_group((1, GROUP_W)),
    ]
    out = pl.pallas_call(
        functools.partial(_ssd_kernel, seq=seq),
        grid=(batch, G),
        in_specs=in_specs,
        out_specs=pl.BlockSpec((None, None, seq, GROUP_W), lambda b, g: (g, b, 0, 0)),
        out_shape=jax.ShapeDtypeStruct((G, batch, seq, GROUP_W), BF16),
        scratch_shapes=[pltpu.VMEM((4, seq + 2 * CONV_HALO, LANES), F32),
                        pltpu.VMEM((4, seq, LANES), F32),
                        pltpu.VMEM((seq, GROUP_W), F32),
                        pltpu.VMEM((nc, nd, SSD_CHUNK), F32),
                        pltpu.VMEM((nc, nd, SSD_CHUNK), F32),
                        pltpu.VMEM((nc, LANES, SSD_CHUNK), F32),
                        pltpu.VMEM((nc, SSD_CHUNK, LANES), F32),
                        pltpu.VMEM((nc, SSM_STATE, GROUP_W), BF16),
                        pltpu.VMEM((SSM_STATE, GROUP_W), F32),
                        pltpu.VMEM((SSM_STATE, GROUP_W), F32)],
        compiler_params=_params("parallel", "parallel"),
        name="ssd",
    )(zx4, zx4, bc4, bc4, dt_row, cwx, cwb, cwc, cbx, cbb, cbc, db_row, al_row, dskip, ng)
    return out.reshape(G, batch * seq, GROUP_W)


def _out_proj_kernel(attn_ref, ssd_ref, w_ref, x_ref, o_ref, lhs):
    for h in range(ATTN_HEADS):
        lhs[:, h * HEAD_DIM:(h + 1) * HEAD_DIM] = attn_ref[h].astype(BF16)
    for g in range(SSM_GROUPS):
        lhs[:, ATTN_W + g * GROUP_W:ATTN_W + (g + 1) * GROUP_W] = ssd_ref[g]
    o_ref[...] = x_ref[...] + jnp.dot(lhs[...], w_ref[...], preferred_element_type=F32)


def _out_proj(attn, ssd, w, x2d, tm=256):
    n, d = x2d.shape
    kk = w.shape[0]
    return pl.pallas_call(
        _out_proj_kernel,
        grid=(n // tm,),
        in_specs=[pl.BlockSpec((ATTN_HEADS, tm, HEAD_DIM), lambda i: (0, i, 0)),
                  pl.BlockSpec((SSM_GROUPS, tm, GROUP_W), lambda i: (0, i, 0)),
                  pl.BlockSpec((kk, d), lambda i: (0, 0), pipeline_mode=pl.Buffered(1)),
                  pl.BlockSpec((tm, d), lambda i: (i, 0))],
        out_specs=pl.BlockSpec((tm, d), lambda i: (i, 0)),
        out_shape=jax.ShapeDtypeStruct((n, d), F32),
        scratch_shapes=[pltpu.VMEM((tm, kk), BF16)],
        compiler_params=_params("parallel"),
        name="out_proj",
    )(attn, ssd, w, x2d)


def _mlp_kernel(x_ref, g_ref, wu_ref, wd_ref, o_ref, hm):
    @pl.when(pl.program_id(1) == 0)
    def _():
        x = x_ref[...]
        ms = jnp.mean(x * x, axis=-1, keepdims=True)
        hm[...] = (x * lax.rsqrt(ms + EPS) * g_ref[...]).astype(hm.dtype)
        o_ref[...] = x
    u = jnp.maximum(jnp.dot(hm[...], wu_ref[...], preferred_element_type=F32), 0.0)
    o_ref[...] += jnp.dot((u * u).astype(BF16), wd_ref[...], preferred_element_type=F32)


def _mlp(x2d, g, wu, wd, tm=512, tf=1024):
    n, d = x2d.shape
    f = wu.shape[1]
    return pl.pallas_call(
        _mlp_kernel,
        grid=(n // tm, f // tf),
        in_specs=[pl.BlockSpec((tm, d), lambda i, j: (i, 0)),
                  pl.BlockSpec((1, d), lambda i, j: (0, 0)),
                  pl.BlockSpec((d, tf), lambda i, j: (0, j)),
                  pl.BlockSpec((tf, d), lambda i, j: (j, 0))],
        out_specs=pl.BlockSpec((tm, d), lambda i, j: (i, 0)),
        out_shape=jax.ShapeDtypeStruct((n, d), F32),
        scratch_shapes=[pltpu.VMEM((tm, d), BF16)],
        compiler_params=_params("parallel", "arbitrary"),
        name="mlp",
    )(x2d, g.reshape(1, d), wu, wd)


def kernel(x, norm_mix_g, w_in, q_norm_g, k_norm_g, rel_bias, conv_w, conv_b, dt_bias, a_log,
           d_skip, ssd_norm_g, w_out, norm_mlp_g, w_up, w_down):
    batch, seq, _ = x.shape
    n = batch * seq
    G, nh = SSM_GROUPS, HEADS_PER_GROUP
    nc = seq // SSD_CHUNK
    o_z = 3 * ATTN_W
    o_bc = o_z + 2 * SSM_W
    o_dt = o_bc + 2 * G * SSM_STATE
    x2d = x.reshape(n, D_MODEL)
    brow = _bias_rows(rel_bias)

    for layer in range(w_in.shape[0]):
        wi = w_in[layer]
        h = _rmsnorm(x2d, norm_mix_g[layer])

        scale = 1.0 / math.sqrt(HEAD_DIM)
        gains = jnp.concatenate([jnp.tile(q_norm_g[layer].astype(F32) * scale, ATTN_HEADS),
                                 jnp.tile(k_norm_g[layer].astype(F32), ATTN_HEADS)]).reshape(1, 2 * ATTN_W)
        qk = _proj_qk(h, wi, gains)
        v = _proj_split(h, wi, 2 * ATTN_W, ATTN_W, HEAD_DIM, "proj_v", out_dtype=F32)
        zx = _proj_split(h, wi, o_z, o_bc - o_z, GROUP_W, "proj_zx")
        bcm = _proj_split(h, wi, o_bc, o_dt - o_bc, SSM_STATE, "proj_bc")
        dt_raw = _proj_plain(h, w_in[layer][:, o_dt:].astype(BF16), 0, 2 * SSM_HEADS, "proj_dt")

        attn = _attention(qk, v, brow, batch, seq)

        dt_row = (dt_raw.reshape(batch, nc, SSD_CHUNK, 2, G, nh)
                  .transpose(0, 4, 1, 3, 5, 2).reshape(batch, G, nc, 2 * nh, SSD_CHUNK))
        per_dir = lambda t: t.astype(F32).reshape(2, G, nh).transpose(1, 0, 2).reshape(G, 2 * nh)
        db, al = per_dir(dt_bias[layer]), per_dir(a_log[layer])
        cw, cbias = conv_w[layer].astype(F32), conv_b[layer].astype(F32)
        gn = G * SSM_STATE
        grp = lambda t, width: t.reshape(t.shape[0], G, width).transpose(1, 0, 2)
        ssd = _ssd(
            zx, bcm, dt_row,
            grp(cw[:, :SSM_W], GROUP_W), grp(cw[:, SSM_W:SSM_W + gn], SSM_STATE),
            grp(cw[:, SSM_W + gn:], SSM_STATE),
            grp(cbias[None, :SSM_W], GROUP_W), grp(cbias[None, SSM_W:SSM_W + gn], SSM_STATE),
            grp(cbias[None, SSM_W + gn:], SSM_STATE),
            db.reshape(G, 2 * nh, 1), al.reshape(G, 2 * nh, 1),
            jnp.repeat(d_skip[layer].astype(F32), SSM_HEAD_DIM).reshape(G, 1, GROUP_W),
            ssd_norm_g[layer].astype(F32).reshape(G, 1, GROUP_W),
            batch, seq)

        x2d = _out_proj(attn, ssd, w_out[layer].astype(BF16), x2d)
        x2d = _mlp(x2d, norm_mlp_g[layer], w_up[layer].astype(BF16), w_down[layer].astype(BF16))
    return x2d.reshape(batch, seq, D_MODEL)
```

```python
import functools
import math

import jax
import jax.numpy as jnp
from jax import lax
from jax.experimental import pallas as pl
from jax.experimental.pallas import tpu as pltpu

D_MODEL = 2048
ATTN_HEADS = 16
HEAD_DIM = 128
ATTN_W = ATTN_HEADS * HEAD_DIM
SSM_HEADS = 32
SSM_HEAD_DIM = 64
SSM_W = SSM_HEADS * SSM_HEAD_DIM
SSM_GROUPS = 8
HEADS_PER_GROUP = SSM_HEADS // SSM_GROUPS
GROUP_W = SSM_W // SSM_GROUPS
SSM_STATE = 128
CONV_WIDTH = 5
D_FF = 4 * D_MODEL
DILATIONS = (1, 4, 16)
HALF_WINDOW = 64
NUM_BUCKETS = 32
MAX_DISTANCE = 1024
NEG_INF = -1e30
EPS = 1e-6

LANES = 128
Q_SUB = 128
K_WIN = 256
SSD_CHUNK = 128
SSD_UNROLL = 4
SSD_UNROLL_FWD = 2
CONV_HALO = 8
VMEM_LIMIT = 56 * 1024 * 1024

F32 = jnp.float32
BF16 = jnp.bfloat16


def _params(*sem):
    return pltpu.CompilerParams(dimension_semantics=sem, vmem_limit_bytes=VMEM_LIMIT)


def _sigmoid(x):
    return 1.0 / (1.0 + jnp.exp(-x))


def _softplus(x):
    return jnp.maximum(x, 0.0) + jnp.log1p(jnp.exp(-jnp.abs(x)))


def _rmsnorm_kernel(x_ref, g_ref, o_ref):
    x = x_ref[...]
    ms = jnp.mean(x * x, axis=-1, keepdims=True)
    o_ref[...] = (x * lax.rsqrt(ms + EPS) * g_ref[...]).astype(o_ref.dtype)


def _rmsnorm(x2d, g, tm=512):
    n, d = x2d.shape
    return pl.pallas_call(
        _rmsnorm_kernel,
        grid=(n // tm,),
        in_specs=[pl.BlockSpec((tm, d), lambda i: (i, 0)),
                  pl.BlockSpec((1, d), lambda i: (0, 0))],
        out_specs=pl.BlockSpec((tm, d), lambda i: (i, 0)),
        out_shape=jax.ShapeDtypeStruct((n, d), BF16),
        compiler_params=_params("parallel"),
        name="rmsnorm",
    )(x2d, g.reshape(1, d))


QK_SUB = 256


def _cast_weight_tile(w_ref, wb):
    @pl.when(pl.program_id(1) == 0)
    def _():
        wb[...] = w_ref[...].astype(wb.dtype)


def _proj_qk_kernel(a_ref, w_ref, g_ref, o_ref, wb, *, heads_per_tile):
    _cast_weight_tile(w_ref, wb)
    a = a_ref[...]
    per = QK_SUB // HEAD_DIM
    for c in range(heads_per_tile // per):
        acc = jnp.dot(a, wb[:, c * QK_SUB:(c + 1) * QK_SUB], preferred_element_type=F32)
        for hh in range(per):
            h = c * per + hh
            s = acc[:, hh * HEAD_DIM:(hh + 1) * HEAD_DIM]
            ms = jnp.mean(s * s, axis=-1, keepdims=True)
            g = g_ref[:, h * HEAD_DIM:(h + 1) * HEAD_DIM]
            o_ref[h] = (s * lax.rsqrt(ms + EPS) * g).astype(o_ref.dtype)


def _proj_split_kernel(a_ref, w_ref, o_ref, wb, *, width):
    _cast_weight_tile(w_ref, wb)
    a = a_ref[...]
    per = QK_SUB // width
    for s in range(wb.shape[1] // QK_SUB):
        acc = jnp.dot(a, wb[:, s * QK_SUB:(s + 1) * QK_SUB], preferred_element_type=F32)
        for c in range(per):
            o_ref[s * per + c] = acc[:, c * width:(c + 1) * width].astype(o_ref.dtype)


def _proj_plain_kernel(a_ref, w_ref, o_ref):
    o_ref[...] = jnp.dot(a_ref[...], w_ref[...], preferred_element_type=F32).astype(o_ref.dtype)


def _proj_qk(h, w, layer, gains, tm=1024, tn=1024):
    n, k = h.shape
    m = 2 * ATTN_W
    hpt = tn // HEAD_DIM
    return pl.pallas_call(
        functools.partial(_proj_qk_kernel, heads_per_tile=hpt),
        grid=(m // tn, n // tm),
        in_specs=[pl.BlockSpec((tm, k), lambda j, i: (i, 0)),
                  pl.BlockSpec((None, k, tn), lambda j, i: (layer, 0, j)),
                  pl.BlockSpec((1, tn), lambda j, i: (0, j))],
        out_specs=pl.BlockSpec((hpt, tm, HEAD_DIM), lambda j, i: (j, i, 0)),
        out_shape=jax.ShapeDtypeStruct((m // HEAD_DIM, n, HEAD_DIM), F32),
        scratch_shapes=[pltpu.VMEM((k, tn), BF16)],
        compiler_params=_params("parallel", "arbitrary"),
        name="proj_qk",
    )(h, w, gains)


def _proj_split(h, w, layer, col0, m, width, name, out_dtype=BF16, tm=1024, tn=1024):
    n, k = h.shape
    cpt = tn // width
    j0 = col0 // tn
    return pl.pallas_call(
        functools.partial(_proj_split_kernel, width=width),
        grid=(m // tn, n // tm),
        in_specs=[pl.BlockSpec((tm, k), lambda j, i: (i, 0)),
                  pl.BlockSpec((None, k, tn), lambda j, i: (layer, 0, j0 + j))],
        out_specs=pl.BlockSpec((cpt, tm, width), lambda j, i: (j, i, 0)),
        out_shape=jax.ShapeDtypeStruct((m // width, n, width), out_dtype),
        scratch_shapes=[pltpu.VMEM((k, tn), BF16)],
        compiler_params=_params("parallel", "arbitrary"),
        name=name,
    )(h, w)


def _proj_plain(h, w, col0, m, name, tm=1024):
    n, k = h.shape
    j0 = col0 // m
    return pl.pallas_call(
        _proj_plain_kernel,
        grid=(n // tm,),
        in_specs=[pl.BlockSpec((tm, k), lambda i: (i, 0)),
                  pl.BlockSpec((k, m), lambda i: (0, j0))],
        out_specs=pl.BlockSpec((tm, m), lambda i: (i, 0)),
        out_shape=jax.ShapeDtypeStruct((n, m), F32),
        compiler_params=_params("parallel"),
        name=name,
    )(h, w)


def _t5_bucket(rel):
    nb = NUM_BUCKETS // 2
    max_exact = nb // 2
    ret = (rel > 0).astype(jnp.int32) * nb
    n = jnp.abs(rel)
    nf = jnp.maximum(n, 1).astype(jnp.float32)
    large = max_exact + (jnp.log(nf / max_exact) / math.log(MAX_DISTANCE / max_exact)
                         * (nb - max_exact)).astype(jnp.int32)
    large = jnp.minimum(large, nb - 1)
    return ret + jnp.where(n < max_exact, n, large)


def _bias_rows(rel_bias):
    period = K_WIN + Q_SUB
    m = jnp.arange(period)
    delta = jnp.where(m < K_WIN, m, m - period)
    rows = []
    for d in DILATIONS:
        for off in (0, -HALF_WINDOW, -2 * HALF_WINDOW):
            rel = delta + off
            valid = jnp.abs(rel) <= HALF_WINDOW
            b = rel_bias[_t5_bucket(rel * d)].astype(F32)
            rows.append(jnp.where(valid[:, None], b, NEG_INF))
    return jnp.stack(rows, axis=0).transpose(2, 0, 1)


ATTN_UNROLL = 16


def _attn_kernel(q_ref, k_ref, v_ref, brow_ref, o_ref, bias_s, tmp_s, xq4, xt4,
                 q16, k1, k4, k16, v1, v4, v16, acc_s, max_s, den_s, *, seq):
    period = K_WIN + Q_SUB
    n4, n16 = seq // 4, seq // 16

    @pl.when(pl.program_id(1) == 0)
    def _():
        for idx in range(9):
            row = jnp.broadcast_to(brow_ref[idx:idx + 1, :], (Q_SUB, period))
            tile = pltpu.roll(row, 0, 1, stride=1, stride_axis=0)
            d = DILATIONS[idx // 3]
            if d == 16:
                bias_s[idx] = tile[:, :K_WIN]
                continue
            for half in range(K_WIN // LANES):
                tmp_s[half] = tile[:, half * LANES:(half + 1) * LANES]
            groups = 16 // d
            for half in range(K_WIN // LANES):
                for g in range(groups):
                    n = Q_SUB // groups
                    bias_s[idx, g * n:(g + 1) * n, half * LANES:(half + 1) * LANES] = (
                        tmp_s[half, pl.ds(g, n, stride=groups), :])

    cp = 256

    def split4(src, dst):
        for r4 in range(4):
            def body(c, carry, r4=r4):
                t0 = pl.multiple_of(c * cp, cp)
                dst[r4, pl.ds(t0, cp), :] = src[pl.ds(r4 + 4 * t0, cp, stride=4), :]
                return carry
            lax.fori_loop(0, n4 // cp, body, 0)

    def split16(src4, dst16, dst4=None):
        for r4 in range(4):
            for a in range(4):
                dst16[4 * a + r4] = src4[r4, pl.ds(a, n16, stride=4), :].astype(BF16)
            if dst4 is not None:
                def body(c, carry, r4=r4):
                    t0 = pl.multiple_of(c * cp, cp)
                    dst4[r4, pl.ds(t0, cp), :] = src4[r4, pl.ds(t0, cp), :].astype(BF16)
                    return carry
                lax.fori_loop(0, n4 // cp, body, 0)

    def cast(src, dst):
        def body(c, carry):
            t0 = pl.multiple_of(c * cp, cp)
            dst[pl.ds(t0, cp), :] = src[pl.ds(t0, cp), :].astype(BF16)
            return carry
        lax.fori_loop(0, seq // cp, body, 0)

    split4(q_ref, xq4)
    split16(xq4, q16)
    for src, d1, d4, d16 in ((k_ref, k1, k4, k16), (v_ref, v1, v4, v16)):
        cast(src, d1)
        split4(src, xt4)
        split16(xt4, d16, d4)

    ones = jnp.ones((K_WIN, LANES), BF16)

    def sub_tile(q, k_s, v_s, length, s0, pi):
        w0 = pl.multiple_of(jnp.clip(s0 - HALF_WINDOW, 0, length - K_WIN), HALF_WINDOW)
        place = jnp.where(s0 == 0, 0, jnp.where(s0 == length - Q_SUB, 2, 1))
        k = k_s[pl.ds(w0, K_WIN), :]
        v = v_s[pl.ds(w0, K_WIN), :]
        s = lax.dot_general(q, k, (((1,), (1,)), ((), ())), preferred_element_type=F32)
        s = s + bias_s[3 * pi + place]
        m = jnp.max(s, axis=-1, keepdims=True)
        p = jnp.exp(s - m).astype(BF16)
        pv = jnp.dot(p, jnp.concatenate([v, ones], axis=1), preferred_element_type=F32)
        return pv[:, :LANES], jnp.broadcast_to(m, (Q_SUB, LANES)), pv[:, LANES:]

    def store(pi, res, slab_of_group, l0, n):
        for g in range(Q_SUB // n):
            for val, dst in zip(res, (acc_s, max_s, den_s)):
                dst[pi, slab_of_group(g), pl.ds(l0, n), :] = val[g * n:(g + 1) * n]

    U = ATTN_UNROLL

    def body1(it, carry):
        res = []
        for u in range(U):
            l0 = pl.multiple_of((it * U + u) * 8, 8)
            q = jnp.concatenate([xq4[r % 4, pl.ds(4 * l0 + r // 4, 8, stride=4), :] for r in range(16)],
                                axis=0).astype(BF16)
            res.append((sub_tile(q, k1, v1, seq, pl.multiple_of(16 * l0, Q_SUB), 0), l0))
        for r3, l0 in res:
            store(0, r3, lambda g: g, l0, 8)
        return carry
    lax.fori_loop(0, seq // Q_SUB // U, body1, 0)

    def body4(it, carry):
        res = []
        for u in range(U // 4):
            l0 = pl.multiple_of((it * (U // 4) + u) * 32, 32)
            for r4 in range(4):
                q = jnp.concatenate([q16[4 * a + r4, pl.ds(l0, 32), :] for a in range(4)], axis=0)
                res.append((sub_tile(q, k4.at[r4], v4.at[r4], n4, pl.multiple_of(4 * l0, Q_SUB), 1),
                            r4, l0))
        for r3, r4, l0 in res:
            store(1, r3, lambda g, r4=r4: 4 * g + r4, l0, 32)
        return carry
    lax.fori_loop(0, n4 // Q_SUB // (U // 4), body4, 0)

    def body16(it, carry):
        g0 = (it // (n16 // Q_SUB)) * U
        l0 = pl.multiple_of((it % (n16 // Q_SUB)) * Q_SUB, Q_SUB)
        res = [sub_tile(q16[g0 + u, pl.ds(l0, Q_SUB), :], k16.at[g0 + u], v16.at[g0 + u], n16, l0, 2)
               for u in range(U)]
        for u, r3 in enumerate(res):
            store(2, r3, lambda g, u=u: g0 + u, l0, Q_SUB)
        return carry
    lax.fori_loop(0, (16 // U) * (n16 // Q_SUB), body16, 0)

    rows = 64
    for r in range(16):
        for pc in range(n16 // rows):
            idx = (r, pl.ds(pc * rows, rows))
            ma, mb, mc = max_s[(0,) + idx], max_s[(1,) + idx], max_s[(2,) + idx]
            mx = jnp.maximum(jnp.maximum(ma, mb), mc)
            ea, eb, ec = jnp.exp(ma - mx), jnp.exp(mb - mx), jnp.exp(mc - mx)
            num = ea * acc_s[(0,) + idx] + eb * acc_s[(1,) + idx] + ec * acc_s[(2,) + idx]
            den = ea * den_s[(0,) + idx] + eb * den_s[(1,) + idx] + ec * den_s[(2,) + idx]
            o_ref[pl.ds(r + 16 * pc * rows, rows, stride=16), :] = num / den


def _attention(qk, v, brow, batch, seq):
    H = ATTN_HEADS
    n4, n16 = seq // 4, seq // 16
    qk4 = qk.reshape(2 * H, batch, seq, HEAD_DIM)
    v4 = v.reshape(H, batch, seq, HEAD_DIM)
    in_specs = [pl.BlockSpec((None, None, seq, HEAD_DIM), lambda h, b: (h, b, 0, 0)),
                pl.BlockSpec((None, None, seq, HEAD_DIM), lambda h, b: (H + h, b, 0, 0)),
                pl.BlockSpec((None, None, seq, HEAD_DIM), lambda h, b: (h, b, 0, 0))]
    in_specs.append(pl.BlockSpec((None, 9, K_WIN + Q_SUB), lambda h, b: (h, 0, 0)))
    kv_slabs = [pltpu.VMEM((seq, LANES), BF16), pltpu.VMEM((4, n4, LANES), BF16),
                pltpu.VMEM((16, n16, LANES), BF16)]
    out = pl.pallas_call(
        functools.partial(_attn_kernel, seq=seq),
        grid=(H, batch),
        in_specs=in_specs,
        out_specs=pl.BlockSpec((None, None, seq, HEAD_DIM), lambda h, b: (h, b, 0, 0)),
        out_shape=jax.ShapeDtypeStruct((H, batch, seq, HEAD_DIM), F32),
        scratch_shapes=[pltpu.VMEM((9, Q_SUB, K_WIN), F32),
                        pltpu.VMEM((K_WIN // LANES, Q_SUB, LANES), F32),
                        pltpu.VMEM((4, n4, LANES), F32),
                        pltpu.VMEM((4, n4, LANES), F32),
                        pltpu.VMEM((16, n16, LANES), BF16)]
                       + kv_slabs + kv_slabs
                       + [pltpu.VMEM((3, 16, n16, LANES), F32)] * 3,
        compiler_params=_params("parallel", "arbitrary"),
        name="dilated_attention",
    )(qk4, qk4, v4, brow)
    return out.reshape(H, batch * seq, HEAD_DIM)


def _ssd_kernel(z_ref, x_ref, b_ref, c_ref, dtr_ref, cwx_ref, cwb_ref, cwc_ref,
                cbx_ref, cbb_ref, cbc_ref, dbr_ref, alr_ref, dsk_ref, ng_ref,
                o_ref, pad, cv, y_s, arg_s, dt_s, rows_s, cols_s, sbs, sf, sb, *, seq):
    T = SSD_CHUNK
    nc = seq // T
    hi = lax.Precision.HIGHEST
    halo = CONV_HALO
    half = T // 2

    pad[:, 0:halo, :] = jnp.zeros((4, halo, LANES), F32)
    pad[:, seq + halo:seq + 2 * halo, :] = jnp.zeros((4, halo, LANES), F32)

    def fill(i, carry):
        t0 = pl.multiple_of(i * T, T)
        xin = x_ref[pl.ds(t0, T), :].astype(F32)
        pad[0, pl.ds(t0 + halo, T), :] = xin[:, :LANES]
        pad[1, pl.ds(t0 + halo, T), :] = xin[:, LANES:]
        pad[2, pl.ds(t0 + halo, T), :] = b_ref[pl.ds(t0, T), :].astype(F32)
        pad[3, pl.ds(t0 + halo, T), :] = c_ref[pl.ds(t0, T), :].astype(F32)
        return carry
    lax.fori_loop(0, nc, fill, 0)

    cws = (cwx_ref[:, :LANES], cwx_ref[:, LANES:], cwb_ref[...], cwc_ref[...])
    cbs = (cbx_ref[:, :LANES], cbx_ref[:, LANES:], cbb_ref[...], cbc_ref[...])

    def conv_chunk(ci):
        t0 = pl.multiple_of(ci * T, T)
        for s in range(4):
            ev = [pad[s, pl.ds(t0 + halo + 2 * k, half, stride=2), :] for k in (-1, 0, 1)]
            od = [pad[s, pl.ds(t0 + halo + 1 + 2 * k, half, stride=2), :] for k in (-1, 0, 1)]
            w = [cws[s][j:j + 1, :] for j in range(CONV_WIDTH)]
            out_e = cbs[s] + w[0] * ev[0] + w[1] * od[0] + w[2] * ev[1] + w[3] * od[1] + w[4] * ev[2]
            out_o = cbs[s] + w[0] * od[0] + w[1] * ev[1] + w[2] * od[1] + w[3] * ev[2] + w[4] * od[2]
            cv[s, pl.ds(t0, half, stride=2), :] = out_e * _sigmoid(out_e)
            cv[s, pl.ds(t0 + 1, half, stride=2), :] = out_o * _sigmoid(out_o)

    def load_x(t0):
        return jnp.concatenate([cv[0, pl.ds(t0, T), :], cv[1, pl.ds(t0, T), :]], axis=1)

    nh = HEADS_PER_GROUP
    nd = 2 * nh
    row = lax.broadcasted_iota(jnp.int32, (T, T), 0)
    col = lax.broadcasted_iota(jnp.int32, (T, T), 1)
    triu = (row <= col).astype(F32)
    lower = col < row
    upper = col > row
    lane_head = lax.broadcasted_iota(jnp.int32, (T, GROUP_W), 1) // SSM_HEAD_DIM

    dt = _softplus(dtr_ref[...] + dbr_ref[...])
    a = dt * (-jnp.exp(alr_ref[...]))
    cum = jnp.dot(a.reshape(nc * nd, T), triu, precision=hi,
                  preferred_element_type=F32).reshape(nc, nd, T)
    last = cum[:, :, T - 1:T]
    exc = cum - a
    fwd = lax.broadcasted_iota(jnp.int32, (nc, nd, T), 1) < nh
    arg_s[...] = jnp.where(fwd, cum, exc)
    dt_s[...] = dt
    rows_s[:, 0:nd, :] = arg_s[...]
    rows_s[:, nd:2 * nd, :] = jnp.exp(jnp.where(fwd, cum, last - exc))
    rows_s[:, 2 * nd:3 * nd, :] = dt * jnp.exp(jnp.where(fwd, last - cum, exc))
    rows_s[:, 3 * nd:, :] = jnp.zeros((nc, LANES - 3 * nd, T), F32)

    def expand(cols, first):
        n = cols.shape[0]
        low = lax.broadcasted_iota(jnp.int32, (n, LANES), 1) < SSM_HEAD_DIM
        halves = []
        for j in range(GROUP_W // LANES):
            c0 = jnp.broadcast_to(cols[:, first + 2 * j:first + 2 * j + 1], (n, LANES))
            c1 = jnp.broadcast_to(cols[:, first + 2 * j + 1:first + 2 * j + 2], (n, LANES))
            halves.append(jnp.where(low, c0, c1))
        return jnp.concatenate(halves, axis=1)

    tn_dims = (((0,), (0,)), ((), ()))

    n_dec = nd * LANES
    cid = lax.broadcasted_iota(jnp.int32, (LANES, n_dec + 2 * GROUP_W), 0)
    lid = lax.broadcasted_iota(jnp.int32, (LANES, n_dec + 2 * GROUP_W), 1)
    want = jnp.where(lid < n_dec, nd + lid // LANES, 2 * nd + (lid - n_dec) // SSM_HEAD_DIM)
    spread = jnp.where(cid == want, 1.0, 0.0).astype(BF16)
    spread_f = spread[:, :n_dec + GROUP_W]
    spread_b = spread[:, n_dec + GROUP_W:]

    sb[...] = jnp.zeros_like(sb)

    def state_back(ci):
        t0 = pl.multiple_of(ci * T, T)
        ct = rows_s[ci].T
        cols_s[ci] = ct
        s_prev = sb[...]
        sbs[ci] = s_prev.astype(BF16)
        wx = jnp.dot(ct.astype(BF16), spread_b, preferred_element_type=F32)
        xs = (load_x(t0) * wx).astype(BF16)
        sb[...] = expand(ct[0:1, :], nd + nh) * s_prev + lax.dot_general(
            cv[2, pl.ds(t0, T), :].astype(BF16), xs, tn_dims, preferred_element_type=F32)

    U = SSD_UNROLL
    for u in range(U):
        conv_chunk(nc - 1 - u)

    def sweep_back(i, carry):
        k = nc - 1 - U * i
        for u in range(U):
            state_back(k + U - u)
        for u in range(U):
            conv_chunk(k - u)
        return carry
    lax.fori_loop(1, nc // U, sweep_back, 0)
    for u in range(U):
        state_back(U - 1 - u)

    sf[...] = jnp.zeros_like(sf)
    low_half = lax.broadcasted_iota(jnp.int32, (T, LANES), 1) < SSM_HEAD_DIM

    def finish(ci):
        t0 = pl.multiple_of(ci * T, T)
        zz = z_ref[pl.ds(t0, T), :].astype(F32)
        y = y_s[pl.ds(t0, T), :] * (zz * _sigmoid(zz))
        ms = jnp.mean(y * y, axis=-1, keepdims=True)
        o_ref[pl.ds(t0, T), :] = (y * lax.rsqrt(ms + EPS) * ng_ref[...]).astype(o_ref.dtype)

    def chunk_fwd(ci):
        t0 = pl.multiple_of(ci * T, T)
        ct = cols_s[ci]
        arg_r = arg_s[ci]
        dt_r = dt_s[ci]
        x = load_x(t0)
        bk = cv[2, pl.ds(t0, T), :].astype(BF16)
        ckf = cv[3, pl.ds(t0, T), :]
        ck = ckf.astype(BF16)
        cb = lax.dot_general(ck, bk, (((1,), (1,)), ((), ())), preferred_element_type=F32)
        spreadv = jnp.dot(ct.astype(BF16), spread_f, preferred_element_type=F32)
        lhs = []
        for h in range(nh):
            hb = nh + h
            arg = jnp.where(lower, ct[:, h:h + 1] - arg_r[h:h + 1, :], arg_r[hb:hb + 1, :] - ct[:, hb:hb + 1])
            wgt = jnp.where(lower, dt_r[h:h + 1, :],
                            jnp.where(upper, dt_r[hb:hb + 1, :], dt_r[h:h + 1, :] + dt_r[hb:hb + 1, :]))
            lhs.append(jnp.concatenate(
                [(cb * jnp.exp(arg) * wgt).astype(BF16),
                 (ckf * spreadv[:, h * LANES:(h + 1) * LANES]).astype(BF16),
                 (ckf * spreadv[:, hb * LANES:(hb + 1) * LANES]).astype(BF16)], axis=1))
        s_prev = sf[...]
        rhs = jnp.concatenate([x.astype(BF16), s_prev.astype(BF16), sbs[ci]], axis=0)
        y4 = jnp.dot(jnp.concatenate(lhs, axis=0), rhs, preferred_element_type=F32)
        y = x * dsk_ref[...] + jnp.concatenate(
            [jnp.where(low_half, y4[2 * j * T:(2 * j + 1) * T, j * LANES:(j + 1) * LANES],
                       y4[(2 * j + 1) * T:(2 * j + 2) * T, j * LANES:(j + 1) * LANES])
             for j in range(GROUP_W // LANES)], axis=1)
        xs = (x * spreadv[:, n_dec:]).astype(BF16)
        sf[...] = expand(ct[T - 1:T, :], nd) * s_prev + lax.dot_general(
            bk, xs, tn_dims, preferred_element_type=F32)
        y_s[pl.ds(t0, T), :] = y

    U = SSD_UNROLL_FWD
    for u in range(U):
        chunk_fwd(u)

    def sweep_fwd(i, carry):
        c = U * i
        for u in range(U):
            finish(c - U + u)
        for u in range(U):
            chunk_fwd(c + u)
        return carry
    lax.fori_loop(1, nc // U, sweep_fwd, 0)
    for u in range(U):
        finish(nc - U + u)


def _ssd(zx, bcm, dt_row, cwx, cwb, cwc, cbx, cbb, cbc, db_row, al_row, dskip, ng, batch, seq):
    G = SSM_GROUPS
    nc = seq // SSD_CHUNK
    zx4 = zx.reshape(2 * G, batch, seq, GROUP_W)
    bc4 = bcm.reshape(2 * G, batch, seq, SSM_STATE)
    nd = 2 * HEADS_PER_GROUP

    def per_group(shape):
        return pl.BlockSpec((None,) + shape, lambda b, g: (g,) + (0,) * len(shape))

    in_specs = [
        pl.BlockSpec((None, None, seq, GROUP_W), lambda b, g: (g, b, 0, 0)),
        pl.BlockSpec((None, None, seq, GROUP_W), lambda b, g: (G + g, b, 0, 0)),
        pl.BlockSpec((None, None, seq, SSM_STATE), lambda b, g: (g, b, 0, 0)),
        pl.BlockSpec((None, None, seq, SSM_STATE), lambda b, g: (G + g, b, 0, 0)),
        pl.BlockSpec((None, None, nc, nd, SSD_CHUNK), lambda b, g: (b, g, 0, 0, 0)),
        per_group((CONV_WIDTH, GROUP_W)), per_group((CONV_WIDTH, SSM_STATE)),
        per_group((CONV_WIDTH, SSM_STATE)),
        per_group((1, GROUP_W)), per_group((1, SSM_STATE)), per_group((1, SSM_STATE)),
        per_group((nd, 1)), per_group((nd, 1)),
        per_group((1, GROUP_W)), per_group((1, GROUP_W)),
    ]
    out = pl.pallas_call(
        functools.partial(_ssd_kernel, seq=seq),
        grid=(batch, G),
        in_specs=in_specs,
        out_specs=pl.BlockSpec((None, None, seq, GROUP_W), lambda b, g: (g, b, 0, 0)),
        out_shape=jax.ShapeDtypeStruct((G, batch, seq, GROUP_W), BF16),
        scratch_shapes=[pltpu.VMEM((4, seq + 2 * CONV_HALO, LANES), F32),
                        pltpu.VMEM((4, seq, LANES), F32),
                        pltpu.VMEM((seq, GROUP_W), F32),
                        pltpu.VMEM((nc, nd, SSD_CHUNK), F32),
                        pltpu.VMEM((nc, nd, SSD_CHUNK), F32),
                        pltpu.VMEM((nc, LANES, SSD_CHUNK), F32),
                        pltpu.VMEM((nc, SSD_CHUNK, LANES), F32),
                        pltpu.VMEM((nc, SSM_STATE, GROUP_W), BF16),
                        pltpu.VMEM((SSM_STATE, GROUP_W), F32),
                        pltpu.VMEM((SSM_STATE, GROUP_W), F32)],
        compiler_params=_params("parallel", "parallel"),
        name="ssd",
    )(zx4, zx4, bc4, bc4, dt_row, cwx, cwb, cwc, cbx, cbb, cbc, db_row, al_row, dskip, ng)
    return out.reshape(G, batch * seq, GROUP_W)


def _out_proj_kernel(attn_ref, ssd_ref, w_ref, x_ref, o_ref, lhs):
    for h in range(ATTN_HEADS):
        lhs[:, h * HEAD_DIM:(h + 1) * HEAD_DIM] = attn_ref[h].astype(BF16)
    for g in range(SSM_GROUPS):
        lhs[:, ATTN_W + g * GROUP_W:ATTN_W + (g + 1) * GROUP_W] = ssd_ref[g]
    o_ref[...] = x_ref[...] + jnp.dot(lhs[...], w_ref[...], preferred_element_type=F32)


def _out_proj(attn, ssd, w, x2d, tm=256):
    n, d = x2d.shape
    kk = w.shape[0]
    return pl.pallas_call(
        _out_proj_kernel,
        grid=(n // tm,),
        in_specs=[pl.BlockSpec((ATTN_HEADS, tm, HEAD_DIM), lambda i: (0, i, 0)),
                  pl.BlockSpec((SSM_GROUPS, tm, GROUP_W), lambda i: (0, i, 0)),
                  pl.BlockSpec((kk, d), lambda i: (0, 0), pipeline_mode=pl.Buffered(1)),
                  pl.BlockSpec((tm, d), lambda i: (i, 0))],
        out_specs=pl.BlockSpec((tm, d), lambda i: (i, 0)),
        out_shape=jax.ShapeDtypeStruct((n, d), F32),
        scratch_shapes=[pltpu.VMEM((tm, kk), BF16)],
        compiler_params=_params("parallel"),
        name="out_proj",
    )(attn, ssd, w, x2d)


def _mlp_kernel(x_ref, g_ref, wu_ref, wd_ref, o_ref, hm):
    @pl.when(pl.program_id(1) == 0)
    def _():
        x = x_ref[...]
        ms = jnp.mean(x * x, axis=-1, keepdims=True)
        hm[...] = (x * lax.rsqrt(ms + EPS) * g_ref[...]).astype(hm.dtype)
        o_ref[...] = x
    u = jnp.maximum(jnp.dot(hm[...], wu_ref[...], preferred_element_type=F32), 0.0)
    o_ref[...] += jnp.dot((u * u).astype(BF16), wd_ref[...], preferred_element_type=F32)


def _mlp(x2d, g, wu, wd, tm=512, tf=1024):
    n, d = x2d.shape
    f = wu.shape[1]
    return pl.pallas_call(
        _mlp_kernel,
        grid=(n // tm, f // tf),
        in_specs=[pl.BlockSpec((tm, d), lambda i, j: (i, 0)),
                  pl.BlockSpec((1, d), lambda i, j: (0, 0)),
                  pl.BlockSpec((d, tf), lambda i, j: (0, j)),
                  pl.BlockSpec((tf, d), lambda i, j: (j, 0))],
        out_specs=pl.BlockSpec((tm, d), lambda i, j: (i, 0)),
        out_shape=jax.ShapeDtypeStruct((n, d), F32),
        scratch_shapes=[pltpu.VMEM((tm, d), BF16)],
        compiler_params=_params("parallel", "arbitrary"),
        name="mlp",
    )(x2d, g.reshape(1, d), wu, wd)


def kernel(x, norm_mix_g, w_in, q_norm_g, k_norm_g, rel_bias, conv_w, conv_b, dt_bias, a_log,
           d_skip, ssd_norm_g, w_out, norm_mlp_g, w_up, w_down):
    batch, seq, _ = x.shape
    n = batch * seq
    G, nh = SSM_GROUPS, HEADS_PER_GROUP
    nc = seq // SSD_CHUNK
    o_z = 3 * ATTN_W
    o_bc = o_z + 2 * SSM_W
    o_dt = o_bc + 2 * G * SSM_STATE
    x2d = x.reshape(n, D_MODEL)
    brow = _bias_rows(rel_bias)

    for layer in range(w_in.shape[0]):
        h = _rmsnorm(x2d, norm_mix_g[layer])

        scale = 1.0 / math.sqrt(HEAD_DIM)
        gains = jnp.concatenate([jnp.tile(q_norm_g[layer].astype(F32) * scale, ATTN_HEADS),
                                 jnp.tile(k_norm_g[layer].astype(F32), ATTN_HEADS)]).reshape(1, 2 * ATTN_W)
        qk = _proj_qk(h, w_in, layer, gains)
        v = _proj_split(h, w_in, layer, 2 * ATTN_W, ATTN_W, HEAD_DIM, "proj_v", out_dtype=F32)
        zx = _proj_split(h, w_in, layer, o_z, o_bc - o_z, GROUP_W, "proj_zx")
        bcm = _proj_split(h, w_in, layer, o_bc, o_dt - o_bc, SSM_STATE, "proj_bc")
        dt_raw = _proj_plain(h, w_in[layer][:, o_dt:].astype(BF16), 0, 2 * SSM_HEADS, "proj_dt")

        attn = _attention(qk, v, brow, batch, seq)

        dt_row = (dt_raw.reshape(batch, nc, SSD_CHUNK, 2, G, nh)
                  .transpose(0, 4, 1, 3, 5, 2).reshape(batch, G, nc, 2 * nh, SSD_CHUNK))
        per_dir = lambda t: t.astype(F32).reshape(2, G, nh).transpose(1, 0, 2).reshape(G, 2 * nh)
        db, al = per_dir(dt_bias[layer]), per_dir(a_log[layer])
        cw, cbias = conv_w[layer].astype(F32), conv_b[layer].astype(F32)
        gn = G * SSM_STATE
        grp = lambda t, width: t.reshape(t.shape[0], G, width).transpose(1, 0, 2)
        ssd = _ssd(
            zx, bcm, dt_row,
            grp(cw[:, :SSM_W], GROUP_W), grp(cw[:, SSM_W:SSM_W + gn], SSM_STATE),
            grp(cw[:, SSM_W + gn:], SSM_STATE),
            grp(cbias[None, :SSM_W], GROUP_W), grp(cbias[None, SSM_W:SSM_W + gn], SSM_STATE),
            grp(cbias[None, SSM_W + gn:], SSM_STATE),
            db.reshape(G, 2 * nh, 1), al.reshape(G, 2 * nh, 1),
            jnp.repeat(d_skip[layer].astype(F32), SSM_HEAD_DIM).reshape(G, 1, GROUP_W),
            ssd_norm_g[layer].astype(F32).reshape(G, 1, GROUP_W),
            batch, seq)

        x2d = _out_proj(attn, ssd, w_out[layer].astype(BF16), x2d)
        x2d = _mlp(x2d, norm_mlp_g[layer], w_up[layer].astype(BF16), w_down[layer].astype(BF16))
    return x2d.reshape(batch, seq, D_MODEL)
```

```python
import functools
import math

import jax
import jax.numpy as jnp
from jax import lax
from jax.experimental import pallas as pl
from jax.experimental.pallas import tpu as pltpu

D_MODEL = 2048
ATTN_HEADS = 16
HEAD_DIM = 128
ATTN_W = ATTN_HEADS * HEAD_DIM
SSM_HEADS = 32
SSM_HEAD_DIM = 64
SSM_W = SSM_HEADS * SSM_HEAD_DIM
SSM_GROUPS = 8
HEADS_PER_GROUP = SSM_HEADS // SSM_GROUPS
GROUP_W = SSM_W // SSM_GROUPS
SSM_STATE = 128
CONV_WIDTH = 5
D_FF = 4 * D_MODEL
DILATIONS = (1, 4, 16)
HALF_WINDOW = 64
NUM_BUCKETS = 32
MAX_DISTANCE = 1024
NEG_INF = -1e30
EPS = 1e-6

LANES = 128
Q_SUB = 128
K_WIN = 256
SSD_CHUNK = 128
SSD_UNROLL = 4
SSD_UNROLL_FWD = 2
CONV_HALO = 8
VMEM_LIMIT = 56 * 1024 * 1024

F32 = jnp.float32
BF16 = jnp.bfloat16


def _params(*sem):
    return pltpu.CompilerParams(dimension_semantics=sem, vmem_limit_bytes=VMEM_LIMIT)


def _sigmoid(x):
    return 1.0 / (1.0 + jnp.exp(-x))


def _softplus(x):
    return jnp.maximum(x, 0.0) + jnp.log1p(jnp.exp(-jnp.abs(x)))


def _rmsnorm_kernel(x_ref, g_ref, o_ref):
    x = x_ref[...]
    ms = jnp.mean(x * x, axis=-1, keepdims=True)
    o_ref[...] = (x * lax.rsqrt(ms + EPS) * g_ref[...]).astype(o_ref.dtype)


def _rmsnorm(x2d, g, tm=512):
    n, d = x2d.shape
    return pl.pallas_call(
        _rmsnorm_kernel,
        grid=(n // tm,),
        in_specs=[pl.BlockSpec((tm, d), lambda i: (i, 0)),
                  pl.BlockSpec((1, d), lambda i: (0, 0))],
        out_specs=pl.BlockSpec((tm, d), lambda i: (i, 0)),
        out_shape=jax.ShapeDtypeStruct((n, d), BF16),
        compiler_params=_params("parallel"),
        name="rmsnorm",
    )(x2d, g.reshape(1, d))


QK_SUB = 256


def _proj_qk_kernel(a_ref, w_ref, g_ref, o_ref, *, heads_per_tile):
    a = a_ref[...]
    per = QK_SUB // HEAD_DIM
    for c in range(heads_per_tile // per):
        acc = jnp.dot(a, w_ref[:, c * QK_SUB:(c + 1) * QK_SUB], preferred_element_type=F32)
        for hh in range(per):
            h = c * per + hh
            s = acc[:, hh * HEAD_DIM:(hh + 1) * HEAD_DIM]
            ms = jnp.mean(s * s, axis=-1, keepdims=True)
            g = g_ref[:, h * HEAD_DIM:(h + 1) * HEAD_DIM]
            o_ref[h] = (s * lax.rsqrt(ms + EPS) * g).astype(o_ref.dtype)


def _proj_split_kernel(a_ref, w_ref, o_ref, *, width):
    a = a_ref[...]
    per = QK_SUB // width
    for s in range(w_ref.shape[1] // QK_SUB):
        acc = jnp.dot(a, w_ref[:, s * QK_SUB:(s + 1) * QK_SUB], preferred_element_type=F32)
        for c in range(per):
            o_ref[s * per + c] = acc[:, c * width:(c + 1) * width].astype(o_ref.dtype)


def _proj_plain_kernel(a_ref, w_ref, o_ref):
    o_ref[...] = jnp.dot(a_ref[...], w_ref[...], preferred_element_type=F32).astype(o_ref.dtype)


def _proj_qk(h, w, gains, tm=1024, tn=1024):
    n, k = h.shape
    m = 2 * ATTN_W
    hpt = tn // HEAD_DIM
    return pl.pallas_call(
        functools.partial(_proj_qk_kernel, heads_per_tile=hpt),
        grid=(m // tn, n // tm),
        in_specs=[pl.BlockSpec((tm, k), lambda j, i: (i, 0)),
                  pl.BlockSpec((k, tn), lambda j, i: (0, j)),
                  pl.BlockSpec((1, tn), lambda j, i: (0, j))],
        out_specs=pl.BlockSpec((hpt, tm, HEAD_DIM), lambda j, i: (j, i, 0)),
        out_shape=jax.ShapeDtypeStruct((m // HEAD_DIM, n, HEAD_DIM), F32),
        compiler_params=_params("parallel", "arbitrary"),
        name="proj_qk",
    )(h, w, gains)


def _proj_split(h, w, col0, m, width, name, out_dtype=BF16, tm=1024, tn=1024):
    n, k = h.shape
    cpt = tn // width
    j0 = col0 // tn
    return pl.pallas_call(
        functools.partial(_proj_split_kernel, width=width),
        grid=(m // tn, n // tm),
        in_specs=[pl.BlockSpec((tm, k), lambda j, i: (i, 0)),
                  pl.BlockSpec((k, tn), lambda j, i: (0, j0 + j))],
        out_specs=pl.BlockSpec((cpt, tm, width), lambda j, i: (j, i, 0)),
        out_shape=jax.ShapeDtypeStruct((m // width, n, width), out_dtype),
        compiler_params=_params("parallel", "arbitrary"),
        name=name,
    )(h, w)


def _proj_plain(h, w, col0, m, name, tm=1024):
    n, k = h.shape
    j0 = col0 // m
    return pl.pallas_call(
        _proj_plain_kernel,
        grid=(n // tm,),
        in_specs=[pl.BlockSpec((tm, k), lambda i: (i, 0)),
                  pl.BlockSpec((k, m), lambda i: (0, j0))],
        out_specs=pl.BlockSpec((tm, m), lambda i: (i, 0)),
        out_shape=jax.ShapeDtypeStruct((n, m), F32),
        compiler_params=_params("parallel"),
        name=name,
    )(h, w)


def _t5_bucket(rel):
    nb = NUM_BUCKETS // 2
    max_exact = nb // 2
    ret = (rel > 0).astype(jnp.int32) * nb
    n = jnp.abs(rel)
    nf = jnp.maximum(n, 1).astype(jnp.float32)
    large = max_exact + (jnp.log(nf / max_exact) / math.log(MAX_DISTANCE / max_exact)
                         * (nb - max_exact)).astype(jnp.int32)
    large = jnp.minimum(large, nb - 1)
    return ret + jnp.where(n < max_exact, n, large)


def _bias_rows(rel_bias):
    period = K_WIN + Q_SUB
    m = jnp.arange(period)
    delta = jnp.where(m < K_WIN, m, m - period)
    rows = []
    for d in DILATIONS:
        for off in (0, -HALF_WINDOW, -2 * HALF_WINDOW):
            rel = delta + off
            valid = jnp.abs(rel) <= HALF_WINDOW
            b = rel_bias[_t5_bucket(rel * d)].astype(F32)
            rows.append(jnp.where(valid[:, None], b, NEG_INF))
    return jnp.stack(rows, axis=0).transpose(2, 0, 1)


ATTN_UNROLL_MERGE = 16
ATTN_UNROLL = 16


def _attn_kernel(q_ref, k_ref, v_ref, brow_ref, o_ref, bias_s, tmp_s, xq4, xt4,
                 q16, k1, k4, k16, v1, v4, v16, acc_s, max_s, den_s, *, seq):
    period = K_WIN + Q_SUB
    n4, n16 = seq // 4, seq // 16

    @pl.when(pl.program_id(1) == 0)
    def _():
        for idx in range(9):
            row = jnp.broadcast_to(brow_ref[idx:idx + 1, :], (Q_SUB, period))
            tile = pltpu.roll(row, 0, 1, stride=1, stride_axis=0)
            d = DILATIONS[idx // 3]
            if d == 16:
                bias_s[idx] = tile[:, :K_WIN]
                continue
            for half in range(K_WIN // LANES):
                tmp_s[half] = tile[:, half * LANES:(half + 1) * LANES]
            groups = 16 // d
            for half in range(K_WIN // LANES):
                for g in range(groups):
                    n = Q_SUB // groups
                    bias_s[idx, g * n:(g + 1) * n, half * LANES:(half + 1) * LANES] = (
                        tmp_s[half, pl.ds(g, n, stride=groups), :])

    cp = 256

    def split4(src, dst):
        for r4 in range(4):
            def body(c, carry, r4=r4):
                t0 = pl.multiple_of(c * cp, cp)
                dst[r4, pl.ds(t0, cp), :] = src[pl.ds(r4 + 4 * t0, cp, stride=4), :]
                return carry
            lax.fori_loop(0, n4 // cp, body, 0)

    def split16(src4, dst16, dst4=None):
        for r4 in range(4):
            for a in range(4):
                dst16[4 * a + r4] = src4[r4, pl.ds(a, n16, stride=4), :].astype(BF16)
            if dst4 is not None:
                def body(c, carry, r4=r4):
                    t0 = pl.multiple_of(c * cp, cp)
                    dst4[r4, pl.ds(t0, cp), :] = src4[r4, pl.ds(t0, cp), :].astype(BF16)
                    return carry
                lax.fori_loop(0, n4 // cp, body, 0)

    def cast(src, dst):
        def body(c, carry):
            t0 = pl.multiple_of(c * cp, cp)
            dst[pl.ds(t0, cp), :] = src[pl.ds(t0, cp), :].astype(BF16)
            return carry
        lax.fori_loop(0, seq // cp, body, 0)

    split4(q_ref, xq4)
    split16(xq4, q16)
    for src, d1, d4, d16 in ((k_ref, k1, k4, k16), (v_ref, v1, v4, v16)):
        cast(src, d1)
        split4(src, xt4)
        split16(xt4, d16, d4)

    ones = jnp.ones((K_WIN, LANES), BF16)

    def sub_tile(q, k_s, v_s, length, s0, pi):
        w0 = pl.multiple_of(jnp.clip(s0 - HALF_WINDOW, 0, length - K_WIN), HALF_WINDOW)
        place = jnp.where(s0 == 0, 0, jnp.where(s0 == length - Q_SUB, 2, 1))
        k = k_s[pl.ds(w0, K_WIN), :]
        v = v_s[pl.ds(w0, K_WIN), :]
        s = lax.dot_general(q, k, (((1,), (1,)), ((), ())), preferred_element_type=F32)
        s = s + bias_s[3 * pi + place]
        m = jnp.max(s, axis=-1, keepdims=True)
        p = jnp.exp(s - m).astype(BF16)
        pv = jnp.dot(p, jnp.concatenate([v, ones], axis=1), preferred_element_type=F32)
        return pv[:, :LANES], jnp.broadcast_to(m, (Q_SUB, LANES)), pv[:, LANES:]

    def store(pi, res, slab_of_group, l0, n):
        for g in range(Q_SUB // n):
            for val, dst in zip(res, (acc_s, max_s, den_s)):
                dst[pi, slab_of_group(g), pl.ds(l0, n), :] = val[g * n:(g + 1) * n]

    U = ATTN_UNROLL

    def body4(it, carry):
        res = []
        for u in range(U // 4):
            l0 = pl.multiple_of((it * (U // 4) + u) * 32, 32)
            for r4 in range(4):
                q = jnp.concatenate([q16[4 * a + r4, pl.ds(l0, 32), :] for a in range(4)], axis=0)
                res.append((sub_tile(q, k4.at[r4], v4.at[r4], n4, pl.multiple_of(4 * l0, Q_SUB), 1),
                            r4, l0))
        for r3, r4, l0 in res:
            store(0, r3, lambda g, r4=r4: 4 * g + r4, l0, 32)
        return carry
    lax.fori_loop(0, n4 // Q_SUB // (U // 4), body4, 0)

    def body16(it, carry):
        g0 = (it // (n16 // Q_SUB)) * U
        l0 = pl.multiple_of((it % (n16 // Q_SUB)) * Q_SUB, Q_SUB)
        res = [sub_tile(q16[g0 + u, pl.ds(l0, Q_SUB), :], k16.at[g0 + u], v16.at[g0 + u], n16, l0, 2)
               for u in range(U)]
        for u, r3 in enumerate(res):
            store(1, r3, lambda g, u=u: g0 + u, l0, Q_SUB)
        return carry
    lax.fori_loop(0, (16 // U) * (n16 // Q_SUB), body16, 0)

    U1 = ATTN_UNROLL_MERGE

    def body1(it, carry):
        res = []
        for u in range(U1):
            l0 = pl.multiple_of((it * U1 + u) * 8, 8)
            q = jnp.concatenate([xq4[r % 4, pl.ds(4 * l0 + r // 4, 8, stride=4), :] for r in range(16)],
                                axis=0).astype(BF16)
            res.append((sub_tile(q, k1, v1, seq, pl.multiple_of(16 * l0, Q_SUB), 0), l0))
        for (aa, ma, da), l0 in res:
            rows = lambda ref, pi: jnp.concatenate([ref[pi, r, pl.ds(l0, 8), :] for r in range(16)], axis=0)
            mb, mc = rows(max_s, 0), rows(max_s, 1)
            mx = jnp.maximum(jnp.maximum(ma, mb), mc)
            ea, eb, ec = jnp.exp(ma - mx), jnp.exp(mb - mx), jnp.exp(mc - mx)
            num = ea * aa + eb * rows(acc_s, 0) + ec * rows(acc_s, 1)
            den = ea * da + eb * rows(den_s, 0) + ec * rows(den_s, 1)
            out = num / den
            for r in range(16):
                o_ref[pl.ds(r + 16 * l0, 8, stride=16), :] = out[r * 8:(r + 1) * 8]
        return carry
    lax.fori_loop(0, seq // Q_SUB // U1, body1, 0)


def _attention(qk, v, brow, batch, seq):
    H = ATTN_HEADS
    n4, n16 = seq // 4, seq // 16
    qk4 = qk.reshape(2 * H, batch, seq, HEAD_DIM)
    v4 = v.reshape(H, batch, seq, HEAD_DIM)
    in_specs = [pl.BlockSpec((None, None, seq, HEAD_DIM), lambda h, b: (h, b, 0, 0)),
                pl.BlockSpec((None, None, seq, HEAD_DIM), lambda h, b: (H + h, b, 0, 0)),
                pl.BlockSpec((None, None, seq, HEAD_DIM), lambda h, b: (h, b, 0, 0))]
    in_specs.append(pl.BlockSpec((None, 9, K_WIN + Q_SUB), lambda h, b: (h, 0, 0)))
    kv_slabs = [pltpu.VMEM((seq, LANES), BF16), pltpu.VMEM((4, n4, LANES), BF16),
                pltpu.VMEM((16, n16, LANES), BF16)]
    out = pl.pallas_call(
        functools.partial(_attn_kernel, seq=seq),
        grid=(H, batch),
        in_specs=in_specs,
        out_specs=pl.BlockSpec((None, None, seq, HEAD_DIM), lambda h, b: (h, b, 0, 0)),
        out_shape=jax.ShapeDtypeStruct((H, batch, seq, HEAD_DIM), F32),
        scratch_shapes=[pltpu.VMEM((9, Q_SUB, K_WIN), F32),
                        pltpu.VMEM((K_WIN // LANES, Q_SUB, LANES), F32),
                        pltpu.VMEM((4, n4, LANES), F32),
                        pltpu.VMEM((4, n4, LANES), F32),
                        pltpu.VMEM((16, n16, LANES), BF16)]
                       + kv_slabs + kv_slabs
                       + [pltpu.VMEM((2, 16, n16, LANES), F32)] * 3,
        compiler_params=_params("parallel", "arbitrary"),
        name="dilated_attention",
    )(qk4, qk4, v4, brow)
    return out.reshape(H, batch * seq, HEAD_DIM)


def _ssd_kernel(z_ref, x_ref, b_ref, c_ref, dtr_ref, cwx_ref, cwb_ref, cwc_ref,
                cbx_ref, cbb_ref, cbc_ref, dbr_ref, alr_ref, dsk_ref, ng_ref,
                o_ref, pad, cv, y_s, arg_s, dt_s, rows_s, cols_s, sbs, sf, sb, *, seq):
    T = SSD_CHUNK
    nc = seq // T
    hi = lax.Precision.HIGHEST
    halo = CONV_HALO
    half = T // 2

    pad[:, 0:halo, :] = jnp.zeros((4, halo, LANES), F32)
    pad[:, seq + halo:seq + 2 * halo, :] = jnp.zeros((4, halo, LANES), F32)

    def fill(i, carry):
        t0 = pl.multiple_of(i * T, T)
        xin = x_ref[pl.ds(t0, T), :].astype(F32)
        pad[0, pl.ds(t0 + halo, T), :] = xin[:, :LANES]
        pad[1, pl.ds(t0 + halo, T), :] = xin[:, LANES:]
        pad[2, pl.ds(t0 + halo, T), :] = b_ref[pl.ds(t0, T), :].astype(F32)
        pad[3, pl.ds(t0 + halo, T), :] = c_ref[pl.ds(t0, T), :].astype(F32)
        return carry
    lax.fori_loop(0, nc, fill, 0)

    cws = (cwx_ref[:, :LANES], cwx_ref[:, LANES:], cwb_ref[...], cwc_ref[...])
    cbs = (cbx_ref[:, :LANES], cbx_ref[:, LANES:], cbb_ref[...], cbc_ref[...])

    def conv_chunk(ci):
        t0 = pl.multiple_of(ci * T, T)
        for s in range(4):
            ev = [pad[s, pl.ds(t0 + halo + 2 * k, half, stride=2), :] for k in (-1, 0, 1)]
            od = [pad[s, pl.ds(t0 + halo + 1 + 2 * k, half, stride=2), :] for k in (-1, 0, 1)]
            w = [cws[s][j:j + 1, :] for j in range(CONV_WIDTH)]
            out_e = cbs[s] + w[0] * ev[0] + w[1] * od[0] + w[2] * ev[1] + w[3] * od[1] + w[4] * ev[2]
            out_o = cbs[s] + w[0] * od[0] + w[1] * ev[1] + w[2] * od[1] + w[3] * ev[2] + w[4] * od[2]
            cv[s, pl.ds(t0, half, stride=2), :] = out_e * _sigmoid(out_e)
            cv[s, pl.ds(t0 + 1, half, stride=2), :] = out_o * _sigmoid(out_o)

    def load_x(t0):
        return jnp.concatenate([cv[0, pl.ds(t0, T), :], cv[1, pl.ds(t0, T), :]], axis=1)

    nh = HEADS_PER_GROUP
    nd = 2 * nh
    row = lax.broadcasted_iota(jnp.int32, (T, T), 0)
    col = lax.broadcasted_iota(jnp.int32, (T, T), 1)
    triu = (row <= col).astype(F32)
    lower = col < row
    upper = col > row
    lane_head = lax.broadcasted_iota(jnp.int32, (T, GROUP_W), 1) // SSM_HEAD_DIM

    dt = _softplus(dtr_ref[...] + dbr_ref[...])
    a = dt * (-jnp.exp(alr_ref[...]))
    cum = jnp.dot(a.reshape(nc * nd, T), triu, precision=hi,
                  preferred_element_type=F32).reshape(nc, nd, T)
    last = cum[:, :, T - 1:T]
    exc = cum - a
    fwd = lax.broadcasted_iota(jnp.int32, (nc, nd, T), 1) < nh
    arg_s[...] = jnp.where(fwd, cum, exc)
    dt_s[...] = dt
    rows_s[:, 0:nd, :] = arg_s[...]
    rows_s[:, nd:2 * nd, :] = jnp.exp(jnp.where(fwd, cum, last - exc))
    rows_s[:, 2 * nd:3 * nd, :] = dt * jnp.exp(jnp.where(fwd, last - cum, exc))
    rows_s[:, 3 * nd:, :] = jnp.zeros((nc, LANES - 3 * nd, T), F32)

    def expand(cols, first):
        n = cols.shape[0]
        low = lax.broadcasted_iota(jnp.int32, (n, LANES), 1) < SSM_HEAD_DIM
        halves = []
        for j in range(GROUP_W // LANES):
            c0 = jnp.broadcast_to(cols[:, first + 2 * j:first + 2 * j + 1], (n, LANES))
            c1 = jnp.broadcast_to(cols[:, first + 2 * j + 1:first + 2 * j + 2], (n, LANES))
            halves.append(jnp.where(low, c0, c1))
        return jnp.concatenate(halves, axis=1)

    tn_dims = (((0,), (0,)), ((), ()))

    cid = lax.broadcasted_iota(jnp.int32, (LANES, 4 * GROUP_W), 0)
    lid = lax.broadcasted_iota(jnp.int32, (LANES, 4 * GROUP_W), 1)
    blk = lid // GROUP_W
    want = nd + (blk % 2) * nh + (blk // 2) * nd + (lid % GROUP_W) // SSM_HEAD_DIM
    spread = jnp.where(cid == want, 1.0, 0.0).astype(BF16)
    spread_f = spread[:, :3 * GROUP_W]
    spread_b = spread[:, 3 * GROUP_W:]

    sb[...] = jnp.zeros_like(sb)

    def state_back(ci):
        t0 = pl.multiple_of(ci * T, T)
        ct = rows_s[ci].T
        cols_s[ci] = ct
        s_prev = sb[...]
        sbs[ci] = s_prev.astype(BF16)
        wx = jnp.dot(ct.astype(BF16), spread_b, preferred_element_type=F32)
        xs = (load_x(t0) * wx).astype(BF16)
        sb[...] = expand(ct[0:1, :], nd + nh) * s_prev + lax.dot_general(
            cv[2, pl.ds(t0, T), :].astype(BF16), xs, tn_dims, preferred_element_type=F32)

    U = SSD_UNROLL
    for u in range(U):
        conv_chunk(nc - 1 - u)

    def sweep_back(i, carry):
        k = nc - 1 - U * i
        for u in range(U):
            state_back(k + U - u)
        for u in range(U):
            conv_chunk(k - u)
        return carry
    lax.fori_loop(1, nc // U, sweep_back, 0)
    for u in range(U):
        state_back(U - 1 - u)

    sf[...] = jnp.zeros_like(sf)
    low_half = lax.broadcasted_iota(jnp.int32, (T, LANES), 1) < SSM_HEAD_DIM

    def finish(ci):
        t0 = pl.multiple_of(ci * T, T)
        zz = z_ref[pl.ds(t0, T), :].astype(F32)
        y = y_s[pl.ds(t0, T), :] * (zz * _sigmoid(zz))
        ms = jnp.mean(y * y, axis=-1, keepdims=True)
        o_ref[pl.ds(t0, T), :] = (y * lax.rsqrt(ms + EPS) * ng_ref[...]).astype(o_ref.dtype)

    def chunk_fwd(ci):
        t0 = pl.multiple_of(ci * T, T)
        ct = cols_s[ci]
        arg_r = arg_s[ci]
        dt_r = dt_s[ci]
        x = load_x(t0)
        bk = cv[2, pl.ds(t0, T), :].astype(BF16)
        ck = cv[3, pl.ds(t0, T), :].astype(BF16)
        cb = lax.dot_general(ck, bk, (((1,), (1,)), ((), ())), preferred_element_type=F32)
        spreadv = jnp.dot(ct.astype(BF16), spread_f, preferred_element_type=F32)
        lhs = []
        for h in range(nh):
            hb = nh + h
            arg = jnp.where(lower, ct[:, h:h + 1] - arg_r[h:h + 1, :], arg_r[hb:hb + 1, :] - ct[:, hb:hb + 1])
            wgt = jnp.where(lower, dt_r[h:h + 1, :],
                            jnp.where(upper, dt_r[hb:hb + 1, :], dt_r[h:h + 1, :] + dt_r[hb:hb + 1, :]))
            lhs.append((cb * jnp.exp(arg) * wgt).astype(BF16))
        s_prev = sf[...]
        y4 = jnp.dot(jnp.concatenate(lhs, axis=0), x.astype(BF16), preferred_element_type=F32)
        off = jnp.dot(ck, jnp.concatenate([s_prev.astype(BF16), sbs[ci]], axis=1),
                      preferred_element_type=F32)
        y = x * dsk_ref[...] + jnp.concatenate(
            [jnp.where(low_half, y4[2 * j * T:(2 * j + 1) * T, j * LANES:(j + 1) * LANES],
                       y4[(2 * j + 1) * T:(2 * j + 2) * T, j * LANES:(j + 1) * LANES])
             for j in range(GROUP_W // LANES)], axis=1)
        y = y + spreadv[:, :GROUP_W] * off[:, :GROUP_W] + spreadv[:, GROUP_W:2 * GROUP_W] * off[:, GROUP_W:]
        xs = (x * spreadv[:, 2 * GROUP_W:]).astype(BF16)
        sf[...] = expand(ct[T - 1:T, :], nd) * s_prev + lax.dot_general(
            bk, xs, tn_dims, preferred_element_type=F32)
        y_s[pl.ds(t0, T), :] = y

    U = SSD_UNROLL_FWD
    for u in range(U):
        chunk_fwd(u)

    def sweep_fwd(i, carry):
        c = U * i
        for u in range(U):
            finish(c - U + u)
        for u in range(U):
            chunk_fwd(c + u)
        return carry
    lax.fori_loop(1, nc // U, sweep_fwd, 0)
    for u in range(U):
        finish(nc - U + u)


def _ssd(zx, bcm, dt_row, cwx, cwb, cwc, cbx, cbb, cbc, db_row, al_row, dskip, ng, batch, seq):
    G = SSM_GROUPS
    nc = seq // SSD_CHUNK
    zx4 = zx.reshape(2 * G, batch, seq, GROUP_W)
    bc4 = bcm.reshape(2 * G, batch, seq, SSM_STATE)
    nd = 2 * HEADS_PER_GROUP

    def per_group(shape):
        return pl.BlockSpec((None,) + shape, lambda b, g: (g,) + (0,) * len(shape))

    in_specs = [
        pl.BlockSpec((None, None, seq, GROUP_W), lambda b, g: (g, b, 0, 0)),
        pl.BlockSpec((None, None, seq, GROUP_W), lambda b, g: (G + g, b, 0, 0)),
        pl.BlockSpec((None, None, seq, SSM_STATE), lambda b, g: (g, b, 0, 0)),
        pl.BlockSpec((None, None, seq, SSM_STATE), lambda b, g: (G + g, b, 0, 0)),
        pl.BlockSpec((None, None, nc, nd, SSD_CHUNK), lambda b, g: (b, g, 0, 0, 0)),
        per_group((CONV_WIDTH, GROUP_W)), per_group((CONV_WIDTH, SSM_STATE)),
        per_group((CONV_WIDTH, SSM_STATE)),
        per_group((1, GROUP_W)), per_group((1, SSM_STATE)), per_group((1, SSM_STATE)),
        per_group((nd, 1)), per_group((nd, 1)),
        per_group((1, GROUP_W)), per_group((1, GROUP_W)),
    ]
    out = pl.pallas_call(
        functools.partial(_ssd_kernel, seq=seq),
        grid=(batch, G),
        in_specs=in_specs,
        out_specs=pl.BlockSpec((None, None, seq, GROUP_W), lambda b, g: (g, b, 0, 0)),
        out_shape=jax.ShapeDtypeStruct((G, batch, seq, GROUP_W), BF16),
        scratch_shapes=[pltpu.VMEM((4, seq + 2 * CONV_HALO, LANES), F32),
                        pltpu.VMEM((4, seq, LANES), F32),
                        pltpu.VMEM((seq, GROUP_W), F32),
                        pltpu.VMEM((nc, nd, SSD_CHUNK), F32),
                        pltpu.VMEM((nc, nd, SSD_CHUNK), F32),
                        pltpu.VMEM((nc, LANES, SSD_CHUNK), F32),
                        pltpu.VMEM((nc, SSD_CHUNK, LANES), F32),
                        pltpu.VMEM((nc, SSM_STATE, GROUP_W), BF16),
                        pltpu.VMEM((SSM_STATE, GROUP_W), F32),
                        pltpu.VMEM((SSM_STATE, GROUP_W), F32)],
        compiler_params=_params("parallel", "parallel"),
        name="ssd",
    )(zx4, zx4, bc4, bc4, dt_row, cwx, cwb, cwc, cbx, cbb, cbc, db_row, al_row, dskip, ng)
    return out.reshape(G, batch * seq, GROUP_W)


def _out_proj_kernel(attn_ref, ssd_ref, w_ref, x_ref, o_ref, lhs):
    for h in range(ATTN_HEADS):
        lhs[:, h * HEAD_DIM:(h + 1) * HEAD_DIM] = attn_ref[h].astype(BF16)
    for g in range(SSM_GROUPS):
        lhs[:, ATTN_W + g * GROUP_W:ATTN_W + (g + 1) * GROUP_W] = ssd_ref[g]
    o_ref[...] = x_ref[...] + jnp.dot(lhs[...], w_ref[...], preferred_element_type=F32)


def _out_proj(attn, ssd, w, x2d, tm=256):
    n, d = x2d.shape
    kk = w.shape[0]
    return pl.pallas_call(
        _out_proj_kernel,
        grid=(n // tm,),
        in_specs=[pl.BlockSpec((ATTN_HEADS, tm, HEAD_DIM), lambda i: (0, i, 0)),
                  pl.BlockSpec((SSM_GROUPS, tm, GROUP_W), lambda i: (0, i, 0)),
                  pl.BlockSpec((kk, d), lambda i: (0, 0), pipeline_mode=pl.Buffered(1)),
                  pl.BlockSpec((tm, d), lambda i: (i, 0))],
        out_specs=pl.BlockSpec((tm, d), lambda i: (i, 0)),
        out_shape=jax.ShapeDtypeStruct((n, d), F32),
        scratch_shapes=[pltpu.VMEM((tm, kk), BF16)],
        compiler_params=_params("parallel"),
        name="out_proj",
    )(attn, ssd, w, x2d)


def _mlp_kernel(x_ref, g_ref, wu_ref, wd_ref, o_ref, hm):
    @pl.when(pl.program_id(1) == 0)
    def _():
        x = x_ref[...]
        ms = jnp.mean(x * x, axis=-1, keepdims=True)
        hm[...] = (x * lax.rsqrt(ms + EPS) * g_ref[...]).astype(hm.dtype)
        o_ref[...] = x
    u = jnp.maximum(jnp.dot(hm[...], wu_ref[...], preferred_element_type=F32), 0.0)
    o_ref[...] += jnp.dot((u * u).astype(BF16), wd_ref[...], preferred_element_type=F32)


def _mlp(x2d, g, wu, wd, tm=512, tf=1024):
    n, d = x2d.shape
    f = wu.shape[1]
    return pl.pallas_call(
        _mlp_kernel,
        grid=(n // tm, f // tf),
        in_specs=[pl.BlockSpec((tm, d), lambda i, j: (i, 0)),
                  pl.BlockSpec((1, d), lambda i, j: (0, 0)),
                  pl.BlockSpec((d, tf), lambda i, j: (0, j)),
                  pl.BlockSpec((tf, d), lambda i, j: (j, 0))],
        out_specs=pl.BlockSpec((tm, d), lambda i, j: (i, 0)),
        out_shape=jax.ShapeDtypeStruct((n, d), F32),
        scratch_shapes=[pltpu.VMEM((tm, d), BF16)],
        compiler_params=_params("parallel", "arbitrary"),
        name="mlp",
    )(x2d, g.reshape(1, d), wu, wd)


def kernel(x, norm_mix_g, w_in, q_norm_g, k_norm_g, rel_bias, conv_w, conv_b, dt_bias, a_log,
           d_skip, ssd_norm_g, w_out, norm_mlp_g, w_up, w_down):
    batch, seq, _ = x.shape
    n = batch * seq
    G, nh = SSM_GROUPS, HEADS_PER_GROUP
    nc = seq // SSD_CHUNK
    o_z = 3 * ATTN_W
    o_bc = o_z + 2 * SSM_W
    o_dt = o_bc + 2 * G * SSM_STATE
    x2d = x.reshape(n, D_MODEL)
    brow = _bias_rows(rel_bias)

    for layer in range(w_in.shape[0]):
        h = _rmsnorm(x2d, norm_mix_g[layer])

        scale = 1.0 / math.sqrt(HEAD_DIM)
        gains = jnp.concatenate([jnp.tile(q_norm_g[layer].astype(F32) * scale, ATTN_HEADS),
                                 jnp.tile(k_norm_g[layer].astype(F32), ATTN_HEADS)]).reshape(1, 2 * ATTN_W)
        wi = w_in[layer].astype(BF16)
        qk = _proj_qk(h, wi, gains)
        v = _proj_split(h, wi, 2 * ATTN_W, ATTN_W, HEAD_DIM, "proj_v", out_dtype=F32)
        zx = _proj_split(h, wi, o_z, o_bc - o_z, GROUP_W, "proj_zx")
        bcm = _proj_split(h, wi, o_bc, o_dt - o_bc, SSM_STATE, "proj_bc")
        dt_raw = _proj_plain(h, wi[:, o_dt:], 0, 2 * SSM_HEADS, "proj_dt")

        attn = _attention(qk, v, brow, batch, seq)

        dt_row = (dt_raw.reshape(batch, nc, SSD_CHUNK, 2, G, nh)
                  .transpose(0, 4, 1, 3, 5, 2).reshape(batch, G, nc, 2 * nh, SSD_CHUNK))
        per_dir = lambda t: t.astype(F32).reshape(2, G, nh).transpose(1, 0, 2).reshape(G, 2 * nh)
        db, al = per_dir(dt_bias[layer]), per_dir(a_log[layer])
        cw, cbias = conv_w[layer].astype(F32), conv_b[layer].astype(F32)
        gn = G * SSM_STATE
        grp = lambda t, width: t.reshape(t.shape[0], G, width).transpose(1, 0, 2)
        ssd = _ssd(
            zx, bcm, dt_row,
            grp(cw[:, :SSM_W], GROUP_W), grp(cw[:, SSM_W:SSM_W + gn], SSM_STATE),
            grp(cw[:, SSM_W + gn:], SSM_STATE),
            grp(cbias[None, :SSM_W], GROUP_W), grp(cbias[None, SSM_W:SSM_W + gn], SSM_STATE),
            grp(cbias[None, SSM_W + gn:], SSM_STATE),
            db.reshape(G, 2 * nh, 1), al.reshape(G, 2 * nh, 1),
            jnp.repeat(d_skip[layer].astype(F32), SSM_HEAD_DIM).reshape(G, 1, GROUP_W),
            ssd_norm_g[layer].astype(F32).reshape(G, 1, GROUP_W),
            batch, seq)

        x2d = _out_proj(attn, ssd, w_out[layer].astype(BF16), x2d)
        x2d = _mlp(x2d, norm_mlp_g[layer], w_up[layer].astype(BF16), w_down[layer].astype(BF16))
    return x2d.reshape(batch, seq, D_MODEL)
```

```python
import functools
import math

import jax
import jax.numpy as jnp
from jax import lax
from jax.experimental import pallas as pl
from jax.experimental.pallas import tpu as pltpu

D_MODEL = 2048
ATTN_HEADS = 16
HEAD_DIM = 128
ATTN_W = ATTN_HEADS * HEAD_DIM
SSM_HEADS = 32
SSM_HEAD_DIM = 64
SSM_W = SSM_HEADS * SSM_HEAD_DIM
SSM_GROUPS = 8
HEADS_PER_GROUP = SSM_HEADS // SSM_GROUPS
GROUP_W = SSM_W // SSM_GROUPS
SSM_STATE = 128
CONV_WIDTH = 5
D_FF = 4 * D_MODEL
DILATIONS = (1, 4, 16)
HALF_WINDOW = 64
NUM_BUCKETS = 32
MAX_DISTANCE = 1024
NEG_INF = -1e30
EPS = 1e-6

LANES = 128
Q_SUB = 128
K_WIN = 256
SSD_CHUNK = 128
SSD_UNROLL = 4
SSD_UNROLL_FWD = 2
CONV_HALO = 8
VMEM_LIMIT = 56 * 1024 * 1024

F32 = jnp.float32
BF16 = jnp.bfloat16


def _params(*sem):
    return pltpu.CompilerParams(dimension_semantics=sem, vmem_limit_bytes=VMEM_LIMIT)


def _sigmoid(x):
    return 1.0 / (1.0 + jnp.exp(-x))


def _softplus(x):
    return jnp.maximum(x, 0.0) + jnp.log1p(jnp.exp(-jnp.abs(x)))


def _rmsnorm_dt_kernel(x_ref, g_ref, w_ref, h_ref, dt_ref):
    x = x_ref[...]
    ms = jnp.mean(x * x, axis=-1, keepdims=True)
    h = (x * lax.rsqrt(ms + EPS) * g_ref[...]).astype(h_ref.dtype)
    h_ref[...] = h
    dt_ref[...] = jnp.dot(h, w_ref[...], preferred_element_type=F32)


def _rmsnorm_dt(x2d, g, w_dt, tm=512):
    n, d = x2d.shape
    m = w_dt.shape[1]
    return pl.pallas_call(
        _rmsnorm_dt_kernel,
        grid=(n // tm,),
        in_specs=[pl.BlockSpec((tm, d), lambda i: (i, 0)),
                  pl.BlockSpec((1, d), lambda i: (0, 0)),
                  pl.BlockSpec((d, m), lambda i: (0, 0))],
        out_specs=[pl.BlockSpec((tm, d), lambda i: (i, 0)),
                   pl.BlockSpec((tm, m), lambda i: (i, 0))],
        out_shape=[jax.ShapeDtypeStruct((n, d), BF16), jax.ShapeDtypeStruct((n, m), F32)],
        compiler_params=_params("parallel"),
        name="rmsnorm_dt",
    )(x2d, g.reshape(1, d), w_dt)


QK_SUB = 256


def _proj_qk_kernel(a_ref, w_ref, g_ref, o_ref, *, heads_per_tile):
    a = a_ref[...]
    per = QK_SUB // HEAD_DIM
    for c in range(heads_per_tile // per):
        acc = jnp.dot(a, w_ref[:, c * QK_SUB:(c + 1) * QK_SUB], preferred_element_type=F32)
        for hh in range(per):
            h = c * per + hh
            s = acc[:, hh * HEAD_DIM:(hh + 1) * HEAD_DIM]
            ms = jnp.mean(s * s, axis=-1, keepdims=True)
            g = g_ref[:, h * HEAD_DIM:(h + 1) * HEAD_DIM]
            o_ref[h] = (s * lax.rsqrt(ms + EPS) * g).astype(o_ref.dtype)


def _proj_split_kernel(a_ref, w_ref, o_ref, *, width):
    a = a_ref[...]
    per = QK_SUB // width
    for s in range(w_ref.shape[1] // QK_SUB):
        acc = jnp.dot(a, w_ref[:, s * QK_SUB:(s + 1) * QK_SUB], preferred_element_type=F32)
        for c in range(per):
            o_ref[s * per + c] = acc[:, c * width:(c + 1) * width].astype(o_ref.dtype)


def _proj_qk(h, w, gains, tm=1024, tn=1024):
    n, k = h.shape
    m = 2 * ATTN_W
    hpt = tn // HEAD_DIM
    return pl.pallas_call(
        functools.partial(_proj_qk_kernel, heads_per_tile=hpt),
        grid=(m // tn, n // tm),
        in_specs=[pl.BlockSpec((tm, k), lambda j, i: (i, 0)),
                  pl.BlockSpec((k, tn), lambda j, i: (0, j)),
                  pl.BlockSpec((1, tn), lambda j, i: (0, j))],
        out_specs=pl.BlockSpec((hpt, tm, HEAD_DIM), lambda j, i: (j, i, 0)),
        out_shape=jax.ShapeDtypeStruct((m // HEAD_DIM, n, HEAD_DIM), F32),
        compiler_params=_params("parallel", "arbitrary"),
        name="proj_qk",
    )(h, w, gains)


def _proj_split(h, w, col0, m, width, name, out_dtype=BF16, tm=1024, tn=1024):
    n, k = h.shape
    cpt = tn // width
    j0 = col0 // tn
    return pl.pallas_call(
        functools.partial(_proj_split_kernel, width=width),
        grid=(m // tn, n // tm),
        in_specs=[pl.BlockSpec((tm, k), lambda j, i: (i, 0)),
                  pl.BlockSpec((k, tn), lambda j, i: (0, j0 + j))],
        out_specs=pl.BlockSpec((cpt, tm, width), lambda j, i: (j, i, 0)),
        out_shape=jax.ShapeDtypeStruct((m // width, n, width), out_dtype),
        compiler_params=_params("parallel", "arbitrary"),
        name=name,
    )(h, w)


def _t5_bucket(rel):
    nb = NUM_BUCKETS // 2
    max_exact = nb // 2
    ret = (rel > 0).astype(jnp.int32) * nb
    n = jnp.abs(rel)
    nf = jnp.maximum(n, 1).astype(jnp.float32)
    large = max_exact + (jnp.log(nf / max_exact) / math.log(MAX_DISTANCE / max_exact)
                         * (nb - max_exact)).astype(jnp.int32)
    large = jnp.minimum(large, nb - 1)
    return ret + jnp.where(n < max_exact, n, large)


def _bias_rows(rel_bias):
    period = K_WIN + Q_SUB
    m = jnp.arange(period)
    delta = jnp.where(m < K_WIN, m, m - period)
    rows = []
    for d in DILATIONS:
        for off in (0, -HALF_WINDOW, -2 * HALF_WINDOW):
            rel = delta + off
            valid = jnp.abs(rel) <= HALF_WINDOW
            b = rel_bias[_t5_bucket(rel * d)].astype(F32)
            rows.append(jnp.where(valid[:, None], b, NEG_INF))
    return jnp.stack(rows, axis=0).transpose(2, 0, 1)


ATTN_UNROLL_MERGE = 32
ATTN_UNROLL = 32


def _attn_kernel(q_ref, k_ref, v_ref, brow_ref, o_ref, bias_s, tmp_s, xq4, xt4,
                 q16, k1, k4, k16, v1, v4, v16, acc_s, max_s, den_s, *, seq):
    period = K_WIN + Q_SUB
    n4, n16 = seq // 4, seq // 16

    @pl.when(pl.program_id(1) == 0)
    def _():
        for idx in range(9):
            row = jnp.broadcast_to(brow_ref[idx:idx + 1, :], (Q_SUB, period))
            tile = pltpu.roll(row, 0, 1, stride=1, stride_axis=0)
            d = DILATIONS[idx // 3]
            if d == 16:
                bias_s[idx] = tile[:, :K_WIN]
                continue
            for half in range(K_WIN // LANES):
                tmp_s[half] = tile[:, half * LANES:(half + 1) * LANES]
            groups = 16 // d
            for half in range(K_WIN // LANES):
                for g in range(groups):
                    n = Q_SUB // groups
                    bias_s[idx, g * n:(g + 1) * n, half * LANES:(half + 1) * LANES] = (
                        tmp_s[half, pl.ds(g, n, stride=groups), :])

    cp = 256

    def split4(src, dst):
        for r4 in range(4):
            def body(c, carry, r4=r4):
                t0 = pl.multiple_of(c * cp, cp)
                dst[r4, pl.ds(t0, cp), :] = src[pl.ds(r4 + 4 * t0, cp, stride=4), :]
                return carry
            lax.fori_loop(0, n4 // cp, body, 0)

    def split16(src4, dst16, dst4=None):
        for r4 in range(4):
            for a in range(4):
                dst16[4 * a + r4] = src4[r4, pl.ds(a, n16, stride=4), :].astype(BF16)
            if dst4 is not None:
                def body(c, carry, r4=r4):
                    t0 = pl.multiple_of(c * cp, cp)
                    dst4[r4, pl.ds(t0, cp), :] = src4[r4, pl.ds(t0, cp), :].astype(BF16)
                    return carry
                lax.fori_loop(0, n4 // cp, body, 0)

    def cast(src, dst):
        def body(c, carry):
            t0 = pl.multiple_of(c * cp, cp)
            dst[pl.ds(t0, cp), :] = src[pl.ds(t0, cp), :].astype(BF16)
            return carry
        lax.fori_loop(0, seq // cp, body, 0)

    split4(q_ref, xq4)
    split16(xq4, q16)
    for src, d1, d4, d16 in ((k_ref, k1, k4, k16), (v_ref, v1, v4, v16)):
        cast(src, d1)
        split4(src, xt4)
        split16(xt4, d16, d4)

    ones = jnp.ones((K_WIN, LANES), BF16)

    def sub_tile(q, k_s, v_s, length, s0, pi):
        if isinstance(s0, int):
            w0 = min(max(s0 - HALF_WINDOW, 0), length - K_WIN)
            place = 0 if s0 == 0 else (2 if s0 == length - Q_SUB else 1)
        else:
            w0 = pl.multiple_of(jnp.clip(s0 - HALF_WINDOW, 0, length - K_WIN), HALF_WINDOW)
            place = jnp.where(s0 == 0, 0, jnp.where(s0 == length - Q_SUB, 2, 1))
        k = k_s[pl.ds(w0, K_WIN), :]
        v = v_s[pl.ds(w0, K_WIN), :]
        s = lax.dot_general(q, k, (((1,), (1,)), ((), ())), preferred_element_type=F32)
        s = s + bias_s[3 * pi + place]
        m = jnp.max(s, axis=-1, keepdims=True)
        p = jnp.exp(s - m).astype(BF16)
        pv = jnp.dot(p, jnp.concatenate([v, ones], axis=1), preferred_element_type=F32)
        return pv[:, :LANES], jnp.broadcast_to(m, (Q_SUB, LANES)), pv[:, LANES:]

    def store(pi, res, slab_of_group, l0, n):
        for g in range(Q_SUB // n):
            for val, dst in zip(res, (acc_s, max_s, den_s)):
                dst[pi, slab_of_group(g), pl.ds(l0, n), :] = val[g * n:(g + 1) * n]

    U = ATTN_UNROLL

    def body4(it, carry):
        res = []
        for u in range(U // 4):
            l0 = pl.multiple_of((it * (U // 4) + u) * 32, 32)
            for r4 in range(4):
                q = jnp.concatenate([q16[4 * a + r4, pl.ds(l0, 32), :] for a in range(4)], axis=0)
                res.append((sub_tile(q, k4.at[r4], v4.at[r4], n4, pl.multiple_of(4 * l0, Q_SUB), 1),
                            r4, l0))
        for r3, r4, l0 in res:
            store(0, r3, lambda g, r4=r4: 4 * g + r4, l0, 32)
        return carry
    lax.fori_loop(0, n4 // Q_SUB // (U // 4), body4, 0)

    per16 = U // (n16 // Q_SUB)

    def body16(it, carry):
        g0 = it * per16
        res = []
        for u in range(per16):
            for l0 in range(0, n16, Q_SUB):
                res.append((sub_tile(q16[g0 + u, pl.ds(l0, Q_SUB), :], k16.at[g0 + u], v16.at[g0 + u],
                                     n16, l0, 2), u, l0))
        for r3, u, l0 in res:
            store(1, r3, lambda g, u=u: g0 + u, l0, Q_SUB)
        return carry
    lax.fori_loop(0, 16 // per16, body16, 0)

    U1 = ATTN_UNROLL_MERGE

    def body1(it, carry):
        res = []
        for u in range(U1):
            l0 = pl.multiple_of((it * U1 + u) * 8, 8)
            q = jnp.concatenate([xq4[r % 4, pl.ds(4 * l0 + r // 4, 8, stride=4), :] for r in range(16)],
                                axis=0).astype(BF16)
            res.append((sub_tile(q, k1, v1, seq, pl.multiple_of(16 * l0, Q_SUB), 0), l0))
        for (aa, ma, da), l0 in res:
            rows = lambda ref, pi: jnp.concatenate([ref[pi, r, pl.ds(l0, 8), :] for r in range(16)], axis=0)
            mb, mc = rows(max_s, 0), rows(max_s, 1)
            mx = jnp.maximum(jnp.maximum(ma, mb), mc)
            ea, eb, ec = jnp.exp(ma - mx), jnp.exp(mb - mx), jnp.exp(mc - mx)
            num = ea * aa + eb * rows(acc_s, 0) + ec * rows(acc_s, 1)
            den = ea * da + eb * rows(den_s, 0) + ec * rows(den_s, 1)
            out = num / den
            for r in range(16):
                o_ref[pl.ds(r + 16 * l0, 8, stride=16), :] = out[r * 8:(r + 1) * 8]
        return carry
    lax.fori_loop(0, seq // Q_SUB // U1, body1, 0)


def _attention(qk, v, brow, batch, seq):
    H = ATTN_HEADS
    n4, n16 = seq // 4, seq // 16
    qk4 = qk.reshape(2 * H, batch, seq, HEAD_DIM)
    v4 = v.reshape(H, batch, seq, HEAD_DIM)
    in_specs = [pl.BlockSpec((None, None, seq, HEAD_DIM), lambda h, b: (h, b, 0, 0)),
                pl.BlockSpec((None, None, seq, HEAD_DIM), lambda h, b: (H + h, b, 0, 0)),
                pl.BlockSpec((None, None, seq, HEAD_DIM), lambda h, b: (h, b, 0, 0))]
    in_specs.append(pl.BlockSpec((None, 9, K_WIN + Q_SUB), lambda h, b: (h, 0, 0)))
    kv_slabs = [pltpu.VMEM((seq, LANES), BF16), pltpu.VMEM((4, n4, LANES), BF16),
                pltpu.VMEM((16, n16, LANES), BF16)]
    out = pl.pallas_call(
        functools.partial(_attn_kernel, seq=seq),
        grid=(H, batch),
        in_specs=in_specs,
        out_specs=pl.BlockSpec((None, None, seq, HEAD_DIM), lambda h, b: (h, b, 0, 0)),
        out_shape=jax.ShapeDtypeStruct((H, batch, seq, HEAD_DIM), F32),
        scratch_shapes=[pltpu.VMEM((9, Q_SUB, K_WIN), F32),
                        pltpu.VMEM((K_WIN // LANES, Q_SUB, LANES), F32),
                        pltpu.VMEM((4, n4, LANES), F32),
                        pltpu.VMEM((4, n4, LANES), F32),
                        pltpu.VMEM((16, n16, LANES), BF16)]
                       + kv_slabs + kv_slabs
                       + [pltpu.VMEM((2, 16, n16, LANES), F32)] * 3,
        compiler_params=_params("parallel", "arbitrary"),
        name="dilated_attention",
    )(qk4, qk4, v4, brow)
    return out.reshape(H, batch * seq, HEAD_DIM)


def _ssd_kernel(z_ref, x_ref, b_ref, c_ref, dtr_ref, cwx_ref, cwb_ref, cwc_ref,
                cbx_ref, cbb_ref, cbc_ref, dbr_ref, alr_ref, dsk_ref, ng_ref,
                o_ref, pad, cv, y_s, arg_s, diag_s, rows_s, cols_s, sbs, sf, sb, *, seq):
    T = SSD_CHUNK
    nc = seq // T
    hi = lax.Precision.HIGHEST
    halo = CONV_HALO
    half = T // 2

    pad[:, 0:halo, :] = jnp.zeros((4, halo, LANES), F32)
    pad[:, seq + halo:seq + 2 * halo, :] = jnp.zeros((4, halo, LANES), F32)

    def fill(i, carry):
        t0 = pl.multiple_of(i * T, T)
        xin = x_ref[pl.ds(t0, T), :].astype(F32)
        pad[0, pl.ds(t0 + halo, T), :] = xin[:, :LANES]
        pad[1, pl.ds(t0 + halo, T), :] = xin[:, LANES:]
        pad[2, pl.ds(t0 + halo, T), :] = b_ref[pl.ds(t0, T), :].astype(F32)
        pad[3, pl.ds(t0 + halo, T), :] = c_ref[pl.ds(t0, T), :].astype(F32)
        return carry
    lax.fori_loop(0, nc, fill, 0)

    cws = (cwx_ref[:, :LANES], cwx_ref[:, LANES:], cwb_ref[...], cwc_ref[...])
    cbs = (cbx_ref[:, :LANES], cbx_ref[:, LANES:], cbb_ref[...], cbc_ref[...])

    def conv_chunk(ci):
        t0 = pl.multiple_of(ci * T, T)
        for s in range(4):
            ev = [pad[s, pl.ds(t0 + halo + 2 * k, half, stride=2), :] for k in (-1, 0, 1)]
            od = [pad[s, pl.ds(t0 + halo + 1 + 2 * k, half, stride=2), :] for k in (-1, 0, 1)]
            w = [cws[s][j:j + 1, :] for j in range(CONV_WIDTH)]
            out_e = cbs[s] + w[0] * ev[0] + w[1] * od[0] + w[2] * ev[1] + w[3] * od[1] + w[4] * ev[2]
            out_o = cbs[s] + w[0] * od[0] + w[1] * ev[1] + w[2] * od[1] + w[3] * ev[2] + w[4] * od[2]
            cv[s, pl.ds(t0, half, stride=2), :] = out_e * _sigmoid(out_e)
            cv[s, pl.ds(t0 + 1, half, stride=2), :] = out_o * _sigmoid(out_o)

    def load_x(t0):
        return jnp.concatenate([cv[0, pl.ds(t0, T), :], cv[1, pl.ds(t0, T), :]], axis=1)

    nh = HEADS_PER_GROUP
    nd = 2 * nh
    row = lax.broadcasted_iota(jnp.int32, (T, T), 0)
    col = lax.broadcasted_iota(jnp.int32, (T, T), 1)
    triu = (row <= col).astype(F32)
    lower = col < row
    upper = col > row
    lane_head = lax.broadcasted_iota(jnp.int32, (T, GROUP_W), 1) // SSM_HEAD_DIM

    dt = _softplus(dtr_ref[...] + dbr_ref[...])
    a = dt * (-jnp.exp(alr_ref[...]))
    cum = jnp.dot(a.reshape(nc * nd, T), triu, precision=hi,
                  preferred_element_type=F32).reshape(nc, nd, T)
    last = cum[:, :, T - 1:T]
    exc = cum - a
    fwd = lax.broadcasted_iota(jnp.int32, (nc, nd, T), 1) < nh
    base = jnp.where(fwd, cum, exc)
    log_dt = jnp.log(dt)
    arg_s[...] = jnp.where(fwd, cum - log_dt, exc + log_dt)
    log_sum = jnp.log(dt[:, 0:nh, :] + dt[:, nh:, :])
    diag_s[...] = jnp.concatenate([log_sum, log_sum], axis=1)
    rows_s[:, 0:nd, :] = base
    rows_s[:, nd:2 * nd, :] = jnp.exp(jnp.where(fwd, cum, last - exc))
    rows_s[:, 2 * nd:3 * nd, :] = dt * jnp.exp(jnp.where(fwd, last - cum, exc))
    rows_s[:, 3 * nd:, :] = jnp.zeros((nc, LANES - 3 * nd, T), F32)

    def expand(cols, first):
        n = cols.shape[0]
        low = lax.broadcasted_iota(jnp.int32, (n, LANES), 1) < SSM_HEAD_DIM
        halves = []
        for j in range(GROUP_W // LANES):
            c0 = jnp.broadcast_to(cols[:, first + 2 * j:first + 2 * j + 1], (n, LANES))
            c1 = jnp.broadcast_to(cols[:, first + 2 * j + 1:first + 2 * j + 2], (n, LANES))
            halves.append(jnp.where(low, c0, c1))
        return jnp.concatenate(halves, axis=1)

    tn_dims = (((0,), (0,)), ((), ()))

    cid = lax.broadcasted_iota(jnp.int32, (LANES, 4 * GROUP_W), 0)
    lid = lax.broadcasted_iota(jnp.int32, (LANES, 4 * GROUP_W), 1)
    blk = lid // GROUP_W
    want = nd + (blk % 2) * nh + (blk // 2) * nd + (lid % GROUP_W) // SSM_HEAD_DIM
    spread = jnp.where(cid == want, 1.0, 0.0).astype(BF16)
    spread_f = spread[:, :3 * GROUP_W]
    spread_b = spread[:, 3 * GROUP_W:]

    sb[...] = jnp.zeros_like(sb)

    def state_back(ci):
        t0 = pl.multiple_of(ci * T, T)
        ct = rows_s[ci].T
        cols_s[ci] = ct
        s_prev = sb[...]
        sbs[ci] = s_prev.astype(BF16)
        wx = jnp.dot(ct.astype(BF16), spread_b, preferred_element_type=F32)
        xs = (load_x(t0) * wx).astype(BF16)
        sb[...] = expand(ct[0:1, :], nd + nh) * s_prev + lax.dot_general(
            cv[2, pl.ds(t0, T), :].astype(BF16), xs, tn_dims, preferred_element_type=F32)

    U = SSD_UNROLL
    for u in range(U):
        conv_chunk(nc - 1 - u)

    def sweep_back(i, carry):
        k = nc - 1 - U * i
        for u in range(U):
            state_back(k + U - u)
        for u in range(U):
            conv_chunk(k - u)
        return carry
    lax.fori_loop(1, nc // U, sweep_back, 0)
    for u in range(U):
        state_back(U - 1 - u)

    sf[...] = jnp.zeros_like(sf)
    low_half = lax.broadcasted_iota(jnp.int32, (T, LANES), 1) < SSM_HEAD_DIM

    def finish(ci):
        t0 = pl.multiple_of(ci * T, T)
        zz = z_ref[pl.ds(t0, T), :].astype(F32)
        y = y_s[pl.ds(t0, T), :] * (zz * _sigmoid(zz))
        ms = jnp.mean(y * y, axis=-1, keepdims=True)
        o_ref[pl.ds(t0, T), :] = (y * lax.rsqrt(ms + EPS) * ng_ref[...]).astype(o_ref.dtype)

    def chunk_fwd(ci):
        t0 = pl.multiple_of(ci * T, T)
        ct = cols_s[ci]
        arg_r = arg_s[ci]
        dt_r = diag_s[ci]
        x = load_x(t0)
        bk = cv[2, pl.ds(t0, T), :].astype(BF16)
        ck = cv[3, pl.ds(t0, T), :].astype(BF16)
        cb = lax.dot_general(ck, bk, (((1,), (1,)), ((), ())), preferred_element_type=F32)
        spreadv = jnp.dot(ct.astype(BF16), spread_f, preferred_element_type=F32)
        lhs = []
        for h in range(nh):
            hb = nh + h
            arg = jnp.where(lower, ct[:, h:h + 1] - arg_r[h:h + 1, :],
                            jnp.where(upper, arg_r[hb:hb + 1, :] - ct[:, hb:hb + 1], dt_r[h:h + 1, :]))
            lhs.append((cb * jnp.exp(arg)).astype(BF16))
        s_prev = sf[...]
        y4 = jnp.dot(jnp.concatenate(lhs, axis=0), x.astype(BF16), preferred_element_type=F32)
        off = jnp.dot(ck, jnp.concatenate([s_prev.astype(BF16), sbs[ci]], axis=1),
                      preferred_element_type=F32)
        y = x * dsk_ref[...] + jnp.concatenate(
            [jnp.where(low_half, y4[2 * j * T:(2 * j + 1) * T, j * LANES:(j + 1) * LANES],
                       y4[(2 * j + 1) * T:(2 * j + 2) * T, j * LANES:(j + 1) * LANES])
             for j in range(GROUP_W // LANES)], axis=1)
        y = y + spreadv[:, :GROUP_W] * off[:, :GROUP_W] + spreadv[:, GROUP_W:2 * GROUP_W] * off[:, GROUP_W:]
        xs = (x * spreadv[:, 2 * GROUP_W:]).astype(BF16)
        sf[...] = expand(ct[T - 1:T, :], nd) * s_prev + lax.dot_general(
            bk, xs, tn_dims, preferred_element_type=F32)
        y_s[pl.ds(t0, T), :] = y

    U = SSD_UNROLL_FWD
    for u in range(U):
        chunk_fwd(u)

    def sweep_fwd(i, carry):
        c = U * i
        for u in range(U):
            finish(c - U + u)
        for u in range(U):
            chunk_fwd(c + u)
        return carry
    lax.fori_loop(1, nc // U, sweep_fwd, 0)
    for u in range(U):
        finish(nc - U + u)


def _ssd(zx, bcm, dt_row, cwx, cwb, cwc, cbx, cbb, cbc, db_row, al_row, dskip, ng, batch, seq):
    G = SSM_GROUPS
    nc = seq // SSD_CHUNK
    zx4 = zx.reshape(2 * G, batch, seq, GROUP_W)
    bc4 = bcm.reshape(2 * G, batch, seq, SSM_STATE)
    nd = 2 * HEADS_PER_GROUP

    def per_group(shape):
        return pl.BlockSpec((None,) + shape, lambda b, g: (g,) + (0,) * len(shape))

    in_specs = [
        pl.BlockSpec((None, None, seq, GROUP_W), lambda b, g: (g, b, 0, 0)),
        pl.BlockSpec((None, None, seq, GROUP_W), lambda b, g: (G + g, b, 0, 0)),
        pl.BlockSpec((None, None, seq, SSM_STATE), lambda b, g: (g, b, 0, 0)),
        pl.BlockSpec((None, None, seq, SSM_STATE), lambda b, g: (G + g, b, 0, 0)),
        pl.BlockSpec((None, None, nc, nd, SSD_CHUNK), lambda b, g: (b, g, 0, 0, 0)),
        per_group((CONV_WIDTH, GROUP_W)), per_group((CONV_WIDTH, SSM_STATE)),
        per_group((CONV_WIDTH, SSM_STATE)),
        per_group((1, GROUP_W)), per_group((1, SSM_STATE)), per_group((1, SSM_STATE)),
        per_group((nd, 1)), per_group((nd, 1)),
        per_group((1, GROUP_W)), per_group((1, GROUP_W)),
    ]
    out = pl.pallas_call(
        functools.partial(_ssd_kernel, seq=seq),
        grid=(batch, G),
        in_specs=in_specs,
        out_specs=pl.BlockSpec((None, None, seq, GROUP_W), lambda b, g: (g, b, 0, 0)),
        out_shape=jax.ShapeDtypeStruct((G, batch, seq, GROUP_W), BF16),
        scratch_shapes=[pltpu.VMEM((4, seq + 2 * CONV_HALO, LANES), F32),
                        pltpu.VMEM((4, seq, LANES), F32),
                        pltpu.VMEM((seq, GROUP_W), F32),
                        pltpu.VMEM((nc, nd, SSD_CHUNK), F32),
                        pltpu.VMEM((nc, nd, SSD_CHUNK), F32),
                        pltpu.VMEM((nc, LANES, SSD_CHUNK), F32),
                        pltpu.VMEM((nc, SSD_CHUNK, LANES), F32),
                        pltpu.VMEM((nc, SSM_STATE, GROUP_W), BF16),
                        pltpu.VMEM((SSM_STATE, GROUP_W), F32),
                        pltpu.VMEM((SSM_STATE, GROUP_W), F32)],
        compiler_params=_params("parallel", "parallel"),
        name="ssd",
    )(zx4, zx4, bc4, bc4, dt_row, cwx, cwb, cwc, cbx, cbb, cbc, db_row, al_row, dskip, ng)
    return out.reshape(G, batch * seq, GROUP_W)


def _out_proj_kernel(attn_ref, ssd_ref, w_ref, x_ref, o_ref, lhs):
    for h in range(ATTN_HEADS):
        lhs[:, h * HEAD_DIM:(h + 1) * HEAD_DIM] = attn_ref[h].astype(BF16)
    for g in range(SSM_GROUPS):
        lhs[:, ATTN_W + g * GROUP_W:ATTN_W + (g + 1) * GROUP_W] = ssd_ref[g]
    o_ref[...] = x_ref[...] + jnp.dot(lhs[...], w_ref[...], preferred_element_type=F32)


def _out_proj(attn, ssd, w, x2d, tm=256):
    n, d = x2d.shape
    kk = w.shape[0]
    return pl.pallas_call(
        _out_proj_kernel,
        grid=(n // tm,),
        in_specs=[pl.BlockSpec((ATTN_HEADS, tm, HEAD_DIM), lambda i: (0, i, 0)),
                  pl.BlockSpec((SSM_GROUPS, tm, GROUP_W), lambda i: (0, i, 0)),
                  pl.BlockSpec((kk, d), lambda i: (0, 0), pipeline_mode=pl.Buffered(1)),
                  pl.BlockSpec((tm, d), lambda i: (i, 0))],
        out_specs=pl.BlockSpec((tm, d), lambda i: (i, 0)),
        out_shape=jax.ShapeDtypeStruct((n, d), F32),
        scratch_shapes=[pltpu.VMEM((tm, kk), BF16)],
        compiler_params=_params("parallel"),
        name="out_proj",
    )(attn, ssd, w, x2d)


def _mlp_kernel(x_ref, g_ref, wu_ref, wd_ref, o_ref, hm):
    @pl.when(pl.program_id(1) == 0)
    def _():
        x = x_ref[...]
        ms = jnp.mean(x * x, axis=-1, keepdims=True)
        hm[...] = (x * lax.rsqrt(ms + EPS) * g_ref[...]).astype(hm.dtype)
        o_ref[...] = x
    u = jnp.maximum(jnp.dot(hm[...], wu_ref[...], preferred_element_type=F32), 0.0)
    o_ref[...] += jnp.dot((u * u).astype(BF16), wd_ref[...], preferred_element_type=F32)


def _mlp(x2d, g, wu, wd, tm=512, tf=1024):
    n, d = x2d.shape
    f = wu.shape[1]
    return pl.pallas_call(
        _mlp_kernel,
        grid=(n // tm, f // tf),
        in_specs=[pl.BlockSpec((tm, d), lambda i, j: (i, 0)),
                  pl.BlockSpec((1, d), lambda i, j: (0, 0)),
                  pl.BlockSpec((d, tf), lambda i, j: (0, j)),
                  pl.BlockSpec((tf, d), lambda i, j: (j, 0))],
        out_specs=pl.BlockSpec((tm, d), lambda i, j: (i, 0)),
        out_shape=jax.ShapeDtypeStruct((n, d), F32),
        scratch_shapes=[pltpu.VMEM((tm, d), BF16)],
        compiler_params=_params("parallel", "arbitrary"),
        name="mlp",
    )(x2d, g.reshape(1, d), wu, wd)


def kernel(x, norm_mix_g, w_in, q_norm_g, k_norm_g, rel_bias, conv_w, conv_b, dt_bias, a_log,
           d_skip, ssd_norm_g, w_out, norm_mlp_g, w_up, w_down):
    batch, seq, _ = x.shape
    n = batch * seq
    G, nh = SSM_GROUPS, HEADS_PER_GROUP
    nc = seq // SSD_CHUNK
    o_z = 3 * ATTN_W
    o_bc = o_z + 2 * SSM_W
    o_dt = o_bc + 2 * G * SSM_STATE
    x2d = x.reshape(n, D_MODEL)
    brow = _bias_rows(rel_bias)

    for layer in range(w_in.shape[0]):
        wi = w_in[layer].astype(BF16)
        h, dt_raw = _rmsnorm_dt(x2d, norm_mix_g[layer], wi[:, o_dt:])

        scale = 1.0 / math.sqrt(HEAD_DIM)
        gains = jnp.concatenate([jnp.tile(q_norm_g[layer].astype(F32) * scale, ATTN_HEADS),
                                 jnp.tile(k_norm_g[layer].astype(F32), ATTN_HEADS)]).reshape(1, 2 * ATTN_W)
        qk = _proj_qk(h, wi, gains)
        v = _proj_split(h, wi, 2 * ATTN_W, ATTN_W, HEAD_DIM, "proj_v", out_dtype=F32)
        zx = _proj_split(h, wi, o_z, o_bc - o_z, GROUP_W, "proj_zx")
        bcm = _proj_split(h, wi, o_bc, o_dt - o_bc, SSM_STATE, "proj_bc")

        attn = _attention(qk, v, brow, batch, seq)

        dt_row = (dt_raw.reshape(batch, nc, SSD_CHUNK, 2, G, nh)
                  .transpose(0, 4, 1, 3, 5, 2).reshape(batch, G, nc, 2 * nh, SSD_CHUNK))
        per_dir = lambda t: t.astype(F32).reshape(2, G, nh).transpose(1, 0, 2).reshape(G, 2 * nh)
        db, al = per_dir(dt_bias[layer]), per_dir(a_log[layer])
        cw, cbias = conv_w[layer].astype(F32), conv_b[layer].astype(F32)
        gn = G * SSM_STATE
        grp = lambda t, width: t.reshape(t.shape[0], G, width).transpose(1, 0, 2)
        ssd = _ssd(
            zx, bcm, dt_row,
            grp(cw[:, :SSM_W], GROUP_W), grp(cw[:, SSM_W:SSM_W + gn], SSM_STATE),
            grp(cw[:, SSM_W + gn:], SSM_STATE),
            grp(cbias[None, :SSM_W], GROUP_W), grp(cbias[None, SSM_W:SSM_W + gn], SSM_STATE),
            grp(cbias[None, SSM_W + gn:], SSM_STATE),
            db.reshape(G, 2 * nh, 1), al.reshape(G, 2 * nh, 1),
            jnp.repeat(d_skip[layer].astype(F32), SSM_HEAD_DIM).reshape(G, 1, GROUP_W),
            ssd_norm_g[layer].astype(F32).reshape(G, 1, GROUP_W),
            batch, seq)

        x2d = _out_proj(attn, ssd, w_out[layer].astype(BF16), x2d)
        x2d = _mlp(x2d, norm_mlp_g[layer], w_up[layer].astype(BF16), w_down[layer].astype(BF16))
    return x2d.reshape(batch, seq, D_MODEL)
```

```python
import functools
import math

import jax
import jax.numpy as jnp
from jax import lax
from jax.experimental import pallas as pl
from jax.experimental.pallas import tpu as pltpu

D_MODEL = 2048
ATTN_HEADS = 16
HEAD_DIM = 128
ATTN_W = ATTN_HEADS * HEAD_DIM
SSM_HEADS = 32
SSM_HEAD_DIM = 64
SSM_W = SSM_HEADS * SSM_HEAD_DIM
SSM_GROUPS = 8
HEADS_PER_GROUP = SSM_HEADS // SSM_GROUPS
GROUP_W = SSM_W // SSM_GROUPS
SSM_STATE = 128
CONV_WIDTH = 5
D_FF = 4 * D_MODEL
DILATIONS = (1, 4, 16)
HALF_WINDOW = 64
NUM_BUCKETS = 32
MAX_DISTANCE = 1024
NEG_INF = -1e30
EPS = 1e-6

LANES = 128
Q_SUB = 128
K_WIN = 256
SSD_CHUNK = 128
SSD_UNROLL = 8
SSD_UNROLL_FWD = 4
CONV_HALO = 8
VMEM_LIMIT = 56 * 1024 * 1024

F32 = jnp.float32
BF16 = jnp.bfloat16


def _params(*sem):
    return pltpu.CompilerParams(dimension_semantics=sem, vmem_limit_bytes=VMEM_LIMIT)


def _sigmoid(x):
    return 1.0 / (1.0 + jnp.exp(-x))


def _softplus(x):
    return jnp.maximum(x, 0.0) + jnp.log1p(jnp.exp(-jnp.abs(x)))


def _rmsnorm_dt_kernel(x_ref, g_ref, w_ref, h_ref, dt_ref):
    x = x_ref[...]
    ms = jnp.mean(x * x, axis=-1, keepdims=True)
    h = (x * lax.rsqrt(ms + EPS) * g_ref[...]).astype(h_ref.dtype)
    h_ref[...] = h
    dt_ref[...] = jnp.dot(h, w_ref[...], preferred_element_type=F32)


def _rmsnorm_dt(x2d, g, w_dt, tm=512):
    n, d = x2d.shape
    m = w_dt.shape[1]
    return pl.pallas_call(
        _rmsnorm_dt_kernel,
        grid=(n // tm,),
        in_specs=[pl.BlockSpec((tm, d), lambda i: (i, 0)),
                  pl.BlockSpec((1, d), lambda i: (0, 0)),
                  pl.BlockSpec((d, m), lambda i: (0, 0))],
        out_specs=[pl.BlockSpec((tm, d), lambda i: (i, 0)),
                   pl.BlockSpec((tm, m), lambda i: (i, 0))],
        out_shape=[jax.ShapeDtypeStruct((n, d), BF16), jax.ShapeDtypeStruct((n, m), F32)],
        compiler_params=_params("parallel"),
        name="rmsnorm_dt",
    )(x2d, g.reshape(1, d), w_dt)


QK_SUB = 256


def _proj_qk_kernel(a_ref, w_ref, g_ref, o_ref, *, heads_per_tile):
    a = a_ref[...]
    per = QK_SUB // HEAD_DIM
    for c in range(heads_per_tile // per):
        acc = jnp.dot(a, w_ref[:, c * QK_SUB:(c + 1) * QK_SUB], preferred_element_type=F32)
        for hh in range(per):
            h = c * per + hh
            s = acc[:, hh * HEAD_DIM:(hh + 1) * HEAD_DIM]
            ms = jnp.mean(s * s, axis=-1, keepdims=True)
            g = g_ref[:, h * HEAD_DIM:(h + 1) * HEAD_DIM]
            o_ref[h] = (s * lax.rsqrt(ms + EPS) * g).astype(o_ref.dtype)


def _proj_split_kernel(a_ref, w_ref, o_ref, *, width):
    a = a_ref[...]
    per = QK_SUB // width
    for s in range(w_ref.shape[1] // QK_SUB):
        acc = jnp.dot(a, w_ref[:, s * QK_SUB:(s + 1) * QK_SUB], preferred_element_type=F32)
        for c in range(per):
            o_ref[s * per + c] = acc[:, c * width:(c + 1) * width].astype(o_ref.dtype)


def _proj_qk(h, w, gains, tm=1024, tn=1024):
    n, k = h.shape
    m = 2 * ATTN_W
    hpt = tn // HEAD_DIM
    return pl.pallas_call(
        functools.partial(_proj_qk_kernel, heads_per_tile=hpt),
        grid=(m // tn, n // tm),
        in_specs=[pl.BlockSpec((tm, k), lambda j, i: (i, 0)),
                  pl.BlockSpec((k, tn), lambda j, i: (0, j)),
                  pl.BlockSpec((1, tn), lambda j, i: (0, j))],
        out_specs=pl.BlockSpec((hpt, tm, HEAD_DIM), lambda j, i: (j, i, 0)),
        out_shape=jax.ShapeDtypeStruct((m // HEAD_DIM, n, HEAD_DIM), F32),
        compiler_params=_params("parallel", "arbitrary"),
        name="proj_qk",
    )(h, w, gains)


def _proj_split(h, w, col0, m, width, name, out_dtype=BF16, tm=1024, tn=1024):
    n, k = h.shape
    cpt = tn // width
    j0 = col0 // tn
    return pl.pallas_call(
        functools.partial(_proj_split_kernel, width=width),
        grid=(m // tn, n // tm),
        in_specs=[pl.BlockSpec((tm, k), lambda j, i: (i, 0)),
                  pl.BlockSpec((k, tn), lambda j, i: (0, j0 + j))],
        out_specs=pl.BlockSpec((cpt, tm, width), lambda j, i: (j, i, 0)),
        out_shape=jax.ShapeDtypeStruct((m // width, n, width), out_dtype),
        compiler_params=_params("parallel", "arbitrary"),
        name=name,
    )(h, w)


def _t5_bucket(rel):
    nb = NUM_BUCKETS // 2
    max_exact = nb // 2
    ret = (rel > 0).astype(jnp.int32) * nb
    n = jnp.abs(rel)
    nf = jnp.maximum(n, 1).astype(jnp.float32)
    large = max_exact + (jnp.log(nf / max_exact) / math.log(MAX_DISTANCE / max_exact)
                         * (nb - max_exact)).astype(jnp.int32)
    large = jnp.minimum(large, nb - 1)
    return ret + jnp.where(n < max_exact, n, large)


def _bias_rows(rel_bias):
    period = K_WIN + Q_SUB
    m = jnp.arange(period)
    delta = jnp.where(m < K_WIN, m, m - period)
    rows = []
    for d in DILATIONS:
        for off in (0, -HALF_WINDOW, -2 * HALF_WINDOW):
            rel = delta + off
            valid = jnp.abs(rel) <= HALF_WINDOW
            b = rel_bias[_t5_bucket(rel * d)].astype(F32)
            rows.append(jnp.where(valid[:, None], b, NEG_INF))
    return jnp.stack(rows, axis=0).transpose(2, 0, 1)


ATTN_UNROLL_MERGE = 32
ATTN_UNROLL = 32


def _attn_kernel(q_ref, k_ref, v_ref, brow_ref, o_ref, bias_s, tmp_s, xq4, xt4,
                 q16, k1, k4, k16, v1, v4, v16, acc_s, max_s, den_s, *, seq):
    period = K_WIN + Q_SUB
    n4, n16 = seq // 4, seq // 16

    @pl.when(pl.program_id(1) == 0)
    def _():
        for idx in range(9):
            row = jnp.broadcast_to(brow_ref[idx:idx + 1, :], (Q_SUB, period))
            tile = pltpu.roll(row, 0, 1, stride=1, stride_axis=0)
            d = DILATIONS[idx // 3]
            if d == 16:
                bias_s[idx] = tile[:, :K_WIN]
                continue
            for half in range(K_WIN // LANES):
                tmp_s[half] = tile[:, half * LANES:(half + 1) * LANES]
            groups = 16 // d
            for half in range(K_WIN // LANES):
                for g in range(groups):
                    n = Q_SUB // groups
                    bias_s[idx, g * n:(g + 1) * n, half * LANES:(half + 1) * LANES] = (
                        tmp_s[half, pl.ds(g, n, stride=groups), :])

    cp = 256

    def split4(src, dst):
        for r4 in range(4):
            def body(c, carry, r4=r4):
                t0 = pl.multiple_of(c * cp, cp)
                dst[r4, pl.ds(t0, cp), :] = src[pl.ds(r4 + 4 * t0, cp, stride=4), :]
                return carry
            lax.fori_loop(0, n4 // cp, body, 0)

    def split16(src4, dst16, dst4=None):
        for r4 in range(4):
            for a in range(4):
                dst16[4 * a + r4] = src4[r4, pl.ds(a, n16, stride=4), :].astype(BF16)
            if dst4 is not None:
                def body(c, carry, r4=r4):
                    t0 = pl.multiple_of(c * cp, cp)
                    dst4[r4, pl.ds(t0, cp), :] = src4[r4, pl.ds(t0, cp), :].astype(BF16)
                    return carry
                lax.fori_loop(0, n4 // cp, body, 0)

    def cast(src, dst):
        def body(c, carry):
            t0 = pl.multiple_of(c * cp, cp)
            dst[pl.ds(t0, cp), :] = src[pl.ds(t0, cp), :].astype(BF16)
            return carry
        lax.fori_loop(0, seq // cp, body, 0)

    split4(q_ref, xq4)
    split16(xq4, q16)
    for src, d1, d4, d16 in ((k_ref, k1, k4, k16), (v_ref, v1, v4, v16)):
        cast(src, d1)
        split4(src, xt4)
        split16(xt4, d16, d4)

    ones = jnp.ones((K_WIN, LANES), BF16)

    def sub_tile(q, k_s, v_s, length, s0, pi):
        if isinstance(s0, int):
            w0 = min(max(s0 - HALF_WINDOW, 0), length - K_WIN)
            place = 0 if s0 == 0 else (2 if s0 == length - Q_SUB else 1)
        else:
            w0 = pl.multiple_of(jnp.clip(s0 - HALF_WINDOW, 0, length - K_WIN), HALF_WINDOW)
            place = jnp.where(s0 == 0, 0, jnp.where(s0 == length - Q_SUB, 2, 1))
        k = k_s[pl.ds(w0, K_WIN), :]
        v = v_s[pl.ds(w0, K_WIN), :]
        s = lax.dot_general(q, k, (((1,), (1,)), ((), ())), preferred_element_type=F32)
        s = s + bias_s[3 * pi + place]
        m = jnp.max(s, axis=-1, keepdims=True)
        p = jnp.exp(s - m).astype(BF16)
        pv = jnp.dot(p, jnp.concatenate([v, ones], axis=1), preferred_element_type=F32)
        return pv[:, :LANES], jnp.broadcast_to(m, (Q_SUB, LANES)), pv[:, LANES:]

    def store(pi, res, slab_of_group, l0, n):
        for g in range(Q_SUB // n):
            for val, dst in zip(res, (acc_s, max_s, den_s)):
                dst[pi, slab_of_group(g), pl.ds(l0, n), :] = val[g * n:(g + 1) * n]

    U = ATTN_UNROLL

    def body4(it, carry):
        res = []
        for u in range(U // 4):
            l0 = pl.multiple_of((it * (U // 4) + u) * 32, 32)
            for r4 in range(4):
                q = jnp.concatenate([q16[4 * a + r4, pl.ds(l0, 32), :] for a in range(4)], axis=0)
                res.append((sub_tile(q, k4.at[r4], v4.at[r4], n4, pl.multiple_of(4 * l0, Q_SUB), 1),
                            r4, l0))
        for r3, r4, l0 in res:
            store(0, r3, lambda g, r4=r4: 4 * g + r4, l0, 32)
        return carry
    lax.fori_loop(0, n4 // Q_SUB // (U // 4), body4, 0)

    per16 = U // (n16 // Q_SUB)

    def body16(it, carry):
        g0 = it * per16
        res = []
        for u in range(per16):
            for l0 in range(0, n16, Q_SUB):
                res.append((sub_tile(q16[g0 + u, pl.ds(l0, Q_SUB), :], k16.at[g0 + u], v16.at[g0 + u],
                                     n16, l0, 2), u, l0))
        for r3, u, l0 in res:
            store(1, r3, lambda g, u=u: g0 + u, l0, Q_SUB)
        return carry
    lax.fori_loop(0, 16 // per16, body16, 0)

    U1 = ATTN_UNROLL_MERGE

    def body1(it, carry):
        res = []
        for u in range(U1):
            l0 = pl.multiple_of((it * U1 + u) * 8, 8)
            q = jnp.concatenate([xq4[r % 4, pl.ds(4 * l0 + r // 4, 8, stride=4), :] for r in range(16)],
                                axis=0).astype(BF16)
            res.append((sub_tile(q, k1, v1, seq, pl.multiple_of(16 * l0, Q_SUB), 0), l0))
        for (aa, ma, da), l0 in res:
            rows = lambda ref, pi: jnp.concatenate([ref[pi, r, pl.ds(l0, 8), :] for r in range(16)], axis=0)
            mb, mc = rows(max_s, 0), rows(max_s, 1)
            mx = jnp.maximum(jnp.maximum(ma, mb), mc)
            ea, eb, ec = jnp.exp(ma - mx), jnp.exp(mb - mx), jnp.exp(mc - mx)
            num = ea * aa + eb * rows(acc_s, 0) + ec * rows(acc_s, 1)
            den = ea * da + eb * rows(den_s, 0) + ec * rows(den_s, 1)
            out = num / den
            for r in range(16):
                o_ref[pl.ds(r + 16 * l0, 8, stride=16), :] = out[r * 8:(r + 1) * 8]
        return carry
    lax.fori_loop(0, seq // Q_SUB // U1, body1, 0)


def _attention(qk, v, brow, batch, seq):
    H = ATTN_HEADS
    n4, n16 = seq // 4, seq // 16
    qk4 = qk.reshape(2 * H, batch, seq, HEAD_DIM)
    v4 = v.reshape(H, batch, seq, HEAD_DIM)
    in_specs = [pl.BlockSpec((None, None, seq, HEAD_DIM), lambda h, b: (h, b, 0, 0)),
                pl.BlockSpec((None, None, seq, HEAD_DIM), lambda h, b: (H + h, b, 0, 0)),
                pl.BlockSpec((None, None, seq, HEAD_DIM), lambda h, b: (h, b, 0, 0))]
    in_specs.append(pl.BlockSpec((None, 9, K_WIN + Q_SUB), lambda h, b: (h, 0, 0)))
    kv_slabs = [pltpu.VMEM((seq, LANES), BF16), pltpu.VMEM((4, n4, LANES), BF16),
                pltpu.VMEM((16, n16, LANES), BF16)]
    out = pl.pallas_call(
        functools.partial(_attn_kernel, seq=seq),
        grid=(H, batch),
        in_specs=in_specs,
        out_specs=pl.BlockSpec((None, None, seq, HEAD_DIM), lambda h, b: (h, b, 0, 0)),
        out_shape=jax.ShapeDtypeStruct((H, batch, seq, HEAD_DIM), F32),
        scratch_shapes=[pltpu.VMEM((9, Q_SUB, K_WIN), F32),
                        pltpu.VMEM((K_WIN // LANES, Q_SUB, LANES), F32),
                        pltpu.VMEM((4, n4, LANES), F32),
                        pltpu.VMEM((4, n4, LANES), F32),
                        pltpu.VMEM((16, n16, LANES), BF16)]
                       + kv_slabs + kv_slabs
                       + [pltpu.VMEM((2, 16, n16, LANES), F32)] * 3,
        compiler_params=_params("parallel", "arbitrary"),
        name="dilated_attention",
    )(qk4, qk4, v4, brow)
    return out.reshape(H, batch * seq, HEAD_DIM)


def _ssd_kernel(z_ref, x_ref, b_ref, c_ref, dtr_ref, cwx_ref, cwb_ref, cwc_ref,
                cbx_ref, cbb_ref, cbc_ref, dbr_ref, alr_ref, dsk_ref, ng_ref,
                o_ref, pad, cv, y_s, arg_s, diag_s, rows_s, cols_s, sbs, sf, sb, *, seq):
    T = SSD_CHUNK
    nc = seq // T
    hi = lax.Precision.HIGHEST
    halo = CONV_HALO
    half = T // 2

    pad[:, 0:halo, :] = jnp.zeros((4, halo, LANES), F32)
    pad[:, seq + halo:seq + 2 * halo, :] = jnp.zeros((4, halo, LANES), F32)

    def fill(i, carry):
        t0 = pl.multiple_of(i * T, T)
        xin = x_ref[pl.ds(t0, T), :].astype(F32)
        pad[0, pl.ds(t0 + halo, T), :] = xin[:, :LANES]
        pad[1, pl.ds(t0 + halo, T), :] = xin[:, LANES:]
        pad[2, pl.ds(t0 + halo, T), :] = b_ref[pl.ds(t0, T), :].astype(F32)
        pad[3, pl.ds(t0 + halo, T), :] = c_ref[pl.ds(t0, T), :].astype(F32)
        return carry
    lax.fori_loop(0, nc, fill, 0)

    cws = (cwx_ref[:, :LANES], cwx_ref[:, LANES:], cwb_ref[...], cwc_ref[...])
    cbs = (cbx_ref[:, :LANES], cbx_ref[:, LANES:], cbb_ref[...], cbc_ref[...])

    def conv_chunk(ci):
        t0 = pl.multiple_of(ci * T, T)
        for s in range(4):
            ev = [pad[s, pl.ds(t0 + halo + 2 * k, half, stride=2), :] for k in (-1, 0, 1)]
            od = [pad[s, pl.ds(t0 + halo + 1 + 2 * k, half, stride=2), :] for k in (-1, 0, 1)]
            w = [cws[s][j:j + 1, :] for j in range(CONV_WIDTH)]
            out_e = cbs[s] + w[0] * ev[0] + w[1] * od[0] + w[2] * ev[1] + w[3] * od[1] + w[4] * ev[2]
            out_o = cbs[s] + w[0] * od[0] + w[1] * ev[1] + w[2] * od[1] + w[3] * ev[2] + w[4] * od[2]
            cv[s, pl.ds(t0, half, stride=2), :] = out_e * _sigmoid(out_e)
            cv[s, pl.ds(t0 + 1, half, stride=2), :] = out_o * _sigmoid(out_o)

    def load_x(t0):
        return jnp.concatenate([cv[0, pl.ds(t0, T), :], cv[1, pl.ds(t0, T), :]], axis=1)

    nh = HEADS_PER_GROUP
    nd = 2 * nh
    row = lax.broadcasted_iota(jnp.int32, (T, T), 0)
    col = lax.broadcasted_iota(jnp.int32, (T, T), 1)
    triu = (row <= col).astype(F32)
    lower = col < row
    upper = col > row
    lane_head = lax.broadcasted_iota(jnp.int32, (T, GROUP_W), 1) // SSM_HEAD_DIM

    dt = _softplus(dtr_ref[...] + dbr_ref[...])
    a = dt * (-jnp.exp(alr_ref[...]))
    cum = jnp.dot(a.reshape(nc * nd, T), triu, precision=hi,
                  preferred_element_type=F32).reshape(nc, nd, T)
    last = cum[:, :, T - 1:T]
    exc = cum - a
    fwd = lax.broadcasted_iota(jnp.int32, (nc, nd, T), 1) < nh
    base = jnp.where(fwd, cum, exc)
    log_dt = jnp.log(dt)
    arg_s[...] = jnp.where(fwd, cum - log_dt, exc + log_dt)
    log_sum = jnp.log(dt[:, 0:nh, :] + dt[:, nh:, :])
    diag_s[...] = jnp.concatenate([log_sum, log_sum], axis=1)
    rows_s[:, 0:nd, :] = base
    rows_s[:, nd:2 * nd, :] = jnp.exp(jnp.where(fwd, cum, last - exc))
    rows_s[:, 2 * nd:3 * nd, :] = dt * jnp.exp(jnp.where(fwd, last - cum, exc))
    rows_s[:, 3 * nd:, :] = jnp.zeros((nc, LANES - 3 * nd, T), F32)

    def expand(cols, first):
        n = cols.shape[0]
        low = lax.broadcasted_iota(jnp.int32, (n, LANES), 1) < SSM_HEAD_DIM
        halves = []
        for j in range(GROUP_W // LANES):
            c0 = jnp.broadcast_to(cols[:, first + 2 * j:first + 2 * j + 1], (n, LANES))
            c1 = jnp.broadcast_to(cols[:, first + 2 * j + 1:first + 2 * j + 2], (n, LANES))
            halves.append(jnp.where(low, c0, c1))
        return jnp.concatenate(halves, axis=1)

    tn_dims = (((0,), (0,)), ((), ()))

    cid = lax.broadcasted_iota(jnp.int32, (LANES, 4 * GROUP_W), 0)
    lid = lax.broadcasted_iota(jnp.int32, (LANES, 4 * GROUP_W), 1)
    blk = lid // GROUP_W
    want = nd + (blk % 2) * nh + (blk // 2) * nd + (lid % GROUP_W) // SSM_HEAD_DIM
    spread = jnp.where(cid == want, 1.0, 0.0).astype(BF16)
    spread_f = spread[:, :3 * GROUP_W]
    spread_b = spread[:, 3 * GROUP_W:]

    sb[...] = jnp.zeros_like(sb)

    def state_back(ci):
        t0 = pl.multiple_of(ci * T, T)
        ct = rows_s[ci].T
        cols_s[ci] = ct
        s_prev = sb[...]
        sbs[ci] = s_prev.astype(BF16)
        wx = jnp.dot(ct.astype(BF16), spread_b, preferred_element_type=F32)
        xs = (load_x(t0) * wx).astype(BF16)
        sb[...] = expand(ct[0:1, :], nd + nh) * s_prev + lax.dot_general(
            cv[2, pl.ds(t0, T), :].astype(BF16), xs, tn_dims, preferred_element_type=F32)

    U = SSD_UNROLL
    for u in range(U):
        conv_chunk(nc - 1 - u)

    def sweep_back(i, carry):
        k = nc - 1 - U * i
        for u in range(U):
            state_back(k + U - u)
        for u in range(U):
            conv_chunk(k - u)
        return carry
    lax.fori_loop(1, nc // U, sweep_back, 0)
    for u in range(U):
        state_back(U - 1 - u)

    sf[...] = jnp.zeros_like(sf)
    low_half = lax.broadcasted_iota(jnp.int32, (T, LANES), 1) < SSM_HEAD_DIM

    def finish(ci):
        t0 = pl.multiple_of(ci * T, T)
        zz = z_ref[pl.ds(t0, T), :].astype(F32)
        y = y_s[pl.ds(t0, T), :] * (zz * _sigmoid(zz))
        ms = jnp.mean(y * y, axis=-1, keepdims=True)
        o_ref[pl.ds(t0, T), :] = (y * lax.rsqrt(ms + EPS) * ng_ref[...]).astype(o_ref.dtype)

    def chunk_fwd(ci):
        t0 = pl.multiple_of(ci * T, T)
        ct = cols_s[ci]
        arg_r = arg_s[ci]
        dt_r = diag_s[ci]
        x = load_x(t0)
        bk = cv[2, pl.ds(t0, T), :].astype(BF16)
        ck = cv[3, pl.ds(t0, T), :].astype(BF16)
        cb = lax.dot_general(ck, bk, (((1,), (1,)), ((), ())), preferred_element_type=F32)
        spreadv = jnp.dot(ct.astype(BF16), spread_f, preferred_element_type=F32)
        lhs = []
        for h in range(nh):
            hb = nh + h
            arg = jnp.where(lower, ct[:, h:h + 1] - arg_r[h:h + 1, :],
                            jnp.where(upper, arg_r[hb:hb + 1, :] - ct[:, hb:hb + 1], dt_r[h:h + 1, :]))
            lhs.append((cb * jnp.exp(arg)).astype(BF16))
        s_prev = sf[...]
        y4 = jnp.dot(jnp.concatenate(lhs, axis=0), x.astype(BF16), preferred_element_type=F32)
        off = jnp.dot(ck, jnp.concatenate([s_prev.astype(BF16), sbs[ci]], axis=1),
                      preferred_element_type=F32)
        y = x * dsk_ref[...] + jnp.concatenate(
            [jnp.where(low_half, y4[2 * j * T:(2 * j + 1) * T, j * LANES:(j + 1) * LANES],
                       y4[(2 * j + 1) * T:(2 * j + 2) * T, j * LANES:(j + 1) * LANES])
             for j in range(GROUP_W // LANES)], axis=1)
        y = y + spreadv[:, :GROUP_W] * off[:, :GROUP_W] + spreadv[:, GROUP_W:2 * GROUP_W] * off[:, GROUP_W:]
        xs = (x * spreadv[:, 2 * GROUP_W:]).astype(BF16)
        sf[...] = expand(ct[T - 1:T, :], nd) * s_prev + lax.dot_general(
            bk, xs, tn_dims, preferred_element_type=F32)
        y_s[pl.ds(t0, T), :] = y

    U = SSD_UNROLL_FWD
    for u in range(U):
        chunk_fwd(u)

    def sweep_fwd(i, carry):
        c = U * i
        for u in range(U):
            finish(c - U + u)
        for u in range(U):
            chunk_fwd(c + u)
        return carry
    lax.fori_loop(1, nc // U, sweep_fwd, 0)
    for u in range(U):
        finish(nc - U + u)


def _ssd(zx, bcm, dt_row, cwx, cwb, cwc, cbx, cbb, cbc, db_row, al_row, dskip, ng, batch, seq):
    G = SSM_GROUPS
    nc = seq // SSD_CHUNK
    zx4 = zx.reshape(2 * G, batch, seq, GROUP_W)
    bc4 = bcm.reshape(2 * G, batch, seq, SSM_STATE)
    nd = 2 * HEADS_PER_GROUP

    def per_group(shape):
        return pl.BlockSpec((None,) + shape, lambda b, g: (g,) + (0,) * len(shape))

    in_specs = [
        pl.BlockSpec((None, None, seq, GROUP_W), lambda b, g: (g, b, 0, 0)),
        pl.BlockSpec((None, None, seq, GROUP_W), lambda b, g: (G + g, b, 0, 0)),
        pl.BlockSpec((None, None, seq, SSM_STATE), lambda b, g: (g, b, 0, 0)),
        pl.BlockSpec((None, None, seq, SSM_STATE), lambda b, g: (G + g, b, 0, 0)),
        pl.BlockSpec((None, None, nc, nd, SSD_CHUNK), lambda b, g: (b, g, 0, 0, 0)),
        per_group((CONV_WIDTH, GROUP_W)), per_group((CONV_WIDTH, SSM_STATE)),
        per_group((CONV_WIDTH, SSM_STATE)),
        per_group((1, GROUP_W)), per_group((1, SSM_STATE)), per_group((1, SSM_STATE)),
        per_group((nd, 1)), per_group((nd, 1)),
        per_group((1, GROUP_W)), per_group((1, GROUP_W)),
    ]
    out = pl.pallas_call(
        functools.partial(_ssd_kernel, seq=seq),
        grid=(batch, G),
        in_specs=in_specs,
        out_specs=pl.BlockSpec((None, None, seq, GROUP_W), lambda b, g: (g, b, 0, 0)),
        out_shape=jax.ShapeDtypeStruct((G, batch, seq, GROUP_W), BF16),
        scratch_shapes=[pltpu.VMEM((4, seq + 2 * CONV_HALO, LANES), F32),
                        pltpu.VMEM((4, seq, LANES), F32),
                        pltpu.VMEM((seq, GROUP_W), F32),
                        pltpu.VMEM((nc, nd, SSD_CHUNK), F32),
                        pltpu.VMEM((nc, nd, SSD_CHUNK), F32),
                        pltpu.VMEM((nc, LANES, SSD_CHUNK), F32),
                        pltpu.VMEM((nc, SSD_CHUNK, LANES), F32),
                        pltpu.VMEM((nc, SSM_STATE, GROUP_W), BF16),
                        pltpu.VMEM((SSM_STATE, GROUP_W), F32),
                        pltpu.VMEM((SSM_STATE, GROUP_W), F32)],
        compiler_params=_params("parallel", "parallel"),
        name="ssd",
    )(zx4, zx4, bc4, bc4, dt_row, cwx, cwb, cwc, cbx, cbb, cbc, db_row, al_row, dskip, ng)
    return out.reshape(G, batch * seq, GROUP_W)


def _out_proj_kernel(attn_ref, ssd_ref, w_ref, x_ref, o_ref, lhs):
    for h in range(ATTN_HEADS):
        lhs[:, h * HEAD_DIM:(h + 1) * HEAD_DIM] = attn_ref[h].astype(BF16)
    for g in range(SSM_GROUPS):
        lhs[:, ATTN_W + g * GROUP_W:ATTN_W + (g + 1) * GROUP_W] = ssd_ref[g]
    o_ref[...] = x_ref[...] + jnp.dot(lhs[...], w_ref[...], preferred_element_type=F32)


def _out_proj(attn, ssd, w, x2d, tm=256):
    n, d = x2d.shape
    kk = w.shape[0]
    return pl.pallas_call(
        _out_proj_kernel,
        grid=(n // tm,),
        in_specs=[pl.BlockSpec((ATTN_HEADS, tm, HEAD_DIM), lambda i: (0, i, 0)),
                  pl.BlockSpec((SSM_GROUPS, tm, GROUP_W), lambda i: (0, i, 0)),
                  pl.BlockSpec((kk, d), lambda i: (0, 0), pipeline_mode=pl.Buffered(1)),
                  pl.BlockSpec((tm, d), lambda i: (i, 0))],
        out_specs=pl.BlockSpec((tm, d), lambda i: (i, 0)),
        out_shape=jax.ShapeDtypeStruct((n, d), F32),
        scratch_shapes=[pltpu.VMEM((tm, kk), BF16)],
        compiler_params=_params("parallel"),
        name="out_proj",
    )(attn, ssd, w, x2d)


def _mlp_kernel(x_ref, g_ref, wu_ref, wd_ref, o_ref, hm):
    @pl.when(pl.program_id(1) == 0)
    def _():
        x = x_ref[...]
        ms = jnp.mean(x * x, axis=-1, keepdims=True)
        hm[...] = (x * lax.rsqrt(ms + EPS) * g_ref[...]).astype(hm.dtype)
        o_ref[...] = x
    u = jnp.maximum(jnp.dot(hm[...], wu_ref[...], preferred_element_type=F32), 0.0)
    o_ref[...] += jnp.dot((u * u).astype(BF16), wd_ref[...], preferred_element_type=F32)


def _mlp(x2d, g, wu, wd, tm=512, tf=1024):
    n, d = x2d.shape
    f = wu.shape[1]
    return pl.pallas_call(
        _mlp_kernel,
        grid=(n // tm, f // tf),
        in_specs=[pl.BlockSpec((tm, d), lambda i, j: (i, 0)),
                  pl.BlockSpec((1, d), lambda i, j: (0, 0)),
                  pl.BlockSpec((d, tf), lambda i, j: (0, j)),
                  pl.BlockSpec((tf, d), lambda i, j: (j, 0))],
        out_specs=pl.BlockSpec((tm, d), lambda i, j: (i, 0)),
        out_shape=jax.ShapeDtypeStruct((n, d), F32),
        scratch_shapes=[pltpu.VMEM((tm, d), BF16)],
        compiler_params=_params("parallel", "arbitrary"),
        name="mlp",
    )(x2d, g.reshape(1, d), wu, wd)


def kernel(x, norm_mix_g, w_in, q_norm_g, k_norm_g, rel_bias, conv_w, conv_b, dt_bias, a_log,
           d_skip, ssd_norm_g, w_out, norm_mlp_g, w_up, w_down):
    batch, seq, _ = x.shape
    n = batch * seq
    G, nh = SSM_GROUPS, HEADS_PER_GROUP
    nc = seq // SSD_CHUNK
    o_z = 3 * ATTN_W
    o_bc = o_z + 2 * SSM_W
    o_dt = o_bc + 2 * G * SSM_STATE
    x2d = x.reshape(n, D_MODEL)
    brow = _bias_rows(rel_bias)

    for layer in range(w_in.shape[0]):
        wi = w_in[layer].astype(BF16)
        h, dt_raw = _rmsnorm_dt(x2d, norm_mix_g[layer], wi[:, o_dt:])

        scale = 1.0 / math.sqrt(HEAD_DIM)
        gains = jnp.concatenate([jnp.tile(q_norm_g[layer].astype(F32) * scale, ATTN_HEADS),
                                 jnp.tile(k_norm_g[layer].astype(F32), ATTN_HEADS)]).reshape(1, 2 * ATTN_W)
        qk = _proj_qk(h, wi, gains)
        v = _proj_split(h, wi, 2 * ATTN_W, ATTN_W, HEAD_DIM, "proj_v", out_dtype=F32)
        zx = _proj_split(h, wi, o_z, o_bc - o_z, GROUP_W, "proj_zx")
        bcm = _proj_split(h, wi, o_bc, o_dt - o_bc, SSM_STATE, "proj_bc")

        attn = _attention(qk, v, brow, batch, seq)

        dt_row = (dt_raw.reshape(batch, nc, SSD_CHUNK, 2, G, nh)
                  .transpose(0, 4, 1, 3, 5, 2).reshape(batch, G, nc, 2 * nh, SSD_CHUNK))
        per_dir = lambda t: t.astype(F32).reshape(2, G, nh).transpose(1, 0, 2).reshape(G, 2 * nh)
        db, al = per_dir(dt_bias[layer]), per_dir(a_log[layer])
        cw, cbias = conv_w[layer].astype(F32), conv_b[layer].astype(F32)
        gn = G * SSM_STATE
        grp = lambda t, width: t.reshape(t.shape[0], G, width).transpose(1, 0, 2)
        ssd = _ssd(
            zx, bcm, dt_row,
            grp(cw[:, :SSM_W], GROUP_W), grp(cw[:, SSM_W:SSM_W + gn], SSM_STATE),
            grp(cw[:, SSM_W + gn:], SSM_STATE),
            grp(cbias[None, :SSM_W], GROUP_W), grp(cbias[None, SSM_W:SSM_W + gn], SSM_STATE),
            grp(cbias[None, SSM_W + gn:], SSM_STATE),
            db.reshape(G, 2 * nh, 1), al.reshape(G, 2 * nh, 1),
            jnp.repeat(d_skip[layer].astype(F32), SSM_HEAD_DIM).reshape(G, 1, GROUP_W),
            ssd_norm_g[layer].astype(F32).reshape(G, 1, GROUP_W),
            batch, seq)

        x2d = _out_proj(attn, ssd, w_out[layer].astype(BF16), x2d)
        x2d = _mlp(x2d, norm_mlp_g[layer], w_up[layer].astype(BF16), w_down[layer].astype(BF16))
    return x2d.reshape(batch, seq, D_MODEL)
```

```python
import functools
import math

import jax
import jax.numpy as jnp
from jax import lax
from jax.experimental import pallas as pl
from jax.experimental.pallas import tpu as pltpu

D_MODEL = 2048
ATTN_HEADS = 16
HEAD_DIM = 128
ATTN_W = ATTN_HEADS * HEAD_DIM
SSM_HEADS = 32
SSM_HEAD_DIM = 64
SSM_W = SSM_HEADS * SSM_HEAD_DIM
SSM_GROUPS = 8
HEADS_PER_GROUP = SSM_HEADS // SSM_GROUPS
GROUP_W = SSM_W // SSM_GROUPS
SSM_STATE = 128
CONV_WIDTH = 5
D_FF = 4 * D_MODEL
DILATIONS = (1, 4, 16)
HALF_WINDOW = 64
NUM_BUCKETS = 32
MAX_DISTANCE = 1024
NEG_INF = -1e30
EPS = 1e-6

LANES = 128
Q_SUB = 128
K_WIN = 256
SSD_CHUNK = 128
SSD_UNROLL = 16
SSD_UNROLL_FWD = 16
CONV_HALO = 8
VMEM_LIMIT = 56 * 1024 * 1024

F32 = jnp.float32
BF16 = jnp.bfloat16


def _params(*sem):
    return pltpu.CompilerParams(dimension_semantics=sem, vmem_limit_bytes=VMEM_LIMIT)


def _sigmoid(x):
    return 1.0 / (1.0 + jnp.exp(-x))


def _softplus(x):
    return jnp.maximum(x, 0.0) + jnp.log1p(jnp.exp(-jnp.abs(x)))


def _rmsnorm_dt_kernel(x_ref, g_ref, w_ref, h_ref, dt_ref):
    x = x_ref[...]
    ms = jnp.mean(x * x, axis=-1, keepdims=True)
    h = (x * lax.rsqrt(ms + EPS) * g_ref[...]).astype(h_ref.dtype)
    h_ref[...] = h
    dt_ref[...] = jnp.dot(h, w_ref[...], preferred_element_type=F32)


def _rmsnorm_dt(x2d, g, w_dt, tm=512):
    n, d = x2d.shape
    m = w_dt.shape[1]
    return pl.pallas_call(
        _rmsnorm_dt_kernel,
        grid=(n // tm,),
        in_specs=[pl.BlockSpec((tm, d), lambda i: (i, 0)),
                  pl.BlockSpec((1, d), lambda i: (0, 0)),
                  pl.BlockSpec((d, m), lambda i: (0, 0))],
        out_specs=[pl.BlockSpec((tm, d), lambda i: (i, 0)),
                   pl.BlockSpec((tm, m), lambda i: (i, 0))],
        out_shape=[jax.ShapeDtypeStruct((n, d), BF16), jax.ShapeDtypeStruct((n, m), F32)],
        compiler_params=_params("parallel"),
        name="rmsnorm_dt",
    )(x2d, g.reshape(1, d), w_dt)


QK_SUB = 256


def _proj_qk_kernel(a_ref, w_ref, g_ref, o_ref, *, heads_per_tile):
    a = a_ref[...]
    per = QK_SUB // HEAD_DIM
    for c in range(heads_per_tile // per):
        acc = jnp.dot(a, w_ref[:, c * QK_SUB:(c + 1) * QK_SUB], preferred_element_type=F32)
        for hh in range(per):
            h = c * per + hh
            s = acc[:, hh * HEAD_DIM:(hh + 1) * HEAD_DIM]
            ms = jnp.mean(s * s, axis=-1, keepdims=True)
            g = g_ref[:, h * HEAD_DIM:(h + 1) * HEAD_DIM]
            o_ref[h] = (s * lax.rsqrt(ms + EPS) * g).astype(o_ref.dtype)


def _proj_split_kernel(a_ref, w_ref, o_ref, *, width):
    a = a_ref[...]
    per = QK_SUB // width
    for s in range(w_ref.shape[1] // QK_SUB):
        acc = jnp.dot(a, w_ref[:, s * QK_SUB:(s + 1) * QK_SUB], preferred_element_type=F32)
        for c in range(per):
            o_ref[s * per + c] = acc[:, c * width:(c + 1) * width].astype(o_ref.dtype)


def _proj_qk(h, w, gains, tm=1024, tn=1024):
    n, k = h.shape
    m = 2 * ATTN_W
    hpt = tn // HEAD_DIM
    return pl.pallas_call(
        functools.partial(_proj_qk_kernel, heads_per_tile=hpt),
        grid=(m // tn, n // tm),
        in_specs=[pl.BlockSpec((tm, k), lambda j, i: (i, 0)),
                  pl.BlockSpec((k, tn), lambda j, i: (0, j)),
                  pl.BlockSpec((1, tn), lambda j, i: (0, j))],
        out_specs=pl.BlockSpec((hpt, tm, HEAD_DIM), lambda j, i: (j, i, 0)),
        out_shape=jax.ShapeDtypeStruct((m // HEAD_DIM, n, HEAD_DIM), F32),
        compiler_params=_params("parallel", "arbitrary"),
        name="proj_qk",
    )(h, w, gains)


def _proj_split(h, w, col0, m, width, name, out_dtype=BF16, tm=1024, tn=1024):
    n, k = h.shape
    cpt = tn // width
    j0 = col0 // tn
    return pl.pallas_call(
        functools.partial(_proj_split_kernel, width=width),
        grid=(m // tn, n // tm),
        in_specs=[pl.BlockSpec((tm, k), lambda j, i: (i, 0)),
                  pl.BlockSpec((k, tn), lambda j, i: (0, j0 + j))],
        out_specs=pl.BlockSpec((cpt, tm, width), lambda j, i: (j, i, 0)),
        out_shape=jax.ShapeDtypeStruct((m // width, n, width), out_dtype),
        compiler_params=_params("parallel", "arbitrary"),
        name=name,
    )(h, w)


def _t5_bucket(rel):
    nb = NUM_BUCKETS // 2
    max_exact = nb // 2
    ret = (rel > 0).astype(jnp.int32) * nb
    n = jnp.abs(rel)
    nf = jnp.maximum(n, 1).astype(jnp.float32)
    large = max_exact + (jnp.log(nf / max_exact) / math.log(MAX_DISTANCE / max_exact)
                         * (nb - max_exact)).astype(jnp.int32)
    large = jnp.minimum(large, nb - 1)
    return ret + jnp.where(n < max_exact, n, large)


def _bias_rows(rel_bias):
    period = K_WIN + Q_SUB
    m = jnp.arange(period)
    delta = jnp.where(m < K_WIN, m, m - period)
    rows = []
    for d in DILATIONS:
        for off in (0, -HALF_WINDOW, -2 * HALF_WINDOW):
            rel = delta + off
            valid = jnp.abs(rel) <= HALF_WINDOW
            b = rel_bias[_t5_bucket(rel * d)].astype(F32)
            rows.append(jnp.where(valid[:, None], b, NEG_INF))
    return jnp.stack(rows, axis=0).transpose(2, 0, 1)


ATTN_UNROLL_MERGE = 32
ATTN_UNROLL = 32


def _attn_kernel(q_ref, k_ref, v_ref, brow_ref, o_ref, bias_s, tmp_s, xq4, xt4,
                 q16, k1, k4, k16, v1, v4, v16, acc_s, max_s, den_s, *, seq):
    period = K_WIN + Q_SUB
    n4, n16 = seq // 4, seq // 16

    @pl.when(pl.program_id(1) == 0)
    def _():
        for idx in range(9):
            row = jnp.broadcast_to(brow_ref[idx:idx + 1, :], (Q_SUB, period))
            tile = pltpu.roll(row, 0, 1, stride=1, stride_axis=0)
            d = DILATIONS[idx // 3]
            if d == 16:
                bias_s[idx] = tile[:, :K_WIN]
                continue
            for half in range(K_WIN // LANES):
                tmp_s[half] = tile[:, half * LANES:(half + 1) * LANES]
            groups = 16 // d
            for half in range(K_WIN // LANES):
                for g in range(groups):
                    n = Q_SUB // groups
                    bias_s[idx, g * n:(g + 1) * n, half * LANES:(half + 1) * LANES] = (
                        tmp_s[half, pl.ds(g, n, stride=groups), :])

    cp = 256

    def split4(src, dst):
        for r4 in range(4):
            def body(c, carry, r4=r4):
                t0 = pl.multiple_of(c * cp, cp)
                dst[r4, pl.ds(t0, cp), :] = src[pl.ds(r4 + 4 * t0, cp, stride=4), :]
                return carry
            lax.fori_loop(0, n4 // cp, body, 0)

    def split16(src4, dst16, dst4=None):
        for r4 in range(4):
            for a in range(4):
                dst16[4 * a + r4] = src4[r4, pl.ds(a, n16, stride=4), :].astype(BF16)
            if dst4 is not None:
                def body(c, carry, r4=r4):
                    t0 = pl.multiple_of(c * cp, cp)
                    dst4[r4, pl.ds(t0, cp), :] = src4[r4, pl.ds(t0, cp), :].astype(BF16)
                    return carry
                lax.fori_loop(0, n4 // cp, body, 0)

    def cast(src, dst):
        def body(c, carry):
            t0 = pl.multiple_of(c * cp, cp)
            dst[pl.ds(t0, cp), :] = src[pl.ds(t0, cp), :].astype(BF16)
            return carry
        lax.fori_loop(0, seq // cp, body, 0)

    split4(q_ref, xq4)
    split16(xq4, q16)
    for src, d1, d4, d16 in ((k_ref, k1, k4, k16), (v_ref, v1, v4, v16)):
        cast(src, d1)
        split4(src, xt4)
        split16(xt4, d16, d4)

    ones = jnp.ones((K_WIN, LANES), BF16)

    def sub_tile(q, k_s, v_s, length, s0, pi):
        if isinstance(s0, int):
            w0 = min(max(s0 - HALF_WINDOW, 0), length - K_WIN)
            place = 0 if s0 == 0 else (2 if s0 == length - Q_SUB else 1)
        else:
            w0 = pl.multiple_of(jnp.clip(s0 - HALF_WINDOW, 0, length - K_WIN), HALF_WINDOW)
            place = jnp.where(s0 == 0, 0, jnp.where(s0 == length - Q_SUB, 2, 1))
        k = k_s[pl.ds(w0, K_WIN), :]
        v = v_s[pl.ds(w0, K_WIN), :]
        s = lax.dot_general(q, k, (((1,), (1,)), ((), ())), preferred_element_type=F32)
        s = s + bias_s[3 * pi + place]
        m = jnp.max(s, axis=-1, keepdims=True)
        p = jnp.exp(s - m).astype(BF16)
        pv = jnp.dot(p, jnp.concatenate([v, ones], axis=1), preferred_element_type=F32)
        return pv[:, :LANES], jnp.broadcast_to(m, (Q_SUB, LANES)), pv[:, LANES:]

    def store(pi, res, slab_of_group, l0, n):
        for g in range(Q_SUB // n):
            for val, dst in zip(res, (acc_s, max_s, den_s)):
                dst[pi, slab_of_group(g), pl.ds(l0, n), :] = val[g * n:(g + 1) * n]

    U = ATTN_UNROLL

    def body4(it, carry):
        res = []
        for u in range(U // 4):
            l0 = pl.multiple_of((it * (U // 4) + u) * 32, 32)
            for r4 in range(4):
                q = jnp.concatenate([q16[4 * a + r4, pl.ds(l0, 32), :] for a in range(4)], axis=0)
                res.append((sub_tile(q, k4.at[r4], v4.at[r4], n4, pl.multiple_of(4 * l0, Q_SUB), 1),
                            r4, l0))
        for r3, r4, l0 in res:
            store(0, r3, lambda g, r4=r4: 4 * g + r4, l0, 32)
        return carry
    lax.fori_loop(0, n4 // Q_SUB // (U // 4), body4, 0)

    per16 = U // (n16 // Q_SUB)

    def body16(it, carry):
        g0 = it * per16
        res = []
        for u in range(per16):
            for l0 in range(0, n16, Q_SUB):
                res.append((sub_tile(q16[g0 + u, pl.ds(l0, Q_SUB), :], k16.at[g0 + u], v16.at[g0 + u],
                                     n16, l0, 2), u, l0))
        for r3, u, l0 in res:
            store(1, r3, lambda g, u=u: g0 + u, l0, Q_SUB)
        return carry
    lax.fori_loop(0, 16 // per16, body16, 0)

    U1 = ATTN_UNROLL_MERGE

    def body1(it, carry):
        res = []
        for u in range(U1):
            l0 = pl.multiple_of((it * U1 + u) * 8, 8)
            q = jnp.concatenate([xq4[r % 4, pl.ds(4 * l0 + r // 4, 8, stride=4), :] for r in range(16)],
                                axis=0).astype(BF16)
            res.append((sub_tile(q, k1, v1, seq, pl.multiple_of(16 * l0, Q_SUB), 0), l0))
        for (aa, ma, da), l0 in res:
            rows = lambda ref, pi: jnp.concatenate([ref[pi, r, pl.ds(l0, 8), :] for r in range(16)], axis=0)
            mb, mc = rows(max_s, 0), rows(max_s, 1)
            mx = jnp.maximum(jnp.maximum(ma, mb), mc)
            ea, eb, ec = jnp.exp(ma - mx), jnp.exp(mb - mx), jnp.exp(mc - mx)
            num = ea * aa + eb * rows(acc_s, 0) + ec * rows(acc_s, 1)
            den = ea * da + eb * rows(den_s, 0) + ec * rows(den_s, 1)
            out = num / den
            for r in range(16):
                o_ref[pl.ds(r + 16 * l0, 8, stride=16), :] = out[r * 8:(r + 1) * 8]
        return carry
    lax.fori_loop(0, seq // Q_SUB // U1, body1, 0)


def _attention(qk, v, brow, batch, seq):
    H = ATTN_HEADS
    n4, n16 = seq // 4, seq // 16
    qk4 = qk.reshape(2 * H, batch, seq, HEAD_DIM)
    v4 = v.reshape(H, batch, seq, HEAD_DIM)
    in_specs = [pl.BlockSpec((None, None, seq, HEAD_DIM), lambda h, b: (h, b, 0, 0)),
                pl.BlockSpec((None, None, seq, HEAD_DIM), lambda h, b: (H + h, b, 0, 0)),
                pl.BlockSpec((None, None, seq, HEAD_DIM), lambda h, b: (h, b, 0, 0))]
    in_specs.append(pl.BlockSpec((None, 9, K_WIN + Q_SUB), lambda h, b: (h, 0, 0)))
    kv_slabs = [pltpu.VMEM((seq, LANES), BF16), pltpu.VMEM((4, n4, LANES), BF16),
                pltpu.VMEM((16, n16, LANES), BF16)]
    out = pl.pallas_call(
        functools.partial(_attn_kernel, seq=seq),
        grid=(H, batch),
        in_specs=in_specs,
        out_specs=pl.BlockSpec((None, None, seq, HEAD_DIM), lambda h, b: (h, b, 0, 0)),
        out_shape=jax.ShapeDtypeStruct((H, batch, seq, HEAD_DIM), F32),
        scratch_shapes=[pltpu.VMEM((9, Q_SUB, K_WIN), F32),
                        pltpu.VMEM((K_WIN // LANES, Q_SUB, LANES), F32),
                        pltpu.VMEM((4, n4, LANES), F32),
                        pltpu.VMEM((4, n4, LANES), F32),
                        pltpu.VMEM((16, n16, LANES), BF16)]
                       + kv_slabs + kv_slabs
                       + [pltpu.VMEM((2, 16, n16, LANES), F32)] * 3,
        compiler_params=_params("parallel", "arbitrary"),
        name="dilated_attention",
    )(qk4, qk4, v4, brow)
    return out.reshape(H, batch * seq, HEAD_DIM)


def _ssd_kernel(z_ref, x_ref, b_ref, c_ref, dtr_ref, cwx_ref, cwb_ref, cwc_ref,
                cbx_ref, cbb_ref, cbc_ref, dbr_ref, alr_ref, dsk_ref, ng_ref,
                o_ref, pad, cv, y_s, arg_s, diag_s, rows_s, cols_s, sbs, sf, sb, *, seq):
    T = SSD_CHUNK
    nc = seq // T
    hi = lax.Precision.HIGHEST
    halo = CONV_HALO
    half = T // 2

    pad[:, 0:halo, :] = jnp.zeros((4, halo, LANES), F32)
    pad[:, seq + halo:seq + 2 * halo, :] = jnp.zeros((4, halo, LANES), F32)

    def fill(i, carry):
        t0 = pl.multiple_of(i * T, T)
        xin = x_ref[pl.ds(t0, T), :].astype(F32)
        pad[0, pl.ds(t0 + halo, T), :] = xin[:, :LANES]
        pad[1, pl.ds(t0 + halo, T), :] = xin[:, LANES:]
        pad[2, pl.ds(t0 + halo, T), :] = b_ref[pl.ds(t0, T), :].astype(F32)
        pad[3, pl.ds(t0 + halo, T), :] = c_ref[pl.ds(t0, T), :].astype(F32)
        return carry
    lax.fori_loop(0, nc, fill, 0)

    cws = (cwx_ref[:, :LANES], cwx_ref[:, LANES:], cwb_ref[...], cwc_ref[...])
    cbs = (cbx_ref[:, :LANES], cbx_ref[:, LANES:], cbb_ref[...], cbc_ref[...])

    def conv_chunk(ci):
        t0 = pl.multiple_of(ci * T, T)
        for s in range(4):
            ev = [pad[s, pl.ds(t0 + halo + 2 * k, half, stride=2), :] for k in (-1, 0, 1)]
            od = [pad[s, pl.ds(t0 + halo + 1 + 2 * k, half, stride=2), :] for k in (-1, 0, 1)]
            w = [cws[s][j:j + 1, :] for j in range(CONV_WIDTH)]
            out_e = cbs[s] + w[0] * ev[0] + w[1] * od[0] + w[2] * ev[1] + w[3] * od[1] + w[4] * ev[2]
            out_o = cbs[s] + w[0] * od[0] + w[1] * ev[1] + w[2] * od[1] + w[3] * ev[2] + w[4] * od[2]
            cv[s, pl.ds(t0, half, stride=2), :] = out_e * _sigmoid(out_e)
            cv[s, pl.ds(t0 + 1, half, stride=2), :] = out_o * _sigmoid(out_o)

    def load_x(t0):
        return jnp.concatenate([cv[0, pl.ds(t0, T), :], cv[1, pl.ds(t0, T), :]], axis=1)

    nh = HEADS_PER_GROUP
    nd = 2 * nh
    row = lax.broadcasted_iota(jnp.int32, (T, T), 0)
    col = lax.broadcasted_iota(jnp.int32, (T, T), 1)
    triu = (row <= col).astype(F32)
    lower = col < row
    upper = col > row
    lane_head = lax.broadcasted_iota(jnp.int32, (T, GROUP_W), 1) // SSM_HEAD_DIM

    dt = _softplus(dtr_ref[...] + dbr_ref[...])
    a = dt * (-jnp.exp(alr_ref[...]))
    cum = jnp.dot(a.reshape(nc * nd, T), triu, precision=hi,
                  preferred_element_type=F32).reshape(nc, nd, T)
    last = cum[:, :, T - 1:T]
    exc = cum - a
    fwd = lax.broadcasted_iota(jnp.int32, (nc, nd, T), 1) < nh
    base = jnp.where(fwd, cum, exc)
    log_dt = jnp.log(dt)
    arg_s[...] = jnp.where(fwd, cum - log_dt, exc + log_dt)
    log_sum = jnp.log(dt[:, 0:nh, :] + dt[:, nh:, :])
    diag_s[...] = jnp.concatenate([log_sum, log_sum], axis=1)
    rows_s[:, 0:nd, :] = base
    rows_s[:, nd:2 * nd, :] = jnp.exp(jnp.where(fwd, cum, last - exc))
    rows_s[:, 2 * nd:3 * nd, :] = dt * jnp.exp(jnp.where(fwd, last - cum, exc))
    rows_s[:, 3 * nd:, :] = jnp.zeros((nc, LANES - 3 * nd, T), F32)

    def expand(cols, first):
        n = cols.shape[0]
        low = lax.broadcasted_iota(jnp.int32, (n, LANES), 1) < SSM_HEAD_DIM
        halves = []
        for j in range(GROUP_W // LANES):
            c0 = jnp.broadcast_to(cols[:, first + 2 * j:first + 2 * j + 1], (n, LANES))
            c1 = jnp.broadcast_to(cols[:, first + 2 * j + 1:first + 2 * j + 2], (n, LANES))
            halves.append(jnp.where(low, c0, c1))
        return jnp.concatenate(halves, axis=1)

    tn_dims = (((0,), (0,)), ((), ()))

    cid = lax.broadcasted_iota(jnp.int32, (LANES, 4 * GROUP_W), 0)
    lid = lax.broadcasted_iota(jnp.int32, (LANES, 4 * GROUP_W), 1)
    blk = lid // GROUP_W
    want = nd + (blk % 2) * nh + (blk // 2) * nd + (lid % GROUP_W) // SSM_HEAD_DIM
    spread = jnp.where(cid == want, 1.0, 0.0).astype(BF16)
    spread_f = spread[:, :3 * GROUP_W]
    spread_b = spread[:, 3 * GROUP_W:]

    sb[...] = jnp.zeros_like(sb)

    def state_back(ci):
        t0 = pl.multiple_of(ci * T, T)
        ct = rows_s[ci].T
        cols_s[ci] = ct
        s_prev = sb[...]
        sbs[ci] = s_prev.astype(BF16)
        wx = jnp.dot(ct.astype(BF16), spread_b, preferred_element_type=F32)
        xs = (load_x(t0) * wx).astype(BF16)
        sb[...] = expand(ct[0:1, :], nd + nh) * s_prev + lax.dot_general(
            cv[2, pl.ds(t0, T), :].astype(BF16), xs, tn_dims, preferred_element_type=F32)

    U = SSD_UNROLL
    for u in range(U):
        conv_chunk(nc - 1 - u)

    def sweep_back(i, carry):
        k = nc - 1 - U * i
        for u in range(U):
            state_back(k + U - u)
        for u in range(U):
            conv_chunk(k - u)
        return carry
    lax.fori_loop(1, nc // U, sweep_back, 0)
    for u in range(U):
        state_back(U - 1 - u)

    sf[...] = jnp.zeros_like(sf)
    low_half = lax.broadcasted_iota(jnp.int32, (T, LANES), 1) < SSM_HEAD_DIM

    def finish(ci):
        t0 = pl.multiple_of(ci * T, T)
        zz = z_ref[pl.ds(t0, T), :].astype(F32)
        y = y_s[pl.ds(t0, T), :] * (zz * _sigmoid(zz))
        ms = jnp.mean(y * y, axis=-1, keepdims=True)
        o_ref[pl.ds(t0, T), :] = (y * lax.rsqrt(ms + EPS) * ng_ref[...]).astype(o_ref.dtype)

    def chunk_fwd(ci):
        t0 = pl.multiple_of(ci * T, T)
        ct = cols_s[ci]
        arg_r = arg_s[ci]
        dt_r = diag_s[ci]
        x = load_x(t0)
        bk = cv[2, pl.ds(t0, T), :].astype(BF16)
        ck = cv[3, pl.ds(t0, T), :].astype(BF16)
        cb = lax.dot_general(ck, bk, (((1,), (1,)), ((), ())), preferred_element_type=F32)
        spreadv = jnp.dot(ct.astype(BF16), spread_f, preferred_element_type=F32)
        lhs = []
        for h in range(nh):
            hb = nh + h
            arg = jnp.where(lower, ct[:, h:h + 1] - arg_r[h:h + 1, :],
                            jnp.where(upper, arg_r[hb:hb + 1, :] - ct[:, hb:hb + 1], dt_r[h:h + 1, :]))
            lhs.append((cb * jnp.exp(arg)).astype(BF16))
        s_prev = sf[...]
        y4 = jnp.dot(jnp.concatenate(lhs, axis=0), x.astype(BF16), preferred_element_type=F32)
        off = jnp.dot(ck, jnp.concatenate([s_prev.astype(BF16), sbs[ci]], axis=1),
                      preferred_element_type=F32)
        y = x * dsk_ref[...] + jnp.concatenate(
            [jnp.where(low_half, y4[2 * j * T:(2 * j + 1) * T, j * LANES:(j + 1) * LANES],
                       y4[(2 * j + 1) * T:(2 * j + 2) * T, j * LANES:(j + 1) * LANES])
             for j in range(GROUP_W // LANES)], axis=1)
        y = y + spreadv[:, :GROUP_W] * off[:, :GROUP_W] + spreadv[:, GROUP_W:2 * GROUP_W] * off[:, GROUP_W:]
        xs = (x * spreadv[:, 2 * GROUP_W:]).astype(BF16)
        sf[...] = expand(ct[T - 1:T, :], nd) * s_prev + lax.dot_general(
            bk, xs, tn_dims, preferred_element_type=F32)
        y_s[pl.ds(t0, T), :] = y

    U = SSD_UNROLL_FWD
    for u in range(U):
        chunk_fwd(u)

    def sweep_fwd(i, carry):
        c = U * i
        for u in range(U):
            finish(c - U + u)
        for u in range(U):
            chunk_fwd(c + u)
        return carry
    lax.fori_loop(1, nc // U, sweep_fwd, 0)
    for u in range(U):
        finish(nc - U + u)


def _ssd(zx, bcm, dt_row, cwx, cwb, cwc, cbx, cbb, cbc, db_row, al_row, dskip, ng, batch, seq):
    G = SSM_GROUPS
    nc = seq // SSD_CHUNK
    zx4 = zx.reshape(2 * G, batch, seq, GROUP_W)
    bc4 = bcm.reshape(2 * G, batch, seq, SSM_STATE)
    nd = 2 * HEADS_PER_GROUP

    def per_group(shape):
        return pl.BlockSpec((None,) + shape, lambda b, g: (g,) + (0,) * len(shape))

    in_specs = [
        pl.BlockSpec((None, None, seq, GROUP_W), lambda b, g: (g, b, 0, 0)),
        pl.BlockSpec((None, None, seq, GROUP_W), lambda b, g: (G + g, b, 0, 0)),
        pl.BlockSpec((None, None, seq, SSM_STATE), lambda b, g: (g, b, 0, 0)),
        pl.BlockSpec((None, None, seq, SSM_STATE), lambda b, g: (G + g, b, 0, 0)),
        pl.BlockSpec((None, None, nc, nd, SSD_CHUNK), lambda b, g: (b, g, 0, 0, 0)),
        per_group((CONV_WIDTH, GROUP_W)), per_group((CONV_WIDTH, SSM_STATE)),
        per_group((CONV_WIDTH, SSM_STATE)),
        per_group((1, GROUP_W)), per_group((1, SSM_STATE)), per_group((1, SSM_STATE)),
        per_group((nd, 1)), per_group((nd, 1)),
        per_group((1, GROUP_W)), per_group((1, GROUP_W)),
    ]
    out = pl.pallas_call(
        functools.partial(_ssd_kernel, seq=seq),
        grid=(batch, G),
        in_specs=in_specs,
        out_specs=pl.BlockSpec((None, None, seq, GROUP_W), lambda b, g: (g, b, 0, 0)),
        out_shape=jax.ShapeDtypeStruct((G, batch, seq, GROUP_W), BF16),
        scratch_shapes=[pltpu.VMEM((4, seq + 2 * CONV_HALO, LANES), F32),
                        pltpu.VMEM((4, seq, LANES), F32),
                        pltpu.VMEM((seq, GROUP_W), F32),
                        pltpu.VMEM((nc, nd, SSD_CHUNK), F32),
                        pltpu.VMEM((nc, nd, SSD_CHUNK), F32),
                        pltpu.VMEM((nc, LANES, SSD_CHUNK), F32),
                        pltpu.VMEM((nc, SSD_CHUNK, LANES), F32),
                        pltpu.VMEM((nc, SSM_STATE, GROUP_W), BF16),
                        pltpu.VMEM((SSM_STATE, GROUP_W), F32),
                        pltpu.VMEM((SSM_STATE, GROUP_W), F32)],
        compiler_params=_params("parallel", "parallel"),
        name="ssd",
    )(zx4, zx4, bc4, bc4, dt_row, cwx, cwb, cwc, cbx, cbb, cbc, db_row, al_row, dskip, ng)
    return out.reshape(G, batch * seq, GROUP_W)


def _out_proj_kernel(attn_ref, ssd_ref, w_ref, x_ref, o_ref, lhs):
    for h in range(ATTN_HEADS):
        lhs[:, h * HEAD_DIM:(h + 1) * HEAD_DIM] = attn_ref[h].astype(BF16)
    for g in range(SSM_GROUPS):
        lhs[:, ATTN_W + g * GROUP_W:ATTN_W + (g + 1) * GROUP_W] = ssd_ref[g]
    o_ref[...] = x_ref[...] + jnp.dot(lhs[...], w_ref[...], preferred_element_type=F32)


def _out_proj(attn, ssd, w, x2d, tm=256):
    n, d = x2d.shape
    kk = w.shape[0]
    return pl.pallas_call(
        _out_proj_kernel,
        grid=(n // tm,),
        in_specs=[pl.BlockSpec((ATTN_HEADS, tm, HEAD_DIM), lambda i: (0, i, 0)),
                  pl.BlockSpec((SSM_GROUPS, tm, GROUP_W), lambda i: (0, i, 0)),
                  pl.BlockSpec((kk, d), lambda i: (0, 0), pipeline_mode=pl.Buffered(1)),
                  pl.BlockSpec((tm, d), lambda i: (i, 0))],
        out_specs=pl.BlockSpec((tm, d), lambda i: (i, 0)),
        out_shape=jax.ShapeDtypeStruct((n, d), F32),
        scratch_shapes=[pltpu.VMEM((tm, kk), BF16)],
        compiler_params=_params("parallel"),
        name="out_proj",
    )(attn, ssd, w, x2d)


def _mlp_kernel(x_ref, g_ref, wu_ref, wd_ref, o_ref, hm):
    @pl.when(pl.program_id(1) == 0)
    def _():
        x = x_ref[...]
        ms = jnp.mean(x * x, axis=-1, keepdims=True)
        hm[...] = (x * lax.rsqrt(ms + EPS) * g_ref[...]).astype(hm.dtype)
        o_ref[...] = x
    u = jnp.maximum(jnp.dot(hm[...], wu_ref[...], preferred_element_type=F32), 0.0)
    o_ref[...] += jnp.dot((u * u).astype(BF16), wd_ref[...], preferred_element_type=F32)


def _mlp(x2d, g, wu, wd, tm=512, tf=1024):
    n, d = x2d.shape
    f = wu.shape[1]
    return pl.pallas_call(
        _mlp_kernel,
        grid=(n // tm, f // tf),
        in_specs=[pl.BlockSpec((tm, d), lambda i, j: (i, 0)),
                  pl.BlockSpec((1, d), lambda i, j: (0, 0)),
                  pl.BlockSpec((d, tf), lambda i, j: (0, j)),
                  pl.BlockSpec((tf, d), lambda i, j: (j, 0))],
        out_specs=pl.BlockSpec((tm, d), lambda i, j: (i, 0)),
        out_shape=jax.ShapeDtypeStruct((n, d), F32),
        scratch_shapes=[pltpu.VMEM((tm, d), BF16)],
        compiler_params=_params("parallel", "arbitrary"),
        name="mlp",
    )(x2d, g.reshape(1, d), wu, wd)


def kernel(x, norm_mix_g, w_in, q_norm_g, k_norm_g, rel_bias, conv_w, conv_b, dt_bias, a_log,
           d_skip, ssd_norm_g, w_out, norm_mlp_g, w_up, w_down):
    batch, seq, _ = x.shape
    n = batch * seq
    G, nh = SSM_GROUPS, HEADS_PER_GROUP
    nc = seq // SSD_CHUNK
    o_z = 3 * ATTN_W
    o_bc = o_z + 2 * SSM_W
    o_dt = o_bc + 2 * G * SSM_STATE
    x2d = x.reshape(n, D_MODEL)
    brow = _bias_rows(rel_bias)

    for layer in range(w_in.shape[0]):
        wi = w_in[layer].astype(BF16)
        h, dt_raw = _rmsnorm_dt(x2d, norm_mix_g[layer], wi[:, o_dt:])

        scale = 1.0 / math.sqrt(HEAD_DIM)
        gains = jnp.concatenate([jnp.tile(q_norm_g[layer].astype(F32) * scale, ATTN_HEADS),
                                 jnp.tile(k_norm_g[layer].astype(F32), ATTN_HEADS)]).reshape(1, 2 * ATTN_W)
        qk = _proj_qk(h, wi, gains)
        v = _proj_split(h, wi, 2 * ATTN_W, ATTN_W, HEAD_DIM, "proj_v", out_dtype=F32)
        zx = _proj_split(h, wi, o_z, o_bc - o_z, GROUP_W, "proj_zx")
        bcm = _proj_split(h, wi, o_bc, o_dt - o_bc, SSM_STATE, "proj_bc")

        attn = _attention(qk, v, brow, batch, seq)

        dt_row = (dt_raw.reshape(batch, nc, SSD_CHUNK, 2, G, nh)
                  .transpose(0, 4, 1, 3, 5, 2).reshape(batch, G, nc, 2 * nh, SSD_CHUNK))
        per_dir = lambda t: t.astype(F32).reshape(2, G, nh).transpose(1, 0, 2).reshape(G, 2 * nh)
        db, al = per_dir(dt_bias[layer]), per_dir(a_log[layer])
        cw, cbias = conv_w[layer].astype(F32), conv_b[layer].astype(F32)
        gn = G * SSM_STATE
        grp = lambda t, width: t.reshape(t.shape[0], G, width).transpose(1, 0, 2)
        ssd = _ssd(
            zx, bcm, dt_row,
            grp(cw[:, :SSM_W], GROUP_W), grp(cw[:, SSM_W:SSM_W + gn], SSM_STATE),
            grp(cw[:, SSM_W + gn:], SSM_STATE),
            grp(cbias[None, :SSM_W], GROUP_W), grp(cbias[None, SSM_W:SSM_W + gn], SSM_STATE),
            grp(cbias[None, SSM_W + gn:], SSM_STATE),
            db.reshape(G, 2 * nh, 1), al.reshape(G, 2 * nh, 1),
            jnp.repeat(d_skip[layer].astype(F32), SSM_HEAD_DIM).reshape(G, 1, GROUP_W),
            ssd_norm_g[layer].astype(F32).reshape(G, 1, GROUP_W),
            batch, seq)

        x2d = _out_proj(attn, ssd, w_out[layer].astype(BF16), x2d)
        x2d = _mlp(x2d, norm_mlp_g[layer], w_up[layer].astype(BF16), w_down[layer].astype(BF16))
    return x2d.reshape(batch, seq, D_MODEL)
```

```python
import functools
import math

import jax
import jax.numpy as jnp
from jax import lax
from jax.experimental import pallas as pl
from jax.experimental.pallas import tpu as pltpu

D_MODEL = 2048
ATTN_HEADS = 16
HEAD_DIM = 128
ATTN_W = ATTN_HEADS * HEAD_DIM
SSM_HEADS = 32
SSM_HEAD_DIM = 64
SSM_W = SSM_HEADS * SSM_HEAD_DIM
SSM_GROUPS = 8
HEADS_PER_GROUP = SSM_HEADS // SSM_GROUPS
GROUP_W = SSM_W // SSM_GROUPS
SSM_STATE = 128
CONV_WIDTH = 5
D_FF = 4 * D_MODEL
DILATIONS = (1, 4, 16)
HALF_WINDOW = 64
NUM_BUCKETS = 32
MAX_DISTANCE = 1024
NEG_INF = -1e30
EPS = 1e-6

LANES = 128
Q_SUB = 128
K_WIN = 256
SSD_CHUNK = 128
SSD_UNROLL = 16
SSD_UNROLL_FWD = 16
CONV_HALO = 8
VMEM_LIMIT = 56 * 1024 * 1024

F32 = jnp.float32
BF16 = jnp.bfloat16


def _params(*sem):
    return pltpu.CompilerParams(dimension_semantics=sem, vmem_limit_bytes=VMEM_LIMIT)


def _sigmoid(x):
    return 1.0 / (1.0 + jnp.exp(-x))


def _softplus(x):
    return jnp.maximum(x, 0.0) + jnp.log1p(jnp.exp(-jnp.abs(x)))


def _rmsnorm_dt_kernel(x_ref, g_ref, w_ref, h_ref, dt_ref):
    x = x_ref[...]
    ms = jnp.mean(x * x, axis=-1, keepdims=True)
    h = (x * lax.rsqrt(ms + EPS) * g_ref[...]).astype(h_ref.dtype)
    h_ref[...] = h
    dt_ref[...] = jnp.dot(h, w_ref[...], preferred_element_type=F32)


def _rmsnorm_dt(x2d, g, w_dt, tm=512):
    n, d = x2d.shape
    m = w_dt.shape[1]
    return pl.pallas_call(
        _rmsnorm_dt_kernel,
        grid=(n // tm,),
        in_specs=[pl.BlockSpec((tm, d), lambda i: (i, 0)),
                  pl.BlockSpec((1, d), lambda i: (0, 0)),
                  pl.BlockSpec((d, m), lambda i: (0, 0))],
        out_specs=[pl.BlockSpec((tm, d), lambda i: (i, 0)),
                   pl.BlockSpec((tm, m), lambda i: (i, 0))],
        out_shape=[jax.ShapeDtypeStruct((n, d), BF16), jax.ShapeDtypeStruct((n, m), F32)],
        compiler_params=_params("parallel"),
        name="rmsnorm_dt",
    )(x2d, g.reshape(1, d), w_dt)


QK_SUB = 256


def _proj_qk_kernel(a_ref, w_ref, g_ref, o_ref, *, heads_per_tile):
    a = a_ref[...]
    per = QK_SUB // HEAD_DIM
    for c in range(heads_per_tile // per):
        acc = jnp.dot(a, w_ref[:, c * QK_SUB:(c + 1) * QK_SUB], preferred_element_type=F32)
        for hh in range(per):
            h = c * per + hh
            s = acc[:, hh * HEAD_DIM:(hh + 1) * HEAD_DIM]
            ms = jnp.mean(s * s, axis=-1, keepdims=True)
            g = g_ref[:, h * HEAD_DIM:(h + 1) * HEAD_DIM]
            o_ref[h] = (s * lax.rsqrt(ms + EPS) * g).astype(o_ref.dtype)


def _proj_split_kernel(a_ref, w_ref, o_ref, *, width):
    a = a_ref[...]
    per = QK_SUB // width
    for s in range(w_ref.shape[1] // QK_SUB):
        acc = jnp.dot(a, w_ref[:, s * QK_SUB:(s + 1) * QK_SUB], preferred_element_type=F32)
        for c in range(per):
            o_ref[s * per + c] = acc[:, c * width:(c + 1) * width].astype(o_ref.dtype)


def _proj_qk(h, w, gains, tm=2048, tn=1024):
    n, k = h.shape
    m = 2 * ATTN_W
    hpt = tn // HEAD_DIM
    return pl.pallas_call(
        functools.partial(_proj_qk_kernel, heads_per_tile=hpt),
        grid=(m // tn, n // tm),
        in_specs=[pl.BlockSpec((tm, k), lambda j, i: (i, 0)),
                  pl.BlockSpec((k, tn), lambda j, i: (0, j)),
                  pl.BlockSpec((1, tn), lambda j, i: (0, j))],
        out_specs=pl.BlockSpec((hpt, tm, HEAD_DIM), lambda j, i: (j, i, 0)),
        out_shape=jax.ShapeDtypeStruct((m // HEAD_DIM, n, HEAD_DIM), F32),
        compiler_params=_params("parallel", "arbitrary"),
        name="proj_qk",
    )(h, w, gains)


def _proj_split(h, w, col0, m, width, name, out_dtype=BF16, tm=2048, tn=1024):
    n, k = h.shape
    cpt = tn // width
    j0 = col0 // tn
    return pl.pallas_call(
        functools.partial(_proj_split_kernel, width=width),
        grid=(m // tn, n // tm),
        in_specs=[pl.BlockSpec((tm, k), lambda j, i: (i, 0)),
                  pl.BlockSpec((k, tn), lambda j, i: (0, j0 + j))],
        out_specs=pl.BlockSpec((cpt, tm, width), lambda j, i: (j, i, 0)),
        out_shape=jax.ShapeDtypeStruct((m // width, n, width), out_dtype),
        compiler_params=_params("parallel", "arbitrary"),
        name=name,
    )(h, w)


def _t5_bucket(rel):
    nb = NUM_BUCKETS // 2
    max_exact = nb // 2
    ret = (rel > 0).astype(jnp.int32) * nb
    n = jnp.abs(rel)
    nf = jnp.maximum(n, 1).astype(jnp.float32)
    large = max_exact + (jnp.log(nf / max_exact) / math.log(MAX_DISTANCE / max_exact)
                         * (nb - max_exact)).astype(jnp.int32)
    large = jnp.minimum(large, nb - 1)
    return ret + jnp.where(n < max_exact, n, large)


def _bias_rows(rel_bias):
    period = K_WIN + Q_SUB
    m = jnp.arange(period)
    delta = jnp.where(m < K_WIN, m, m - period)
    rows = []
    for d in DILATIONS:
        for off in (0, -HALF_WINDOW, -2 * HALF_WINDOW):
            rel = delta + off
            valid = jnp.abs(rel) <= HALF_WINDOW
            b = rel_bias[_t5_bucket(rel * d)].astype(F32)
            rows.append(jnp.where(valid[:, None], b, NEG_INF))
    return jnp.stack(rows, axis=0).transpose(2, 0, 1)


ATTN_UNROLL_MERGE = 32
ATTN_UNROLL = 32


def _attn_kernel(q_ref, k_ref, v_ref, brow_ref, o_ref, bias_s, tmp_s, xq4, xt4,
                 q16, k1, k4, k16, v1, v4, v16, acc_s, max_s, den_s, *, seq):
    period = K_WIN + Q_SUB
    n4, n16 = seq // 4, seq // 16

    @pl.when(pl.program_id(1) == 0)
    def _():
        for idx in range(9):
            row = jnp.broadcast_to(brow_ref[idx:idx + 1, :], (Q_SUB, period))
            tile = pltpu.roll(row, 0, 1, stride=1, stride_axis=0)
            d = DILATIONS[idx // 3]
            if d == 16:
                bias_s[idx] = tile[:, :K_WIN]
                continue
            for half in range(K_WIN // LANES):
                tmp_s[half] = tile[:, half * LANES:(half + 1) * LANES]
            groups = 16 // d
            for half in range(K_WIN // LANES):
                for g in range(groups):
                    n = Q_SUB // groups
                    bias_s[idx, g * n:(g + 1) * n, half * LANES:(half + 1) * LANES] = (
                        tmp_s[half, pl.ds(g, n, stride=groups), :])

    cp = 256

    def split4(src, dst):
        for r4 in range(4):
            def body(c, carry, r4=r4):
                t0 = pl.multiple_of(c * cp, cp)
                dst[r4, pl.ds(t0, cp), :] = src[pl.ds(r4 + 4 * t0, cp, stride=4), :]
                return carry
            lax.fori_loop(0, n4 // cp, body, 0)

    def split16(src4, dst16, dst4=None):
        for r4 in range(4):
            for a in range(4):
                dst16[4 * a + r4] = src4[r4, pl.ds(a, n16, stride=4), :].astype(BF16)
            if dst4 is not None:
                def body(c, carry, r4=r4):
                    t0 = pl.multiple_of(c * cp, cp)
                    dst4[r4, pl.ds(t0, cp), :] = src4[r4, pl.ds(t0, cp), :].astype(BF16)
                    return carry
                lax.fori_loop(0, n4 // cp, body, 0)

    def cast(src, dst):
        def body(c, carry):
            t0 = pl.multiple_of(c * cp, cp)
            dst[pl.ds(t0, cp), :] = src[pl.ds(t0, cp), :].astype(BF16)
            return carry
        lax.fori_loop(0, seq // cp, body, 0)

    split4(q_ref, xq4)
    split16(xq4, q16)
    for src, d1, d4, d16 in ((k_ref, k1, k4, k16), (v_ref, v1, v4, v16)):
        cast(src, d1)
        split4(src, xt4)
        split16(xt4, d16, d4)

    ones = jnp.ones((K_WIN, LANES), BF16)

    def sub_tile(q, k_s, v_s, length, s0, pi):
        if isinstance(s0, int):
            w0 = min(max(s0 - HALF_WINDOW, 0), length - K_WIN)
            place = 0 if s0 == 0 else (2 if s0 == length - Q_SUB else 1)
        else:
            w0 = pl.multiple_of(jnp.clip(s0 - HALF_WINDOW, 0, length - K_WIN), HALF_WINDOW)
            place = jnp.where(s0 == 0, 0, jnp.where(s0 == length - Q_SUB, 2, 1))
        k = k_s[pl.ds(w0, K_WIN), :]
        v = v_s[pl.ds(w0, K_WIN), :]
        s = lax.dot_general(q, k, (((1,), (1,)), ((), ())), preferred_element_type=F32)
        s = s + bias_s[3 * pi + place]
        m = jnp.max(s, axis=-1, keepdims=True)
        p = jnp.exp(s - m).astype(BF16)
        pv = jnp.dot(p, jnp.concatenate([v, ones], axis=1), preferred_element_type=F32)
        return pv[:, :LANES], jnp.broadcast_to(m, (Q_SUB, LANES)), pv[:, LANES:]

    def store(pi, res, slab_of_group, l0, n):
        for g in range(Q_SUB // n):
            for val, dst in zip(res, (acc_s, max_s, den_s)):
                dst[pi, slab_of_group(g), pl.ds(l0, n), :] = val[g * n:(g + 1) * n]

    U = ATTN_UNROLL

    def body4(it, carry):
        res = []
        for u in range(U // 4):
            l0 = pl.multiple_of((it * (U // 4) + u) * 32, 32)
            for r4 in range(4):
                q = jnp.concatenate([q16[4 * a + r4, pl.ds(l0, 32), :] for a in range(4)], axis=0)
                res.append((sub_tile(q, k4.at[r4], v4.at[r4], n4, pl.multiple_of(4 * l0, Q_SUB), 1),
                            r4, l0))
        for r3, r4, l0 in res:
            store(0, r3, lambda g, r4=r4: 4 * g + r4, l0, 32)
        return carry
    lax.fori_loop(0, n4 // Q_SUB // (U // 4), body4, 0)

    per16 = U // (n16 // Q_SUB)

    def body16(it, carry):
        g0 = it * per16
        res = []
        for u in range(per16):
            for l0 in range(0, n16, Q_SUB):
                res.append((sub_tile(q16[g0 + u, pl.ds(l0, Q_SUB), :], k16.at[g0 + u], v16.at[g0 + u],
                                     n16, l0, 2), u, l0))
        for r3, u, l0 in res:
            store(1, r3, lambda g, u=u: g0 + u, l0, Q_SUB)
        return carry
    lax.fori_loop(0, 16 // per16, body16, 0)

    U1 = ATTN_UNROLL_MERGE

    def body1(it, carry):
        res = []
        for u in range(U1):
            l0 = pl.multiple_of((it * U1 + u) * 8, 8)
            q = jnp.concatenate([xq4[r % 4, pl.ds(4 * l0 + r // 4, 8, stride=4), :] for r in range(16)],
                                axis=0).astype(BF16)
            res.append((sub_tile(q, k1, v1, seq, pl.multiple_of(16 * l0, Q_SUB), 0), l0))
        for (aa, ma, da), l0 in res:
            rows = lambda ref, pi: jnp.concatenate([ref[pi, r, pl.ds(l0, 8), :] for r in range(16)], axis=0)
            mb, mc = rows(max_s, 0), rows(max_s, 1)
            mx = jnp.maximum(jnp.maximum(ma, mb), mc)
            ea, eb, ec = jnp.exp(ma - mx), jnp.exp(mb - mx), jnp.exp(mc - mx)
            num = ea * aa + eb * rows(acc_s, 0) + ec * rows(acc_s, 1)
            den = ea * da + eb * rows(den_s, 0) + ec * rows(den_s, 1)
            out = num / den
            for r in range(16):
                o_ref[pl.ds(r + 16 * l0, 8, stride=16), :] = out[r * 8:(r + 1) * 8]
        return carry
    lax.fori_loop(0, seq // Q_SUB // U1, body1, 0)


def _attention(qk, v, brow, batch, seq):
    H = ATTN_HEADS
    n4, n16 = seq // 4, seq // 16
    qk4 = qk.reshape(2 * H, batch, seq, HEAD_DIM)
    v4 = v.reshape(H, batch, seq, HEAD_DIM)
    in_specs = [pl.BlockSpec((None, None, seq, HEAD_DIM), lambda h, b: (h, b, 0, 0)),
                pl.BlockSpec((None, None, seq, HEAD_DIM), lambda h, b: (H + h, b, 0, 0)),
                pl.BlockSpec((None, None, seq, HEAD_DIM), lambda h, b: (h, b, 0, 0))]
    in_specs.append(pl.BlockSpec((None, 9, K_WIN + Q_SUB), lambda h, b: (h, 0, 0)))
    kv_slabs = [pltpu.VMEM((seq, LANES), BF16), pltpu.VMEM((4, n4, LANES), BF16),
                pltpu.VMEM((16, n16, LANES), BF16)]
    out = pl.pallas_call(
        functools.partial(_attn_kernel, seq=seq),
        grid=(H, batch),
        in_specs=in_specs,
        out_specs=pl.BlockSpec((None, None, seq, HEAD_DIM), lambda h, b: (h, b, 0, 0)),
        out_shape=jax.ShapeDtypeStruct((H, batch, seq, HEAD_DIM), F32),
        scratch_shapes=[pltpu.VMEM((9, Q_SUB, K_WIN), F32),
                        pltpu.VMEM((K_WIN // LANES, Q_SUB, LANES), F32),
                        pltpu.VMEM((4, n4, LANES), F32),
                        pltpu.VMEM((4, n4, LANES), F32),
                        pltpu.VMEM((16, n16, LANES), BF16)]
                       + kv_slabs + kv_slabs
                       + [pltpu.VMEM((2, 16, n16, LANES), F32)] * 3,
        compiler_params=_params("parallel", "arbitrary"),
        name="dilated_attention",
    )(qk4, qk4, v4, brow)
    return out.reshape(H, batch * seq, HEAD_DIM)


def _ssd_kernel(z_ref, x_ref, b_ref, c_ref, dtr_ref, cwx_ref, cwb_ref, cwc_ref,
                cbx_ref, cbb_ref, cbc_ref, dbr_ref, alr_ref, dsk_ref, ng_ref,
                o_ref, pad, cv, y_s, arg_s, diag_s, rows_s, cols_s, sbs, sf, sb, *, seq):
    T = SSD_CHUNK
    nc = seq // T
    hi = lax.Precision.HIGHEST
    halo = CONV_HALO
    half = T // 2

    pad[:, 0:halo, :] = jnp.zeros((4, halo, LANES), F32)
    pad[:, seq + halo:seq + 2 * halo, :] = jnp.zeros((4, halo, LANES), F32)

    def fill(i, carry):
        t0 = pl.multiple_of(i * T, T)
        xin = x_ref[pl.ds(t0, T), :].astype(F32)
        pad[0, pl.ds(t0 + halo, T), :] = xin[:, :LANES]
        pad[1, pl.ds(t0 + halo, T), :] = xin[:, LANES:]
        pad[2, pl.ds(t0 + halo, T), :] = b_ref[pl.ds(t0, T), :].astype(F32)
        pad[3, pl.ds(t0 + halo, T), :] = c_ref[pl.ds(t0, T), :].astype(F32)
        return carry
    lax.fori_loop(0, nc, fill, 0)

    cws = (cwx_ref[:, :LANES], cwx_ref[:, LANES:], cwb_ref[...], cwc_ref[...])
    cbs = (cbx_ref[:, :LANES], cbx_ref[:, LANES:], cbb_ref[...], cbc_ref[...])

    def conv_chunk(ci):
        t0 = pl.multiple_of(ci * T, T)
        for s in range(4):
            ev = [pad[s, pl.ds(t0 + halo + 2 * k, half, stride=2), :] for k in (-1, 0, 1)]
            od = [pad[s, pl.ds(t0 + halo + 1 + 2 * k, half, stride=2), :] for k in (-1, 0, 1)]
            w = [cws[s][j:j + 1, :] for j in range(CONV_WIDTH)]
            out_e = cbs[s] + w[0] * ev[0] + w[1] * od[0] + w[2] * ev[1] + w[3] * od[1] + w[4] * ev[2]
            out_o = cbs[s] + w[0] * od[0] + w[1] * ev[1] + w[2] * od[1] + w[3] * ev[2] + w[4] * od[2]
            cv[s, pl.ds(t0, half, stride=2), :] = out_e * _sigmoid(out_e)
            cv[s, pl.ds(t0 + 1, half, stride=2), :] = out_o * _sigmoid(out_o)

    def load_x(t0):
        return jnp.concatenate([cv[0, pl.ds(t0, T), :], cv[1, pl.ds(t0, T), :]], axis=1)

    nh = HEADS_PER_GROUP
    nd = 2 * nh
    row = lax.broadcasted_iota(jnp.int32, (T, T), 0)
    col = lax.broadcasted_iota(jnp.int32, (T, T), 1)
    triu = (row <= col).astype(F32)
    lower = col < row
    upper = col > row
    lane_head = lax.broadcasted_iota(jnp.int32, (T, GROUP_W), 1) // SSM_HEAD_DIM

    dt = _softplus(dtr_ref[...] + dbr_ref[...])
    a = dt * (-jnp.exp(alr_ref[...]))
    cum = jnp.dot(a.reshape(nc * nd, T), triu, precision=hi,
                  preferred_element_type=F32).reshape(nc, nd, T)
    last = cum[:, :, T - 1:T]
    exc = cum - a
    fwd = lax.broadcasted_iota(jnp.int32, (nc, nd, T), 1) < nh
    base = jnp.where(fwd, cum, exc)
    log_dt = jnp.log(dt)
    arg_s[...] = jnp.where(fwd, cum - log_dt, exc + log_dt)
    log_sum = jnp.log(dt[:, 0:nh, :] + dt[:, nh:, :])
    diag_s[...] = jnp.concatenate([log_sum, log_sum], axis=1)
    rows_s[:, 0:nd, :] = base
    rows_s[:, nd:2 * nd, :] = jnp.exp(jnp.where(fwd, cum, last - exc))
    rows_s[:, 2 * nd:3 * nd, :] = dt * jnp.exp(jnp.where(fwd, last - cum, exc))
    rows_s[:, 3 * nd:, :] = jnp.zeros((nc, LANES - 3 * nd, T), F32)

    def expand(cols, first):
        n = cols.shape[0]
        low = lax.broadcasted_iota(jnp.int32, (n, LANES), 1) < SSM_HEAD_DIM
        halves = []
        for j in range(GROUP_W // LANES):
            c0 = jnp.broadcast_to(cols[:, first + 2 * j:first + 2 * j + 1], (n, LANES))
            c1 = jnp.broadcast_to(cols[:, first + 2 * j + 1:first + 2 * j + 2], (n, LANES))
            halves.append(jnp.where(low, c0, c1))
        return jnp.concatenate(halves, axis=1)

    tn_dims = (((0,), (0,)), ((), ()))

    cid = lax.broadcasted_iota(jnp.int32, (LANES, 4 * GROUP_W), 0)
    lid = lax.broadcasted_iota(jnp.int32, (LANES, 4 * GROUP_W), 1)
    blk = lid // GROUP_W
    want = nd + (blk % 2) * nh + (blk // 2) * nd + (lid % GROUP_W) // SSM_HEAD_DIM
    spread = jnp.where(cid == want, 1.0, 0.0).astype(BF16)
    spread_f = spread[:, :3 * GROUP_W]
    spread_b = spread[:, 3 * GROUP_W:]

    sb[...] = jnp.zeros_like(sb)

    def state_back(ci):
        t0 = pl.multiple_of(ci * T, T)
        ct = rows_s[ci].T
        cols_s[ci] = ct
        s_prev = sb[...]
        sbs[ci] = s_prev.astype(BF16)
        wx = jnp.dot(ct.astype(BF16), spread_b, preferred_element_type=F32)
        xs = (load_x(t0) * wx).astype(BF16)
        sb[...] = expand(ct[0:1, :], nd + nh) * s_prev + lax.dot_general(
            cv[2, pl.ds(t0, T), :].astype(BF16), xs, tn_dims, preferred_element_type=F32)

    U = SSD_UNROLL
    for u in range(U):
        conv_chunk(nc - 1 - u)

    def sweep_back(i, carry):
        k = nc - 1 - U * i
        for u in range(U):
            state_back(k + U - u)
        for u in range(U):
            conv_chunk(k - u)
        return carry
    lax.fori_loop(1, nc // U, sweep_back, 0)
    for u in range(U):
        state_back(U - 1 - u)

    sf[...] = jnp.zeros_like(sf)
    low_half = lax.broadcasted_iota(jnp.int32, (T, LANES), 1) < SSM_HEAD_DIM

    def finish(ci):
        t0 = pl.multiple_of(ci * T, T)
        zz = z_ref[pl.ds(t0, T), :].astype(F32)
        y = y_s[pl.ds(t0, T), :] * (zz * _sigmoid(zz))
        ms = jnp.mean(y * y, axis=-1, keepdims=True)
        o_ref[pl.ds(t0, T), :] = (y * lax.rsqrt(ms + EPS) * ng_ref[...]).astype(o_ref.dtype)

    def chunk_fwd(ci):
        t0 = pl.multiple_of(ci * T, T)
        ct = cols_s[ci]
        arg_r = arg_s[ci]
        dt_r = diag_s[ci]
        x = load_x(t0)
        bk = cv[2, pl.ds(t0, T), :].astype(BF16)
        ck = cv[3, pl.ds(t0, T), :].astype(BF16)
        cb = lax.dot_general(ck, bk, (((1,), (1,)), ((), ())), preferred_element_type=F32)
        spreadv = jnp.dot(ct.astype(BF16), spread_f, preferred_element_type=F32)
        lhs = []
        for h in range(nh):
            hb = nh + h
            arg = jnp.where(lower, ct[:, h:h + 1] - arg_r[h:h + 1, :],
                            jnp.where(upper, arg_r[hb:hb + 1, :] - ct[:, hb:hb + 1], dt_r[h:h + 1, :]))
            lhs.append((cb * jnp.exp(arg)).astype(BF16))
        s_prev = sf[...]
        y4 = jnp.dot(jnp.concatenate(lhs, axis=0), x.astype(BF16), preferred_element_type=F32)
        off = jnp.dot(ck, jnp.concatenate([s_prev.astype(BF16), sbs[ci]], axis=1),
                      preferred_element_type=F32)
        y = x * dsk_ref[...] + jnp.concatenate(
            [jnp.where(low_half, y4[2 * j * T:(2 * j + 1) * T, j * LANES:(j + 1) * LANES],
                       y4[(2 * j + 1) * T:(2 * j + 2) * T, j * LANES:(j + 1) * LANES])
             for j in range(GROUP_W // LANES)], axis=1)
        y = y + spreadv[:, :GROUP_W] * off[:, :GROUP_W] + spreadv[:, GROUP_W:2 * GROUP_W] * off[:, GROUP_W:]
        xs = (x * spreadv[:, 2 * GROUP_W:]).astype(BF16)
        sf[...] = expand(ct[T - 1:T, :], nd) * s_prev + lax.dot_general(
            bk, xs, tn_dims, preferred_element_type=F32)
        y_s[pl.ds(t0, T), :] = y

    U = SSD_UNROLL_FWD
    for u in range(U):
        chunk_fwd(u)

    def sweep_fwd(i, carry):
        c = U * i
        for u in range(U):
            finish(c - U + u)
        for u in range(U):
            chunk_fwd(c + u)
        return carry
    lax.fori_loop(1, nc // U, sweep_fwd, 0)
    for u in range(U):
        finish(nc - U + u)


def _ssd(zx, bcm, dt_row, cwx, cwb, cwc, cbx, cbb, cbc, db_row, al_row, dskip, ng, batch, seq):
    G = SSM_GROUPS
    nc = seq // SSD_CHUNK
    zx4 = zx.reshape(2 * G, batch, seq, GROUP_W)
    bc4 = bcm.reshape(2 * G, batch, seq, SSM_STATE)
    nd = 2 * HEADS_PER_GROUP

    def per_group(shape):
        return pl.BlockSpec((None,) + shape, lambda b, g: (g,) + (0,) * len(shape))

    in_specs = [
        pl.BlockSpec((None, None, seq, GROUP_W), lambda b, g: (g, b, 0, 0)),
        pl.BlockSpec((None, None, seq, GROUP_W), lambda b, g: (G + g, b, 0, 0)),
        pl.BlockSpec((None, None, seq, SSM_STATE), lambda b, g: (g, b, 0, 0)),
        pl.BlockSpec((None, None, seq, SSM_STATE), lambda b, g: (G + g, b, 0, 0)),
        pl.BlockSpec((None, None, nc, nd, SSD_CHUNK), lambda b, g: (b, g, 0, 0, 0)),
        per_group((CONV_WIDTH, GROUP_W)), per_group((CONV_WIDTH, SSM_STATE)),
        per_group((CONV_WIDTH, SSM_STATE)),
        per_group((1, GROUP_W)), per_group((1, SSM_STATE)), per_group((1, SSM_STATE)),
        per_group((nd, 1)), per_group((nd, 1)),
        per_group((1, GROUP_W)), per_group((1, GROUP_W)),
    ]
    out = pl.pallas_call(
        functools.partial(_ssd_kernel, seq=seq),
        grid=(batch, G),
        in_specs=in_specs,
        out_specs=pl.BlockSpec((None, None, seq, GROUP_W), lambda b, g: (g, b, 0, 0)),
        out_shape=jax.ShapeDtypeStruct((G, batch, seq, GROUP_W), BF16),
        scratch_shapes=[pltpu.VMEM((4, seq + 2 * CONV_HALO, LANES), F32),
                        pltpu.VMEM((4, seq, LANES), F32),
                        pltpu.VMEM((seq, GROUP_W), F32),
                        pltpu.VMEM((nc, nd, SSD_CHUNK), F32),
                        pltpu.VMEM((nc, nd, SSD_CHUNK), F32),
                        pltpu.VMEM((nc, LANES, SSD_CHUNK), F32),
                        pltpu.VMEM((nc, SSD_CHUNK, LANES), F32),
                        pltpu.VMEM((nc, SSM_STATE, GROUP_W), BF16),
                        pltpu.VMEM((SSM_STATE, GROUP_W), F32),
                        pltpu.VMEM((SSM_STATE, GROUP_W), F32)],
        compiler_params=_params("parallel", "parallel"),
        name="ssd",
    )(zx4, zx4, bc4, bc4, dt_row, cwx, cwb, cwc, cbx, cbb, cbc, db_row, al_row, dskip, ng)
    return out.reshape(G, batch * seq, GROUP_W)


def _out_proj_kernel(attn_ref, ssd_ref, w_ref, x_ref, o_ref, lhs):
    for h in range(ATTN_HEADS):
        lhs[:, h * HEAD_DIM:(h + 1) * HEAD_DIM] = attn_ref[h].astype(BF16)
    for g in range(SSM_GROUPS):
        lhs[:, ATTN_W + g * GROUP_W:ATTN_W + (g + 1) * GROUP_W] = ssd_ref[g]
    o_ref[...] = x_ref[...] + jnp.dot(lhs[...], w_ref[...], preferred_element_type=F32)


def _out_proj(attn, ssd, w, x2d, tm=512):
    n, d = x2d.shape
    kk = w.shape[0]
    return pl.pallas_call(
        _out_proj_kernel,
        grid=(n // tm,),
        in_specs=[pl.BlockSpec((ATTN_HEADS, tm, HEAD_DIM), lambda i: (0, i, 0)),
                  pl.BlockSpec((SSM_GROUPS, tm, GROUP_W), lambda i: (0, i, 0)),
                  pl.BlockSpec((kk, d), lambda i: (0, 0), pipeline_mode=pl.Buffered(1)),
                  pl.BlockSpec((tm, d), lambda i: (i, 0))],
        out_specs=pl.BlockSpec((tm, d), lambda i: (i, 0)),
        out_shape=jax.ShapeDtypeStruct((n, d), F32),
        scratch_shapes=[pltpu.VMEM((tm, kk), BF16)],
        compiler_params=_params("parallel"),
        name="out_proj",
    )(attn, ssd, w, x2d)


def _mlp_kernel(x_ref, g_ref, wu_ref, wd_ref, o_ref, hm):
    @pl.when(pl.program_id(1) == 0)
    def _():
        x = x_ref[...]
        ms = jnp.mean(x * x, axis=-1, keepdims=True)
        hm[...] = (x * lax.rsqrt(ms + EPS) * g_ref[...]).astype(hm.dtype)
        o_ref[...] = x
    u = jnp.maximum(jnp.dot(hm[...], wu_ref[...], preferred_element_type=F32), 0.0)
    o_ref[...] += jnp.dot((u * u).astype(BF16), wd_ref[...], preferred_element_type=F32)


def _mlp(x2d, g, wu, wd, tm=512, tf=1024):
    n, d = x2d.shape
    f = wu.shape[1]
    return pl.pallas_call(
        _mlp_kernel,
        grid=(n // tm, f // tf),
        in_specs=[pl.BlockSpec((tm, d), lambda i, j: (i, 0)),
                  pl.BlockSpec((1, d), lambda i, j: (0, 0)),
                  pl.BlockSpec((d, tf), lambda i, j: (0, j)),
                  pl.BlockSpec((tf, d), lambda i, j: (j, 0))],
        out_specs=pl.BlockSpec((tm, d), lambda i, j: (i, 0)),
        out_shape=jax.ShapeDtypeStruct((n, d), F32),
        scratch_shapes=[pltpu.VMEM((tm, d), BF16)],
        compiler_params=_params("parallel", "arbitrary"),
        name="mlp",
    )(x2d, g.reshape(1, d), wu, wd)


def kernel(x, norm_mix_g, w_in, q_norm_g, k_norm_g, rel_bias, conv_w, conv_b, dt_bias, a_log,
           d_skip, ssd_norm_g, w_out, norm_mlp_g, w_up, w_down):
    batch, seq, _ = x.shape
    n = batch * seq
    G, nh = SSM_GROUPS, HEADS_PER_GROUP
    nc = seq // SSD_CHUNK
    o_z = 3 * ATTN_W
    o_bc = o_z + 2 * SSM_W
    o_dt = o_bc + 2 * G * SSM_STATE
    x2d = x.reshape(n, D_MODEL)
    brow = _bias_rows(rel_bias)

    for layer in range(w_in.shape[0]):
        wi = w_in[layer].astype(BF16)
        h, dt_raw = _rmsnorm_dt(x2d, norm_mix_g[layer], wi[:, o_dt:])

        scale = 1.0 / math.sqrt(HEAD_DIM)
        gains = jnp.concatenate([jnp.tile(q_norm_g[layer].astype(F32) * scale, ATTN_HEADS),
                                 jnp.tile(k_norm_g[layer].astype(F32), ATTN_HEADS)]).reshape(1, 2 * ATTN_W)
        qk = _proj_qk(h, wi, gains)
        v = _proj_split(h, wi, 2 * ATTN_W, ATTN_W, HEAD_DIM, "proj_v", out_dtype=F32)
        zx = _proj_split(h, wi, o_z, o_bc - o_z, GROUP_W, "proj_zx")
        bcm = _proj_split(h, wi, o_bc, o_dt - o_bc, SSM_STATE, "proj_bc")

        attn = _attention(qk, v, brow, batch, seq)

        dt_row = (dt_raw.reshape(batch, nc, SSD_CHUNK, 2, G, nh)
                  .transpose(0, 4, 1, 3, 5, 2).reshape(batch, G, nc, 2 * nh, SSD_CHUNK))
        per_dir = lambda t: t.astype(F32).reshape(2, G, nh).transpose(1, 0, 2).reshape(G, 2 * nh)
        db, al = per_dir(dt_bias[layer]), per_dir(a_log[layer])
        cw, cbias = conv_w[layer].astype(F32), conv_b[layer].astype(F32)
        gn = G * SSM_STATE
        grp = lambda t, width: t.reshape(t.shape[0], G, width).transpose(1, 0, 2)
        ssd = _ssd(
            zx, bcm, dt_row,
            grp(cw[:, :SSM_W], GROUP_W), grp(cw[:, SSM_W:SSM_W + gn], SSM_STATE),
            grp(cw[:, SSM_W + gn:], SSM_STATE),
            grp(cbias[None, :SSM_W], GROUP_W), grp(cbias[None, SSM_W:SSM_W + gn], SSM_STATE),
            grp(cbias[None, SSM_W + gn:], SSM_STATE),
            db.reshape(G, 2 * nh, 1), al.reshape(G, 2 * nh, 1),
            jnp.repeat(d_skip[layer].astype(F32), SSM_HEAD_DIM).reshape(G, 1, GROUP_W),
            ssd_norm_g[layer].astype(F32).reshape(G, 1, GROUP_W),
            batch, seq)

        x2d = _out_proj(attn, ssd, w_out[layer].astype(BF16), x2d)
        x2d = _mlp(x2d, norm_mlp_g[layer], w_up[layer].astype(BF16), w_down[layer].astype(BF16))
    return x2d.reshape(batch, seq, D_MODEL)
```

```python
import functools
import math

import jax
import jax.numpy as jnp
from jax import lax
from jax.experimental import pallas as pl
from jax.experimental.pallas import tpu as pltpu

D_MODEL = 2048
ATTN_HEADS = 16
HEAD_DIM = 128
ATTN_W = ATTN_HEADS * HEAD_DIM
SSM_HEADS = 32
SSM_HEAD_DIM = 64
SSM_W = SSM_HEADS * SSM_HEAD_DIM
SSM_GROUPS = 8
HEADS_PER_GROUP = SSM_HEADS // SSM_GROUPS
GROUP_W = SSM_W // SSM_GROUPS
SSM_STATE = 128
CONV_WIDTH = 5
DILATIONS = (1, 4, 16)
HALF_WINDOW = 64
NUM_BUCKETS = 32
MAX_DISTANCE = 1024
NEG_INF = -1e30
EPS = 1e-6

LANES = 128
Q_SUB = 128
K_WIN = 256
SSD_CHUNK = 128
SSD_UNROLL = 16
SSD_UNROLL_FWD = 16
CONV_HALO = 8
VMEM_LIMIT = 56 * 1024 * 1024

F32 = jnp.float32
BF16 = jnp.bfloat16


def _params(*sem):
    return pltpu.CompilerParams(dimension_semantics=sem, vmem_limit_bytes=VMEM_LIMIT)


def _sigmoid(x):
    return 1.0 / (1.0 + jnp.exp2(x * (-1.0 / math.log(2.0))))


def _softplus(x):
    return jnp.maximum(x, 0.0) + jnp.log1p(jnp.exp(-jnp.abs(x)))


def _rmsnorm_dt_kernel(x_ref, g_ref, w_ref, h_ref, dt_ref):
    x = x_ref[...]
    ms = jnp.mean(x * x, axis=-1, keepdims=True)
    h = (x * lax.rsqrt(ms + EPS) * g_ref[...]).astype(h_ref.dtype)
    h_ref[...] = h
    dt_ref[...] = jnp.dot(h, w_ref[...], preferred_element_type=F32)


def _rmsnorm_dt(x2d, g, w_dt, tm=512):
    n, d = x2d.shape
    m = w_dt.shape[1]
    return pl.pallas_call(
        _rmsnorm_dt_kernel,
        grid=(n // tm,),
        in_specs=[pl.BlockSpec((tm, d), lambda i: (i, 0)),
                  pl.BlockSpec((1, d), lambda i: (0, 0)),
                  pl.BlockSpec((d, m), lambda i: (0, 0))],
        out_specs=[pl.BlockSpec((tm, d), lambda i: (i, 0)),
                   pl.BlockSpec((tm, m), lambda i: (i, 0))],
        out_shape=[jax.ShapeDtypeStruct((n, d), BF16), jax.ShapeDtypeStruct((n, m), F32)],
        compiler_params=_params("parallel"),
        name="rmsnorm_dt",
    )(x2d, g.reshape(1, d), w_dt)


QK_SUB = 256


def _proj_qk_kernel(a_ref, w_ref, g_ref, o_ref, *, heads_per_tile):
    a = a_ref[...]
    per = QK_SUB // HEAD_DIM
    for c in range(heads_per_tile // per):
        acc = jnp.dot(a, w_ref[:, c * QK_SUB:(c + 1) * QK_SUB], preferred_element_type=F32)
        for hh in range(per):
            h = c * per + hh
            s = acc[:, hh * HEAD_DIM:(hh + 1) * HEAD_DIM]
            ms = jnp.mean(s * s, axis=-1, keepdims=True)
            g = g_ref[:, h * HEAD_DIM:(h + 1) * HEAD_DIM]
            o_ref[h] = (s * lax.rsqrt(ms + EPS) * g).astype(o_ref.dtype)


def _proj_split_kernel(a_ref, w_ref, o_ref, *, width):
    a = a_ref[...]
    per = QK_SUB // width
    for s in range(w_ref.shape[1] // QK_SUB):
        acc = jnp.dot(a, w_ref[:, s * QK_SUB:(s + 1) * QK_SUB], preferred_element_type=F32)
        for c in range(per):
            o_ref[s * per + c] = acc[:, c * width:(c + 1) * width].astype(o_ref.dtype)


def _proj_qk(h, w, gains, tm=1024, tn=1024):
    n, k = h.shape
    m = 2 * ATTN_W
    hpt = tn // HEAD_DIM
    return pl.pallas_call(
        functools.partial(_proj_qk_kernel, heads_per_tile=hpt),
        grid=(m // tn, n // tm),
        in_specs=[pl.BlockSpec((tm, k), lambda j, i: (i, 0)),
                  pl.BlockSpec((k, tn), lambda j, i: (0, j)),
                  pl.BlockSpec((1, tn), lambda j, i: (0, j))],
        out_specs=pl.BlockSpec((hpt, tm, HEAD_DIM), lambda j, i: (j, i, 0)),
        out_shape=jax.ShapeDtypeStruct((m // HEAD_DIM, n, HEAD_DIM), F32),
        compiler_params=_params("parallel", "arbitrary"),
        name="proj_qk",
    )(h, w, gains)


def _proj_split(h, w, col0, m, width, name, out_dtype=BF16, tm=2048, tn=1024):
    n, k = h.shape
    cpt = tn // width
    j0 = col0 // tn
    return pl.pallas_call(
        functools.partial(_proj_split_kernel, width=width),
        grid=(m // tn, n // tm),
        in_specs=[pl.BlockSpec((tm, k), lambda j, i: (i, 0)),
                  pl.BlockSpec((k, tn), lambda j, i: (0, j0 + j))],
        out_specs=pl.BlockSpec((cpt, tm, width), lambda j, i: (j, i, 0)),
        out_shape=jax.ShapeDtypeStruct((m // width, n, width), out_dtype),
        compiler_params=_params("parallel", "arbitrary"),
        name=name,
    )(h, w)


def _t5_bucket(rel):
    nb = NUM_BUCKETS // 2
    max_exact = nb // 2
    ret = (rel > 0).astype(jnp.int32) * nb
    n = jnp.abs(rel)
    nf = jnp.maximum(n, 1).astype(jnp.float32)
    large = max_exact + (jnp.log(nf / max_exact) / math.log(MAX_DISTANCE / max_exact)
                         * (nb - max_exact)).astype(jnp.int32)
    large = jnp.minimum(large, nb - 1)
    return ret + jnp.where(n < max_exact, n, large)


def _bias_rows(rel_bias):
    period = K_WIN + Q_SUB
    m = jnp.arange(period)
    delta = jnp.where(m < K_WIN, m, m - period)
    rows = []
    for d in DILATIONS:
        for off in (0, -HALF_WINDOW, -2 * HALF_WINDOW):
            rel = delta + off
            valid = jnp.abs(rel) <= HALF_WINDOW
            b = rel_bias[_t5_bucket(rel * d)].astype(F32)
            rows.append(jnp.where(valid[:, None], b, NEG_INF))
    return jnp.stack(rows, axis=0).transpose(2, 0, 1)


ATTN_UNROLL_MERGE = 32
ATTN_UNROLL = 32


def _attn_kernel(q_ref, k_ref, v_ref, brow_ref, o_ref, bias_s, tmp_s, xq4, xt4,
                 q16, k1, k4, k16, v1, v4, v16, acc_s, max_s, den_s, *, seq):
    period = K_WIN + Q_SUB
    n4, n16 = seq // 4, seq // 16

    @pl.when(pl.program_id(1) == 0)
    def _():
        for idx in range(9):
            row = jnp.broadcast_to(brow_ref[idx:idx + 1, :], (Q_SUB, period))
            tile = pltpu.roll(row, 0, 1, stride=1, stride_axis=0)
            d = DILATIONS[idx // 3]
            if d == 16:
                bias_s[idx] = tile[:, :K_WIN]
                continue
            for half in range(K_WIN // LANES):
                tmp_s[half] = tile[:, half * LANES:(half + 1) * LANES]
            groups = 16 // d
            for half in range(K_WIN // LANES):
                for g in range(groups):
                    n = Q_SUB // groups
                    bias_s[idx, g * n:(g + 1) * n, half * LANES:(half + 1) * LANES] = (
                        tmp_s[half, pl.ds(g, n, stride=groups), :])

    cp = 256

    def split4(src, dst):
        for r4 in range(4):
            def body(c, carry, r4=r4):
                t0 = pl.multiple_of(c * cp, cp)
                dst[r4, pl.ds(t0, cp), :] = src[pl.ds(r4 + 4 * t0, cp, stride=4), :]
                return carry
            lax.fori_loop(0, n4 // cp, body, 0)

    def split16(src4, dst16, dst4=None):
        for r4 in range(4):
            for a in range(4):
                dst16[4 * a + r4] = src4[r4, pl.ds(a, n16, stride=4), :].astype(BF16)
            if dst4 is not None:
                def body(c, carry, r4=r4):
                    t0 = pl.multiple_of(c * cp, cp)
                    dst4[r4, pl.ds(t0, cp), :] = src4[r4, pl.ds(t0, cp), :].astype(BF16)
                    return carry
                lax.fori_loop(0, n4 // cp, body, 0)

    def cast(src, dst):
        def body(c, carry):
            t0 = pl.multiple_of(c * cp, cp)
            dst[pl.ds(t0, cp), :] = src[pl.ds(t0, cp), :].astype(BF16)
            return carry
        lax.fori_loop(0, seq // cp, body, 0)

    split4(q_ref, xq4)
    split16(xq4, q16)
    for src, d1, d4, d16 in ((k_ref, k1, k4, k16), (v_ref, v1, v4, v16)):
        cast(src, d1)
        split4(src, xt4)
        split16(xt4, d16, d4)

    ones = jnp.ones((K_WIN, LANES), BF16)

    def sub_tile(q, k_s, v_s, length, s0, pi):
        if isinstance(s0, int):
            w0 = min(max(s0 - HALF_WINDOW, 0), length - K_WIN)
            place = 0 if s0 == 0 else (2 if s0 == length - Q_SUB else 1)
        else:
            w0 = pl.multiple_of(jnp.clip(s0 - HALF_WINDOW, 0, length - K_WIN), HALF_WINDOW)
            place = jnp.where(s0 == 0, 0, jnp.where(s0 == length - Q_SUB, 2, 1))
        k = k_s[pl.ds(w0, K_WIN), :]
        v = v_s[pl.ds(w0, K_WIN), :]
        s = lax.dot_general(q, k, (((1,), (1,)), ((), ())), preferred_element_type=F32)
        s = s + bias_s[3 * pi + place]
        m = jnp.max(s, axis=-1, keepdims=True)
        p = jnp.exp(s - m).astype(BF16)
        pv = jnp.dot(p, jnp.concatenate([v, ones], axis=1), preferred_element_type=F32)
        return pv[:, :LANES], jnp.broadcast_to(m, (Q_SUB, LANES)), pv[:, LANES:]

    def store(pi, res, slab_of_group, l0, n):
        for g in range(Q_SUB // n):
            for val, dst in zip(res, (acc_s, max_s, den_s)):
                dst[pi, slab_of_group(g), pl.ds(l0, n), :] = val[g * n:(g + 1) * n]

    U = ATTN_UNROLL

    def body4(it, carry):
        res = []
        for u in range(U // 4):
            l0 = pl.multiple_of((it * (U // 4) + u) * 32, 32)
            for r4 in range(4):
                q = jnp.concatenate([q16[4 * a + r4, pl.ds(l0, 32), :] for a in range(4)], axis=0)
                res.append((sub_tile(q, k4.at[r4], v4.at[r4], n4, pl.multiple_of(4 * l0, Q_SUB), 1),
                            r4, l0))
        for r3, r4, l0 in res:
            store(0, r3, lambda g, r4=r4: 4 * g + r4, l0, 32)
        return carry
    lax.fori_loop(0, n4 // Q_SUB // (U // 4), body4, 0)

    per16 = U // (n16 // Q_SUB)

    def body16(it, carry):
        g0 = it * per16
        res = []
        for u in range(per16):
            for l0 in range(0, n16, Q_SUB):
                res.append((sub_tile(q16[g0 + u, pl.ds(l0, Q_SUB), :], k16.at[g0 + u], v16.at[g0 + u],
                                     n16, l0, 2), u, l0))
        for r3, u, l0 in res:
            store(1, r3, lambda g, u=u: g0 + u, l0, Q_SUB)
        return carry
    lax.fori_loop(0, 16 // per16, body16, 0)

    U1 = ATTN_UNROLL_MERGE

    def body1(it, carry):
        res = []
        for u in range(U1):
            l0 = pl.multiple_of((it * U1 + u) * 8, 8)
            q = jnp.concatenate([xq4[r % 4, pl.ds(4 * l0 + r // 4, 8, stride=4), :] for r in range(16)],
                                axis=0).astype(BF16)
            res.append((sub_tile(q, k1, v1, seq, pl.multiple_of(16 * l0, Q_SUB), 0), l0))
        for (aa, ma, da), l0 in res:
            rows = lambda ref, pi: jnp.concatenate([ref[pi, r, pl.ds(l0, 8), :] for r in range(16)], axis=0)
            mb, mc = rows(max_s, 0), rows(max_s, 1)
            mx = jnp.maximum(jnp.maximum(ma, mb), mc)
            ea, eb, ec = jnp.exp(ma - mx), jnp.exp(mb - mx), jnp.exp(mc - mx)
            num = ea * aa + eb * rows(acc_s, 0) + ec * rows(acc_s, 1)
            den = ea * da + eb * rows(den_s, 0) + ec * rows(den_s, 1)
            out = num / den
            for r in range(16):
                o_ref[pl.ds(r + 16 * l0, 8, stride=16), :] = out[r * 8:(r + 1) * 8]
        return carry
    lax.fori_loop(0, seq // Q_SUB // U1, body1, 0)


def _attention(qk, v, brow, batch, seq):
    H = ATTN_HEADS
    n4, n16 = seq // 4, seq // 16
    qk4 = qk.reshape(2 * H, batch, seq, HEAD_DIM)
    v4 = v.reshape(H, batch, seq, HEAD_DIM)
    in_specs = [pl.BlockSpec((None, None, seq, HEAD_DIM), lambda h, b: (h, b, 0, 0)),
                pl.BlockSpec((None, None, seq, HEAD_DIM), lambda h, b: (H + h, b, 0, 0)),
                pl.BlockSpec((None, None, seq, HEAD_DIM), lambda h, b: (h, b, 0, 0))]
    in_specs.append(pl.BlockSpec((None, 9, K_WIN + Q_SUB), lambda h, b: (h, 0, 0)))
    kv_slabs = [pltpu.VMEM((seq, LANES), BF16), pltpu.VMEM((4, n4, LANES), BF16),
                pltpu.VMEM((16, n16, LANES), BF16)]
    out = pl.pallas_call(
        functools.partial(_attn_kernel, seq=seq),
        grid=(H, batch),
        in_specs=in_specs,
        out_specs=pl.BlockSpec((None, None, seq, HEAD_DIM), lambda h, b: (h, b, 0, 0)),
        out_shape=jax.ShapeDtypeStruct((H, batch, seq, HEAD_DIM), F32),
        scratch_shapes=[pltpu.VMEM((9, Q_SUB, K_WIN), F32),
                        pltpu.VMEM((K_WIN // LANES, Q_SUB, LANES), F32),
                        pltpu.VMEM((4, n4, LANES), F32),
                        pltpu.VMEM((4, n4, LANES), F32),
                        pltpu.VMEM((16, n16, LANES), BF16)]
                       + kv_slabs + kv_slabs
                       + [pltpu.VMEM((2, 16, n16, LANES), F32)] * 3,
        compiler_params=_params("parallel", "arbitrary"),
        name="dilated_attention",
    )(qk4, qk4, v4, brow)
    return out.reshape(H, batch * seq, HEAD_DIM)


def _ssd_kernel(z_ref, x_ref, b_ref, c_ref, dtr_ref, cwx_ref, cwb_ref, cwc_ref,
                cbx_ref, cbb_ref, cbc_ref, dbr_ref, alr_ref, dsk_ref, ng_ref,
                o_ref, pad, cv, y_s, arg_s, diag_s, rows_s, cols_s, sbs, sf, sb, *, seq):
    T = SSD_CHUNK
    nc = seq // T
    hi = lax.Precision.HIGHEST
    halo = CONV_HALO
    half = T // 2

    pad[:, 0:halo, :] = jnp.zeros((4, halo, LANES), F32)
    pad[:, seq + halo:seq + 2 * halo, :] = jnp.zeros((4, halo, LANES), F32)

    def fill(i, carry):
        t0 = pl.multiple_of(i * T, T)
        xin = x_ref[pl.ds(t0, T), :].astype(F32)
        pad[0, pl.ds(t0 + halo, T), :] = xin[:, :LANES]
        pad[1, pl.ds(t0 + halo, T), :] = xin[:, LANES:]
        pad[2, pl.ds(t0 + halo, T), :] = b_ref[pl.ds(t0, T), :].astype(F32)
        pad[3, pl.ds(t0 + halo, T), :] = c_ref[pl.ds(t0, T), :].astype(F32)
        return carry
    lax.fori_loop(0, nc, fill, 0)

    cws = (cwx_ref[:, :LANES], cwx_ref[:, LANES:], cwb_ref[...], cwc_ref[...])
    cbs = (cbx_ref[:, :LANES], cbx_ref[:, LANES:], cbb_ref[...], cbc_ref[...])

    def conv_chunk(ci):
        t0 = pl.multiple_of(ci * T, T)
        for s in range(4):
            ev = [pad[s, pl.ds(t0 + halo + 2 * k, half, stride=2), :] for k in (-1, 0, 1)]
            od = [pad[s, pl.ds(t0 + halo + 1 + 2 * k, half, stride=2), :] for k in (-1, 0, 1)]
            w = [cws[s][j:j + 1, :] for j in range(CONV_WIDTH)]
            out_e = cbs[s] + w[0] * ev[0] + w[1] * od[0] + w[2] * ev[1] + w[3] * od[1] + w[4] * ev[2]
            out_o = cbs[s] + w[0] * od[0] + w[1] * ev[1] + w[2] * od[1] + w[3] * ev[2] + w[4] * od[2]
            cv[s, pl.ds(t0, half, stride=2), :] = out_e * _sigmoid(out_e)
            cv[s, pl.ds(t0 + 1, half, stride=2), :] = out_o * _sigmoid(out_o)

    def load_x(t0):
        return jnp.concatenate([cv[0, pl.ds(t0, T), :], cv[1, pl.ds(t0, T), :]], axis=1)

    nh = HEADS_PER_GROUP
    nd = 2 * nh
    row = lax.broadcasted_iota(jnp.int32, (T, T), 0)
    col = lax.broadcasted_iota(jnp.int32, (T, T), 1)
    triu = (row <= col).astype(F32)
    lower = col < row
    upper = col > row

    dt = _softplus(dtr_ref[...] + dbr_ref[...])
    a = dt * (-jnp.exp(alr_ref[...]))
    cum = jnp.dot(a.reshape(nc * nd, T), triu, precision=hi,
                  preferred_element_type=F32).reshape(nc, nd, T)
    last = cum[:, :, T - 1:T]
    exc = cum - a
    fwd = lax.broadcasted_iota(jnp.int32, (nc, nd, T), 1) < nh
    base = jnp.where(fwd, cum, exc)
    log2e = 1.0 / math.log(2.0)
    log_dt = jnp.log(dt)
    arg_s[...] = jnp.where(fwd, cum - log_dt, exc + log_dt) * log2e
    log_sum = jnp.log(dt[:, 0:nh, :] + dt[:, nh:, :]) * log2e
    diag_s[...] = jnp.concatenate([log_sum, log_sum], axis=1)
    rows_s[:, 0:nd, :] = base * log2e
    rows_s[:, nd:2 * nd, :] = jnp.exp(jnp.where(fwd, cum, last - exc))
    rows_s[:, 2 * nd:3 * nd, :] = dt * jnp.exp(jnp.where(fwd, last - cum, exc))
    rows_s[:, 3 * nd:, :] = jnp.zeros((nc, LANES - 3 * nd, T), F32)

    def expand(cols, first):
        n = cols.shape[0]
        low = lax.broadcasted_iota(jnp.int32, (n, LANES), 1) < SSM_HEAD_DIM
        halves = []
        for j in range(GROUP_W // LANES):
            c0 = jnp.broadcast_to(cols[:, first + 2 * j:first + 2 * j + 1], (n, LANES))
            c1 = jnp.broadcast_to(cols[:, first + 2 * j + 1:first + 2 * j + 2], (n, LANES))
            halves.append(jnp.where(low, c0, c1))
        return jnp.concatenate(halves, axis=1)

    tn_dims = (((0,), (0,)), ((), ()))

    cid = lax.broadcasted_iota(jnp.int32, (LANES, 4 * GROUP_W), 0)
    lid = lax.broadcasted_iota(jnp.int32, (LANES, 4 * GROUP_W), 1)
    blk = lid // GROUP_W
    want = nd + (blk % 2) * nh + (blk // 2) * nd + (lid % GROUP_W) // SSM_HEAD_DIM
    spread = jnp.where(cid == want, 1.0, 0.0).astype(BF16)
    spread_f = spread[:, :3 * GROUP_W]
    spread_b = spread[:, 3 * GROUP_W:]

    sb[...] = jnp.zeros_like(sb)

    def state_back(ci):
        t0 = pl.multiple_of(ci * T, T)
        ct = rows_s[ci].T
        cols_s[ci] = ct
        s_prev = sb[...]
        sbs[ci] = s_prev.astype(BF16)
        wx = jnp.dot(ct.astype(BF16), spread_b, preferred_element_type=F32)
        xs = (load_x(t0) * wx).astype(BF16)
        sb[...] = expand(ct[0:1, :], nd + nh) * s_prev + lax.dot_general(
            cv[2, pl.ds(t0, T), :].astype(BF16), xs, tn_dims, preferred_element_type=F32)

    U = SSD_UNROLL
    for u in range(U):
        conv_chunk(nc - 1 - u)

    def sweep_back(i, carry):
        k = nc - 1 - U * i
        for u in range(U):
            state_back(k + U - u)
        for u in range(U):
            conv_chunk(k - u)
        return carry
    lax.fori_loop(1, nc // U, sweep_back, 0)
    for u in range(U):
        state_back(U - 1 - u)

    sf[...] = jnp.zeros_like(sf)
    low_half = lax.broadcasted_iota(jnp.int32, (T, LANES), 1) < SSM_HEAD_DIM

    def finish(ci):
        t0 = pl.multiple_of(ci * T, T)
        zz = z_ref[pl.ds(t0, T), :].astype(F32)
        y = y_s[pl.ds(t0, T), :] * (zz * _sigmoid(zz))
        ms = jnp.mean(y * y, axis=-1, keepdims=True)
        o_ref[pl.ds(t0, T), :] = (y * lax.rsqrt(ms + EPS) * ng_ref[...]).astype(o_ref.dtype)

    def chunk_fwd(ci):
        t0 = pl.multiple_of(ci * T, T)
        ct = cols_s[ci]
        arg_r = arg_s[ci]
        dt_r = diag_s[ci]
        x = load_x(t0)
        bk = cv[2, pl.ds(t0, T), :].astype(BF16)
        ck = cv[3, pl.ds(t0, T), :].astype(BF16)
        cb = lax.dot_general(ck, bk, (((1,), (1,)), ((), ())), preferred_element_type=F32)
        spreadv = jnp.dot(ct.astype(BF16), spread_f, preferred_element_type=F32)
        lhs = []
        for h in range(nh):
            hb = nh + h
            arg = jnp.where(lower, ct[:, h:h + 1] - arg_r[h:h + 1, :],
                            jnp.where(upper, arg_r[hb:hb + 1, :] - ct[:, hb:hb + 1], dt_r[h:h + 1, :]))
            lhs.append((cb * jnp.exp2(arg)).astype(BF16))
        s_prev = sf[...]
        y4 = jnp.dot(jnp.concatenate(lhs, axis=0), x.astype(BF16), preferred_element_type=F32)
        off = jnp.dot(ck, jnp.concatenate([s_prev.astype(BF16), sbs[ci]], axis=1),
                      preferred_element_type=F32)
        y = x * dsk_ref[...] + jnp.concatenate(
            [jnp.where(low_half, y4[2 * j * T:(2 * j + 1) * T, j * LANES:(j + 1) * LANES],
                       y4[(2 * j + 1) * T:(2 * j + 2) * T, j * LANES:(j + 1) * LANES])
             for j in range(GROUP_W // LANES)], axis=1)
        y = y + spreadv[:, :GROUP_W] * off[:, :GROUP_W] + spreadv[:, GROUP_W:2 * GROUP_W] * off[:, GROUP_W:]
        xs = (x * spreadv[:, 2 * GROUP_W:]).astype(BF16)
        sf[...] = expand(ct[T - 1:T, :], nd) * s_prev + lax.dot_general(
            bk, xs, tn_dims, preferred_element_type=F32)
        y_s[pl.ds(t0, T), :] = y

    U = SSD_UNROLL_FWD
    for u in range(U):
        chunk_fwd(u)

    def sweep_fwd(i, carry):
        c = U * i
        for u in range(U):
            finish(c - U + u)
        for u in range(U):
            chunk_fwd(c + u)
        return carry
    lax.fori_loop(1, nc // U, sweep_fwd, 0)
    for u in range(U):
        finish(nc - U + u)


def _ssd(zx, bcm, dt_row, cwx, cwb, cwc, cbx, cbb, cbc, db_row, al_row, dskip, ng, batch, seq):
    G = SSM_GROUPS
    nc = seq // SSD_CHUNK
    zx4 = zx.reshape(2 * G, batch, seq, GROUP_W)
    bc4 = bcm.reshape(2 * G, batch, seq, SSM_STATE)
    nd = 2 * HEADS_PER_GROUP

    def per_group(shape):
        return pl.BlockSpec((None,) + shape, lambda b, g: (g,) + (0,) * len(shape))

    in_specs = [
        pl.BlockSpec((None, None, seq, GROUP_W), lambda b, g: (g, b, 0, 0)),
        pl.BlockSpec((None, None, seq, GROUP_W), lambda b, g: (G + g, b, 0, 0)),
        pl.BlockSpec((None, None, seq, SSM_STATE), lambda b, g: (g, b, 0, 0)),
        pl.BlockSpec((None, None, seq, SSM_STATE), lambda b, g: (G + g, b, 0, 0)),
        pl.BlockSpec((None, None, nc, nd, SSD_CHUNK), lambda b, g: (b, g, 0, 0, 0)),
        per_group((CONV_WIDTH, GROUP_W)), per_group((CONV_WIDTH, SSM_STATE)),
        per_group((CONV_WIDTH, SSM_STATE)),
        per_group((1, GROUP_W)), per_group((1, SSM_STATE)), per_group((1, SSM_STATE)),
        per_group((nd, 1)), per_group((nd, 1)),
        per_group((1, GROUP_W)), per_group((1, GROUP_W)),
    ]
    out = pl.pallas_call(
        functools.partial(_ssd_kernel, seq=seq),
        grid=(batch, G),
        in_specs=in_specs,
        out_specs=pl.BlockSpec((None, None, seq, GROUP_W), lambda b, g: (g, b, 0, 0)),
        out_shape=jax.ShapeDtypeStruct((G, batch, seq, GROUP_W), BF16),
        scratch_shapes=[pltpu.VMEM((4, seq + 2 * CONV_HALO, LANES), F32),
                        pltpu.VMEM((4, seq, LANES), F32),
                        pltpu.VMEM((seq, GROUP_W), F32),
                        pltpu.VMEM((nc, nd, SSD_CHUNK), F32),
                        pltpu.VMEM((nc, nd, SSD_CHUNK), F32),
                        pltpu.VMEM((nc, LANES, SSD_CHUNK), F32),
                        pltpu.VMEM((nc, SSD_CHUNK, LANES), F32),
                        pltpu.VMEM((nc, SSM_STATE, GROUP_W), BF16),
                        pltpu.VMEM((SSM_STATE, GROUP_W), F32),
                        pltpu.VMEM((SSM_STATE, GROUP_W), F32)],
        compiler_params=_params("parallel", "parallel"),
        name="ssd",
    )(zx4, zx4, bc4, bc4, dt_row, cwx, cwb, cwc, cbx, cbb, cbc, db_row, al_row, dskip, ng)
    return out.reshape(G, batch * seq, GROUP_W)


def _out_proj_kernel(attn_ref, ssd_ref, w_ref, x_ref, o_ref, lhs):
    for h in range(ATTN_HEADS):
        lhs[:, h * HEAD_DIM:(h + 1) * HEAD_DIM] = attn_ref[h].astype(BF16)
    for g in range(SSM_GROUPS):
        lhs[:, ATTN_W + g * GROUP_W:ATTN_W + (g + 1) * GROUP_W] = ssd_ref[g]
    o_ref[...] = x_ref[...] + jnp.dot(lhs[...], w_ref[...], preferred_element_type=F32)


def _out_proj(attn, ssd, w, x2d, tm=512):
    n, d = x2d.shape
    kk = w.shape[0]
    return pl.pallas_call(
        _out_proj_kernel,
        grid=(n // tm,),
        in_specs=[pl.BlockSpec((ATTN_HEADS, tm, HEAD_DIM), lambda i: (0, i, 0)),
                  pl.BlockSpec((SSM_GROUPS, tm, GROUP_W), lambda i: (0, i, 0)),
                  pl.BlockSpec((kk, d), lambda i: (0, 0), pipeline_mode=pl.Buffered(1)),
                  pl.BlockSpec((tm, d), lambda i: (i, 0))],
        out_specs=pl.BlockSpec((tm, d), lambda i: (i, 0)),
        out_shape=jax.ShapeDtypeStruct((n, d), F32),
        scratch_shapes=[pltpu.VMEM((tm, kk), BF16)],
        compiler_params=_params("parallel"),
        name="out_proj",
    )(attn, ssd, w, x2d)


def _mlp_kernel(x_ref, g_ref, wu_ref, wd_ref, o_ref, hm):
    @pl.when(pl.program_id(1) == 0)
    def _():
        x = x_ref[...]
        ms = jnp.mean(x * x, axis=-1, keepdims=True)
        hm[...] = (x * lax.rsqrt(ms + EPS) * g_ref[...]).astype(hm.dtype)
        o_ref[...] = x
    u = jnp.maximum(jnp.dot(hm[...], wu_ref[...], preferred_element_type=F32), 0.0)
    o_ref[...] += jnp.dot((u * u).astype(BF16), wd_ref[...], preferred_element_type=F32)


def _mlp(x2d, g, wu, wd, tm=512, tf=1024):
    n, d = x2d.shape
    f = wu.shape[1]
    return pl.pallas_call(
        _mlp_kernel,
        grid=(n // tm, f // tf),
        in_specs=[pl.BlockSpec((tm, d), lambda i, j: (i, 0)),
                  pl.BlockSpec((1, d), lambda i, j: (0, 0)),
                  pl.BlockSpec((d, tf), lambda i, j: (0, j)),
                  pl.BlockSpec((tf, d), lambda i, j: (j, 0))],
        out_specs=pl.BlockSpec((tm, d), lambda i, j: (i, 0)),
        out_shape=jax.ShapeDtypeStruct((n, d), F32),
        scratch_shapes=[pltpu.VMEM((tm, d), BF16)],
        compiler_params=_params("parallel", "arbitrary"),
        name="mlp",
    )(x2d, g.reshape(1, d), wu, wd)


def kernel(x, norm_mix_g, w_in, q_norm_g, k_norm_g, rel_bias, conv_w, conv_b, dt_bias, a_log,
           d_skip, ssd_norm_g, w_out, norm_mlp_g, w_up, w_down):
    batch, seq, _ = x.shape
    n = batch * seq
    G, nh = SSM_GROUPS, HEADS_PER_GROUP
    nc = seq // SSD_CHUNK
    o_z = 3 * ATTN_W
    o_bc = o_z + 2 * SSM_W
    o_dt = o_bc + 2 * G * SSM_STATE
    x2d = x.reshape(n, D_MODEL)
    brow = _bias_rows(rel_bias)

    for layer in range(w_in.shape[0]):
        wi = w_in[layer].astype(BF16)
        h, dt_raw = _rmsnorm_dt(x2d, norm_mix_g[layer], wi[:, o_dt:])

        scale = 1.0 / math.sqrt(HEAD_DIM)
        gains = jnp.concatenate([jnp.tile(q_norm_g[layer].astype(F32) * scale, ATTN_HEADS),
                                 jnp.tile(k_norm_g[layer].astype(F32), ATTN_HEADS)]).reshape(1, 2 * ATTN_W)
        qk = _proj_qk(h, wi, gains)
        v = _proj_split(h, wi, 2 * ATTN_W, ATTN_W, HEAD_DIM, "proj_v", out_dtype=F32)
        zx = _proj_split(h, wi, o_z, o_bc - o_z, GROUP_W, "proj_zx")
        bcm = _proj_split(h, wi, o_bc, o_dt - o_bc, SSM_STATE, "proj_bc")

        attn = _attention(qk, v, brow, batch, seq)

        dt_row = (dt_raw.reshape(batch, nc, SSD_CHUNK, 2, G, nh)
                  .transpose(0, 4, 1, 3, 5, 2).reshape(batch, G, nc, 2 * nh, SSD_CHUNK))
        per_dir = lambda t: t.astype(F32).reshape(2, G, nh).transpose(1, 0, 2).reshape(G, 2 * nh)
        db, al = per_dir(dt_bias[layer]), per_dir(a_log[layer])
        cw, cbias = conv_w[layer].astype(F32), conv_b[layer].astype(F32)
        gn = G * SSM_STATE
        grp = lambda t, width: t.reshape(t.shape[0], G, width).transpose(1, 0, 2)
        ssd = _ssd(
            zx, bcm, dt_row,
            grp(cw[:, :SSM_W], GROUP_W), grp(cw[:, SSM_W:SSM_W + gn], SSM_STATE),
            grp(cw[:, SSM_W + gn:], SSM_STATE),
            grp(cbias[None, :SSM_W], GROUP_W), grp(cbias[None, SSM_W:SSM_W + gn], SSM_STATE),
            grp(cbias[None, SSM_W + gn:], SSM_STATE),
            db.reshape(G, 2 * nh, 1), al.reshape(G, 2 * nh, 1),
            jnp.repeat(d_skip[layer].astype(F32), SSM_HEAD_DIM).reshape(G, 1, GROUP_W),
            ssd_norm_g[layer].astype(F32).reshape(G, 1, GROUP_W),
            batch, seq)

        x2d = _out_proj(attn, ssd, w_out[layer].astype(BF16), x2d)
        x2d = _mlp(x2d, norm_mlp_g[layer], w_up[layer].astype(BF16), w_down[layer].astype(BF16))
    return x2d.reshape(batch, seq, D_MODEL)
```

```python
import functools
import math

import jax
import jax.numpy as jnp
from jax import lax
from jax.experimental import pallas as pl
from jax.experimental.pallas import tpu as pltpu

D_MODEL = 2048
ATTN_HEADS = 16
HEAD_DIM = 128
ATTN_W = ATTN_HEADS * HEAD_DIM
SSM_HEADS = 32
SSM_HEAD_DIM = 64
SSM_W = SSM_HEADS * SSM_HEAD_DIM
SSM_GROUPS = 8
HEADS_PER_GROUP = SSM_HEADS // SSM_GROUPS
GROUP_W = SSM_W // SSM_GROUPS
SSM_STATE = 128
CONV_WIDTH = 5
DILATIONS = (1, 4, 16)
HALF_WINDOW = 64
NUM_BUCKETS = 32
MAX_DISTANCE = 1024
NEG_INF = -1e30
EPS = 1e-6

LANES = 128
Q_SUB = 128
K_WIN = 256
SSD_CHUNK = 128
CONV_HALO = 8
VMEM_LIMIT = 56 * 1024 * 1024

F32 = jnp.float32
BF16 = jnp.bfloat16


def _params(*sem):
    return pltpu.CompilerParams(dimension_semantics=sem, vmem_limit_bytes=VMEM_LIMIT)


def _sigmoid(x):
    return 1.0 / (1.0 + jnp.exp2(x * (-1.0 / math.log(2.0))))


def _softplus(x):
    return jnp.maximum(x, 0.0) + jnp.log1p(jnp.exp(-jnp.abs(x)))


def _rmsnorm_dt_kernel(x_ref, g_ref, w_ref, h_ref, dt_ref):
    x = x_ref[...]
    ms = jnp.mean(x * x, axis=-1, keepdims=True)
    h = (x * lax.rsqrt(ms + EPS) * g_ref[...]).astype(h_ref.dtype)
    h_ref[...] = h
    dt_ref[...] = jnp.dot(h, w_ref[...], preferred_element_type=F32)


def _rmsnorm_dt(x2d, g, w_dt, tm=512):
    n, d = x2d.shape
    m = w_dt.shape[1]
    return pl.pallas_call(
        _rmsnorm_dt_kernel,
        grid=(n // tm,),
        in_specs=[pl.BlockSpec((tm, d), lambda i: (i, 0)),
                  pl.BlockSpec((1, d), lambda i: (0, 0)),
                  pl.BlockSpec((d, m), lambda i: (0, 0))],
        out_specs=[pl.BlockSpec((tm, d), lambda i: (i, 0)),
                   pl.BlockSpec((tm, m), lambda i: (i, 0))],
        out_shape=[jax.ShapeDtypeStruct((n, d), BF16), jax.ShapeDtypeStruct((n, m), F32)],
        compiler_params=_params("parallel"),
        name="rmsnorm_dt",
    )(x2d, g.reshape(1, d), w_dt)


QK_SUB = 256


def _proj_qk_kernel(a_ref, w_ref, g_ref, o_ref, *, heads_per_tile):
    a = a_ref[...]
    per = QK_SUB // HEAD_DIM
    for c in range(heads_per_tile // per):
        acc = jnp.dot(a, w_ref[:, c * QK_SUB:(c + 1) * QK_SUB], preferred_element_type=F32)
        for hh in range(per):
            h = c * per + hh
            s = acc[:, hh * HEAD_DIM:(hh + 1) * HEAD_DIM]
            ms = jnp.mean(s * s, axis=-1, keepdims=True)
            g = g_ref[:, h * HEAD_DIM:(h + 1) * HEAD_DIM]
            o_ref[h] = (s * lax.rsqrt(ms + EPS) * g).astype(o_ref.dtype)


def _proj_split_kernel(a_ref, w_ref, o_ref, *, width):
    a = a_ref[...]
    per = QK_SUB // width
    for s in range(w_ref.shape[1] // QK_SUB):
        acc = jnp.dot(a, w_ref[:, s * QK_SUB:(s + 1) * QK_SUB], preferred_element_type=F32)
        for c in range(per):
            o_ref[s * per + c] = acc[:, c * width:(c + 1) * width].astype(o_ref.dtype)


def _proj_qk(h, w, gains, tm=1024, tn=1024):
    n, k = h.shape
    m = 2 * ATTN_W
    hpt = tn // HEAD_DIM
    return pl.pallas_call(
        functools.partial(_proj_qk_kernel, heads_per_tile=hpt),
        grid=(m // tn, n // tm),
        in_specs=[pl.BlockSpec((tm, k), lambda j, i: (i, 0)),
                  pl.BlockSpec((k, tn), lambda j, i: (0, j)),
                  pl.BlockSpec((1, tn), lambda j, i: (0, j))],
        out_specs=pl.BlockSpec((hpt, tm, HEAD_DIM), lambda j, i: (j, i, 0)),
        out_shape=jax.ShapeDtypeStruct((m // HEAD_DIM, n, HEAD_DIM), F32),
        compiler_params=_params("parallel", "arbitrary"),
        name="proj_qk",
    )(h, w, gains)


def _proj_split(h, w, col0, m, width, name, out_dtype=BF16, tm=2048, tn=1024):
    n, k = h.shape
    cpt = tn // width
    j0 = col0 // tn
    return pl.pallas_call(
        functools.partial(_proj_split_kernel, width=width),
        grid=(m // tn, n // tm),
        in_specs=[pl.BlockSpec((tm, k), lambda j, i: (i, 0)),
                  pl.BlockSpec((k, tn), lambda j, i: (0, j0 + j))],
        out_specs=pl.BlockSpec((cpt, tm, width), lambda j, i: (j, i, 0)),
        out_shape=jax.ShapeDtypeStruct((m // width, n, width), out_dtype),
        compiler_params=_params("parallel", "arbitrary"),
        name=name,
    )(h, w)


def _t5_bucket(rel):
    nb = NUM_BUCKETS // 2
    max_exact = nb // 2
    ret = (rel > 0).astype(jnp.int32) * nb
    n = jnp.abs(rel)
    nf = jnp.maximum(n, 1).astype(jnp.float32)
    large = max_exact + (jnp.log(nf / max_exact) / math.log(MAX_DISTANCE / max_exact)
                         * (nb - max_exact)).astype(jnp.int32)
    large = jnp.minimum(large, nb - 1)
    return ret + jnp.where(n < max_exact, n, large)


def _bias_rows(rel_bias):
    period = K_WIN + Q_SUB
    m = jnp.arange(period)
    delta = jnp.where(m < K_WIN, m, m - period)
    rows = []
    for d in DILATIONS:
        for off in (0, -HALF_WINDOW, -2 * HALF_WINDOW):
            rel = delta + off
            valid = jnp.abs(rel) <= HALF_WINDOW
            b = rel_bias[_t5_bucket(rel * d)].astype(F32)
            rows.append(jnp.where(valid[:, None], b, NEG_INF))
    return jnp.stack(rows, axis=0).transpose(2, 0, 1)


def _attn_kernel(q_ref, k_ref, v_ref, brow_ref, o_ref, bias_s, tmp_s, xq4, xt4,
                 q16, k1, k4, k16, v1, v4, v16, acc_s, max_s, den_s, *, seq):
    period = K_WIN + Q_SUB
    n4, n16 = seq // 4, seq // 16

    @pl.when(pl.program_id(1) == 0)
    def _():
        for idx in range(9):
            row = jnp.broadcast_to(brow_ref[idx:idx + 1, :], (Q_SUB, period))
            tile = pltpu.roll(row, 0, 1, stride=1, stride_axis=0)
            d = DILATIONS[idx // 3]
            if d == 16:
                bias_s[idx] = tile[:, :K_WIN]
                continue
            for half in range(K_WIN // LANES):
                tmp_s[half] = tile[:, half * LANES:(half + 1) * LANES]
            groups = 16 // d
            for half in range(K_WIN // LANES):
                for g in range(groups):
                    n = Q_SUB // groups
                    bias_s[idx, g * n:(g + 1) * n, half * LANES:(half + 1) * LANES] = (
                        tmp_s[half, pl.ds(g, n, stride=groups), :])

    cp = 256

    def split4(src, dst):
        for r4 in range(4):
            def body(c, carry, r4=r4):
                t0 = pl.multiple_of(c * cp, cp)
                dst[r4, pl.ds(t0, cp), :] = src[pl.ds(r4 + 4 * t0, cp, stride=4), :]
                return carry
            lax.fori_loop(0, n4 // cp, body, 0)

    def split16(src4, dst16, dst4=None):
        for r4 in range(4):
            for a in range(4):
                dst16[4 * a + r4] = src4[r4, pl.ds(a, n16, stride=4), :].astype(BF16)
            if dst4 is not None:
                def body(c, carry, r4=r4):
                    t0 = pl.multiple_of(c * cp, cp)
                    dst4[r4, pl.ds(t0, cp), :] = src4[r4, pl.ds(t0, cp), :].astype(BF16)
                    return carry
                lax.fori_loop(0, n4 // cp, body, 0)

    def cast(src, dst):
        def body(c, carry):
            t0 = pl.multiple_of(c * cp, cp)
            dst[pl.ds(t0, cp), :] = src[pl.ds(t0, cp), :].astype(BF16)
            return carry
        lax.fori_loop(0, seq // cp, body, 0)

    split4(q_ref, xq4)
    split16(xq4, q16)
    for src, d1, d4, d16 in ((k_ref, k1, k4, k16), (v_ref, v1, v4, v16)):
        cast(src, d1)
        split4(src, xt4)
        split16(xt4, d16, d4)

    ones = jnp.ones((K_WIN, LANES), BF16)

    def sub_tile(q, k_s, v_s, length, s0, pi):
        if isinstance(s0, int):
            w0 = min(max(s0 - HALF_WINDOW, 0), length - K_WIN)
            place = 0 if s0 == 0 else (2 if s0 == length - Q_SUB else 1)
        else:
            w0 = pl.multiple_of(jnp.clip(s0 - HALF_WINDOW, 0, length - K_WIN), HALF_WINDOW)
            place = jnp.where(s0 == 0, 0, jnp.where(s0 == length - Q_SUB, 2, 1))
        k = k_s[pl.ds(w0, K_WIN), :]
        v = v_s[pl.ds(w0, K_WIN), :]
        s = lax.dot_general(q, k, (((1,), (1,)), ((), ())), preferred_element_type=F32)
        s = s + bias_s[3 * pi + place]
        m = jnp.max(s, axis=-1, keepdims=True)
        p = jnp.exp(s - m).astype(BF16)
        pv = jnp.dot(p, jnp.concatenate([v, ones], axis=1), preferred_element_type=F32)
        return pv[:, :LANES], jnp.broadcast_to(m, (Q_SUB, LANES)), pv[:, LANES:]

    stats = (acc_s, max_s, den_s)


    res = []
    for l0 in range(0, n16, Q_SUB // 4):
        for r4 in range(4):
            q = jnp.concatenate([q16[4 * a + r4, pl.ds(l0, Q_SUB // 4), :] for a in range(4)], axis=0)
            res.append((sub_tile(q, k4.at[r4], v4.at[r4], n4, 4 * l0, 1), r4, l0))
    for r3, r4, l0 in res:
        n = Q_SUB // 4
        for a in range(4):
            for val, dst in zip(r3, stats):
                dst[0, 4 * a + r4, pl.ds(l0, n), :] = val[a * n:(a + 1) * n]

    res = []
    for r in range(16):
        for l0 in range(0, n16, Q_SUB):
            res.append((sub_tile(q16[r, pl.ds(l0, Q_SUB), :], k16.at[r], v16.at[r], n16, l0, 2), r, l0))
    for r3, r, l0 in res:
        for val, dst in zip(r3, stats):
            dst[1, r, pl.ds(l0, Q_SUB), :] = val

    res = []
    for l0 in range(0, n16, Q_SUB // 16):
        q = jnp.concatenate([xq4[r % 4, pl.ds(4 * l0 + r // 4, Q_SUB // 16, stride=4), :] for r in range(16)],
                            axis=0).astype(BF16)
        res.append((sub_tile(q, k1, v1, seq, 16 * l0, 0), l0))
    for (aa, ma, da), l0 in res:
        n = Q_SUB // 16
        rows = lambda ref, pi: jnp.concatenate([ref[pi, r, pl.ds(l0, n), :] for r in range(16)], axis=0)
        mb, mc = rows(max_s, 0), rows(max_s, 1)
        mx = jnp.maximum(jnp.maximum(ma, mb), mc)
        ea, eb, ec = jnp.exp(ma - mx), jnp.exp(mb - mx), jnp.exp(mc - mx)
        num = ea * aa + eb * rows(acc_s, 0) + ec * rows(acc_s, 1)
        den = ea * da + eb * rows(den_s, 0) + ec * rows(den_s, 1)
        out = num / den
        for r in range(16):
            o_ref[pl.ds(r + 16 * l0, n, stride=16), :] = out[r * n:(r + 1) * n]


def _attention(qk, v, brow, batch, seq):
    H = ATTN_HEADS
    n4, n16 = seq // 4, seq // 16
    qk4 = qk.reshape(2 * H, batch, seq, HEAD_DIM)
    v4 = v.reshape(H, batch, seq, HEAD_DIM)
    in_specs = [pl.BlockSpec((None, None, seq, HEAD_DIM), lambda h, b: (h, b, 0, 0)),
                pl.BlockSpec((None, None, seq, HEAD_DIM), lambda h, b: (H + h, b, 0, 0)),
                pl.BlockSpec((None, None, seq, HEAD_DIM), lambda h, b: (h, b, 0, 0))]
    in_specs.append(pl.BlockSpec((None, 9, K_WIN + Q_SUB), lambda h, b: (h, 0, 0)))
    kv_slabs = [pltpu.VMEM((seq, LANES), BF16), pltpu.VMEM((4, n4, LANES), BF16),
                pltpu.VMEM((16, n16, LANES), BF16)]
    out = pl.pallas_call(
        functools.partial(_attn_kernel, seq=seq),
        grid=(H, batch),
        in_specs=in_specs,
        out_specs=pl.BlockSpec((None, None, seq, HEAD_DIM), lambda h, b: (h, b, 0, 0)),
        out_shape=jax.ShapeDtypeStruct((H, batch, seq, HEAD_DIM), F32),
        scratch_shapes=[pltpu.VMEM((9, Q_SUB, K_WIN), F32),
                        pltpu.VMEM((K_WIN // LANES, Q_SUB, LANES), F32),
                        pltpu.VMEM((4, n4, LANES), F32),
                        pltpu.VMEM((4, n4, LANES), F32),
                        pltpu.VMEM((16, n16, LANES), BF16)]
                       + kv_slabs + kv_slabs
                       + [pltpu.VMEM((2, 16, n16, LANES), F32)] * 3,
        compiler_params=_params("parallel", "arbitrary"),
        name="dilated_attention",
    )(qk4, qk4, v4, brow)
    return out.reshape(H, batch * seq, HEAD_DIM)


def _ssd_kernel(z_ref, x_ref, b_ref, c_ref, dtr_ref, cwx_ref, cwb_ref, cwc_ref,
                cbx_ref, cbb_ref, cbc_ref, dbr_ref, alr_ref, dsk_ref, ng_ref,
                o_ref, pad, cv, y_s, arg_s, diag_s, rows_s, cols_s, sbs, sf, sb, *, seq):
    T = SSD_CHUNK
    nc = seq // T
    hi = lax.Precision.HIGHEST
    halo = CONV_HALO
    half = T // 2

    pad[:, 0:halo, :] = jnp.zeros((4, halo, LANES), F32)
    pad[:, seq + halo:seq + 2 * halo, :] = jnp.zeros((4, halo, LANES), F32)

    def fill(i, carry):
        t0 = pl.multiple_of(i * T, T)
        xin = x_ref[pl.ds(t0, T), :].astype(F32)
        pad[0, pl.ds(t0 + halo, T), :] = xin[:, :LANES]
        pad[1, pl.ds(t0 + halo, T), :] = xin[:, LANES:]
        pad[2, pl.ds(t0 + halo, T), :] = b_ref[pl.ds(t0, T), :].astype(F32)
        pad[3, pl.ds(t0 + halo, T), :] = c_ref[pl.ds(t0, T), :].astype(F32)
        return carry
    lax.fori_loop(0, nc, fill, 0)

    cws = (cwx_ref[:, :LANES], cwx_ref[:, LANES:], cwb_ref[...], cwc_ref[...])
    cbs = (cbx_ref[:, :LANES], cbx_ref[:, LANES:], cbb_ref[...], cbc_ref[...])

    def conv_chunk(ci):
        t0 = ci * T
        for s in range(4):
            ev = [pad[s, pl.ds(t0 + halo + 2 * k, half, stride=2), :] for k in (-1, 0, 1)]
            od = [pad[s, pl.ds(t0 + halo + 1 + 2 * k, half, stride=2), :] for k in (-1, 0, 1)]
            w = [cws[s][j:j + 1, :] for j in range(CONV_WIDTH)]
            out_e = cbs[s] + w[0] * ev[0] + w[1] * od[0] + w[2] * ev[1] + w[3] * od[1] + w[4] * ev[2]
            out_o = cbs[s] + w[0] * od[0] + w[1] * ev[1] + w[2] * od[1] + w[3] * ev[2] + w[4] * od[2]
            cv[s, pl.ds(t0, half, stride=2), :] = out_e * _sigmoid(out_e)
            cv[s, pl.ds(t0 + 1, half, stride=2), :] = out_o * _sigmoid(out_o)

    def load_x(t0):
        return jnp.concatenate([cv[0, pl.ds(t0, T), :], cv[1, pl.ds(t0, T), :]], axis=1)

    nh = HEADS_PER_GROUP
    nd = 2 * nh
    row = lax.broadcasted_iota(jnp.int32, (T, T), 0)
    col = lax.broadcasted_iota(jnp.int32, (T, T), 1)
    triu = (row <= col).astype(F32)
    lower = col < row
    upper = col > row

    dt = _softplus(dtr_ref[...] + dbr_ref[...])
    a = dt * (-jnp.exp(alr_ref[...]))
    cum = jnp.dot(a.reshape(nc * nd, T), triu, precision=hi,
                  preferred_element_type=F32).reshape(nc, nd, T)
    last = cum[:, :, T - 1:T]
    exc = cum - a
    fwd = lax.broadcasted_iota(jnp.int32, (nc, nd, T), 1) < nh
    base = jnp.where(fwd, cum, exc)
    log2e = 1.0 / math.log(2.0)
    log_dt = jnp.log(dt)
    arg_s[...] = jnp.where(fwd, cum - log_dt, exc + log_dt) * log2e
    log_sum = jnp.log(dt[:, 0:nh, :] + dt[:, nh:, :]) * log2e
    diag_s[...] = jnp.concatenate([log_sum, log_sum], axis=1)
    rows_s[:, 0:nd, :] = base * log2e
    rows_s[:, nd:2 * nd, :] = jnp.exp(jnp.where(fwd, cum, last - exc))
    rows_s[:, 2 * nd:3 * nd, :] = dt * jnp.exp(jnp.where(fwd, last - cum, exc))
    rows_s[:, 3 * nd:, :] = jnp.zeros((nc, LANES - 3 * nd, T), F32)

    def expand(cols, first):
        n = cols.shape[0]
        low = lax.broadcasted_iota(jnp.int32, (n, LANES), 1) < SSM_HEAD_DIM
        halves = []
        for j in range(GROUP_W // LANES):
            c0 = jnp.broadcast_to(cols[:, first + 2 * j:first + 2 * j + 1], (n, LANES))
            c1 = jnp.broadcast_to(cols[:, first + 2 * j + 1:first + 2 * j + 2], (n, LANES))
            halves.append(jnp.where(low, c0, c1))
        return jnp.concatenate(halves, axis=1)

    tn_dims = (((0,), (0,)), ((), ()))

    cid = lax.broadcasted_iota(jnp.int32, (LANES, 4 * GROUP_W), 0)
    lid = lax.broadcasted_iota(jnp.int32, (LANES, 4 * GROUP_W), 1)
    blk = lid // GROUP_W
    want = nd + (blk % 2) * nh + (blk // 2) * nd + (lid % GROUP_W) // SSM_HEAD_DIM
    spread = jnp.where(cid == want, 1.0, 0.0).astype(BF16)
    spread_f = spread[:, :3 * GROUP_W]
    spread_b = spread[:, 3 * GROUP_W:]

    sb[...] = jnp.zeros_like(sb)

    def state_back(ci):
        t0 = ci * T
        ct = rows_s[ci].T
        cols_s[ci] = ct
        s_prev = sb[...]
        sbs[ci] = s_prev.astype(BF16)
        wx = jnp.dot(ct.astype(BF16), spread_b, preferred_element_type=F32)
        xs = (load_x(t0) * wx).astype(BF16)
        sb[...] = expand(ct[0:1, :], nd + nh) * s_prev + lax.dot_general(
            cv[2, pl.ds(t0, T), :].astype(BF16), xs, tn_dims, preferred_element_type=F32)

    conv_chunk(nc - 1)
    for k in range(nc - 2, -1, -1):
        conv_chunk(k)
        state_back(k + 1)
    state_back(0)

    sf[...] = jnp.zeros_like(sf)
    low_half = lax.broadcasted_iota(jnp.int32, (T, LANES), 1) < SSM_HEAD_DIM

    def finish(ci):
        t0 = ci * T
        zz = z_ref[pl.ds(t0, T), :].astype(F32)
        y = y_s[pl.ds(t0, T), :] * (zz * _sigmoid(zz))
        ms = jnp.mean(y * y, axis=-1, keepdims=True)
        o_ref[pl.ds(t0, T), :] = (y * lax.rsqrt(ms + EPS) * ng_ref[...]).astype(o_ref.dtype)

    def chunk_fwd(ci):
        t0 = ci * T
        ct = cols_s[ci]
        arg_r = arg_s[ci]
        dt_r = diag_s[ci]
        x = load_x(t0)
        bk = cv[2, pl.ds(t0, T), :].astype(BF16)
        ck = cv[3, pl.ds(t0, T), :].astype(BF16)
        cb = lax.dot_general(ck, bk, (((1,), (1,)), ((), ())), preferred_element_type=F32)
        spreadv = jnp.dot(ct.astype(BF16), spread_f, preferred_element_type=F32)
        lhs = []
        for h in range(nh):
            hb = nh + h
            arg = jnp.where(lower, ct[:, h:h + 1] - arg_r[h:h + 1, :],
                            jnp.where(upper, arg_r[hb:hb + 1, :] - ct[:, hb:hb + 1], dt_r[h:h + 1, :]))
            lhs.append((cb * jnp.exp2(arg)).astype(BF16))
        s_prev = sf[...]
        y4 = jnp.dot(jnp.concatenate(lhs, axis=0), x.astype(BF16), preferred_element_type=F32)
        off = jnp.dot(ck, jnp.concatenate([s_prev.astype(BF16), sbs[ci]], axis=1),
                      preferred_element_type=F32)
        y = x * dsk_ref[...] + jnp.concatenate(
            [jnp.where(low_half, y4[2 * j * T:(2 * j + 1) * T, j * LANES:(j + 1) * LANES],
                       y4[(2 * j + 1) * T:(2 * j + 2) * T, j * LANES:(j + 1) * LANES])
             for j in range(GROUP_W // LANES)], axis=1)
        y = y + spreadv[:, :GROUP_W] * off[:, :GROUP_W] + spreadv[:, GROUP_W:2 * GROUP_W] * off[:, GROUP_W:]
        xs = (x * spreadv[:, 2 * GROUP_W:]).astype(BF16)
        sf[...] = expand(ct[T - 1:T, :], nd) * s_prev + lax.dot_general(
            bk, xs, tn_dims, preferred_element_type=F32)
        y_s[pl.ds(t0, T), :] = y

    chunk_fwd(0)
    for c in range(1, nc):
        chunk_fwd(c)
        finish(c - 1)
    finish(nc - 1)


def _ssd(zx, bcm, dt_row, cwx, cwb, cwc, cbx, cbb, cbc, db_row, al_row, dskip, ng, batch, seq):
    G = SSM_GROUPS
    nc = seq // SSD_CHUNK
    zx4 = zx.reshape(2 * G, batch, seq, GROUP_W)
    bc4 = bcm.reshape(2 * G, batch, seq, SSM_STATE)
    nd = 2 * HEADS_PER_GROUP

    def per_group(shape):
        return pl.BlockSpec((None,) + shape, lambda b, g: (g,) + (0,) * len(shape))

    in_specs = [
        pl.BlockSpec((None, None, seq, GROUP_W), lambda b, g: (g, b, 0, 0)),
        pl.BlockSpec((None, None, seq, GROUP_W), lambda b, g: (G + g, b, 0, 0)),
        pl.BlockSpec((None, None, seq, SSM_STATE), lambda b, g: (g, b, 0, 0)),
        pl.BlockSpec((None, None, seq, SSM_STATE), lambda b, g: (G + g, b, 0, 0)),
        pl.BlockSpec((None, None, nc, nd, SSD_CHUNK), lambda b, g: (b, g, 0, 0, 0)),
        per_group((CONV_WIDTH, GROUP_W)), per_group((CONV_WIDTH, SSM_STATE)),
        per_group((CONV_WIDTH, SSM_STATE)),
        per_group((1, GROUP_W)), per_group((1, SSM_STATE)), per_group((1, SSM_STATE)),
        per_group((nd, 1)), per_group((nd, 1)),
        per_group((1, GROUP_W)), per_group((1, GROUP_W)),
    ]
    out = pl.pallas_call(
        functools.partial(_ssd_kernel, seq=seq),
        grid=(batch, G),
        in_specs=in_specs,
        out_specs=pl.BlockSpec((None, None, seq, GROUP_W), lambda b, g: (g, b, 0, 0)),
        out_shape=jax.ShapeDtypeStruct((G, batch, seq, GROUP_W), BF16),
        scratch_shapes=[pltpu.VMEM((4, seq + 2 * CONV_HALO, LANES), F32),
                        pltpu.VMEM((4, seq, LANES), F32),
                        pltpu.VMEM((seq, GROUP_W), F32),
                        pltpu.VMEM((nc, nd, SSD_CHUNK), F32),
                        pltpu.VMEM((nc, nd, SSD_CHUNK), F32),
                        pltpu.VMEM((nc, LANES, SSD_CHUNK), F32),
                        pltpu.VMEM((nc, SSD_CHUNK, LANES), F32),
                        pltpu.VMEM((nc, SSM_STATE, GROUP_W), BF16),
                        pltpu.VMEM((SSM_STATE, GROUP_W), F32),
                        pltpu.VMEM((SSM_STATE, GROUP_W), F32)],
        compiler_params=_params("parallel", "parallel"),
        name="ssd",
    )(zx4, zx4, bc4, bc4, dt_row, cwx, cwb, cwc, cbx, cbb, cbc, db_row, al_row, dskip, ng)
    return out.reshape(G, batch * seq, GROUP_W)


def _out_proj_kernel(attn_ref, ssd_ref, w_ref, x_ref, o_ref, lhs):
    for h in range(ATTN_HEADS):
        lhs[:, h * HEAD_DIM:(h + 1) * HEAD_DIM] = attn_ref[h].astype(BF16)
    for g in range(SSM_GROUPS):
        lhs[:, ATTN_W + g * GROUP_W:ATTN_W + (g + 1) * GROUP_W] = ssd_ref[g]
    o_ref[...] = x_ref[...] + jnp.dot(lhs[...], w_ref[...], preferred_element_type=F32)


def _out_proj(attn, ssd, w, x2d, tm=512):
    n, d = x2d.shape
    kk = w.shape[0]
    return pl.pallas_call(
        _out_proj_kernel,
        grid=(n // tm,),
        in_specs=[pl.BlockSpec((ATTN_HEADS, tm, HEAD_DIM), lambda i: (0, i, 0)),
                  pl.BlockSpec((SSM_GROUPS, tm, GROUP_W), lambda i: (0, i, 0)),
                  pl.BlockSpec((kk, d), lambda i: (0, 0), pipeline_mode=pl.Buffered(1)),
                  pl.BlockSpec((tm, d), lambda i: (i, 0))],
        out_specs=pl.BlockSpec((tm, d), lambda i: (i, 0)),
        out_shape=jax.ShapeDtypeStruct((n, d), F32),
        scratch_shapes=[pltpu.VMEM((tm, kk), BF16)],
        compiler_params=_params("parallel"),
        name="out_proj",
    )(attn, ssd, w, x2d)


def _mlp_kernel(x_ref, g_ref, wu_ref, wd_ref, o_ref, hm):
    @pl.when(pl.program_id(1) == 0)
    def _():
        x = x_ref[...]
        ms = jnp.mean(x * x, axis=-1, keepdims=True)
        hm[...] = (x * lax.rsqrt(ms + EPS) * g_ref[...]).astype(hm.dtype)
        o_ref[...] = x
    u = jnp.maximum(jnp.dot(hm[...], wu_ref[...], preferred_element_type=F32), 0.0)
    o_ref[...] += jnp.dot((u * u).astype(BF16), wd_ref[...], preferred_element_type=F32)


def _mlp(x2d, g, wu, wd, tm=512, tf=1024):
    n, d = x2d.shape
    f = wu.shape[1]
    return pl.pallas_call(
        _mlp_kernel,
        grid=(n // tm, f // tf),
        in_specs=[pl.BlockSpec((tm, d), lambda i, j: (i, 0)),
                  pl.BlockSpec((1, d), lambda i, j: (0, 0)),
                  pl.BlockSpec((d, tf), lambda i, j: (0, j)),
                  pl.BlockSpec((tf, d), lambda i, j: (j, 0))],
        out_specs=pl.BlockSpec((tm, d), lambda i, j: (i, 0)),
        out_shape=jax.ShapeDtypeStruct((n, d), F32),
        scratch_shapes=[pltpu.VMEM((tm, d), BF16)],
        compiler_params=_params("parallel", "arbitrary"),
        name="mlp",
    )(x2d, g.reshape(1, d), wu, wd)


def kernel(x, norm_mix_g, w_in, q_norm_g, k_norm_g, rel_bias, conv_w, conv_b, dt_bias, a_log,
           d_skip, ssd_norm_g, w_out, norm_mlp_g, w_up, w_down):
    batch, seq, _ = x.shape
    n = batch * seq
    G, nh = SSM_GROUPS, HEADS_PER_GROUP
    nc = seq // SSD_CHUNK
    o_z = 3 * ATTN_W
    o_bc = o_z + 2 * SSM_W
    o_dt = o_bc + 2 * G * SSM_STATE
    x2d = x.reshape(n, D_MODEL)
    brow = _bias_rows(rel_bias)

    for layer in range(w_in.shape[0]):
        wi = w_in[layer].astype(BF16)
        h, dt_raw = _rmsnorm_dt(x2d, norm_mix_g[layer], wi[:, o_dt:])

        scale = 1.0 / math.sqrt(HEAD_DIM)
        gains = jnp.concatenate([jnp.tile(q_norm_g[layer].astype(F32) * scale, ATTN_HEADS),
                                 jnp.tile(k_norm_g[layer].astype(F32), ATTN_HEADS)]).reshape(1, 2 * ATTN_W)
        qk = _proj_qk(h, wi, gains)
        v = _proj_split(h, wi, 2 * ATTN_W, ATTN_W, HEAD_DIM, "proj_v", out_dtype=F32)
        zx = _proj_split(h, wi, o_z, o_bc - o_z, GROUP_W, "proj_zx")
        bcm = _proj_split(h, wi, o_bc, o_dt - o_bc, SSM_STATE, "proj_bc")

        attn = _attention(qk, v, brow, batch, seq)

        dt_row = (dt_raw.reshape(batch, nc, SSD_CHUNK, 2, G, nh)
                  .transpose(0, 4, 1, 3, 5, 2).reshape(batch, G, nc, 2 * nh, SSD_CHUNK))
        per_dir = lambda t: t.astype(F32).reshape(2, G, nh).transpose(1, 0, 2).reshape(G, 2 * nh)
        db, al = per_dir(dt_bias[layer]), per_dir(a_log[layer])
        cw, cbias = conv_w[layer].astype(F32), conv_b[layer].astype(F32)
        gn = G * SSM_STATE
        grp = lambda t, width: t.reshape(t.shape[0], G, width).transpose(1, 0, 2)
        ssd = _ssd(
            zx, bcm, dt_row,
            grp(cw[:, :SSM_W], GROUP_W), grp(cw[:, SSM_W:SSM_W + gn], SSM_STATE),
            grp(cw[:, SSM_W + gn:], SSM_STATE),
            grp(cbias[None, :SSM_W], GROUP_W), grp(cbias[None, SSM_W:SSM_W + gn], SSM_STATE),
            grp(cbias[None, SSM_W + gn:], SSM_STATE),
            db.reshape(G, 2 * nh, 1), al.reshape(G, 2 * nh, 1),
            jnp.repeat(d_skip[layer].astype(F32), SSM_HEAD_DIM).reshape(G, 1, GROUP_W),
            ssd_norm_g[layer].astype(F32).reshape(G, 1, GROUP_W),
            batch, seq)

        x2d = _out_proj(attn, ssd, w_out[layer].astype(BF16), x2d)
        x2d = _mlp(x2d, norm_mlp_g[layer], w_up[layer].astype(BF16), w_down[layer].astype(BF16))
    return x2d.reshape(batch, seq, D_MODEL)
```

```python
import functools
import math

import jax
import jax.numpy as jnp
from jax import lax
from jax.experimental import pallas as pl
from jax.experimental.pallas import tpu as pltpu

D_MODEL = 2048
ATTN_HEADS = 16
HEAD_DIM = 128
ATTN_W = ATTN_HEADS * HEAD_DIM
SSM_HEADS = 32
SSM_HEAD_DIM = 64
SSM_W = SSM_HEADS * SSM_HEAD_DIM
SSM_GROUPS = 8
HEADS_PER_GROUP = SSM_HEADS // SSM_GROUPS
GROUP_W = SSM_W // SSM_GROUPS
SSM_STATE = 128
CONV_WIDTH = 5
DILATIONS = (1, 4, 16)
HALF_WINDOW = 64
NUM_BUCKETS = 32
MAX_DISTANCE = 1024
NEG_INF = -1e30
EPS = 1e-6

LANES = 128
Q_SUB = 128
K_WIN = 256
SSD_CHUNK = 128
CONV_HALO = 8
VMEM_LIMIT = 56 * 1024 * 1024

F32 = jnp.float32
BF16 = jnp.bfloat16


def _params(*sem):
    return pltpu.CompilerParams(dimension_semantics=sem, vmem_limit_bytes=VMEM_LIMIT)


def _sigmoid(x):
    return 1.0 / (1.0 + jnp.exp2(x * (-1.0 / math.log(2.0))))


def _softplus(x):
    return jnp.maximum(x, 0.0) + jnp.log1p(jnp.exp(-jnp.abs(x)))


def _rmsnorm_dt_kernel(x_ref, g_ref, w_ref, h_ref, dt_ref):
    x = x_ref[...]
    ms = jnp.mean(x * x, axis=-1, keepdims=True)
    h = (x * lax.rsqrt(ms + EPS) * g_ref[...]).astype(h_ref.dtype)
    h_ref[...] = h
    dt_ref[...] = jnp.dot(h, w_ref[...], preferred_element_type=F32)


def _rmsnorm_dt(x2d, g, w_dt, tm=512):
    n, d = x2d.shape
    m = w_dt.shape[1]
    return pl.pallas_call(
        _rmsnorm_dt_kernel,
        grid=(n // tm,),
        in_specs=[pl.BlockSpec((tm, d), lambda i: (i, 0)),
                  pl.BlockSpec((1, d), lambda i: (0, 0)),
                  pl.BlockSpec((d, m), lambda i: (0, 0))],
        out_specs=[pl.BlockSpec((tm, d), lambda i: (i, 0)),
                   pl.BlockSpec((tm, m), lambda i: (i, 0))],
        out_shape=[jax.ShapeDtypeStruct((n, d), BF16), jax.ShapeDtypeStruct((n, m), F32)],
        compiler_params=_params("parallel"),
        name="rmsnorm_dt",
    )(x2d, g.reshape(1, d), w_dt)


QK_SUB = 256


def _proj_qk_kernel(a_ref, w_ref, g_ref, o_ref, *, heads_per_tile):
    a = a_ref[...]
    per = QK_SUB // HEAD_DIM
    for c in range(heads_per_tile // per):
        acc = jnp.dot(a, w_ref[:, c * QK_SUB:(c + 1) * QK_SUB], preferred_element_type=F32)
        for hh in range(per):
            h = c * per + hh
            s = acc[:, hh * HEAD_DIM:(hh + 1) * HEAD_DIM]
            ms = jnp.mean(s * s, axis=-1, keepdims=True)
            g = g_ref[:, h * HEAD_DIM:(h + 1) * HEAD_DIM]
            o_ref[h] = (s * lax.rsqrt(ms + EPS) * g).astype(o_ref.dtype)


def _proj_split_kernel(a_ref, w_ref, o_ref, *, width):
    a = a_ref[...]
    per = QK_SUB // width
    for s in range(w_ref.shape[1] // QK_SUB):
        acc = jnp.dot(a, w_ref[:, s * QK_SUB:(s + 1) * QK_SUB], preferred_element_type=F32)
        for c in range(per):
            o_ref[s * per + c] = acc[:, c * width:(c + 1) * width].astype(o_ref.dtype)


def _proj_qk(h, w, gains, tm=1024, tn=1024):
    n, k = h.shape
    m = 2 * ATTN_W
    hpt = tn // HEAD_DIM
    return pl.pallas_call(
        functools.partial(_proj_qk_kernel, heads_per_tile=hpt),
        grid=(m // tn, n // tm),
        in_specs=[pl.BlockSpec((tm, k), lambda j, i: (i, 0)),
                  pl.BlockSpec((k, tn), lambda j, i: (0, j)),
                  pl.BlockSpec((1, tn), lambda j, i: (0, j))],
        out_specs=pl.BlockSpec((hpt, tm, HEAD_DIM), lambda j, i: (j, i, 0)),
        out_shape=jax.ShapeDtypeStruct((m // HEAD_DIM, n, HEAD_DIM), F32),
        compiler_params=_params("parallel", "arbitrary"),
        name="proj_qk",
    )(h, w, gains)


def _proj_split(h, w, col0, m, width, name, out_dtype=BF16, tm=2048, tn=1024):
    n, k = h.shape
    cpt = tn // width
    j0 = col0 // tn
    return pl.pallas_call(
        functools.partial(_proj_split_kernel, width=width),
        grid=(m // tn, n // tm),
        in_specs=[pl.BlockSpec((tm, k), lambda j, i: (i, 0)),
                  pl.BlockSpec((k, tn), lambda j, i: (0, j0 + j))],
        out_specs=pl.BlockSpec((cpt, tm, width), lambda j, i: (j, i, 0)),
        out_shape=jax.ShapeDtypeStruct((m // width, n, width), out_dtype),
        compiler_params=_params("parallel", "arbitrary"),
        name=name,
    )(h, w)


def _t5_bucket(rel):
    nb = NUM_BUCKETS // 2
    max_exact = nb // 2
    ret = (rel > 0).astype(jnp.int32) * nb
    n = jnp.abs(rel)
    nf = jnp.maximum(n, 1).astype(jnp.float32)
    large = max_exact + (jnp.log(nf / max_exact) / math.log(MAX_DISTANCE / max_exact)
                         * (nb - max_exact)).astype(jnp.int32)
    large = jnp.minimum(large, nb - 1)
    return ret + jnp.where(n < max_exact, n, large)


def _bias_rows(rel_bias):
    period = K_WIN + Q_SUB
    m = jnp.arange(period)
    delta = jnp.where(m < K_WIN, m, m - period)
    rows = []
    for d in DILATIONS:
        for off in (0, -HALF_WINDOW, -2 * HALF_WINDOW):
            rel = delta + off
            valid = jnp.abs(rel) <= HALF_WINDOW
            b = rel_bias[_t5_bucket(rel * d)].astype(F32)
            rows.append(jnp.where(valid[:, None], b, NEG_INF))
    return jnp.stack(rows, axis=0).transpose(2, 0, 1)


def _attn_kernel(q_ref, k_ref, v_ref, brow_ref, o_ref, bias_s, tmp_s, xq4, xk4, xv4,
                 q16, k1, k4, k16, v1, v4, v16, acc_s, max_s, den_s, *, seq):
    period = K_WIN + Q_SUB
    n4, n16 = seq // 4, seq // 16

    @pl.when(pl.program_id(1) == 0)
    def _():
        for idx in range(9):
            row = jnp.broadcast_to(brow_ref[idx:idx + 1, :], (Q_SUB, period))
            tile = pltpu.roll(row, 0, 1, stride=1, stride_axis=0)
            d = DILATIONS[idx // 3]
            if d == 16:
                bias_s[idx] = tile[:, :K_WIN]
                continue
            for half in range(K_WIN // LANES):
                tmp_s[half] = tile[:, half * LANES:(half + 1) * LANES]
            groups = 16 // d
            for half in range(K_WIN // LANES):
                for g in range(groups):
                    n = Q_SUB // groups
                    bias_s[idx, g * n:(g + 1) * n, half * LANES:(half + 1) * LANES] = (
                        tmp_s[half, pl.ds(g, n, stride=groups), :])

    cp = 256

    def split4(src, dst):
        for r4 in range(4):
            for t0 in range(0, n4, cp):
                dst[r4, pl.ds(t0, cp), :] = src[pl.ds(r4 + 4 * t0, cp, stride=4), :]

    def split16(src4, dst16, dst4=None):
        for r4 in range(4):
            for a in range(4):
                dst16[4 * a + r4] = src4[r4, pl.ds(a, n16, stride=4), :].astype(BF16)
            if dst4 is not None:
                for t0 in range(0, n4, cp):
                    dst4[r4, pl.ds(t0, cp), :] = src4[r4, pl.ds(t0, cp), :].astype(BF16)

    def cast(src, dst):
        for t0 in range(0, seq, cp):
            dst[pl.ds(t0, cp), :] = src[pl.ds(t0, cp), :].astype(BF16)

    split4(q_ref, xq4)
    split16(xq4, q16)
    for src, x4, d1, d4, d16 in ((k_ref, xk4, k1, k4, k16), (v_ref, xv4, v1, v4, v16)):
        cast(src, d1)
        split4(src, x4)
        split16(x4, d16, d4)

    ones = jnp.ones((K_WIN, LANES), BF16)

    def sub_tile(q, k_s, v_s, length, s0, pi):
        if isinstance(s0, int):
            w0 = min(max(s0 - HALF_WINDOW, 0), length - K_WIN)
            place = 0 if s0 == 0 else (2 if s0 == length - Q_SUB else 1)
        else:
            w0 = pl.multiple_of(jnp.clip(s0 - HALF_WINDOW, 0, length - K_WIN), HALF_WINDOW)
            place = jnp.where(s0 == 0, 0, jnp.where(s0 == length - Q_SUB, 2, 1))
        k = k_s[pl.ds(w0, K_WIN), :]
        v = v_s[pl.ds(w0, K_WIN), :]
        s = lax.dot_general(q, k, (((1,), (1,)), ((), ())), preferred_element_type=F32)
        s = s + bias_s[3 * pi + place]
        m = jnp.max(s, axis=-1, keepdims=True)
        p = jnp.exp(s - m).astype(BF16)
        pv = jnp.dot(p, jnp.concatenate([v, ones], axis=1), preferred_element_type=F32)
        return pv[:, :LANES], jnp.broadcast_to(m, (Q_SUB, LANES)), pv[:, LANES:]

    stats = (acc_s, max_s, den_s)


    res = []
    for l0 in range(0, n16, Q_SUB // 4):
        for r4 in range(4):
            q = jnp.concatenate([q16[4 * a + r4, pl.ds(l0, Q_SUB // 4), :] for a in range(4)], axis=0)
            res.append((sub_tile(q, k4.at[r4], v4.at[r4], n4, 4 * l0, 1), r4, l0))
    for r3, r4, l0 in res:
        n = Q_SUB // 4
        for a in range(4):
            for val, dst in zip(r3, stats):
                dst[0, 4 * a + r4, pl.ds(l0, n), :] = val[a * n:(a + 1) * n]

    res = []
    for r in range(16):
        for l0 in range(0, n16, Q_SUB):
            res.append((sub_tile(q16[r, pl.ds(l0, Q_SUB), :], k16.at[r], v16.at[r], n16, l0, 2), r, l0))
    for r3, r, l0 in res:
        for val, dst in zip(r3, stats):
            dst[1, r, pl.ds(l0, Q_SUB), :] = val

    res = []
    for l0 in range(0, n16, Q_SUB // 16):
        q = jnp.concatenate([xq4[r % 4, pl.ds(4 * l0 + r // 4, Q_SUB // 16, stride=4), :] for r in range(16)],
                            axis=0).astype(BF16)
        res.append((sub_tile(q, k1, v1, seq, 16 * l0, 0), l0))
    for (aa, ma, da), l0 in res:
        n = Q_SUB // 16
        rows = lambda ref, pi: jnp.concatenate([ref[pi, r, pl.ds(l0, n), :] for r in range(16)], axis=0)
        mb, mc = rows(max_s, 0), rows(max_s, 1)
        mx = jnp.maximum(jnp.maximum(ma, mb), mc)
        ea, eb, ec = jnp.exp(ma - mx), jnp.exp(mb - mx), jnp.exp(mc - mx)
        num = ea * aa + eb * rows(acc_s, 0) + ec * rows(acc_s, 1)
        den = ea * da + eb * rows(den_s, 0) + ec * rows(den_s, 1)
        out = num / den
        for r in range(16):
            o_ref[pl.ds(r + 16 * l0, n, stride=16), :] = out[r * n:(r + 1) * n]


def _attention(qk, v, brow, batch, seq):
    H = ATTN_HEADS
    n4, n16 = seq // 4, seq // 16
    qk4 = qk.reshape(2 * H, batch, seq, HEAD_DIM)
    v4 = v.reshape(H, batch, seq, HEAD_DIM)
    in_specs = [pl.BlockSpec((None, None, seq, HEAD_DIM), lambda h, b: (h, b, 0, 0)),
                pl.BlockSpec((None, None, seq, HEAD_DIM), lambda h, b: (H + h, b, 0, 0)),
                pl.BlockSpec((None, None, seq, HEAD_DIM), lambda h, b: (h, b, 0, 0))]
    in_specs.append(pl.BlockSpec((None, 9, K_WIN + Q_SUB), lambda h, b: (h, 0, 0)))
    kv_slabs = [pltpu.VMEM((seq, LANES), BF16), pltpu.VMEM((4, n4, LANES), BF16),
                pltpu.VMEM((16, n16, LANES), BF16)]
    out = pl.pallas_call(
        functools.partial(_attn_kernel, seq=seq),
        grid=(H, batch),
        in_specs=in_specs,
        out_specs=pl.BlockSpec((None, None, seq, HEAD_DIM), lambda h, b: (h, b, 0, 0)),
        out_shape=jax.ShapeDtypeStruct((H, batch, seq, HEAD_DIM), F32),
        scratch_shapes=[pltpu.VMEM((9, Q_SUB, K_WIN), F32),
                        pltpu.VMEM((K_WIN // LANES, Q_SUB, LANES), F32),
                        pltpu.VMEM((4, n4, LANES), F32),
                        pltpu.VMEM((4, n4, LANES), F32),
                        pltpu.VMEM((4, n4, LANES), F32),
                        pltpu.VMEM((16, n16, LANES), BF16)]
                       + kv_slabs + kv_slabs
                       + [pltpu.VMEM((2, 16, n16, LANES), F32)] * 3,
        compiler_params=_params("parallel", "arbitrary"),
        name="dilated_attention",
    )(qk4, qk4, v4, brow)
    return out.reshape(H, batch * seq, HEAD_DIM)


def _ssd_kernel(z_ref, x_ref, b_ref, c_ref, dtr_ref, cwx_ref, cwb_ref, cwc_ref,
                cbx_ref, cbb_ref, cbc_ref, dbr_ref, alr_ref, dsk_ref, ng_ref,
                o_ref, pad, cv, y_s, arg_s, diag_s, rows_s, cols_s, sbs, sf, sb, *, seq):
    T = SSD_CHUNK
    nc = seq // T
    hi = lax.Precision.HIGHEST
    halo = CONV_HALO
    half = T // 2

    pad[:, 0:halo, :] = jnp.zeros((4, halo, LANES), F32)
    pad[:, seq + halo:seq + 2 * halo, :] = jnp.zeros((4, halo, LANES), F32)

    for t0 in range(0, seq, T):
        xin = x_ref[pl.ds(t0, T), :].astype(F32)
        pad[0, pl.ds(t0 + halo, T), :] = xin[:, :LANES]
        pad[1, pl.ds(t0 + halo, T), :] = xin[:, LANES:]
        pad[2, pl.ds(t0 + halo, T), :] = b_ref[pl.ds(t0, T), :].astype(F32)
        pad[3, pl.ds(t0 + halo, T), :] = c_ref[pl.ds(t0, T), :].astype(F32)

    cws = (cwx_ref[:, :LANES], cwx_ref[:, LANES:], cwb_ref[...], cwc_ref[...])
    cbs = (cbx_ref[:, :LANES], cbx_ref[:, LANES:], cbb_ref[...], cbc_ref[...])

    def conv_chunk(ci):
        t0 = ci * T
        for s in range(4):
            ev = [pad[s, pl.ds(t0 + halo + 2 * k, half, stride=2), :] for k in (-1, 0, 1)]
            od = [pad[s, pl.ds(t0 + halo + 1 + 2 * k, half, stride=2), :] for k in (-1, 0, 1)]
            w = [cws[s][j:j + 1, :] for j in range(CONV_WIDTH)]
            out_e = cbs[s] + w[0] * ev[0] + w[1] * od[0] + w[2] * ev[1] + w[3] * od[1] + w[4] * ev[2]
            out_o = cbs[s] + w[0] * od[0] + w[1] * ev[1] + w[2] * od[1] + w[3] * ev[2] + w[4] * od[2]
            cv[s, pl.ds(t0, half, stride=2), :] = out_e * _sigmoid(out_e)
            cv[s, pl.ds(t0 + 1, half, stride=2), :] = out_o * _sigmoid(out_o)

    def load_x(t0):
        return jnp.concatenate([cv[0, pl.ds(t0, T), :], cv[1, pl.ds(t0, T), :]], axis=1)

    nh = HEADS_PER_GROUP
    nd = 2 * nh
    row = lax.broadcasted_iota(jnp.int32, (T, T), 0)
    col = lax.broadcasted_iota(jnp.int32, (T, T), 1)
    triu = (row <= col).astype(F32)
    lower = col < row
    upper = col > row

    dt = _softplus(dtr_ref[...] + dbr_ref[...])
    a = dt * (-jnp.exp(alr_ref[...]))
    cum = jnp.dot(a.reshape(nc * nd, T), triu, precision=hi,
                  preferred_element_type=F32).reshape(nc, nd, T)
    last = cum[:, :, T - 1:T]
    exc = cum - a
    fwd = lax.broadcasted_iota(jnp.int32, (nc, nd, T), 1) < nh
    base = jnp.where(fwd, cum, exc)
    log2e = 1.0 / math.log(2.0)
    log_dt = jnp.log(dt)
    arg_s[...] = jnp.where(fwd, cum - log_dt, exc + log_dt) * log2e
    log_sum = jnp.log(dt[:, 0:nh, :] + dt[:, nh:, :]) * log2e
    diag_s[...] = jnp.concatenate([log_sum, log_sum], axis=1)
    rows_s[:, 0:nd, :] = base * log2e
    rows_s[:, nd:2 * nd, :] = jnp.exp(jnp.where(fwd, cum, last - exc))
    rows_s[:, 2 * nd:3 * nd, :] = dt * jnp.exp(jnp.where(fwd, last - cum, exc))
    rows_s[:, 3 * nd:, :] = jnp.zeros((nc, LANES - 3 * nd, T), F32)

    def expand(cols, first):
        n = cols.shape[0]
        low = lax.broadcasted_iota(jnp.int32, (n, LANES), 1) < SSM_HEAD_DIM
        halves = []
        for j in range(GROUP_W // LANES):
            c0 = jnp.broadcast_to(cols[:, first + 2 * j:first + 2 * j + 1], (n, LANES))
            c1 = jnp.broadcast_to(cols[:, first + 2 * j + 1:first + 2 * j + 2], (n, LANES))
            halves.append(jnp.where(low, c0, c1))
        return jnp.concatenate(halves, axis=1)

    tn_dims = (((0,), (0,)), ((), ()))

    cid = lax.broadcasted_iota(jnp.int32, (LANES, 4 * GROUP_W), 0)
    lid = lax.broadcasted_iota(jnp.int32, (LANES, 4 * GROUP_W), 1)
    blk = lid // GROUP_W
    want = nd + (blk % 2) * nh + (blk // 2) * nd + (lid % GROUP_W) // SSM_HEAD_DIM
    spread = jnp.where(cid == want, 1.0, 0.0).astype(BF16)
    spread_f = spread[:, :3 * GROUP_W]
    spread_b = spread[:, 3 * GROUP_W:]

    sb[...] = jnp.zeros_like(sb)

    def state_back(ci):
        t0 = ci * T
        ct = rows_s[ci].T
        cols_s[ci] = ct
        s_prev = sb[...]
        sbs[ci] = s_prev.astype(BF16)
        wx = jnp.dot(ct.astype(BF16), spread_b, preferred_element_type=F32)
        xs = (load_x(t0) * wx).astype(BF16)
        sb[...] = expand(ct[0:1, :], nd + nh) * s_prev + lax.dot_general(
            cv[2, pl.ds(t0, T), :].astype(BF16), xs, tn_dims, preferred_element_type=F32)

    conv_chunk(nc - 1)
    for k in range(nc - 2, -1, -1):
        conv_chunk(k)
        state_back(k + 1)
    state_back(0)

    sf[...] = jnp.zeros_like(sf)
    low_half = lax.broadcasted_iota(jnp.int32, (T, LANES), 1) < SSM_HEAD_DIM

    def finish(ci):
        t0 = ci * T
        zz = z_ref[pl.ds(t0, T), :].astype(F32)
        y = y_s[pl.ds(t0, T), :] * (zz * _sigmoid(zz))
        ms = jnp.mean(y * y, axis=-1, keepdims=True)
        o_ref[pl.ds(t0, T), :] = (y * lax.rsqrt(ms + EPS) * ng_ref[...]).astype(o_ref.dtype)

    def chunk_fwd(ci):
        t0 = ci * T
        ct = cols_s[ci]
        arg_r = arg_s[ci]
        dt_r = diag_s[ci]
        x = load_x(t0)
        bk = cv[2, pl.ds(t0, T), :].astype(BF16)
        ck = cv[3, pl.ds(t0, T), :].astype(BF16)
        cb = lax.dot_general(ck, bk, (((1,), (1,)), ((), ())), preferred_element_type=F32)
        spreadv = jnp.dot(ct.astype(BF16), spread_f, preferred_element_type=F32)
        lhs = []
        for h in range(nh):
            hb = nh + h
            arg = jnp.where(lower, ct[:, h:h + 1] - arg_r[h:h + 1, :],
                            jnp.where(upper, arg_r[hb:hb + 1, :] - ct[:, hb:hb + 1], dt_r[h:h + 1, :]))
            lhs.append((cb * jnp.exp2(arg)).astype(BF16))
        s_prev = sf[...]
        y4 = jnp.dot(jnp.concatenate(lhs, axis=0), x.astype(BF16), preferred_element_type=F32)
        off = jnp.dot(ck, jnp.concatenate([s_prev.astype(BF16), sbs[ci]], axis=1),
                      preferred_element_type=F32)
        y = x * dsk_ref[...] + jnp.concatenate(
            [jnp.where(low_half, y4[2 * j * T:(2 * j + 1) * T, j * LANES:(j + 1) * LANES],
                       y4[(2 * j + 1) * T:(2 * j + 2) * T, j * LANES:(j + 1) * LANES])
             for j in range(GROUP_W // LANES)], axis=1)
        y = y + spreadv[:, :GROUP_W] * off[:, :GROUP_W] + spreadv[:, GROUP_W:2 * GROUP_W] * off[:, GROUP_W:]
        xs = (x * spreadv[:, 2 * GROUP_W:]).astype(BF16)
        sf[...] = expand(ct[T - 1:T, :], nd) * s_prev + lax.dot_general(
            bk, xs, tn_dims, preferred_element_type=F32)
        y_s[pl.ds(t0, T), :] = y

    chunk_fwd(0)
    for c in range(1, nc):
        chunk_fwd(c)
        finish(c - 1)
    finish(nc - 1)


def _ssd(zx, bcm, dt_row, cwx, cwb, cwc, cbx, cbb, cbc, db_row, al_row, dskip, ng, batch, seq):
    G = SSM_GROUPS
    nc = seq // SSD_CHUNK
    zx4 = zx.reshape(2 * G, batch, seq, GROUP_W)
    bc4 = bcm.reshape(2 * G, batch, seq, SSM_STATE)
    nd = 2 * HEADS_PER_GROUP

    def per_group(shape):
        return pl.BlockSpec((None,) + shape, lambda b, g: (g,) + (0,) * len(shape))

    in_specs = [
        pl.BlockSpec((None, None, seq, GROUP_W), lambda b, g: (g, b, 0, 0)),
        pl.BlockSpec((None, None, seq, GROUP_W), lambda b, g: (G + g, b, 0, 0)),
        pl.BlockSpec((None, None, seq, SSM_STATE), lambda b, g: (g, b, 0, 0)),
        pl.BlockSpec((None, None, seq, SSM_STATE), lambda b, g: (G + g, b, 0, 0)),
        pl.BlockSpec((None, None, nc, nd, SSD_CHUNK), lambda b, g: (b, g, 0, 0, 0)),
        per_group((CONV_WIDTH, GROUP_W)), per_group((CONV_WIDTH, SSM_STATE)),
        per_group((CONV_WIDTH, SSM_STATE)),
        per_group((1, GROUP_W)), per_group((1, SSM_STATE)), per_group((1, SSM_STATE)),
        per_group((nd, 1)), per_group((nd, 1)),
        per_group((1, GROUP_W)), per_group((1, GROUP_W)),
    ]
    out = pl.pallas_call(
        functools.partial(_ssd_kernel, seq=seq),
        grid=(batch, G),
        in_specs=in_specs,
        out_specs=pl.BlockSpec((None, None, seq, GROUP_W), lambda b, g: (g, b, 0, 0)),
        out_shape=jax.ShapeDtypeStruct((G, batch, seq, GROUP_W), BF16),
        scratch_shapes=[pltpu.VMEM((4, seq + 2 * CONV_HALO, LANES), F32),
                        pltpu.VMEM((4, seq, LANES), F32),
                        pltpu.VMEM((seq, GROUP_W), F32),
                        pltpu.VMEM((nc, nd, SSD_CHUNK), F32),
                        pltpu.VMEM((nc, nd, SSD_CHUNK), F32),
                        pltpu.VMEM((nc, LANES, SSD_CHUNK), F32),
                        pltpu.VMEM((nc, SSD_CHUNK, LANES), F32),
                        pltpu.VMEM((nc, SSM_STATE, GROUP_W), BF16),
                        pltpu.VMEM((SSM_STATE, GROUP_W), F32),
                        pltpu.VMEM((SSM_STATE, GROUP_W), F32)],
        compiler_params=_params("parallel", "parallel"),
        name="ssd",
    )(zx4, zx4, bc4, bc4, dt_row, cwx, cwb, cwc, cbx, cbb, cbc, db_row, al_row, dskip, ng)
    return out.reshape(G, batch * seq, GROUP_W)


def _out_proj_kernel(attn_ref, ssd_ref, w_ref, x_ref, o_ref, lhs):
    for h in range(ATTN_HEADS):
        lhs[:, h * HEAD_DIM:(h + 1) * HEAD_DIM] = attn_ref[h].astype(BF16)
    for g in range(SSM_GROUPS):
        lhs[:, ATTN_W + g * GROUP_W:ATTN_W + (g + 1) * GROUP_W] = ssd_ref[g]
    o_ref[...] = x_ref[...] + jnp.dot(lhs[...], w_ref[...], preferred_element_type=F32)


def _out_proj(attn, ssd, w, x2d, tm=512):
    n, d = x2d.shape
    kk = w.shape[0]
    return pl.pallas_call(
        _out_proj_kernel,
        grid=(n // tm,),
        in_specs=[pl.BlockSpec((ATTN_HEADS, tm, HEAD_DIM), lambda i: (0, i, 0)),
                  pl.BlockSpec((SSM_GROUPS, tm, GROUP_W), lambda i: (0, i, 0)),
                  pl.BlockSpec((kk, d), lambda i: (0, 0), pipeline_mode=pl.Buffered(1)),
                  pl.BlockSpec((tm, d), lambda i: (i, 0))],
        out_specs=pl.BlockSpec((tm, d), lambda i: (i, 0)),
        out_shape=jax.ShapeDtypeStruct((n, d), F32),
        scratch_shapes=[pltpu.VMEM((tm, kk), BF16)],
        compiler_params=_params("parallel"),
        name="out_proj",
    )(attn, ssd, w, x2d)


def _mlp_kernel(x_ref, g_ref, wu_ref, wd_ref, o_ref, hm):
    @pl.when(pl.program_id(1) == 0)
    def _():
        x = x_ref[...]
        ms = jnp.mean(x * x, axis=-1, keepdims=True)
        hm[...] = (x * lax.rsqrt(ms + EPS) * g_ref[...]).astype(hm.dtype)
        o_ref[...] = x
    u = jnp.maximum(jnp.dot(hm[...], wu_ref[...], preferred_element_type=F32), 0.0)
    o_ref[...] += jnp.dot((u * u).astype(BF16), wd_ref[...], preferred_element_type=F32)


def _mlp(x2d, g, wu, wd, tm=512, tf=1024):
    n, d = x2d.shape
    f = wu.shape[1]
    return pl.pallas_call(
        _mlp_kernel,
        grid=(n // tm, f // tf),
        in_specs=[pl.BlockSpec((tm, d), lambda i, j: (i, 0)),
                  pl.BlockSpec((1, d), lambda i, j: (0, 0)),
                  pl.BlockSpec((d, tf), lambda i, j: (0, j)),
                  pl.BlockSpec((tf, d), lambda i, j: (j, 0))],
        out_specs=pl.BlockSpec((tm, d), lambda i, j: (i, 0)),
        out_shape=jax.ShapeDtypeStruct((n, d), F32),
        scratch_shapes=[pltpu.VMEM((tm, d), BF16)],
        compiler_params=_params("parallel", "arbitrary"),
        name="mlp",
    )(x2d, g.reshape(1, d), wu, wd)


def kernel(x, norm_mix_g, w_in, q_norm_g, k_norm_g, rel_bias, conv_w, conv_b, dt_bias, a_log,
           d_skip, ssd_norm_g, w_out, norm_mlp_g, w_up, w_down):
    batch, seq, _ = x.shape
    n = batch * seq
    G, nh = SSM_GROUPS, HEADS_PER_GROUP
    nc = seq // SSD_CHUNK
    o_z = 3 * ATTN_W
    o_bc = o_z + 2 * SSM_W
    o_dt = o_bc + 2 * G * SSM_STATE
    x2d = x.reshape(n, D_MODEL)
    brow = _bias_rows(rel_bias)

    for layer in range(w_in.shape[0]):
        wi = w_in[layer].astype(BF16)
        h, dt_raw = _rmsnorm_dt(x2d, norm_mix_g[layer], wi[:, o_dt:])

        scale = 1.0 / math.sqrt(HEAD_DIM)
        gains = jnp.concatenate([jnp.tile(q_norm_g[layer].astype(F32) * scale, ATTN_HEADS),
                                 jnp.tile(k_norm_g[layer].astype(F32), ATTN_HEADS)]).reshape(1, 2 * ATTN_W)
        qk = _proj_qk(h, wi, gains)
        v = _proj_split(h, wi, 2 * ATTN_W, ATTN_W, HEAD_DIM, "proj_v", out_dtype=F32)
        zx = _proj_split(h, wi, o_z, o_bc - o_z, GROUP_W, "proj_zx")
        bcm = _proj_split(h, wi, o_bc, o_dt - o_bc, SSM_STATE, "proj_bc")

        attn = _attention(qk, v, brow, batch, seq)

        dt_row = (dt_raw.reshape(batch, nc, SSD_CHUNK, 2, G, nh)
                  .transpose(0, 4, 1, 3, 5, 2).reshape(batch, G, nc, 2 * nh, SSD_CHUNK))
        per_dir = lambda t: t.astype(F32).reshape(2, G, nh).transpose(1, 0, 2).reshape(G, 2 * nh)
        db, al = per_dir(dt_bias[layer]), per_dir(a_log[layer])
        cw, cbias = conv_w[layer].astype(F32), conv_b[layer].astype(F32)
        gn = G * SSM_STATE
        grp = lambda t, width: t.reshape(t.shape[0], G, width).transpose(1, 0, 2)
        ssd = _ssd(
            zx, bcm, dt_row,
            grp(cw[:, :SSM_W], GROUP_W), grp(cw[:, SSM_W:SSM_W + gn], SSM_STATE),
            grp(cw[:, SSM_W + gn:], SSM_STATE),
            grp(cbias[None, :SSM_W], GROUP_W), grp(cbias[None, SSM_W:SSM_W + gn], SSM_STATE),
            grp(cbias[None, SSM_W + gn:], SSM_STATE),
            db.reshape(G, 2 * nh, 1), al.reshape(G, 2 * nh, 1),
            jnp.repeat(d_skip[layer].astype(F32), SSM_HEAD_DIM).reshape(G, 1, GROUP_W),
            ssd_norm_g[layer].astype(F32).reshape(G, 1, GROUP_W),
            batch, seq)

        x2d = _out_proj(attn, ssd, w_out[layer].astype(BF16), x2d)
        x2d = _mlp(x2d, norm_mlp_g[layer], w_up[layer].astype(BF16), w_down[layer].astype(BF16))
    return x2d.reshape(batch, seq, D_MODEL)
```

```python
import functools
import math

import jax
import jax.numpy as jnp
from jax import lax
from jax.experimental import pallas as pl
from jax.experimental.pallas import tpu as pltpu

D_MODEL = 2048
ATTN_HEADS = 16
HEAD_DIM = 128
ATTN_W = ATTN_HEADS * HEAD_DIM
SSM_HEADS = 32
SSM_HEAD_DIM = 64
SSM_W = SSM_HEADS * SSM_HEAD_DIM
SSM_GROUPS = 8
HEADS_PER_GROUP = SSM_HEADS // SSM_GROUPS
GROUP_W = SSM_W // SSM_GROUPS
SSM_STATE = 128
CONV_WIDTH = 5
DILATIONS = (1, 4, 16)
HALF_WINDOW = 64
NUM_BUCKETS = 32
MAX_DISTANCE = 1024
NEG_INF = -1e30
EPS = 1e-6

LANES = 128
Q_SUB = 128
K_WIN = 256
SSD_CHUNK = 128
CONV_HALO = 8
VMEM_LIMIT = 56 * 1024 * 1024

F32 = jnp.float32
BF16 = jnp.bfloat16


def _params(*sem):
    return pltpu.CompilerParams(dimension_semantics=sem, vmem_limit_bytes=VMEM_LIMIT)


def _sigmoid(x):
    return 1.0 / (1.0 + jnp.exp2(x * (-1.0 / math.log(2.0))))


def _softplus(x):
    return jnp.maximum(x, 0.0) + jnp.log1p(jnp.exp(-jnp.abs(x)))


def _rmsnorm_dt_kernel(x_ref, g_ref, w_ref, h_ref, dt_ref):
    x = x_ref[...]
    ms = jnp.mean(x * x, axis=-1, keepdims=True)
    h = (x * lax.rsqrt(ms + EPS) * g_ref[...]).astype(h_ref.dtype)
    h_ref[...] = h
    dt_ref[...] = jnp.dot(h, w_ref[...], preferred_element_type=F32)


def _rmsnorm_dt(x2d, g, w_dt, tm=512):
    n, d = x2d.shape
    m = w_dt.shape[1]
    return pl.pallas_call(
        _rmsnorm_dt_kernel,
        grid=(n // tm,),
        in_specs=[pl.BlockSpec((tm, d), lambda i: (i, 0)),
                  pl.BlockSpec((1, d), lambda i: (0, 0)),
                  pl.BlockSpec((d, m), lambda i: (0, 0))],
        out_specs=[pl.BlockSpec((tm, d), lambda i: (i, 0)),
                   pl.BlockSpec((tm, m), lambda i: (i, 0))],
        out_shape=[jax.ShapeDtypeStruct((n, d), BF16), jax.ShapeDtypeStruct((n, m), F32)],
        compiler_params=_params("parallel"),
        name="rmsnorm_dt",
    )(x2d, g.reshape(1, d), w_dt)


QK_SUB = 256


def _proj_qk_kernel(a_ref, w_ref, g_ref, o_ref, *, heads_per_tile):
    a = a_ref[...]
    per = QK_SUB // HEAD_DIM
    for c in range(heads_per_tile // per):
        acc = jnp.dot(a, w_ref[:, c * QK_SUB:(c + 1) * QK_SUB], preferred_element_type=F32)
        for hh in range(per):
            h = c * per + hh
            s = acc[:, hh * HEAD_DIM:(hh + 1) * HEAD_DIM]
            ms = jnp.mean(s * s, axis=-1, keepdims=True)
            g = g_ref[:, h * HEAD_DIM:(h + 1) * HEAD_DIM]
            o_ref[h] = (s * lax.rsqrt(ms + EPS) * g).astype(o_ref.dtype)


def _proj_split_kernel(a_ref, w_ref, o_ref, *, width):
    a = a_ref[...]
    per = QK_SUB // width
    for s in range(w_ref.shape[1] // QK_SUB):
        acc = jnp.dot(a, w_ref[:, s * QK_SUB:(s + 1) * QK_SUB], preferred_element_type=F32)
        for c in range(per):
            o_ref[s * per + c] = acc[:, c * width:(c + 1) * width].astype(o_ref.dtype)


def _proj_qk(h, w, gains, tm=1024, tn=1024):
    n, k = h.shape
    m = 2 * ATTN_W
    hpt = tn // HEAD_DIM
    return pl.pallas_call(
        functools.partial(_proj_qk_kernel, heads_per_tile=hpt),
        grid=(m // tn, n // tm),
        in_specs=[pl.BlockSpec((tm, k), lambda j, i: (i, 0)),
                  pl.BlockSpec((k, tn), lambda j, i: (0, j)),
                  pl.BlockSpec((1, tn), lambda j, i: (0, j))],
        out_specs=pl.BlockSpec((hpt, tm, HEAD_DIM), lambda j, i: (j, i, 0)),
        out_shape=jax.ShapeDtypeStruct((m // HEAD_DIM, n, HEAD_DIM), F32),
        compiler_params=_params("parallel", "arbitrary"),
        name="proj_qk",
    )(h, w, gains)


def _proj_split(h, w, col0, m, width, name, out_dtype=BF16, tm=2048, tn=1024):
    n, k = h.shape
    cpt = tn // width
    j0 = col0 // tn
    return pl.pallas_call(
        functools.partial(_proj_split_kernel, width=width),
        grid=(m // tn, n // tm),
        in_specs=[pl.BlockSpec((tm, k), lambda j, i: (i, 0)),
                  pl.BlockSpec((k, tn), lambda j, i: (0, j0 + j))],
        out_specs=pl.BlockSpec((cpt, tm, width), lambda j, i: (j, i, 0)),
        out_shape=jax.ShapeDtypeStruct((m // width, n, width), out_dtype),
        compiler_params=_params("parallel", "arbitrary"),
        name=name,
    )(h, w)


CONV_ROWS = 512
CONV_MM_ROWS = 1024


def _proj_conv_kernel(a_ref, w_ref, cw_ref, cb_ref, o_ref, scr, *, seq):
    halo = CONV_HALO
    half = CONV_ROWS // 2
    n_half = QK_SUB // LANES
    n_slab = w_ref.shape[1] // QK_SUB
    n_blk = seq // CONV_MM_ROWS
    scr[:, :, 0:halo, :] = jnp.zeros((2, n_half, halo, LANES), F32)
    scr[:, :, seq + halo:seq + 2 * halo, :] = jnp.zeros((2, n_half, halo, LANES), F32)

    def matmul(s, r):
        r0 = r * CONV_MM_ROWS
        acc = jnp.dot(a_ref[r0:r0 + CONV_MM_ROWS, :], w_ref[:, s * QK_SUB:(s + 1) * QK_SUB],
                      preferred_element_type=F32)
        for hf in range(n_half):
            scr[s % 2, hf, halo + r0:halo + r0 + CONV_MM_ROWS, :] = acc[:, hf * LANES:(hf + 1) * LANES]

    def conv_rows(s, lo, hi):
        for hf in range(n_half):
            c0 = s * QK_SUB + hf * LANES
            w = [cw_ref[j:j + 1, c0:c0 + LANES] for j in range(CONV_WIDTH)]
            b = cb_ref[:, c0:c0 + LANES]
            src = scr.at[s % 2, hf]
            for r0 in range(lo, hi, CONV_ROWS):
                ev = [src[pl.ds(halo + r0 + 2 * k, half, stride=2), :] for k in (-1, 0, 1)]
                od = [src[pl.ds(halo + r0 + 1 + 2 * k, half, stride=2), :] for k in (-1, 0, 1)]
                out_e = b + w[0] * ev[0] + w[1] * od[0] + w[2] * ev[1] + w[3] * od[1] + w[4] * ev[2]
                out_o = b + w[0] * od[0] + w[1] * ev[1] + w[2] * od[1] + w[3] * ev[2] + w[4] * od[2]
                o_ref[s * n_half + hf, pl.ds(r0, half, stride=2), :] = out_e * _sigmoid(out_e)
                o_ref[s * n_half + hf, pl.ds(r0 + 1, half, stride=2), :] = out_o * _sigmoid(out_o)

    def conv_upto(s, r):
        lo = max(r * CONV_MM_ROWS - CONV_ROWS, 0)
        hi = seq if r == n_blk - 1 else (r + 1) * CONV_MM_ROWS - CONV_ROWS
        conv_rows(s, lo, hi)

    pending = None
    for s in range(n_slab):
        for r in range(n_blk):
            matmul(s, r)
            if pending is not None:
                conv_upto(*pending)
            pending = (s, r)
    conv_upto(*pending)


def _proj_conv(h, w, col0, cw, cb, batch, seq, tn=512):
    n, k = h.shape
    m = cw.shape[1]
    j0 = col0 // tn
    out = pl.pallas_call(
        functools.partial(_proj_conv_kernel, seq=seq),
        grid=(batch, m // tn),
        in_specs=[pl.BlockSpec((None, seq, k), lambda b, j: (b, 0, 0), pipeline_mode=pl.Buffered(1)),
                  pl.BlockSpec((k, tn), lambda b, j: (0, j0 + j)),
                  pl.BlockSpec((CONV_WIDTH, tn), lambda b, j: (0, j)),
                  pl.BlockSpec((1, tn), lambda b, j: (0, j))],
        out_specs=pl.BlockSpec((tn // LANES, None, seq, LANES), lambda b, j: (j, b, 0, 0)),
        out_shape=jax.ShapeDtypeStruct((m // LANES, batch, seq, LANES), F32),
        scratch_shapes=[pltpu.VMEM((2, QK_SUB // LANES, seq + 2 * CONV_HALO, LANES), F32)],
        compiler_params=_params("parallel", "arbitrary"),
        name="proj_conv",
    )(h.reshape(batch, seq, k), w, cw, cb)
    return out.reshape(m // LANES, n, LANES)


def _t5_bucket(rel):
    nb = NUM_BUCKETS // 2
    max_exact = nb // 2
    ret = (rel > 0).astype(jnp.int32) * nb
    n = jnp.abs(rel)
    nf = jnp.maximum(n, 1).astype(jnp.float32)
    large = max_exact + (jnp.log(nf / max_exact) / math.log(MAX_DISTANCE / max_exact)
                         * (nb - max_exact)).astype(jnp.int32)
    large = jnp.minimum(large, nb - 1)
    return ret + jnp.where(n < max_exact, n, large)


def _bias_rows(rel_bias):
    period = K_WIN + Q_SUB
    m = jnp.arange(period)
    delta = jnp.where(m < K_WIN, m, m - period)
    rows = []
    for d in DILATIONS:
        for off in (0, -HALF_WINDOW, -2 * HALF_WINDOW):
            rel = delta + off
            valid = jnp.abs(rel) <= HALF_WINDOW
            b = rel_bias[_t5_bucket(rel * d)].astype(F32)
            rows.append(jnp.where(valid[:, None], b, NEG_INF))
    return jnp.stack(rows, axis=0).transpose(2, 0, 1)


def _attn_kernel(q_ref, k_ref, v_ref, brow_ref, o_ref, bias_s, tmp_s, xq4, xk4, xv4,
                 q16, k1, k4, k16, v1, v4, v16, acc_s, max_s, den_s, *, seq):
    period = K_WIN + Q_SUB
    n4, n16 = seq // 4, seq // 16

    @pl.when(pl.program_id(1) == 0)
    def _():
        for idx in range(9):
            row = jnp.broadcast_to(brow_ref[idx:idx + 1, :], (Q_SUB, period))
            tile = pltpu.roll(row, 0, 1, stride=1, stride_axis=0)
            d = DILATIONS[idx // 3]
            if d == 16:
                bias_s[idx] = tile[:, :K_WIN]
                continue
            for half in range(K_WIN // LANES):
                tmp_s[half] = tile[:, half * LANES:(half + 1) * LANES]
            groups = 16 // d
            for half in range(K_WIN // LANES):
                for g in range(groups):
                    n = Q_SUB // groups
                    bias_s[idx, g * n:(g + 1) * n, half * LANES:(half + 1) * LANES] = (
                        tmp_s[half, pl.ds(g, n, stride=groups), :])

    cp = 256

    def split4(src, dst):
        for r4 in range(4):
            for t0 in range(0, n4, cp):
                dst[r4, pl.ds(t0, cp), :] = src[pl.ds(r4 + 4 * t0, cp, stride=4), :]

    def split16(src4, dst16, dst4=None):
        for r4 in range(4):
            for a in range(4):
                dst16[4 * a + r4] = src4[r4, pl.ds(a, n16, stride=4), :].astype(BF16)
            if dst4 is not None:
                for t0 in range(0, n4, cp):
                    dst4[r4, pl.ds(t0, cp), :] = src4[r4, pl.ds(t0, cp), :].astype(BF16)

    def cast(src, dst):
        for t0 in range(0, seq, cp):
            dst[pl.ds(t0, cp), :] = src[pl.ds(t0, cp), :].astype(BF16)

    split4(q_ref, xq4)
    split16(xq4, q16)
    for src, x4, d1, d4, d16 in ((k_ref, xk4, k1, k4, k16), (v_ref, xv4, v1, v4, v16)):
        cast(src, d1)
        split4(src, x4)
        split16(x4, d16, d4)

    ones = jnp.ones((K_WIN, LANES), BF16)

    def sub_tile(q, k_s, v_s, length, s0, pi):
        if isinstance(s0, int):
            w0 = min(max(s0 - HALF_WINDOW, 0), length - K_WIN)
            place = 0 if s0 == 0 else (2 if s0 == length - Q_SUB else 1)
        else:
            w0 = pl.multiple_of(jnp.clip(s0 - HALF_WINDOW, 0, length - K_WIN), HALF_WINDOW)
            place = jnp.where(s0 == 0, 0, jnp.where(s0 == length - Q_SUB, 2, 1))
        k = k_s[pl.ds(w0, K_WIN), :]
        v = v_s[pl.ds(w0, K_WIN), :]
        s = lax.dot_general(q, k, (((1,), (1,)), ((), ())), preferred_element_type=F32)
        s = s + bias_s[3 * pi + place]
        m = jnp.max(s, axis=-1, keepdims=True)
        p = jnp.exp(s - m).astype(BF16)
        pv = jnp.dot(p, jnp.concatenate([v, ones], axis=1), preferred_element_type=F32)
        return pv[:, :LANES], jnp.broadcast_to(m, (Q_SUB, LANES)), pv[:, LANES:]

    stats = (acc_s, max_s, den_s)


    res = []
    for l0 in range(0, n16, Q_SUB // 4):
        for r4 in range(4):
            q = jnp.concatenate([q16[4 * a + r4, pl.ds(l0, Q_SUB // 4), :] for a in range(4)], axis=0)
            res.append((sub_tile(q, k4.at[r4], v4.at[r4], n4, 4 * l0, 1), r4, l0))
    for r3, r4, l0 in res:
        n = Q_SUB // 4
        for a in range(4):
            for val, dst in zip(r3, stats):
                dst[0, 4 * a + r4, pl.ds(l0, n), :] = val[a * n:(a + 1) * n]

    res = []
    for r in range(16):
        for l0 in range(0, n16, Q_SUB):
            res.append((sub_tile(q16[r, pl.ds(l0, Q_SUB), :], k16.at[r], v16.at[r], n16, l0, 2), r, l0))
    for r3, r, l0 in res:
        for val, dst in zip(r3, stats):
            dst[1, r, pl.ds(l0, Q_SUB), :] = val

    res = []
    for l0 in range(0, n16, Q_SUB // 16):
        q = jnp.concatenate([xq4[r % 4, pl.ds(4 * l0 + r // 4, Q_SUB // 16, stride=4), :] for r in range(16)],
                            axis=0).astype(BF16)
        res.append((sub_tile(q, k1, v1, seq, 16 * l0, 0), l0))
    for (aa, ma, da), l0 in res:
        n = Q_SUB // 16
        rows = lambda ref, pi: jnp.concatenate([ref[pi, r, pl.ds(l0, n), :] for r in range(16)], axis=0)
        mb, mc = rows(max_s, 0), rows(max_s, 1)
        mx = jnp.maximum(jnp.maximum(ma, mb), mc)
        ea, eb, ec = jnp.exp(ma - mx), jnp.exp(mb - mx), jnp.exp(mc - mx)
        num = ea * aa + eb * rows(acc_s, 0) + ec * rows(acc_s, 1)
        den = ea * da + eb * rows(den_s, 0) + ec * rows(den_s, 1)
        out = num / den
        for r in range(16):
            o_ref[pl.ds(r + 16 * l0, n, stride=16), :] = out[r * n:(r + 1) * n]


def _attention(qk, v, brow, batch, seq):
    H = ATTN_HEADS
    n4, n16 = seq // 4, seq // 16
    qk4 = qk.reshape(2 * H, batch, seq, HEAD_DIM)
    v4 = v.reshape(H, batch, seq, HEAD_DIM)
    in_specs = [pl.BlockSpec((None, None, seq, HEAD_DIM), lambda h, b: (h, b, 0, 0)),
                pl.BlockSpec((None, None, seq, HEAD_DIM), lambda h, b: (H + h, b, 0, 0)),
                pl.BlockSpec((None, None, seq, HEAD_DIM), lambda h, b: (h, b, 0, 0))]
    in_specs.append(pl.BlockSpec((None, 9, K_WIN + Q_SUB), lambda h, b: (h, 0, 0)))
    kv_slabs = [pltpu.VMEM((seq, LANES), BF16), pltpu.VMEM((4, n4, LANES), BF16),
                pltpu.VMEM((16, n16, LANES), BF16)]
    out = pl.pallas_call(
        functools.partial(_attn_kernel, seq=seq),
        grid=(H, batch),
        in_specs=in_specs,
        out_specs=pl.BlockSpec((None, None, seq, HEAD_DIM), lambda h, b: (h, b, 0, 0)),
        out_shape=jax.ShapeDtypeStruct((H, batch, seq, HEAD_DIM), F32),
        scratch_shapes=[pltpu.VMEM((9, Q_SUB, K_WIN), F32),
                        pltpu.VMEM((K_WIN // LANES, Q_SUB, LANES), F32),
                        pltpu.VMEM((4, n4, LANES), F32),
                        pltpu.VMEM((4, n4, LANES), F32),
                        pltpu.VMEM((4, n4, LANES), F32),
                        pltpu.VMEM((16, n16, LANES), BF16)]
                       + kv_slabs + kv_slabs
                       + [pltpu.VMEM((2, 16, n16, LANES), F32)] * 3,
        compiler_params=_params("parallel", "arbitrary"),
        name="dilated_attention",
    )(qk4, qk4, v4, brow)
    return out.reshape(H, batch * seq, HEAD_DIM)


def _ssd_kernel(z_ref, x0_ref, x1_ref, b_ref, c_ref, dtr_ref, dbr_ref, alr_ref, dsk_ref, ng_ref,
                o_ref, y_s, arg_s, diag_s, rows_s, cols_s, sbs, sf, sb, *, seq):
    T = SSD_CHUNK
    nc = seq // T
    hi = lax.Precision.HIGHEST

    def load_x(t0):
        return jnp.concatenate([x0_ref[pl.ds(t0, T), :], x1_ref[pl.ds(t0, T), :]], axis=1)

    nh = HEADS_PER_GROUP
    nd = 2 * nh
    row = lax.broadcasted_iota(jnp.int32, (T, T), 0)
    col = lax.broadcasted_iota(jnp.int32, (T, T), 1)
    triu = (row <= col).astype(F32)
    lower = col < row
    upper = col > row

    dt = _softplus(dtr_ref[...] + dbr_ref[...])
    a = dt * (-jnp.exp(alr_ref[...]))
    cum = jnp.dot(a.reshape(nc * nd, T), triu, precision=hi,
                  preferred_element_type=F32).reshape(nc, nd, T)
    last = cum[:, :, T - 1:T]
    exc = cum - a
    fwd = lax.broadcasted_iota(jnp.int32, (nc, nd, T), 1) < nh
    base = jnp.where(fwd, cum, exc)
    log2e = 1.0 / math.log(2.0)
    log_dt = jnp.log(dt)
    arg_s[...] = jnp.where(fwd, cum - log_dt, exc + log_dt) * log2e
    log_sum = jnp.log(dt[:, 0:nh, :] + dt[:, nh:, :]) * log2e
    diag_s[...] = jnp.concatenate([log_sum, log_sum], axis=1)
    rows_s[:, 0:nd, :] = base * log2e
    rows_s[:, nd:2 * nd, :] = jnp.exp(jnp.where(fwd, cum, last - exc))
    rows_s[:, 2 * nd:3 * nd, :] = dt * jnp.exp(jnp.where(fwd, last - cum, exc))
    rows_s[:, 3 * nd:, :] = jnp.zeros((nc, LANES - 3 * nd, T), F32)

    def expand(cols, first):
        n = cols.shape[0]
        low = lax.broadcasted_iota(jnp.int32, (n, LANES), 1) < SSM_HEAD_DIM
        halves = []
        for j in range(GROUP_W // LANES):
            c0 = jnp.broadcast_to(cols[:, first + 2 * j:first + 2 * j + 1], (n, LANES))
            c1 = jnp.broadcast_to(cols[:, first + 2 * j + 1:first + 2 * j + 2], (n, LANES))
            halves.append(jnp.where(low, c0, c1))
        return jnp.concatenate(halves, axis=1)

    tn_dims = (((0,), (0,)), ((), ()))

    cid = lax.broadcasted_iota(jnp.int32, (LANES, 4 * GROUP_W), 0)
    lid = lax.broadcasted_iota(jnp.int32, (LANES, 4 * GROUP_W), 1)
    blk = lid // GROUP_W
    want = nd + (blk % 2) * nh + (blk // 2) * nd + (lid % GROUP_W) // SSM_HEAD_DIM
    spread = jnp.where(cid == want, 1.0, 0.0).astype(BF16)
    spread_f = spread[:, :3 * GROUP_W]
    spread_b = spread[:, 3 * GROUP_W:]

    sb[...] = jnp.zeros_like(sb)

    def state_back(ci):
        t0 = ci * T
        ct = rows_s[ci].T
        cols_s[ci] = ct
        s_prev = sb[...]
        sbs[ci] = s_prev.astype(BF16)
        wx = jnp.dot(ct.astype(BF16), spread_b, preferred_element_type=F32)
        xs = (load_x(t0) * wx).astype(BF16)
        sb[...] = expand(ct[0:1, :], nd + nh) * s_prev + lax.dot_general(
            b_ref[pl.ds(t0, T), :].astype(BF16), xs, tn_dims, preferred_element_type=F32)

    for k in range(nc - 1, -1, -1):
        state_back(k)

    sf[...] = jnp.zeros_like(sf)
    low_half = lax.broadcasted_iota(jnp.int32, (T, LANES), 1) < SSM_HEAD_DIM

    def finish(ci):
        t0 = ci * T
        zz = z_ref[pl.ds(t0, T), :].astype(F32)
        y = y_s[pl.ds(t0, T), :] * (zz * _sigmoid(zz))
        ms = jnp.mean(y * y, axis=-1, keepdims=True)
        o_ref[pl.ds(t0, T), :] = (y * lax.rsqrt(ms + EPS) * ng_ref[...]).astype(o_ref.dtype)

    def chunk_fwd(ci):
        t0 = ci * T
        ct = cols_s[ci]
        arg_r = arg_s[ci]
        dt_r = diag_s[ci]
        x = load_x(t0)
        bk = b_ref[pl.ds(t0, T), :].astype(BF16)
        ck = c_ref[pl.ds(t0, T), :].astype(BF16)
        cb = lax.dot_general(ck, bk, (((1,), (1,)), ((), ())), preferred_element_type=F32)
        spreadv = jnp.dot(ct.astype(BF16), spread_f, preferred_element_type=F32)
        lhs = []
        for h in range(nh):
            hb = nh + h
            arg = jnp.where(lower, ct[:, h:h + 1] - arg_r[h:h + 1, :],
                            jnp.where(upper, arg_r[hb:hb + 1, :] - ct[:, hb:hb + 1], dt_r[h:h + 1, :]))
            lhs.append((cb * jnp.exp2(arg)).astype(BF16))
        s_prev = sf[...]
        y4 = jnp.dot(jnp.concatenate(lhs, axis=0), x.astype(BF16), preferred_element_type=F32)
        off = jnp.dot(ck, jnp.concatenate([s_prev.astype(BF16), sbs[ci]], axis=1),
                      preferred_element_type=F32)
        y = x * dsk_ref[...] + jnp.concatenate(
            [jnp.where(low_half, y4[2 * j * T:(2 * j + 1) * T, j * LANES:(j + 1) * LANES],
                       y4[(2 * j + 1) * T:(2 * j + 2) * T, j * LANES:(j + 1) * LANES])
             for j in range(GROUP_W // LANES)], axis=1)
        y = y + spreadv[:, :GROUP_W] * off[:, :GROUP_W] + spreadv[:, GROUP_W:2 * GROUP_W] * off[:, GROUP_W:]
        xs = (x * spreadv[:, 2 * GROUP_W:]).astype(BF16)
        sf[...] = expand(ct[T - 1:T, :], nd) * s_prev + lax.dot_general(
            bk, xs, tn_dims, preferred_element_type=F32)
        y_s[pl.ds(t0, T), :] = y

    chunk_fwd(0)
    for c in range(1, nc):
        chunk_fwd(c)
        finish(c - 1)
    finish(nc - 1)


def _ssd(z, xbc, dt_row, db_row, al_row, dskip, ng, batch, seq):
    G = SSM_GROUPS
    nc = seq // SSD_CHUNK
    z4 = z.reshape(G, batch, seq, GROUP_W)
    xbc4 = xbc.reshape(4 * G, batch, seq, LANES)
    nd = 2 * HEADS_PER_GROUP

    def per_group(shape):
        return pl.BlockSpec((None,) + shape, lambda b, g: (g,) + (0,) * len(shape))

    def slab(first, step):
        return pl.BlockSpec((None, None, seq, LANES), lambda b, g: (first + step * g, b, 0, 0))

    in_specs = [
        pl.BlockSpec((None, None, seq, GROUP_W), lambda b, g: (g, b, 0, 0)),
        slab(0, 2), slab(1, 2), slab(2 * G, 1), slab(3 * G, 1),
        pl.BlockSpec((None, None, nc, nd, SSD_CHUNK), lambda b, g: (b, g, 0, 0, 0)),
        per_group((nd, 1)), per_group((nd, 1)),
        per_group((1, GROUP_W)), per_group((1, GROUP_W)),
    ]
    out = pl.pallas_call(
        functools.partial(_ssd_kernel, seq=seq),
        grid=(batch, G),
        in_specs=in_specs,
        out_specs=pl.BlockSpec((None, None, seq, GROUP_W), lambda b, g: (g, b, 0, 0)),
        out_shape=jax.ShapeDtypeStruct((G, batch, seq, GROUP_W), BF16),
        scratch_shapes=[pltpu.VMEM((seq, GROUP_W), F32),
                        pltpu.VMEM((nc, nd, SSD_CHUNK), F32),
                        pltpu.VMEM((nc, nd, SSD_CHUNK), F32),
                        pltpu.VMEM((nc, LANES, SSD_CHUNK), F32),
                        pltpu.VMEM((nc, SSD_CHUNK, LANES), F32),
                        pltpu.VMEM((nc, SSM_STATE, GROUP_W), BF16),
                        pltpu.VMEM((SSM_STATE, GROUP_W), F32),
                        pltpu.VMEM((SSM_STATE, GROUP_W), F32)],
        compiler_params=_params("parallel", "parallel"),
        name="ssd",
    )(z4, xbc4, xbc4, xbc4, xbc4, dt_row, db_row, al_row, dskip, ng)
    return out.reshape(G, batch * seq, GROUP_W)


def _out_proj_kernel(attn_ref, ssd_ref, w_ref, x_ref, o_ref, lhs):
    for h in range(ATTN_HEADS):
        lhs[:, h * HEAD_DIM:(h + 1) * HEAD_DIM] = attn_ref[h].astype(BF16)
    for g in range(SSM_GROUPS):
        lhs[:, ATTN_W + g * GROUP_W:ATTN_W + (g + 1) * GROUP_W] = ssd_ref[g]
    o_ref[...] = x_ref[...] + jnp.dot(lhs[...], w_ref[...], preferred_element_type=F32)


def _out_proj(attn, ssd, w, x2d, tm=512):
    n, d = x2d.shape
    kk = w.shape[0]
    return pl.pallas_call(
        _out_proj_kernel,
        grid=(n // tm,),
        in_specs=[pl.BlockSpec((ATTN_HEADS, tm, HEAD_DIM), lambda i: (0, i, 0)),
                  pl.BlockSpec((SSM_GROUPS, tm, GROUP_W), lambda i: (0, i, 0)),
                  pl.BlockSpec((kk, d), lambda i: (0, 0), pipeline_mode=pl.Buffered(1)),
                  pl.BlockSpec((tm, d), lambda i: (i, 0))],
        out_specs=pl.BlockSpec((tm, d), lambda i: (i, 0)),
        out_shape=jax.ShapeDtypeStruct((n, d), F32),
        scratch_shapes=[pltpu.VMEM((tm, kk), BF16)],
        compiler_params=_params("parallel"),
        name="out_proj",
    )(attn, ssd, w, x2d)


def _mlp_kernel(x_ref, g_ref, wu_ref, wd_ref, o_ref, hm):
    @pl.when(pl.program_id(1) == 0)
    def _():
        x = x_ref[...]
        ms = jnp.mean(x * x, axis=-1, keepdims=True)
        hm[...] = (x * lax.rsqrt(ms + EPS) * g_ref[...]).astype(hm.dtype)
        o_ref[...] = x
    u = jnp.maximum(jnp.dot(hm[...], wu_ref[...], preferred_element_type=F32), 0.0)
    o_ref[...] += jnp.dot((u * u).astype(BF16), wd_ref[...], preferred_element_type=F32)


def _mlp(x2d, g, wu, wd, tm=512, tf=1024):
    n, d = x2d.shape
    f = wu.shape[1]
    return pl.pallas_call(
        _mlp_kernel,
        grid=(n // tm, f // tf),
        in_specs=[pl.BlockSpec((tm, d), lambda i, j: (i, 0)),
                  pl.BlockSpec((1, d), lambda i, j: (0, 0)),
                  pl.BlockSpec((d, tf), lambda i, j: (0, j)),
                  pl.BlockSpec((tf, d), lambda i, j: (j, 0))],
        out_specs=pl.BlockSpec((tm, d), lambda i, j: (i, 0)),
        out_shape=jax.ShapeDtypeStruct((n, d), F32),
        scratch_shapes=[pltpu.VMEM((tm, d), BF16)],
        compiler_params=_params("parallel", "arbitrary"),
        name="mlp",
    )(x2d, g.reshape(1, d), wu, wd)


def kernel(x, norm_mix_g, w_in, q_norm_g, k_norm_g, rel_bias, conv_w, conv_b, dt_bias, a_log,
           d_skip, ssd_norm_g, w_out, norm_mlp_g, w_up, w_down):
    batch, seq, _ = x.shape
    n = batch * seq
    G, nh = SSM_GROUPS, HEADS_PER_GROUP
    nc = seq // SSD_CHUNK
    o_z = 3 * ATTN_W
    o_bc = o_z + 2 * SSM_W
    o_dt = o_bc + 2 * G * SSM_STATE
    x2d = x.reshape(n, D_MODEL)
    brow = _bias_rows(rel_bias)

    for layer in range(w_in.shape[0]):
        wi = w_in[layer].astype(BF16)
        h, dt_raw = _rmsnorm_dt(x2d, norm_mix_g[layer], wi[:, o_dt:])

        scale = 1.0 / math.sqrt(HEAD_DIM)
        gains = jnp.concatenate([jnp.tile(q_norm_g[layer].astype(F32) * scale, ATTN_HEADS),
                                 jnp.tile(k_norm_g[layer].astype(F32), ATTN_HEADS)]).reshape(1, 2 * ATTN_W)
        qk = _proj_qk(h, wi, gains)
        v = _proj_split(h, wi, 2 * ATTN_W, ATTN_W, HEAD_DIM, "proj_v", out_dtype=F32)
        z = _proj_split(h, wi, o_z, SSM_W, GROUP_W, "proj_z")
        xbc = _proj_conv(h, wi, o_z + SSM_W, conv_w[layer].astype(F32),
                         conv_b[layer].astype(F32).reshape(1, -1), batch, seq)

        attn = _attention(qk, v, brow, batch, seq)

        dt_row = (dt_raw.reshape(batch, nc, SSD_CHUNK, 2, G, nh)
                  .transpose(0, 4, 1, 3, 5, 2).reshape(batch, G, nc, 2 * nh, SSD_CHUNK))
        per_dir = lambda t: t.astype(F32).reshape(2, G, nh).transpose(1, 0, 2).reshape(G, 2 * nh)
        db, al = per_dir(dt_bias[layer]), per_dir(a_log[layer])
        ssd = _ssd(
            z, xbc, dt_row,
            db.reshape(G, 2 * nh, 1), al.reshape(G, 2 * nh, 1),
            jnp.repeat(d_skip[layer].astype(F32), SSM_HEAD_DIM).reshape(G, 1, GROUP_W),
            ssd_norm_g[layer].astype(F32).reshape(G, 1, GROUP_W),
            batch, seq)

        x2d = _out_proj(attn, ssd, w_out[layer].astype(BF16), x2d)
        x2d = _mlp(x2d, norm_mlp_g[layer], w_up[layer].astype(BF16), w_down[layer].astype(BF16))
    return x2d.reshape(batch, seq, D_MODEL)
```

```python
import functools
import math

import jax
import jax.numpy as jnp
from jax import lax
from jax.experimental import pallas as pl
from jax.experimental.pallas import tpu as pltpu

D_MODEL = 2048
ATTN_HEADS = 16
HEAD_DIM = 128
ATTN_W = ATTN_HEADS * HEAD_DIM
SSM_HEADS = 32
SSM_HEAD_DIM = 64
SSM_W = SSM_HEADS * SSM_HEAD_DIM
SSM_GROUPS = 8
HEADS_PER_GROUP = SSM_HEADS // SSM_GROUPS
GROUP_W = SSM_W // SSM_GROUPS
SSM_STATE = 128
CONV_WIDTH = 5
DILATIONS = (1, 4, 16)
HALF_WINDOW = 64
NUM_BUCKETS = 32
MAX_DISTANCE = 1024
NEG_INF = -1e30
EPS = 1e-6

LANES = 128
Q_SUB = 128
K_WIN = 256
SSD_CHUNK = 128
CONV_HALO = 8
VMEM_LIMIT = 56 * 1024 * 1024

F32 = jnp.float32
BF16 = jnp.bfloat16


def _params(*sem):
    return pltpu.CompilerParams(dimension_semantics=sem, vmem_limit_bytes=VMEM_LIMIT)


def _sigmoid(x):
    return 1.0 / (1.0 + jnp.exp2(x * (-1.0 / math.log(2.0))))


def _softplus(x):
    return jnp.maximum(x, 0.0) + jnp.log1p(jnp.exp(-jnp.abs(x)))


def _rmsnorm_dt_kernel(x_ref, g_ref, w_ref, h_ref, dt_ref):
    x = x_ref[...]
    ms = jnp.mean(x * x, axis=-1, keepdims=True)
    h = (x * lax.rsqrt(ms + EPS) * g_ref[...]).astype(h_ref.dtype)
    h_ref[...] = h
    dt_ref[...] = jnp.dot(h, w_ref[...], preferred_element_type=F32)


def _rmsnorm_dt(x2d, g, w_dt, tm=1024):
    n, d = x2d.shape
    m = w_dt.shape[1]
    return pl.pallas_call(
        _rmsnorm_dt_kernel,
        grid=(n // tm,),
        in_specs=[pl.BlockSpec((tm, d), lambda i: (i, 0)),
                  pl.BlockSpec((1, d), lambda i: (0, 0)),
                  pl.BlockSpec((d, m), lambda i: (0, 0))],
        out_specs=[pl.BlockSpec((tm, d), lambda i: (i, 0)),
                   pl.BlockSpec((tm, m), lambda i: (i, 0))],
        out_shape=[jax.ShapeDtypeStruct((n, d), BF16), jax.ShapeDtypeStruct((n, m), F32)],
        compiler_params=_params("parallel"),
        name="rmsnorm_dt",
    )(x2d, g.reshape(1, d), w_dt)


QK_SUB = 256


def _proj_qk_kernel(a_ref, w_ref, g_ref, o_ref, *, heads_per_tile):
    a = a_ref[...]
    per = QK_SUB // HEAD_DIM
    for c in range(heads_per_tile // per):
        acc = jnp.dot(a, w_ref[:, c * QK_SUB:(c + 1) * QK_SUB], preferred_element_type=F32)
        for hh in range(per):
            h = c * per + hh
            s = acc[:, hh * HEAD_DIM:(hh + 1) * HEAD_DIM]
            ms = jnp.mean(s * s, axis=-1, keepdims=True)
            g = g_ref[:, h * HEAD_DIM:(h + 1) * HEAD_DIM]
            o_ref[h] = (s * lax.rsqrt(ms + EPS) * g).astype(o_ref.dtype)


def _proj_split_kernel(a_ref, w_ref, o_ref, *, width):
    a = a_ref[...]
    per = QK_SUB // width
    for s in range(w_ref.shape[1] // QK_SUB):
        acc = jnp.dot(a, w_ref[:, s * QK_SUB:(s + 1) * QK_SUB], preferred_element_type=F32)
        for c in range(per):
            o_ref[s * per + c] = acc[:, c * width:(c + 1) * width].astype(o_ref.dtype)


def _proj_qk(h, w, gains, tm=1024, tn=1024):
    n, k = h.shape
    m = 2 * ATTN_W
    hpt = tn // HEAD_DIM
    return pl.pallas_call(
        functools.partial(_proj_qk_kernel, heads_per_tile=hpt),
        grid=(m // tn, n // tm),
        in_specs=[pl.BlockSpec((tm, k), lambda j, i: (i, 0)),
                  pl.BlockSpec((k, tn), lambda j, i: (0, j)),
                  pl.BlockSpec((1, tn), lambda j, i: (0, j))],
        out_specs=pl.BlockSpec((hpt, tm, HEAD_DIM), lambda j, i: (j, i, 0)),
        out_shape=jax.ShapeDtypeStruct((m // HEAD_DIM, n, HEAD_DIM), F32),
        compiler_params=_params("parallel", "arbitrary"),
        name="proj_qk",
    )(h, w, gains)


def _proj_split(h, w, col0, m, width, name, out_dtype=BF16, tm=2048, tn=1024):
    n, k = h.shape
    cpt = tn // width
    j0 = col0 // tn
    return pl.pallas_call(
        functools.partial(_proj_split_kernel, width=width),
        grid=(m // tn, n // tm),
        in_specs=[pl.BlockSpec((tm, k), lambda j, i: (i, 0)),
                  pl.BlockSpec((k, tn), lambda j, i: (0, j0 + j))],
        out_specs=pl.BlockSpec((cpt, tm, width), lambda j, i: (j, i, 0)),
        out_shape=jax.ShapeDtypeStruct((m // width, n, width), out_dtype),
        compiler_params=_params("parallel", "arbitrary"),
        name=name,
    )(h, w)


def _t5_bucket(rel):
    nb = NUM_BUCKETS // 2
    max_exact = nb // 2
    ret = (rel > 0).astype(jnp.int32) * nb
    n = jnp.abs(rel)
    nf = jnp.maximum(n, 1).astype(jnp.float32)
    large = max_exact + (jnp.log(nf / max_exact) / math.log(MAX_DISTANCE / max_exact)
                         * (nb - max_exact)).astype(jnp.int32)
    large = jnp.minimum(large, nb - 1)
    return ret + jnp.where(n < max_exact, n, large)


def _bias_rows(rel_bias):
    period = K_WIN + Q_SUB
    m = jnp.arange(period)
    delta = jnp.where(m < K_WIN, m, m - period)
    rows = []
    for d in DILATIONS:
        for off in (0, -HALF_WINDOW, -2 * HALF_WINDOW):
            rel = delta + off
            valid = jnp.abs(rel) <= HALF_WINDOW
            b = rel_bias[_t5_bucket(rel * d)].astype(F32)
            rows.append(jnp.where(valid[:, None], b, NEG_INF))
    return jnp.stack(rows, axis=0).transpose(2, 0, 1)


def _attn_kernel(q_ref, k_ref, v_ref, brow_ref, o_ref, bias_s, tmp_s, xq4, xk4, xv4,
                 q16, k1, k4, k16, v1, v4, v16, acc_s, max_s, den_s, *, seq):
    period = K_WIN + Q_SUB
    n4, n16 = seq // 4, seq // 16

    @pl.when(pl.program_id(1) == 0)
    def _():
        for idx in range(9):
            row = jnp.broadcast_to(brow_ref[idx:idx + 1, :], (Q_SUB, period))
            tile = pltpu.roll(row, 0, 1, stride=1, stride_axis=0)
            d = DILATIONS[idx // 3]
            if d == 16:
                bias_s[idx] = tile[:, :K_WIN]
                continue
            for half in range(K_WIN // LANES):
                tmp_s[half] = tile[:, half * LANES:(half + 1) * LANES]
            groups = 16 // d
            for half in range(K_WIN // LANES):
                for g in range(groups):
                    n = Q_SUB // groups
                    bias_s[idx, g * n:(g + 1) * n, half * LANES:(half + 1) * LANES] = (
                        tmp_s[half, pl.ds(g, n, stride=groups), :])

    cp = 256

    def split4(src, dst):
        for r4 in range(4):
            for t0 in range(0, n4, cp):
                dst[r4, pl.ds(t0, cp), :] = src[pl.ds(r4 + 4 * t0, cp, stride=4), :]

    def split16(src4, dst16, dst4=None):
        for r4 in range(4):
            for a in range(4):
                dst16[4 * a + r4] = src4[r4, pl.ds(a, n16, stride=4), :].astype(BF16)
            if dst4 is not None:
                for t0 in range(0, n4, cp):
                    dst4[r4, pl.ds(t0, cp), :] = src4[r4, pl.ds(t0, cp), :].astype(BF16)

    def cast(src, dst):
        for t0 in range(0, seq, cp):
            dst[pl.ds(t0, cp), :] = src[pl.ds(t0, cp), :].astype(BF16)

    split4(q_ref, xq4)
    split16(xq4, q16)
    for src, x4, d1, d4, d16 in ((k_ref, xk4, k1, k4, k16), (v_ref, xv4, v1, v4, v16)):
        cast(src, d1)
        split4(src, x4)
        split16(x4, d16, d4)

    ones = jnp.ones((K_WIN, LANES), BF16)

    def sub_tile(q, k_s, v_s, length, s0, pi):
        if isinstance(s0, int):
            w0 = min(max(s0 - HALF_WINDOW, 0), length - K_WIN)
            place = 0 if s0 == 0 else (2 if s0 == length - Q_SUB else 1)
        else:
            w0 = pl.multiple_of(jnp.clip(s0 - HALF_WINDOW, 0, length - K_WIN), HALF_WINDOW)
            place = jnp.where(s0 == 0, 0, jnp.where(s0 == length - Q_SUB, 2, 1))
        k = k_s[pl.ds(w0, K_WIN), :]
        v = v_s[pl.ds(w0, K_WIN), :]
        s = lax.dot_general(q, k, (((1,), (1,)), ((), ())), preferred_element_type=F32)
        s = s + bias_s[3 * pi + place]
        m = jnp.max(s, axis=-1, keepdims=True)
        p = jnp.exp(s - m).astype(BF16)
        pv = jnp.dot(p, jnp.concatenate([v, ones], axis=1), preferred_element_type=F32)
        return pv[:, :LANES], jnp.broadcast_to(m, (Q_SUB, LANES)), pv[:, LANES:]

    stats = (acc_s, max_s, den_s)


    res = []
    for l0 in range(0, n16, Q_SUB // 4):
        for r4 in range(4):
            q = jnp.concatenate([q16[4 * a + r4, pl.ds(l0, Q_SUB // 4), :] for a in range(4)], axis=0)
            res.append((sub_tile(q, k4.at[r4], v4.at[r4], n4, 4 * l0, 1), r4, l0))
    for r3, r4, l0 in res:
        n = Q_SUB // 4
        for a in range(4):
            for val, dst in zip(r3, stats):
                dst[0, 4 * a + r4, pl.ds(l0, n), :] = val[a * n:(a + 1) * n]

    res = []
    for r in range(16):
        for l0 in range(0, n16, Q_SUB):
            res.append((sub_tile(q16[r, pl.ds(l0, Q_SUB), :], k16.at[r], v16.at[r], n16, l0, 2), r, l0))
    for r3, r, l0 in res:
        for val, dst in zip(r3, stats):
            dst[1, r, pl.ds(l0, Q_SUB), :] = val

    res = []
    for l0 in range(0, n16, Q_SUB // 16):
        q = jnp.concatenate([xq4[r % 4, pl.ds(4 * l0 + r // 4, Q_SUB // 16, stride=4), :] for r in range(16)],
                            axis=0).astype(BF16)
        res.append((sub_tile(q, k1, v1, seq, 16 * l0, 0), l0))
    for (aa, ma, da), l0 in res:
        n = Q_SUB // 16
        rows = lambda ref, pi: jnp.concatenate([ref[pi, r, pl.ds(l0, n), :] for r in range(16)], axis=0)
        mb, mc = rows(max_s, 0), rows(max_s, 1)
        mx = jnp.maximum(jnp.maximum(ma, mb), mc)
        ea, eb, ec = jnp.exp(ma - mx), jnp.exp(mb - mx), jnp.exp(mc - mx)
        num = ea * aa + eb * rows(acc_s, 0) + ec * rows(acc_s, 1)
        den = ea * da + eb * rows(den_s, 0) + ec * rows(den_s, 1)
        out = num / den
        for r in range(16):
            o_ref[pl.ds(r + 16 * l0, n, stride=16), :] = out[r * n:(r + 1) * n]


def _attention(qk, v, brow, batch, seq):
    H = ATTN_HEADS
    n4, n16 = seq // 4, seq // 16
    qk4 = qk.reshape(2 * H, batch, seq, HEAD_DIM)
    v4 = v.reshape(H, batch, seq, HEAD_DIM)
    in_specs = [pl.BlockSpec((None, None, seq, HEAD_DIM), lambda h, b: (h, b, 0, 0)),
                pl.BlockSpec((None, None, seq, HEAD_DIM), lambda h, b: (H + h, b, 0, 0)),
                pl.BlockSpec((None, None, seq, HEAD_DIM), lambda h, b: (h, b, 0, 0))]
    in_specs.append(pl.BlockSpec((None, 9, K_WIN + Q_SUB), lambda h, b: (h, 0, 0)))
    kv_slabs = [pltpu.VMEM((seq, LANES), BF16), pltpu.VMEM((4, n4, LANES), BF16),
                pltpu.VMEM((16, n16, LANES), BF16)]
    out = pl.pallas_call(
        functools.partial(_attn_kernel, seq=seq),
        grid=(H, batch),
        in_specs=in_specs,
        out_specs=pl.BlockSpec((None, None, seq, HEAD_DIM), lambda h, b: (h, b, 0, 0)),
        out_shape=jax.ShapeDtypeStruct((H, batch, seq, HEAD_DIM), F32),
        scratch_shapes=[pltpu.VMEM((9, Q_SUB, K_WIN), F32),
                        pltpu.VMEM((K_WIN // LANES, Q_SUB, LANES), F32),
                        pltpu.VMEM((4, n4, LANES), F32),
                        pltpu.VMEM((4, n4, LANES), F32),
                        pltpu.VMEM((4, n4, LANES), F32),
                        pltpu.VMEM((16, n16, LANES), BF16)]
                       + kv_slabs + kv_slabs
                       + [pltpu.VMEM((2, 16, n16, LANES), F32)] * 3,
        compiler_params=_params("parallel", "arbitrary"),
        name="dilated_attention",
    )(qk4, qk4, v4, brow)
    return out.reshape(H, batch * seq, HEAD_DIM)


def _ssd_kernel(z_ref, x_ref, b_ref, c_ref, dtr_ref, cwx_ref, cwb_ref, cwc_ref,
                cbx_ref, cbb_ref, cbc_ref, dbr_ref, alr_ref, dsk_ref, ng_ref,
                o_ref, pad, cv, y_s, arg_s, diag_s, rows_s, cols_s, sbs, sf, sb, *, seq):
    T = SSD_CHUNK
    nc = seq // T
    hi = lax.Precision.HIGHEST
    halo = CONV_HALO
    half = T // 2

    pad[:, 0:halo, :] = jnp.zeros((4, halo, LANES), F32)
    pad[:, seq + halo:seq + 2 * halo, :] = jnp.zeros((4, halo, LANES), F32)

    for t0 in range(0, seq, T):
        xin = x_ref[pl.ds(t0, T), :].astype(F32)
        pad[0, pl.ds(t0 + halo, T), :] = xin[:, :LANES]
        pad[1, pl.ds(t0 + halo, T), :] = xin[:, LANES:]
        pad[2, pl.ds(t0 + halo, T), :] = b_ref[pl.ds(t0, T), :].astype(F32)
        pad[3, pl.ds(t0 + halo, T), :] = c_ref[pl.ds(t0, T), :].astype(F32)

    cws = (cwx_ref[:, :LANES], cwx_ref[:, LANES:], cwb_ref[...], cwc_ref[...])
    cbs = (cbx_ref[:, :LANES], cbx_ref[:, LANES:], cbb_ref[...], cbc_ref[...])

    def conv_chunk(ci):
        t0 = ci * T
        for s in range(4):
            ev = [pad[s, pl.ds(t0 + halo + 2 * k, half, stride=2), :] for k in (-1, 0, 1)]
            od = [pad[s, pl.ds(t0 + halo + 1 + 2 * k, half, stride=2), :] for k in (-1, 0, 1)]
            w = [cws[s][j:j + 1, :] for j in range(CONV_WIDTH)]
            out_e = cbs[s] + w[0] * ev[0] + w[1] * od[0] + w[2] * ev[1] + w[3] * od[1] + w[4] * ev[2]
            out_o = cbs[s] + w[0] * od[0] + w[1] * ev[1] + w[2] * od[1] + w[3] * ev[2] + w[4] * od[2]
            cv[s, pl.ds(t0, half, stride=2), :] = out_e * _sigmoid(out_e)
            cv[s, pl.ds(t0 + 1, half, stride=2), :] = out_o * _sigmoid(out_o)

    def load_x(t0):
        return jnp.concatenate([cv[0, pl.ds(t0, T), :], cv[1, pl.ds(t0, T), :]], axis=1)

    nh = HEADS_PER_GROUP
    nd = 2 * nh
    row = lax.broadcasted_iota(jnp.int32, (T, T), 0)
    col = lax.broadcasted_iota(jnp.int32, (T, T), 1)
    triu = (row <= col).astype(F32)
    lower = col < row
    upper = col > row

    dt = _softplus(dtr_ref[...] + dbr_ref[...])
    a = dt * (-jnp.exp(alr_ref[...]))
    cum = jnp.dot(a.reshape(nc * nd, T), triu, precision=hi,
                  preferred_element_type=F32).reshape(nc, nd, T)
    last = cum[:, :, T - 1:T]
    exc = cum - a
    fwd = lax.broadcasted_iota(jnp.int32, (nc, nd, T), 1) < nh
    base = jnp.where(fwd, cum, exc)
    log2e = 1.0 / math.log(2.0)
    log_dt = jnp.log(dt)
    arg_s[...] = jnp.where(fwd, cum - log_dt, exc + log_dt) * log2e
    log_sum = jnp.log(dt[:, 0:nh, :] + dt[:, nh:, :]) * log2e
    diag_s[...] = jnp.concatenate([log_sum, log_sum], axis=1)
    rows_s[:, 0:nd, :] = base * log2e
    rows_s[:, nd:2 * nd, :] = jnp.exp(jnp.where(fwd, cum, last - exc))
    rows_s[:, 2 * nd:3 * nd, :] = dt * jnp.exp(jnp.where(fwd, last - cum, exc))
    rows_s[:, 3 * nd:, :] = jnp.zeros((nc, LANES - 3 * nd, T), F32)

    def expand(cols, first):
        n = cols.shape[0]
        low = lax.broadcasted_iota(jnp.int32, (n, LANES), 1) < SSM_HEAD_DIM
        halves = []
        for j in range(GROUP_W // LANES):
            c0 = jnp.broadcast_to(cols[:, first + 2 * j:first + 2 * j + 1], (n, LANES))
            c1 = jnp.broadcast_to(cols[:, first + 2 * j + 1:first + 2 * j + 2], (n, LANES))
            halves.append(jnp.where(low, c0, c1))
        return jnp.concatenate(halves, axis=1)

    tn_dims = (((0,), (0,)), ((), ()))

    cid = lax.broadcasted_iota(jnp.int32, (LANES, 4 * GROUP_W), 0)
    lid = lax.broadcasted_iota(jnp.int32, (LANES, 4 * GROUP_W), 1)
    blk = lid // GROUP_W
    want = nd + (blk % 2) * nh + (blk // 2) * nd + (lid % GROUP_W) // SSM_HEAD_DIM
    spread = jnp.where(cid == want, 1.0, 0.0).astype(BF16)
    spread_f = spread[:, :3 * GROUP_W]
    spread_b = spread[:, 3 * GROUP_W:]

    sb[...] = jnp.zeros_like(sb)

    def state_back(ci):
        t0 = ci * T
        ct = rows_s[ci].T
        cols_s[ci] = ct
        s_prev = sb[...]
        sbs[ci] = s_prev.astype(BF16)
        wx = jnp.dot(ct.astype(BF16), spread_b, preferred_element_type=F32)
        xs = (load_x(t0) * wx).astype(BF16)
        sb[...] = expand(ct[0:1, :], nd + nh) * s_prev + lax.dot_general(
            cv[2, pl.ds(t0, T), :].astype(BF16), xs, tn_dims, preferred_element_type=F32)

    conv_chunk(nc - 1)
    for k in range(nc - 2, -1, -1):
        conv_chunk(k)
        state_back(k + 1)
    state_back(0)

    sf[...] = jnp.zeros_like(sf)
    low_half = lax.broadcasted_iota(jnp.int32, (T, LANES), 1) < SSM_HEAD_DIM

    def finish(ci):
        t0 = ci * T
        zz = z_ref[pl.ds(t0, T), :].astype(F32)
        y = y_s[pl.ds(t0, T), :] * (zz * _sigmoid(zz))
        ms = jnp.mean(y * y, axis=-1, keepdims=True)
        o_ref[pl.ds(t0, T), :] = (y * lax.rsqrt(ms + EPS) * ng_ref[...]).astype(o_ref.dtype)

    def chunk_fwd(ci):
        t0 = ci * T
        ct = cols_s[ci]
        arg_r = arg_s[ci]
        dt_r = diag_s[ci]
        x = load_x(t0)
        bk = cv[2, pl.ds(t0, T), :].astype(BF16)
        ck = cv[3, pl.ds(t0, T), :].astype(BF16)
        cb = lax.dot_general(ck, bk, (((1,), (1,)), ((), ())), preferred_element_type=F32)
        spreadv = jnp.dot(ct.astype(BF16), spread_f, preferred_element_type=F32)
        lhs = []
        for h in range(nh):
            hb = nh + h
            arg = jnp.where(lower, ct[:, h:h + 1] - arg_r[h:h + 1, :],
                            jnp.where(upper, arg_r[hb:hb + 1, :] - ct[:, hb:hb + 1], dt_r[h:h + 1, :]))
            lhs.append((cb * jnp.exp2(arg)).astype(BF16))
        s_prev = sf[...]
        y4 = jnp.dot(jnp.concatenate(lhs, axis=0), x.astype(BF16), preferred_element_type=F32)
        off = jnp.dot(ck, jnp.concatenate([s_prev.astype(BF16), sbs[ci]], axis=1),
                      preferred_element_type=F32)
        y = x * dsk_ref[...] + jnp.concatenate(
            [jnp.where(low_half, y4[2 * j * T:(2 * j + 1) * T, j * LANES:(j + 1) * LANES],
                       y4[(2 * j + 1) * T:(2 * j + 2) * T, j * LANES:(j + 1) * LANES])
             for j in range(GROUP_W // LANES)], axis=1)
        y = y + spreadv[:, :GROUP_W] * off[:, :GROUP_W] + spreadv[:, GROUP_W:2 * GROUP_W] * off[:, GROUP_W:]
        xs = (x * spreadv[:, 2 * GROUP_W:]).astype(BF16)
        sf[...] = expand(ct[T - 1:T, :], nd) * s_prev + lax.dot_general(
            bk, xs, tn_dims, preferred_element_type=F32)
        y_s[pl.ds(t0, T), :] = y

    chunk_fwd(0)
    for c in range(1, nc):
        chunk_fwd(c)
        finish(c - 1)
    finish(nc - 1)


def _ssd(zx, bcm, dt_row, cwx, cwb, cwc, cbx, cbb, cbc, db_row, al_row, dskip, ng, batch, seq):
    G = SSM_GROUPS
    nc = seq // SSD_CHUNK
    zx4 = zx.reshape(2 * G, batch, seq, GROUP_W)
    bc4 = bcm.reshape(2 * G, batch, seq, SSM_STATE)
    nd = 2 * HEADS_PER_GROUP

    def per_group(shape):
        return pl.BlockSpec((None,) + shape, lambda b, g: (g,) + (0,) * len(shape))

    in_specs = [
        pl.BlockSpec((None, None, seq, GROUP_W), lambda b, g: (g, b, 0, 0)),
        pl.BlockSpec((None, None, seq, GROUP_W), lambda b, g: (G + g, b, 0, 0)),
        pl.BlockSpec((None, None, seq, SSM_STATE), lambda b, g: (g, b, 0, 0)),
        pl.BlockSpec((None, None, seq, SSM_STATE), lambda b, g: (G + g, b, 0, 0)),
        pl.BlockSpec((None, None, nc, nd, SSD_CHUNK), lambda b, g: (b, g, 0, 0, 0)),
        per_group((CONV_WIDTH, GROUP_W)), per_group((CONV_WIDTH, SSM_STATE)),
        per_group((CONV_WIDTH, SSM_STATE)),
        per_group((1, GROUP_W)), per_group((1, SSM_STATE)), per_group((1, SSM_STATE)),
        per_group((nd, 1)), per_group((nd, 1)),
        per_group((1, GROUP_W)), per_group((1, GROUP_W)),
    ]
    out = pl.pallas_call(
        functools.partial(_ssd_kernel, seq=seq),
        grid=(batch, G),
        in_specs=in_specs,
        out_specs=pl.BlockSpec((None, None, seq, GROUP_W), lambda b, g: (g, b, 0, 0)),
        out_shape=jax.ShapeDtypeStruct((G, batch, seq, GROUP_W), BF16),
        scratch_shapes=[pltpu.VMEM((4, seq + 2 * CONV_HALO, LANES), F32),
                        pltpu.VMEM((4, seq, LANES), F32),
                        pltpu.VMEM((seq, GROUP_W), F32),
                        pltpu.VMEM((nc, nd, SSD_CHUNK), F32),
                        pltpu.VMEM((nc, nd, SSD_CHUNK), F32),
                        pltpu.VMEM((nc, LANES, SSD_CHUNK), F32),
                        pltpu.VMEM((nc, SSD_CHUNK, LANES), F32),
                        pltpu.VMEM((nc, SSM_STATE, GROUP_W), BF16),
                        pltpu.VMEM((SSM_STATE, GROUP_W), F32),
                        pltpu.VMEM((SSM_STATE, GROUP_W), F32)],
        compiler_params=_params("parallel", "parallel"),
        name="ssd",
    )(zx4, zx4, bc4, bc4, dt_row, cwx, cwb, cwc, cbx, cbb, cbc, db_row, al_row, dskip, ng)
    return out.reshape(G, batch * seq, GROUP_W)


def _out_proj_kernel(attn_ref, ssd_ref, w_ref, x_ref, o_ref, lhs):
    for h in range(ATTN_HEADS):
        lhs[:, h * HEAD_DIM:(h + 1) * HEAD_DIM] = attn_ref[h].astype(BF16)
    for g in range(SSM_GROUPS):
        lhs[:, ATTN_W + g * GROUP_W:ATTN_W + (g + 1) * GROUP_W] = ssd_ref[g]
    o_ref[...] = x_ref[...] + jnp.dot(lhs[...], w_ref[...], preferred_element_type=F32)


def _out_proj(attn, ssd, w, x2d, tm=512):
    n, d = x2d.shape
    kk = w.shape[0]
    return pl.pallas_call(
        _out_proj_kernel,
        grid=(n // tm,),
        in_specs=[pl.BlockSpec((ATTN_HEADS, tm, HEAD_DIM), lambda i: (0, i, 0)),
                  pl.BlockSpec((SSM_GROUPS, tm, GROUP_W), lambda i: (0, i, 0)),
                  pl.BlockSpec((kk, d), lambda i: (0, 0), pipeline_mode=pl.Buffered(1)),
                  pl.BlockSpec((tm, d), lambda i: (i, 0))],
        out_specs=pl.BlockSpec((tm, d), lambda i: (i, 0)),
        out_shape=jax.ShapeDtypeStruct((n, d), F32),
        scratch_shapes=[pltpu.VMEM((tm, kk), BF16)],
        compiler_params=_params("parallel"),
        name="out_proj",
    )(attn, ssd, w, x2d)


def _mlp_kernel(x_ref, g_ref, wu_ref, wd_ref, o_ref, hm):
    def ffn():
        u = jnp.maximum(jnp.dot(hm[...], wu_ref[...], preferred_element_type=F32), 0.0)
        return jnp.dot((u * u).astype(BF16), wd_ref[...], preferred_element_type=F32)

    @pl.when(pl.program_id(1) == 0)
    def _():
        x = x_ref[...]
        ms = jnp.mean(x * x, axis=-1, keepdims=True)
        hm[...] = (x * lax.rsqrt(ms + EPS) * g_ref[...]).astype(hm.dtype)
        o_ref[...] = x + ffn()

    @pl.when(pl.program_id(1) > 0)
    def _():
        o_ref[...] += ffn()


def _mlp(x2d, g, wu, wd, tm=512, tf=1024):
    n, d = x2d.shape
    f = wu.shape[1]
    return pl.pallas_call(
        _mlp_kernel,
        grid=(n // tm, f // tf),
        in_specs=[pl.BlockSpec((tm, d), lambda i, j: (i, 0)),
                  pl.BlockSpec((1, d), lambda i, j: (0, 0)),
                  pl.BlockSpec((d, tf), lambda i, j: (0, j)),
                  pl.BlockSpec((tf, d), lambda i, j: (j, 0))],
        out_specs=pl.BlockSpec((tm, d), lambda i, j: (i, 0)),
        out_shape=jax.ShapeDtypeStruct((n, d), F32),
        scratch_shapes=[pltpu.VMEM((tm, d), BF16)],
        compiler_params=_params("parallel", "arbitrary"),
        name="mlp",
    )(x2d, g.reshape(1, d), wu, wd)


def kernel(x, norm_mix_g, w_in, q_norm_g, k_norm_g, rel_bias, conv_w, conv_b, dt_bias, a_log,
           d_skip, ssd_norm_g, w_out, norm_mlp_g, w_up, w_down):
    batch, seq, _ = x.shape
    n = batch * seq
    G, nh = SSM_GROUPS, HEADS_PER_GROUP
    nc = seq // SSD_CHUNK
    o_z = 3 * ATTN_W
    o_bc = o_z + 2 * SSM_W
    o_dt = o_bc + 2 * G * SSM_STATE
    x2d = x.reshape(n, D_MODEL)
    brow = _bias_rows(rel_bias)

    for layer in range(w_in.shape[0]):
        wi = w_in[layer].astype(BF16)
        h, dt_raw = _rmsnorm_dt(x2d, norm_mix_g[layer], wi[:, o_dt:])

        scale = 1.0 / math.sqrt(HEAD_DIM)
        gains = jnp.concatenate([jnp.tile(q_norm_g[layer].astype(F32) * scale, ATTN_HEADS),
                                 jnp.tile(k_norm_g[layer].astype(F32), ATTN_HEADS)]).reshape(1, 2 * ATTN_W)
        qk = _proj_qk(h, wi, gains)
        v = _proj_split(h, wi, 2 * ATTN_W, ATTN_W, HEAD_DIM, "proj_v", out_dtype=F32)
        zx = _proj_split(h, wi, o_z, o_bc - o_z, GROUP_W, "proj_zx")
        bcm = _proj_split(h, wi, o_bc, o_dt - o_bc, SSM_STATE, "proj_bc")

        attn = _attention(qk, v, brow, batch, seq)

        dt_row = (dt_raw.reshape(batch, nc, SSD_CHUNK, 2, G, nh)
                  .transpose(0, 4, 1, 3, 5, 2).reshape(batch, G, nc, 2 * nh, SSD_CHUNK))
        per_dir = lambda t: t.astype(F32).reshape(2, G, nh).transpose(1, 0, 2).reshape(G, 2 * nh)
        db, al = per_dir(dt_bias[layer]), per_dir(a_log[layer])
        cw, cbias = conv_w[layer].astype(F32), conv_b[layer].astype(F32)
        gn = G * SSM_STATE
        grp = lambda t, width: t.reshape(t.shape[0], G, width).transpose(1, 0, 2)
        ssd = _ssd(
            zx, bcm, dt_row,
            grp(cw[:, :SSM_W], GROUP_W), grp(cw[:, SSM_W:SSM_W + gn], SSM_STATE),
            grp(cw[:, SSM_W + gn:], SSM_STATE),
            grp(cbias[None, :SSM_W], GROUP_W), grp(cbias[None, SSM_W:SSM_W + gn], SSM_STATE),
            grp(cbias[None, SSM_W + gn:], SSM_STATE),
            db.reshape(G, 2 * nh, 1), al.reshape(G, 2 * nh, 1),
            jnp.repeat(d_skip[layer].astype(F32), SSM_HEAD_DIM).reshape(G, 1, GROUP_W),
            ssd_norm_g[layer].astype(F32).reshape(G, 1, GROUP_W),
            batch, seq)

        x2d = _out_proj(attn, ssd, w_out[layer].astype(BF16), x2d)
        x2d = _mlp(x2d, norm_mlp_g[layer], w_up[layer].astype(BF16), w_down[layer].astype(BF16))
    return x2d.reshape(batch, seq, D_MODEL)
```

```python
import functools
import math

import jax
import jax.numpy as jnp
from jax import lax
from jax.experimental import pallas as pl
from jax.experimental.pallas import tpu as pltpu

D_MODEL = 2048
ATTN_HEADS = 16
HEAD_DIM = 128
ATTN_W = ATTN_HEADS * HEAD_DIM
SSM_HEADS = 32
SSM_HEAD_DIM = 64
SSM_W = SSM_HEADS * SSM_HEAD_DIM
SSM_GROUPS = 8
HEADS_PER_GROUP = SSM_HEADS // SSM_GROUPS
GROUP_W = SSM_W // SSM_GROUPS
SSM_STATE = 128
CONV_WIDTH = 5
DILATIONS = (1, 4, 16)
HALF_WINDOW = 64
NUM_BUCKETS = 32
MAX_DISTANCE = 1024
NEG_INF = -1e30
EPS = 1e-6

LANES = 128
Q_SUB = 128
K_WIN = 256
SSD_CHUNK = 128
CONV_HALO = 8
VMEM_LIMIT = 56 * 1024 * 1024

F32 = jnp.float32
BF16 = jnp.bfloat16


def _params(*sem):
    return pltpu.CompilerParams(dimension_semantics=sem, vmem_limit_bytes=VMEM_LIMIT)


def _sigmoid(x):
    return 1.0 / (1.0 + jnp.exp2(x * (-1.0 / math.log(2.0))))


def _softplus(x):
    return jnp.maximum(x, 0.0) + jnp.log1p(jnp.exp(-jnp.abs(x)))


def _rmsnorm_dt_kernel(x_ref, g_ref, w_ref, h_ref, dt_ref):
    x = x_ref[...]
    ms = jnp.mean(x * x, axis=-1, keepdims=True)
    h = (x * lax.rsqrt(ms + EPS) * g_ref[...]).astype(h_ref.dtype)
    h_ref[...] = h
    dt_ref[...] = jnp.dot(h, w_ref[...], preferred_element_type=F32)


def _rmsnorm_dt(x2d, g, w_dt, tm=1024):
    n, d = x2d.shape
    m = w_dt.shape[1]
    return pl.pallas_call(
        _rmsnorm_dt_kernel,
        grid=(n // tm,),
        in_specs=[pl.BlockSpec((tm, d), lambda i: (i, 0)),
                  pl.BlockSpec((1, d), lambda i: (0, 0)),
                  pl.BlockSpec((d, m), lambda i: (0, 0))],
        out_specs=[pl.BlockSpec((tm, d), lambda i: (i, 0)),
                   pl.BlockSpec((tm, m), lambda i: (i, 0))],
        out_shape=[jax.ShapeDtypeStruct((n, d), BF16), jax.ShapeDtypeStruct((n, m), F32)],
        compiler_params=_params("parallel"),
        name="rmsnorm_dt",
    )(x2d, g.reshape(1, d), w_dt)


QK_SUB = 256


def _proj_qk_kernel(a_ref, w_ref, g_ref, o_ref, *, heads_per_tile):
    a = a_ref[...]
    per = QK_SUB // HEAD_DIM
    for c in range(heads_per_tile // per):
        acc = jnp.dot(a, w_ref[:, c * QK_SUB:(c + 1) * QK_SUB], preferred_element_type=F32)
        for hh in range(per):
            h = c * per + hh
            s = acc[:, hh * HEAD_DIM:(hh + 1) * HEAD_DIM]
            ms = jnp.mean(s * s, axis=-1, keepdims=True)
            g = g_ref[:, h * HEAD_DIM:(h + 1) * HEAD_DIM]
            o_ref[h] = (s * lax.rsqrt(ms + EPS) * g).astype(o_ref.dtype)


def _proj_split_kernel(a_ref, w_ref, o_ref, *, width):
    a = a_ref[...]
    per = QK_SUB // width
    for s in range(w_ref.shape[1] // QK_SUB):
        acc = jnp.dot(a, w_ref[:, s * QK_SUB:(s + 1) * QK_SUB], preferred_element_type=F32)
        for c in range(per):
            o_ref[s * per + c] = acc[:, c * width:(c + 1) * width].astype(o_ref.dtype)


def _proj_qk(h, w, gains, tm=1024, tn=2048):
    n, k = h.shape
    m = 2 * ATTN_W
    hpt = tn // HEAD_DIM
    return pl.pallas_call(
        functools.partial(_proj_qk_kernel, heads_per_tile=hpt),
        grid=(m // tn, n // tm),
        in_specs=[pl.BlockSpec((tm, k), lambda j, i: (i, 0)),
                  pl.BlockSpec((k, tn), lambda j, i: (0, j)),
                  pl.BlockSpec((1, tn), lambda j, i: (0, j))],
        out_specs=pl.BlockSpec((hpt, tm, HEAD_DIM), lambda j, i: (j, i, 0)),
        out_shape=jax.ShapeDtypeStruct((m // HEAD_DIM, n, HEAD_DIM), F32),
        compiler_params=_params("parallel", "arbitrary"),
        name="proj_qk",
    )(h, w, gains)


def _proj_split(h, w, col0, m, width, name, out_dtype=BF16, tm=2048, tn=1024):
    n, k = h.shape
    cpt = tn // width
    j0 = col0 // tn
    return pl.pallas_call(
        functools.partial(_proj_split_kernel, width=width),
        grid=(m // tn, n // tm),
        in_specs=[pl.BlockSpec((tm, k), lambda j, i: (i, 0)),
                  pl.BlockSpec((k, tn), lambda j, i: (0, j0 + j))],
        out_specs=pl.BlockSpec((cpt, tm, width), lambda j, i: (j, i, 0)),
        out_shape=jax.ShapeDtypeStruct((m // width, n, width), out_dtype),
        compiler_params=_params("parallel", "arbitrary"),
        name=name,
    )(h, w)


def _t5_bucket(rel):
    nb = NUM_BUCKETS // 2
    max_exact = nb // 2
    ret = (rel > 0).astype(jnp.int32) * nb
    n = jnp.abs(rel)
    nf = jnp.maximum(n, 1).astype(jnp.float32)
    large = max_exact + (jnp.log(nf / max_exact) / math.log(MAX_DISTANCE / max_exact)
                         * (nb - max_exact)).astype(jnp.int32)
    large = jnp.minimum(large, nb - 1)
    return ret + jnp.where(n < max_exact, n, large)


def _bias_rows(rel_bias):
    period = K_WIN + Q_SUB
    m = jnp.arange(period)
    delta = jnp.where(m < K_WIN, m, m - period)
    rows = []
    for d in DILATIONS:
        for off in (0, -HALF_WINDOW, -2 * HALF_WINDOW):
            rel = delta + off
            valid = jnp.abs(rel) <= HALF_WINDOW
            b = rel_bias[_t5_bucket(rel * d)].astype(F32)
            rows.append(jnp.where(valid[:, None], b, NEG_INF))
    return jnp.stack(rows, axis=0).transpose(2, 0, 1)


def _attn_kernel(q_ref, k_ref, v_ref, brow_ref, o_ref, bias_s, tmp_s, xq4, xk4, xv4,
                 q16, k1, k4, k16, v1, v4, v16, acc_s, max_s, den_s, *, seq):
    period = K_WIN + Q_SUB
    n4, n16 = seq // 4, seq // 16

    @pl.when(pl.program_id(1) == 0)
    def _():
        for idx in range(9):
            row = jnp.broadcast_to(brow_ref[idx:idx + 1, :], (Q_SUB, period))
            tile = pltpu.roll(row, 0, 1, stride=1, stride_axis=0)
            d = DILATIONS[idx // 3]
            if d == 16:
                bias_s[idx] = tile[:, :K_WIN]
                continue
            for half in range(K_WIN // LANES):
                tmp_s[half] = tile[:, half * LANES:(half + 1) * LANES]
            groups = 16 // d
            for half in range(K_WIN // LANES):
                for g in range(groups):
                    n = Q_SUB // groups
                    bias_s[idx, g * n:(g + 1) * n, half * LANES:(half + 1) * LANES] = (
                        tmp_s[half, pl.ds(g, n, stride=groups), :])

    cp = 256

    def split4(src, dst):
        for r4 in range(4):
            for t0 in range(0, n4, cp):
                dst[r4, pl.ds(t0, cp), :] = src[pl.ds(r4 + 4 * t0, cp, stride=4), :]

    def split16(src4, dst16, dst4=None):
        for r4 in range(4):
            for a in range(4):
                dst16[4 * a + r4] = src4[r4, pl.ds(a, n16, stride=4), :].astype(BF16)
            if dst4 is not None:
                for t0 in range(0, n4, cp):
                    dst4[r4, pl.ds(t0, cp), :] = src4[r4, pl.ds(t0, cp), :].astype(BF16)

    def cast(src, dst):
        for t0 in range(0, seq, cp):
            dst[pl.ds(t0, cp), :] = src[pl.ds(t0, cp), :].astype(BF16)

    split4(q_ref, xq4)
    split16(xq4, q16)
    for src, x4, d1, d4, d16 in ((k_ref, xk4, k1, k4, k16), (v_ref, xv4, v1, v4, v16)):
        cast(src, d1)
        split4(src, x4)
        split16(x4, d16, d4)

    ones = jnp.ones((K_WIN, LANES), BF16)

    def sub_tile(q, k_s, v_s, length, s0, pi):
        if isinstance(s0, int):
            w0 = min(max(s0 - HALF_WINDOW, 0), length - K_WIN)
            place = 0 if s0 == 0 else (2 if s0 == length - Q_SUB else 1)
        else:
            w0 = pl.multiple_of(jnp.clip(s0 - HALF_WINDOW, 0, length - K_WIN), HALF_WINDOW)
            place = jnp.where(s0 == 0, 0, jnp.where(s0 == length - Q_SUB, 2, 1))
        k = k_s[pl.ds(w0, K_WIN), :]
        v = v_s[pl.ds(w0, K_WIN), :]
        s = lax.dot_general(q, k, (((1,), (1,)), ((), ())), preferred_element_type=F32)
        s = s + bias_s[3 * pi + place]
        m = jnp.max(s, axis=-1, keepdims=True)
        p = jnp.exp(s - m).astype(BF16)
        pv = jnp.dot(p, jnp.concatenate([v, ones], axis=1), preferred_element_type=F32)
        return pv[:, :LANES], jnp.broadcast_to(m, (Q_SUB, LANES)), pv[:, LANES:]

    stats = (acc_s, max_s, den_s)


    res = []
    for l0 in range(0, n16, Q_SUB // 4):
        for r4 in range(4):
            q = jnp.concatenate([q16[4 * a + r4, pl.ds(l0, Q_SUB // 4), :] for a in range(4)], axis=0)
            res.append((sub_tile(q, k4.at[r4], v4.at[r4], n4, 4 * l0, 1), r4, l0))
    for r3, r4, l0 in res:
        n = Q_SUB // 4
        for a in range(4):
            for val, dst in zip(r3, stats):
                dst[0, 4 * a + r4, pl.ds(l0, n), :] = val[a * n:(a + 1) * n]

    res = []
    for r in range(16):
        for l0 in range(0, n16, Q_SUB):
            res.append((sub_tile(q16[r, pl.ds(l0, Q_SUB), :], k16.at[r], v16.at[r], n16, l0, 2), r, l0))
    for r3, r, l0 in res:
        for val, dst in zip(r3, stats):
            dst[1, r, pl.ds(l0, Q_SUB), :] = val

    res = []
    for l0 in range(0, n16, Q_SUB // 16):
        q = jnp.concatenate([xq4[r % 4, pl.ds(4 * l0 + r // 4, Q_SUB // 16, stride=4), :] for r in range(16)],
                            axis=0).astype(BF16)
        res.append((sub_tile(q, k1, v1, seq, 16 * l0, 0), l0))
    for (aa, ma, da), l0 in res:
        n = Q_SUB // 16
        rows = lambda ref, pi: jnp.concatenate([ref[pi, r, pl.ds(l0, n), :] for r in range(16)], axis=0)
        mb, mc = rows(max_s, 0), rows(max_s, 1)
        mx = jnp.maximum(jnp.maximum(ma, mb), mc)
        ea, eb, ec = jnp.exp(ma - mx), jnp.exp(mb - mx), jnp.exp(mc - mx)
        num = ea * aa + eb * rows(acc_s, 0) + ec * rows(acc_s, 1)
        den = ea * da + eb * rows(den_s, 0) + ec * rows(den_s, 1)
        out = num / den
        for r in range(16):
            o_ref[pl.ds(r + 16 * l0, n, stride=16), :] = out[r * n:(r + 1) * n]


def _attention(qk, v, brow, batch, seq):
    H = ATTN_HEADS
    n4, n16 = seq // 4, seq // 16
    qk4 = qk.reshape(2 * H, batch, seq, HEAD_DIM)
    v4 = v.reshape(H, batch, seq, HEAD_DIM)
    in_specs = [pl.BlockSpec((None, None, seq, HEAD_DIM), lambda h, b: (h, b, 0, 0)),
                pl.BlockSpec((None, None, seq, HEAD_DIM), lambda h, b: (H + h, b, 0, 0)),
                pl.BlockSpec((None, None, seq, HEAD_DIM), lambda h, b: (h, b, 0, 0))]
    in_specs.append(pl.BlockSpec((None, 9, K_WIN + Q_SUB), lambda h, b: (h, 0, 0)))
    kv_slabs = [pltpu.VMEM((seq, LANES), BF16), pltpu.VMEM((4, n4, LANES), BF16),
                pltpu.VMEM((16, n16, LANES), BF16)]
    out = pl.pallas_call(
        functools.partial(_attn_kernel, seq=seq),
        grid=(H, batch),
        in_specs=in_specs,
        out_specs=pl.BlockSpec((None, None, seq, HEAD_DIM), lambda h, b: (h, b, 0, 0)),
        out_shape=jax.ShapeDtypeStruct((H, batch, seq, HEAD_DIM), F32),
        scratch_shapes=[pltpu.VMEM((9, Q_SUB, K_WIN), F32),
                        pltpu.VMEM((K_WIN // LANES, Q_SUB, LANES), F32),
                        pltpu.VMEM((4, n4, LANES), F32),
                        pltpu.VMEM((4, n4, LANES), F32),
                        pltpu.VMEM((4, n4, LANES), F32),
                        pltpu.VMEM((16, n16, LANES), BF16)]
                       + kv_slabs + kv_slabs
                       + [pltpu.VMEM((2, 16, n16, LANES), F32)] * 3,
        compiler_params=_params("parallel", "arbitrary"),
        name="dilated_attention",
    )(qk4, qk4, v4, brow)
    return out.reshape(H, batch * seq, HEAD_DIM)


def _ssd_kernel(z_ref, x_ref, b_ref, c_ref, dtr_ref, cwx_ref, cwb_ref, cwc_ref,
                cbx_ref, cbb_ref, cbc_ref, dbr_ref, alr_ref, dsk_ref, ng_ref,
                o_ref, pad, cv, y_s, arg_s, diag_s, rows_s, cols_s, sbs, sf, sb, *, seq):
    T = SSD_CHUNK
    nc = seq // T
    hi = lax.Precision.HIGHEST
    halo = CONV_HALO
    half = T // 2

    pad[:, 0:halo, :] = jnp.zeros((4, halo, LANES), F32)
    pad[:, seq + halo:seq + 2 * halo, :] = jnp.zeros((4, halo, LANES), F32)

    for t0 in range(0, seq, T):
        xin = x_ref[pl.ds(t0, T), :].astype(F32)
        pad[0, pl.ds(t0 + halo, T), :] = xin[:, :LANES]
        pad[1, pl.ds(t0 + halo, T), :] = xin[:, LANES:]
        pad[2, pl.ds(t0 + halo, T), :] = b_ref[pl.ds(t0, T), :].astype(F32)
        pad[3, pl.ds(t0 + halo, T), :] = c_ref[pl.ds(t0, T), :].astype(F32)

    cws = (cwx_ref[:, :LANES], cwx_ref[:, LANES:], cwb_ref[...], cwc_ref[...])
    cbs = (cbx_ref[:, :LANES], cbx_ref[:, LANES:], cbb_ref[...], cbc_ref[...])

    def conv_chunk(ci):
        t0 = ci * T
        for s in range(4):
            ev = [pad[s, pl.ds(t0 + halo + 2 * k, half, stride=2), :] for k in (-1, 0, 1)]
            od = [pad[s, pl.ds(t0 + halo + 1 + 2 * k, half, stride=2), :] for k in (-1, 0, 1)]
            w = [cws[s][j:j + 1, :] for j in range(CONV_WIDTH)]
            out_e = cbs[s] + w[0] * ev[0] + w[1] * od[0] + w[2] * ev[1] + w[3] * od[1] + w[4] * ev[2]
            out_o = cbs[s] + w[0] * od[0] + w[1] * ev[1] + w[2] * od[1] + w[3] * ev[2] + w[4] * od[2]
            cv[s, pl.ds(t0, half, stride=2), :] = out_e * _sigmoid(out_e)
            cv[s, pl.ds(t0 + 1, half, stride=2), :] = out_o * _sigmoid(out_o)

    def load_x(t0):
        return jnp.concatenate([cv[0, pl.ds(t0, T), :], cv[1, pl.ds(t0, T), :]], axis=1)

    nh = HEADS_PER_GROUP
    nd = 2 * nh
    row = lax.broadcasted_iota(jnp.int32, (T, T), 0)
    col = lax.broadcasted_iota(jnp.int32, (T, T), 1)
    triu = (row <= col).astype(F32)
    lower = col < row
    upper = col > row

    dt = _softplus(dtr_ref[...] + dbr_ref[...])
    a = dt * (-jnp.exp(alr_ref[...]))
    cum = jnp.dot(a.reshape(nc * nd, T), triu, precision=hi,
                  preferred_element_type=F32).reshape(nc, nd, T)
    last = cum[:, :, T - 1:T]
    exc = cum - a
    fwd = lax.broadcasted_iota(jnp.int32, (nc, nd, T), 1) < nh
    base = jnp.where(fwd, cum, exc)
    log2e = 1.0 / math.log(2.0)
    log_dt = jnp.log(dt)
    arg_s[...] = jnp.where(fwd, cum - log_dt, exc + log_dt) * log2e
    log_sum = jnp.log(dt[:, 0:nh, :] + dt[:, nh:, :]) * log2e
    diag_s[...] = jnp.concatenate([log_sum, log_sum], axis=1)
    rows_s[:, 0:nd, :] = base * log2e
    rows_s[:, nd:2 * nd, :] = jnp.exp(jnp.where(fwd, cum, last - exc))
    rows_s[:, 2 * nd:3 * nd, :] = dt * jnp.exp(jnp.where(fwd, last - cum, exc))
    rows_s[:, 3 * nd:, :] = jnp.zeros((nc, LANES - 3 * nd, T), F32)

    def expand(cols, first):
        n = cols.shape[0]
        low = lax.broadcasted_iota(jnp.int32, (n, LANES), 1) < SSM_HEAD_DIM
        halves = []
        for j in range(GROUP_W // LANES):
            c0 = jnp.broadcast_to(cols[:, first + 2 * j:first + 2 * j + 1], (n, LANES))
            c1 = jnp.broadcast_to(cols[:, first + 2 * j + 1:first + 2 * j + 2], (n, LANES))
            halves.append(jnp.where(low, c0, c1))
        return jnp.concatenate(halves, axis=1)

    tn_dims = (((0,), (0,)), ((), ()))

    cid = lax.broadcasted_iota(jnp.int32, (LANES, 4 * GROUP_W), 0)
    lid = lax.broadcasted_iota(jnp.int32, (LANES, 4 * GROUP_W), 1)
    blk = lid // GROUP_W
    want = nd + (blk % 2) * nh + (blk // 2) * nd + (lid % GROUP_W) // SSM_HEAD_DIM
    spread = jnp.where(cid == want, 1.0, 0.0).astype(BF16)
    spread_f = spread[:, :3 * GROUP_W]
    spread_b = spread[:, 3 * GROUP_W:]

    sb[...] = jnp.zeros_like(sb)

    def state_back(ci):
        t0 = ci * T
        ct = rows_s[ci].T
        cols_s[ci] = ct
        s_prev = sb[...]
        sbs[ci] = s_prev.astype(BF16)
        wx = jnp.dot(ct.astype(BF16), spread_b, preferred_element_type=F32)
        xs = (load_x(t0) * wx).astype(BF16)
        sb[...] = expand(ct[0:1, :], nd + nh) * s_prev + lax.dot_general(
            cv[2, pl.ds(t0, T), :].astype(BF16), xs, tn_dims, preferred_element_type=F32)

    conv_chunk(nc - 1)
    for k in range(nc - 2, -1, -1):
        conv_chunk(k)
        state_back(k + 1)
    state_back(0)

    sf[...] = jnp.zeros_like(sf)
    low_half = lax.broadcasted_iota(jnp.int32, (T, LANES), 1) < SSM_HEAD_DIM

    def finish(ci):
        t0 = ci * T
        zz = z_ref[pl.ds(t0, T), :].astype(F32)
        y = y_s[pl.ds(t0, T), :] * (zz * _sigmoid(zz))
        ms = jnp.mean(y * y, axis=-1, keepdims=True)
        o_ref[pl.ds(t0, T), :] = (y * lax.rsqrt(ms + EPS) * ng_ref[...]).astype(o_ref.dtype)

    def chunk_fwd(ci):
        t0 = ci * T
        ct = cols_s[ci]
        arg_r = arg_s[ci]
        dt_r = diag_s[ci]
        x = load_x(t0)
        bk = cv[2, pl.ds(t0, T), :].astype(BF16)
        ck = cv[3, pl.ds(t0, T), :].astype(BF16)
        cb = lax.dot_general(ck, bk, (((1,), (1,)), ((), ())), preferred_element_type=F32)
        spreadv = jnp.dot(ct.astype(BF16), spread_f, preferred_element_type=F32)
        lhs = []
        for h in range(nh):
            hb = nh + h
            arg = jnp.where(lower, ct[:, h:h + 1] - arg_r[h:h + 1, :],
                            jnp.where(upper, arg_r[hb:hb + 1, :] - ct[:, hb:hb + 1], dt_r[h:h + 1, :]))
            lhs.append((cb * jnp.exp2(arg)).astype(BF16))
        s_prev = sf[...]
        y4 = jnp.dot(jnp.concatenate(lhs, axis=0), x.astype(BF16), preferred_element_type=F32)
        off = jnp.dot(ck, jnp.concatenate([s_prev.astype(BF16), sbs[ci]], axis=1),
                      preferred_element_type=F32)
        y = x * dsk_ref[...] + jnp.concatenate(
            [jnp.where(low_half, y4[2 * j * T:(2 * j + 1) * T, j * LANES:(j + 1) * LANES],
                       y4[(2 * j + 1) * T:(2 * j + 2) * T, j * LANES:(j + 1) * LANES])
             for j in range(GROUP_W // LANES)], axis=1)
        y = y + spreadv[:, :GROUP_W] * off[:, :GROUP_W] + spreadv[:, GROUP_W:2 * GROUP_W] * off[:, GROUP_W:]
        xs = (x * spreadv[:, 2 * GROUP_W:]).astype(BF16)
        sf[...] = expand(ct[T - 1:T, :], nd) * s_prev + lax.dot_general(
            bk, xs, tn_dims, preferred_element_type=F32)
        y_s[pl.ds(t0, T), :] = y

    chunk_fwd(0)
    for c in range(1, nc):
        chunk_fwd(c)
        finish(c - 1)
    finish(nc - 1)


def _ssd(zx, bcm, dt_row, cwx, cwb, cwc, cbx, cbb, cbc, db_row, al_row, dskip, ng, batch, seq):
    G = SSM_GROUPS
    nc = seq // SSD_CHUNK
    zx4 = zx.reshape(2 * G, batch, seq, GROUP_W)
    bc4 = bcm.reshape(2 * G, batch, seq, SSM_STATE)
    nd = 2 * HEADS_PER_GROUP

    def per_group(shape):
        return pl.BlockSpec((None,) + shape, lambda b, g: (g,) + (0,) * len(shape))

    in_specs = [
        pl.BlockSpec((None, None, seq, GROUP_W), lambda b, g: (g, b, 0, 0)),
        pl.BlockSpec((None, None, seq, GROUP_W), lambda b, g: (G + g, b, 0, 0)),
        pl.BlockSpec((None, None, seq, SSM_STATE), lambda b, g: (g, b, 0, 0)),
        pl.BlockSpec((None, None, seq, SSM_STATE), lambda b, g: (G + g, b, 0, 0)),
        pl.BlockSpec((None, None, nc, nd, SSD_CHUNK), lambda b, g: (b, g, 0, 0, 0)),
        per_group((CONV_WIDTH, GROUP_W)), per_group((CONV_WIDTH, SSM_STATE)),
        per_group((CONV_WIDTH, SSM_STATE)),
        per_group((1, GROUP_W)), per_group((1, SSM_STATE)), per_group((1, SSM_STATE)),
        per_group((nd, 1)), per_group((nd, 1)),
        per_group((1, GROUP_W)), per_group((1, GROUP_W)),
    ]
    out = pl.pallas_call(
        functools.partial(_ssd_kernel, seq=seq),
        grid=(batch, G),
        in_specs=in_specs,
        out_specs=pl.BlockSpec((None, None, seq, GROUP_W), lambda b, g: (g, b, 0, 0)),
        out_shape=jax.ShapeDtypeStruct((G, batch, seq, GROUP_W), BF16),
        scratch_shapes=[pltpu.VMEM((4, seq + 2 * CONV_HALO, LANES), F32),
                        pltpu.VMEM((4, seq, LANES), F32),
                        pltpu.VMEM((seq, GROUP_W), F32),
                        pltpu.VMEM((nc, nd, SSD_CHUNK), F32),
                        pltpu.VMEM((nc, nd, SSD_CHUNK), F32),
                        pltpu.VMEM((nc, LANES, SSD_CHUNK), F32),
                        pltpu.VMEM((nc, SSD_CHUNK, LANES), F32),
                        pltpu.VMEM((nc, SSM_STATE, GROUP_W), BF16),
                        pltpu.VMEM((SSM_STATE, GROUP_W), F32),
                        pltpu.VMEM((SSM_STATE, GROUP_W), F32)],
        compiler_params=_params("parallel", "parallel"),
        name="ssd",
    )(zx4, zx4, bc4, bc4, dt_row, cwx, cwb, cwc, cbx, cbb, cbc, db_row, al_row, dskip, ng)
    return out.reshape(G, batch * seq, GROUP_W)


def _out_proj_kernel(attn_ref, ssd_ref, w_ref, x_ref, o_ref, lhs):
    for h in range(ATTN_HEADS):
        lhs[:, h * HEAD_DIM:(h + 1) * HEAD_DIM] = attn_ref[h].astype(BF16)
    for g in range(SSM_GROUPS):
        lhs[:, ATTN_W + g * GROUP_W:ATTN_W + (g + 1) * GROUP_W] = ssd_ref[g]
    o_ref[...] = x_ref[...] + jnp.dot(lhs[...], w_ref[...], preferred_element_type=F32)


def _out_proj(attn, ssd, w, x2d, tm=512):
    n, d = x2d.shape
    kk = w.shape[0]
    return pl.pallas_call(
        _out_proj_kernel,
        grid=(n // tm,),
        in_specs=[pl.BlockSpec((ATTN_HEADS, tm, HEAD_DIM), lambda i: (0, i, 0)),
                  pl.BlockSpec((SSM_GROUPS, tm, GROUP_W), lambda i: (0, i, 0)),
                  pl.BlockSpec((kk, d), lambda i: (0, 0), pipeline_mode=pl.Buffered(1)),
                  pl.BlockSpec((tm, d), lambda i: (i, 0))],
        out_specs=pl.BlockSpec((tm, d), lambda i: (i, 0)),
        out_shape=jax.ShapeDtypeStruct((n, d), F32),
        scratch_shapes=[pltpu.VMEM((tm, kk), BF16)],
        compiler_params=_params("parallel"),
        name="out_proj",
    )(attn, ssd, w, x2d)


def _mlp_kernel(x_ref, g_ref, wu_ref, wd_ref, o_ref, hm):
    def ffn():
        u = jnp.maximum(jnp.dot(hm[...], wu_ref[...], preferred_element_type=F32), 0.0)
        return jnp.dot((u * u).astype(BF16), wd_ref[...], preferred_element_type=F32)

    @pl.when(pl.program_id(1) == 0)
    def _():
        x = x_ref[...]
        ms = jnp.mean(x * x, axis=-1, keepdims=True)
        hm[...] = (x * lax.rsqrt(ms + EPS) * g_ref[...]).astype(hm.dtype)
        o_ref[...] = x + ffn()

    @pl.when(pl.program_id(1) > 0)
    def _():
        o_ref[...] += ffn()


def _mlp(x2d, g, wu, wd, tm=512, tf=1024):
    n, d = x2d.shape
    f = wu.shape[1]
    return pl.pallas_call(
        _mlp_kernel,
        grid=(n // tm, f // tf),
        in_specs=[pl.BlockSpec((tm, d), lambda i, j: (i, 0)),
                  pl.BlockSpec((1, d), lambda i, j: (0, 0)),
                  pl.BlockSpec((d, tf), lambda i, j: (0, j)),
                  pl.BlockSpec((tf, d), lambda i, j: (j, 0))],
        out_specs=pl.BlockSpec((tm, d), lambda i, j: (i, 0)),
        out_shape=jax.ShapeDtypeStruct((n, d), F32),
        scratch_shapes=[pltpu.VMEM((tm, d), BF16)],
        compiler_params=_params("parallel", "arbitrary"),
        name="mlp",
    )(x2d, g.reshape(1, d), wu, wd)


def kernel(x, norm_mix_g, w_in, q_norm_g, k_norm_g, rel_bias, conv_w, conv_b, dt_bias, a_log,
           d_skip, ssd_norm_g, w_out, norm_mlp_g, w_up, w_down):
    batch, seq, _ = x.shape
    n = batch * seq
    G, nh = SSM_GROUPS, HEADS_PER_GROUP
    nc = seq // SSD_CHUNK
    o_z = 3 * ATTN_W
    o_bc = o_z + 2 * SSM_W
    o_dt = o_bc + 2 * G * SSM_STATE
    x2d = x.reshape(n, D_MODEL)
    brow = _bias_rows(rel_bias)

    for layer in range(w_in.shape[0]):
        wi = w_in[layer].astype(BF16)
        h, dt_raw = _rmsnorm_dt(x2d, norm_mix_g[layer], wi[:, o_dt:])

        scale = 1.0 / math.sqrt(HEAD_DIM)
        gains = jnp.concatenate([jnp.tile(q_norm_g[layer].astype(F32) * scale, ATTN_HEADS),
                                 jnp.tile(k_norm_g[layer].astype(F32), ATTN_HEADS)]).reshape(1, 2 * ATTN_W)
        qk = _proj_qk(h, wi, gains)
        v = _proj_split(h, wi, 2 * ATTN_W, ATTN_W, HEAD_DIM, "proj_v", out_dtype=F32)
        zx = _proj_split(h, wi, o_z, o_bc - o_z, GROUP_W, "proj_zx")
        bcm = _proj_split(h, wi, o_bc, o_dt - o_bc, SSM_STATE, "proj_bc")

        attn = _attention(qk, v, brow, batch, seq)

        dt_row = (dt_raw.reshape(batch, nc, SSD_CHUNK, 2, G, nh)
                  .transpose(0, 4, 1, 3, 5, 2).reshape(batch, G, nc, 2 * nh, SSD_CHUNK))
        per_dir = lambda t: t.astype(F32).reshape(2, G, nh).transpose(1, 0, 2).reshape(G, 2 * nh)
        db, al = per_dir(dt_bias[layer]), per_dir(a_log[layer])
        cw, cbias = conv_w[layer].astype(F32), conv_b[layer].astype(F32)
        gn = G * SSM_STATE
        grp = lambda t, width: t.reshape(t.shape[0], G, width).transpose(1, 0, 2)
        ssd = _ssd(
            zx, bcm, dt_row,
            grp(cw[:, :SSM_W], GROUP_W), grp(cw[:, SSM_W:SSM_W + gn], SSM_STATE),
            grp(cw[:, SSM_W + gn:], SSM_STATE),
            grp(cbias[None, :SSM_W], GROUP_W), grp(cbias[None, SSM_W:SSM_W + gn], SSM_STATE),
            grp(cbias[None, SSM_W + gn:], SSM_STATE),
            db.reshape(G, 2 * nh, 1), al.reshape(G, 2 * nh, 1),
            jnp.repeat(d_skip[layer].astype(F32), SSM_HEAD_DIM).reshape(G, 1, GROUP_W),
            ssd_norm_g[layer].astype(F32).reshape(G, 1, GROUP_W),
            batch, seq)

        x2d = _out_proj(attn, ssd, w_out[layer].astype(BF16), x2d)
        x2d = _mlp(x2d, norm_mlp_g[layer], w_up[layer].astype(BF16), w_down[layer].astype(BF16))
    return x2d.reshape(batch, seq, D_MODEL)
```

```python
import functools
import math

import jax
import jax.numpy as jnp
from jax import lax
from jax.experimental import pallas as pl
from jax.experimental.pallas import tpu as pltpu

D_MODEL = 2048
ATTN_HEADS = 16
HEAD_DIM = 128
ATTN_W = ATTN_HEADS * HEAD_DIM
SSM_HEADS = 32
SSM_HEAD_DIM = 64
SSM_W = SSM_HEADS * SSM_HEAD_DIM
SSM_GROUPS = 8
HEADS_PER_GROUP = SSM_HEADS // SSM_GROUPS
GROUP_W = SSM_W // SSM_GROUPS
SSM_STATE = 128
CONV_WIDTH = 5
DILATIONS = (1, 4, 16)
HALF_WINDOW = 64
NUM_BUCKETS = 32
MAX_DISTANCE = 1024
NEG_INF = -1e30
EPS = 1e-6

LANES = 128
Q_SUB = 128
K_WIN = 256
SSD_CHUNK = 128
CONV_HALO = 8
VMEM_LIMIT = 56 * 1024 * 1024

F32 = jnp.float32
BF16 = jnp.bfloat16


def _params(*sem):
    return pltpu.CompilerParams(dimension_semantics=sem, vmem_limit_bytes=VMEM_LIMIT)


def _sigmoid(x):
    return 1.0 / (1.0 + jnp.exp2(x * (-1.0 / math.log(2.0))))


def _softplus(x):
    return jnp.maximum(x, 0.0) + jnp.log1p(jnp.exp(-jnp.abs(x)))


def _rmsnorm_dt_kernel(x_ref, g_ref, w_ref, h_ref, dt_ref):
    x = x_ref[...]
    ms = jnp.mean(x * x, axis=-1, keepdims=True)
    h = (x * lax.rsqrt(ms + EPS) * g_ref[...]).astype(h_ref.dtype)
    h_ref[...] = h
    dt_ref[...] = jnp.dot(h, w_ref[...], preferred_element_type=F32)


def _rmsnorm_dt(x2d, g, w_dt, tm=1024):
    n, d = x2d.shape
    m = w_dt.shape[1]
    return pl.pallas_call(
        _rmsnorm_dt_kernel,
        grid=(n // tm,),
        in_specs=[pl.BlockSpec((tm, d), lambda i: (i, 0)),
                  pl.BlockSpec((1, d), lambda i: (0, 0)),
                  pl.BlockSpec((d, m), lambda i: (0, 0))],
        out_specs=[pl.BlockSpec((tm, d), lambda i: (i, 0)),
                   pl.BlockSpec((tm, m), lambda i: (i, 0))],
        out_shape=[jax.ShapeDtypeStruct((n, d), BF16), jax.ShapeDtypeStruct((n, m), F32)],
        compiler_params=_params("parallel"),
        name="rmsnorm_dt",
    )(x2d, g.reshape(1, d), w_dt)


QK_SUB = 256


def _proj_qk_kernel(a_ref, w_ref, g_ref, o_ref, *, heads_per_tile):
    a = a_ref[...]
    per = QK_SUB // HEAD_DIM
    for c in range(heads_per_tile // per):
        acc = jnp.dot(a, w_ref[:, c * QK_SUB:(c + 1) * QK_SUB], preferred_element_type=F32)
        for hh in range(per):
            h = c * per + hh
            s = acc[:, hh * HEAD_DIM:(hh + 1) * HEAD_DIM]
            ms = jnp.mean(s * s, axis=-1, keepdims=True)
            g = g_ref[:, h * HEAD_DIM:(h + 1) * HEAD_DIM]
            o_ref[h] = (s * lax.rsqrt(ms + EPS) * g).astype(o_ref.dtype)


def _proj_split_kernel(a_ref, w_ref, o_ref, *, width):
    a = a_ref[...]
    per = QK_SUB // width
    for s in range(w_ref.shape[1] // QK_SUB):
        acc = jnp.dot(a, w_ref[:, s * QK_SUB:(s + 1) * QK_SUB], preferred_element_type=F32)
        for c in range(per):
            o_ref[s * per + c] = acc[:, c * width:(c + 1) * width].astype(o_ref.dtype)


def _proj_qk(h, w, gains, tm=1024, tn=2048):
    n, k = h.shape
    m = 2 * ATTN_W
    hpt = tn // HEAD_DIM
    return pl.pallas_call(
        functools.partial(_proj_qk_kernel, heads_per_tile=hpt),
        grid=(m // tn, n // tm),
        in_specs=[pl.BlockSpec((tm, k), lambda j, i: (i, 0)),
                  pl.BlockSpec((k, tn), lambda j, i: (0, j)),
                  pl.BlockSpec((1, tn), lambda j, i: (0, j))],
        out_specs=pl.BlockSpec((hpt, tm, HEAD_DIM), lambda j, i: (j, i, 0)),
        out_shape=jax.ShapeDtypeStruct((m // HEAD_DIM, n, HEAD_DIM), F32),
        compiler_params=_params("parallel", "arbitrary"),
        name="proj_qk",
    )(h, w, gains)


def _proj_split(h, w, col0, m, width, name, out_dtype=BF16, tm=2048, tn=1024):
    n, k = h.shape
    cpt = tn // width
    j0 = col0 // tn
    return pl.pallas_call(
        functools.partial(_proj_split_kernel, width=width),
        grid=(m // tn, n // tm),
        in_specs=[pl.BlockSpec((tm, k), lambda j, i: (i, 0)),
                  pl.BlockSpec((k, tn), lambda j, i: (0, j0 + j))],
        out_specs=pl.BlockSpec((cpt, tm, width), lambda j, i: (j, i, 0)),
        out_shape=jax.ShapeDtypeStruct((m // width, n, width), out_dtype),
        compiler_params=_params("parallel", "arbitrary"),
        name=name,
    )(h, w)


def _t5_bucket(rel):
    nb = NUM_BUCKETS // 2
    max_exact = nb // 2
    ret = (rel > 0).astype(jnp.int32) * nb
    n = jnp.abs(rel)
    nf = jnp.maximum(n, 1).astype(jnp.float32)
    large = max_exact + (jnp.log(nf / max_exact) / math.log(MAX_DISTANCE / max_exact)
                         * (nb - max_exact)).astype(jnp.int32)
    large = jnp.minimum(large, nb - 1)
    return ret + jnp.where(n < max_exact, n, large)


def _bias_rows(rel_bias):
    period = K_WIN + Q_SUB
    m = jnp.arange(period)
    delta = jnp.where(m < K_WIN, m, m - period)
    rows = []
    for d in DILATIONS:
        for off in (0, -HALF_WINDOW, -2 * HALF_WINDOW):
            rel = delta + off
            valid = jnp.abs(rel) <= HALF_WINDOW
            b = rel_bias[_t5_bucket(rel * d)].astype(F32)
            rows.append(jnp.where(valid[:, None], b, NEG_INF))
    return jnp.stack(rows, axis=0).transpose(2, 0, 1)


def _attn_kernel(q_ref, k_ref, v_ref, brow_ref, o_ref, bias_s, tmp_s, xq4, xk4, xv4,
                 q16, k1, k4, k16, v1, v4, v16, acc_s, max_s, den_s, *, seq):
    period = K_WIN + Q_SUB
    n4, n16 = seq // 4, seq // 16

    @pl.when(pl.program_id(1) == 0)
    def _():
        for idx in range(9):
            row = jnp.broadcast_to(brow_ref[idx:idx + 1, :], (Q_SUB, period))
            tile = pltpu.roll(row, 0, 1, stride=1, stride_axis=0)
            d = DILATIONS[idx // 3]
            if d == 16:
                bias_s[idx] = tile[:, :K_WIN]
                continue
            for half in range(K_WIN // LANES):
                tmp_s[half] = tile[:, half * LANES:(half + 1) * LANES]
            groups = 16 // d
            for half in range(K_WIN // LANES):
                for g in range(groups):
                    n = Q_SUB // groups
                    bias_s[idx, g * n:(g + 1) * n, half * LANES:(half + 1) * LANES] = (
                        tmp_s[half, pl.ds(g, n, stride=groups), :])

    cp = 256

    def split4(src, dst):
        for r4 in range(4):
            for t0 in range(0, n4, cp):
                dst[r4, pl.ds(t0, cp), :] = src[pl.ds(r4 + 4 * t0, cp, stride=4), :]

    def split16(src4, dst16, dst4=None):
        for r4 in range(4):
            for a in range(4):
                dst16[4 * a + r4] = src4[r4, pl.ds(a, n16, stride=4), :].astype(BF16)
            if dst4 is not None:
                for t0 in range(0, n4, cp):
                    dst4[r4, pl.ds(t0, cp), :] = src4[r4, pl.ds(t0, cp), :].astype(BF16)

    def cast(src, dst):
        for t0 in range(0, seq, cp):
            dst[pl.ds(t0, cp), :] = src[pl.ds(t0, cp), :].astype(BF16)

    split4(q_ref, xq4)
    split16(xq4, q16)
    for src, x4, d1, d4, d16 in ((k_ref, xk4, k1, k4, k16), (v_ref, xv4, v1, v4, v16)):
        cast(src, d1)
        split4(src, x4)
        split16(x4, d16, d4)

    ones = jnp.ones((K_WIN, LANES), BF16)

    def sub_tile(q, k_s, v_s, length, s0, pi):
        if isinstance(s0, int):
            w0 = min(max(s0 - HALF_WINDOW, 0), length - K_WIN)
            place = 0 if s0 == 0 else (2 if s0 == length - Q_SUB else 1)
        else:
            w0 = pl.multiple_of(jnp.clip(s0 - HALF_WINDOW, 0, length - K_WIN), HALF_WINDOW)
            place = jnp.where(s0 == 0, 0, jnp.where(s0 == length - Q_SUB, 2, 1))
        k = k_s[pl.ds(w0, K_WIN), :]
        v = v_s[pl.ds(w0, K_WIN), :]
        s = lax.dot_general(q, k, (((1,), (1,)), ((), ())), preferred_element_type=F32)
        s = s + bias_s[3 * pi + place]
        m = jnp.max(s, axis=-1, keepdims=True)
        p = jnp.exp(s - m).astype(BF16)
        pv = jnp.dot(p, jnp.concatenate([v, ones], axis=1), preferred_element_type=F32)
        return pv[:, :LANES], jnp.broadcast_to(m, (Q_SUB, LANES)), pv[:, LANES:]

    stats = (acc_s, max_s, den_s)


    res = []
    for l0 in range(0, n16, Q_SUB // 4):
        for r4 in range(4):
            q = jnp.concatenate([q16[4 * a + r4, pl.ds(l0, Q_SUB // 4), :] for a in range(4)], axis=0)
            res.append((sub_tile(q, k4.at[r4], v4.at[r4], n4, 4 * l0, 1), r4, l0))
    for r3, r4, l0 in res:
        n = Q_SUB // 4
        for a in range(4):
            for val, dst in zip(r3, stats):
                dst[0, 4 * a + r4, pl.ds(l0, n), :] = val[a * n:(a + 1) * n]

    res = []
    for r in range(16):
        for l0 in range(0, n16, Q_SUB):
            res.append((sub_tile(q16[r, pl.ds(l0, Q_SUB), :], k16.at[r], v16.at[r], n16, l0, 2), r, l0))
    for r3, r, l0 in res:
        for val, dst in zip(r3, stats):
            dst[1, r, pl.ds(l0, Q_SUB), :] = val

    res = []
    for l0 in range(0, n16, Q_SUB // 16):
        q = jnp.concatenate([xq4[r % 4, pl.ds(4 * l0 + r // 4, Q_SUB // 16, stride=4), :] for r in range(16)],
                            axis=0).astype(BF16)
        res.append((sub_tile(q, k1, v1, seq, 16 * l0, 0), l0))
    for (aa, ma, da), l0 in res:
        n = Q_SUB // 16
        rows = lambda ref, pi: jnp.concatenate([ref[pi, r, pl.ds(l0, n), :] for r in range(16)], axis=0)
        mb, mc = rows(max_s, 0), rows(max_s, 1)
        mx = jnp.maximum(jnp.maximum(ma, mb), mc)
        ea, eb, ec = jnp.exp(ma - mx), jnp.exp(mb - mx), jnp.exp(mc - mx)
        num = ea * aa + eb * rows(acc_s, 0) + ec * rows(acc_s, 1)
        den = ea * da + eb * rows(den_s, 0) + ec * rows(den_s, 1)
        out = num / den
        for r in range(16):
            o_ref[pl.ds(r + 16 * l0, n, stride=16), :] = out[r * n:(r + 1) * n]


def _attention(qk, v, brow, batch, seq):
    H = ATTN_HEADS
    n4, n16 = seq // 4, seq // 16
    qk4 = qk.reshape(2 * H, batch, seq, HEAD_DIM)
    v4 = v.reshape(H, batch, seq, HEAD_DIM)
    in_specs = [pl.BlockSpec((None, None, seq, HEAD_DIM), lambda h, b: (h, b, 0, 0)),
                pl.BlockSpec((None, None, seq, HEAD_DIM), lambda h, b: (H + h, b, 0, 0)),
                pl.BlockSpec((None, None, seq, HEAD_DIM), lambda h, b: (h, b, 0, 0))]
    in_specs.append(pl.BlockSpec((None, 9, K_WIN + Q_SUB), lambda h, b: (h, 0, 0)))
    kv_slabs = [pltpu.VMEM((seq, LANES), BF16), pltpu.VMEM((4, n4, LANES), BF16),
                pltpu.VMEM((16, n16, LANES), BF16)]
    out = pl.pallas_call(
        functools.partial(_attn_kernel, seq=seq),
        grid=(H, batch),
        in_specs=in_specs,
        out_specs=pl.BlockSpec((None, None, seq, HEAD_DIM), lambda h, b: (h, b, 0, 0)),
        out_shape=jax.ShapeDtypeStruct((H, batch, seq, HEAD_DIM), F32),
        scratch_shapes=[pltpu.VMEM((9, Q_SUB, K_WIN), F32),
                        pltpu.VMEM((K_WIN // LANES, Q_SUB, LANES), F32),
                        pltpu.VMEM((4, n4, LANES), F32),
                        pltpu.VMEM((4, n4, LANES), F32),
                        pltpu.VMEM((4, n4, LANES), F32),
                        pltpu.VMEM((16, n16, LANES), BF16)]
                       + kv_slabs + kv_slabs
                       + [pltpu.VMEM((2, 16, n16, LANES), F32)] * 3,
        compiler_params=_params("parallel", "arbitrary"),
        name="dilated_attention",
    )(qk4, qk4, v4, brow)
    return out.reshape(H, batch * seq, HEAD_DIM)


def _ssd_kernel(z_ref, x_ref, b_ref, c_ref, dtr_ref, cwx_ref, cwb_ref, cwc_ref,
                cbx_ref, cbb_ref, cbc_ref, dbr_ref, alr_ref, dsk_ref, ng_ref,
                o_ref, pad, cv, y_s, arg_s, diag_s, rows_s, cols_s, sbs, sf, sb, *, seq):
    T = SSD_CHUNK
    nc = seq // T
    hi = lax.Precision.HIGHEST
    halo = CONV_HALO
    half = T // 2

    pad[:, 0:halo, :] = jnp.zeros((4, halo, LANES), F32)
    pad[:, seq + halo:seq + 2 * halo, :] = jnp.zeros((4, halo, LANES), F32)

    for t0 in range(0, seq, T):
        xin = x_ref[pl.ds(t0, T), :].astype(F32)
        pad[0, pl.ds(t0 + halo, T), :] = xin[:, :LANES]
        pad[1, pl.ds(t0 + halo, T), :] = xin[:, LANES:]
        pad[2, pl.ds(t0 + halo, T), :] = b_ref[pl.ds(t0, T), :].astype(F32)
        pad[3, pl.ds(t0 + halo, T), :] = c_ref[pl.ds(t0, T), :].astype(F32)

    cws = (cwx_ref[:, :LANES], cwx_ref[:, LANES:], cwb_ref[...], cwc_ref[...])
    cbs = (cbx_ref[:, :LANES], cbx_ref[:, LANES:], cbb_ref[...], cbc_ref[...])

    def conv_chunk(ci):
        t0 = ci * T
        for s in range(4):
            ev = [pad[s, pl.ds(t0 + halo + 2 * k, half, stride=2), :] for k in (-1, 0, 1)]
            od = [pad[s, pl.ds(t0 + halo + 1 + 2 * k, half, stride=2), :] for k in (-1, 0, 1)]
            w = [cws[s][j:j + 1, :] for j in range(CONV_WIDTH)]
            out_e = cbs[s] + w[0] * ev[0] + w[1] * od[0] + w[2] * ev[1] + w[3] * od[1] + w[4] * ev[2]
            out_o = cbs[s] + w[0] * od[0] + w[1] * ev[1] + w[2] * od[1] + w[3] * ev[2] + w[4] * od[2]
            cv[s, pl.ds(t0, half, stride=2), :] = out_e * _sigmoid(out_e)
            cv[s, pl.ds(t0 + 1, half, stride=2), :] = out_o * _sigmoid(out_o)

    def load_x(t0):
        return jnp.concatenate([cv[0, pl.ds(t0, T), :], cv[1, pl.ds(t0, T), :]], axis=1)

    nh = HEADS_PER_GROUP
    nd = 2 * nh
    row = lax.broadcasted_iota(jnp.int32, (T, T), 0)
    col = lax.broadcasted_iota(jnp.int32, (T, T), 1)
    triu = (row <= col).astype(F32)
    lower = col < row
    upper = col > row

    dt = _softplus(dtr_ref[...] + dbr_ref[...])
    a = dt * (-jnp.exp(alr_ref[...]))
    cum = jnp.dot(a.reshape(nc * nd, T), triu, precision=hi,
                  preferred_element_type=F32).reshape(nc, nd, T)
    last = cum[:, :, T - 1:T]
    exc = cum - a
    fwd = lax.broadcasted_iota(jnp.int32, (nc, nd, T), 1) < nh
    base = jnp.where(fwd, cum, exc)
    log2e = 1.0 / math.log(2.0)
    log_dt = jnp.log(dt)
    arg_s[...] = jnp.where(fwd, cum - log_dt, exc + log_dt) * log2e
    log_sum = jnp.log(dt[:, 0:nh, :] + dt[:, nh:, :]) * log2e
    diag_s[...] = jnp.concatenate([log_sum, log_sum], axis=1)
    rows_s[:, 0:nd, :] = base * log2e
    rows_s[:, nd:2 * nd, :] = jnp.exp(jnp.where(fwd, cum, last - exc))
    rows_s[:, 2 * nd:3 * nd, :] = dt * jnp.exp(jnp.where(fwd, last - cum, exc))
    rows_s[:, 3 * nd:, :] = jnp.zeros((nc, LANES - 3 * nd, T), F32)

    def expand(cols, first):
        n = cols.shape[0]
        low = lax.broadcasted_iota(jnp.int32, (n, LANES), 1) < SSM_HEAD_DIM
        halves = []
        for j in range(GROUP_W // LANES):
            c0 = jnp.broadcast_to(cols[:, first + 2 * j:first + 2 * j + 1], (n, LANES))
            c1 = jnp.broadcast_to(cols[:, first + 2 * j + 1:first + 2 * j + 2], (n, LANES))
            halves.append(jnp.where(low, c0, c1))
        return jnp.concatenate(halves, axis=1)

    tn_dims = (((0,), (0,)), ((), ()))

    cid = lax.broadcasted_iota(jnp.int32, (LANES, 4 * GROUP_W), 0)
    lid = lax.broadcasted_iota(jnp.int32, (LANES, 4 * GROUP_W), 1)
    blk = lid // GROUP_W
    want = nd + (blk % 2) * nh + (blk // 2) * nd + (lid % GROUP_W) // SSM_HEAD_DIM
    spread = jnp.where(cid == want, 1.0, 0.0).astype(BF16)
    spread_f = spread[:, :3 * GROUP_W]
    spread_b = spread[:, 3 * GROUP_W:]

    sb[...] = jnp.zeros_like(sb)

    def state_back(ci):
        t0 = ci * T
        ct = rows_s[ci].T
        cols_s[ci] = ct
        s_prev = sb[...]
        sbs[ci] = s_prev.astype(BF16)
        wx = jnp.dot(ct.astype(BF16), spread_b, preferred_element_type=F32)
        xs = (load_x(t0) * wx).astype(BF16)
        sb[...] = expand(ct[0:1, :], nd + nh) * s_prev + lax.dot_general(
            cv[2, pl.ds(t0, T), :].astype(BF16), xs, tn_dims, preferred_element_type=F32)

    conv_chunk(nc - 1)
    for k in range(nc - 2, -1, -1):
        conv_chunk(k)
        state_back(k + 1)
    state_back(0)

    sf[...] = jnp.zeros_like(sf)
    low_half = lax.broadcasted_iota(jnp.int32, (T, LANES), 1) < SSM_HEAD_DIM

    def finish(ci):
        t0 = ci * T
        zz = z_ref[pl.ds(t0, T), :].astype(F32)
        y = y_s[pl.ds(t0, T), :] * (zz * _sigmoid(zz))
        ms = jnp.mean(y * y, axis=-1, keepdims=True)
        o_ref[pl.ds(t0, T), :] = (y * lax.rsqrt(ms + EPS) * ng_ref[...]).astype(o_ref.dtype)

    def chunk_fwd(ci):
        t0 = ci * T
        ct = cols_s[ci]
        arg_r = arg_s[ci]
        dt_r = diag_s[ci]
        x = load_x(t0)
        bk = cv[2, pl.ds(t0, T), :].astype(BF16)
        ck = cv[3, pl.ds(t0, T), :].astype(BF16)
        cb = lax.dot_general(ck, bk, (((1,), (1,)), ((), ())), preferred_element_type=F32)
        spreadv = jnp.dot(ct.astype(BF16), spread_f, preferred_element_type=F32)
        lhs = []
        for h in range(nh):
            hb = nh + h
            arg = jnp.where(lower, ct[:, h:h + 1] - arg_r[h:h + 1, :],
                            jnp.where(upper, arg_r[hb:hb + 1, :] - ct[:, hb:hb + 1], dt_r[h:h + 1, :]))
            lhs.append((cb * jnp.exp2(arg)).astype(BF16))
        s_prev = sf[...]
        y4 = jnp.dot(jnp.concatenate(lhs, axis=0), x.astype(BF16), preferred_element_type=F32)
        off = jnp.dot(ck, jnp.concatenate([s_prev.astype(BF16), sbs[ci]], axis=1),
                      preferred_element_type=F32)
        y = x * dsk_ref[...] + jnp.concatenate(
            [jnp.where(low_half, y4[2 * j * T:(2 * j + 1) * T, j * LANES:(j + 1) * LANES],
                       y4[(2 * j + 1) * T:(2 * j + 2) * T, j * LANES:(j + 1) * LANES])
             for j in range(GROUP_W // LANES)], axis=1)
        y = y + spreadv[:, :GROUP_W] * off[:, :GROUP_W] + spreadv[:, GROUP_W:2 * GROUP_W] * off[:, GROUP_W:]
        xs = (x * spreadv[:, 2 * GROUP_W:]).astype(BF16)
        sf[...] = expand(ct[T - 1:T, :], nd) * s_prev + lax.dot_general(
            bk, xs, tn_dims, preferred_element_type=F32)
        y_s[pl.ds(t0, T), :] = y

    chunk_fwd(0)
    for c in range(1, nc):
        chunk_fwd(c)
        finish(c - 1)
    finish(nc - 1)


def _ssd(zx, bcm, dt_row, cwx, cwb, cwc, cbx, cbb, cbc, db_row, al_row, dskip, ng, batch, seq):
    G = SSM_GROUPS
    nc = seq // SSD_CHUNK
    zx4 = zx.reshape(3 * G, batch, seq, GROUP_W)
    bc4 = zx4
    per_slab = GROUP_W // SSM_STATE
    nd = 2 * HEADS_PER_GROUP

    def per_group(shape):
        return pl.BlockSpec((None,) + shape, lambda b, g: (g,) + (0,) * len(shape))

    in_specs = [
        pl.BlockSpec((None, None, seq, GROUP_W), lambda b, g: (g, b, 0, 0)),
        pl.BlockSpec((None, None, seq, GROUP_W), lambda b, g: (G + g, b, 0, 0)),
        pl.BlockSpec((None, None, seq, SSM_STATE),
                     lambda b, g: (2 * G + g // per_slab, b, 0, g % per_slab)),
        pl.BlockSpec((None, None, seq, SSM_STATE),
                     lambda b, g: (2 * G + G // per_slab + g // per_slab, b, 0, g % per_slab)),
        pl.BlockSpec((None, None, nc, nd, SSD_CHUNK), lambda b, g: (b, g, 0, 0, 0)),
        per_group((CONV_WIDTH, GROUP_W)), per_group((CONV_WIDTH, SSM_STATE)),
        per_group((CONV_WIDTH, SSM_STATE)),
        per_group((1, GROUP_W)), per_group((1, SSM_STATE)), per_group((1, SSM_STATE)),
        per_group((nd, 1)), per_group((nd, 1)),
        per_group((1, GROUP_W)), per_group((1, GROUP_W)),
    ]
    out = pl.pallas_call(
        functools.partial(_ssd_kernel, seq=seq),
        grid=(batch, G),
        in_specs=in_specs,
        out_specs=pl.BlockSpec((None, None, seq, GROUP_W), lambda b, g: (g, b, 0, 0)),
        out_shape=jax.ShapeDtypeStruct((G, batch, seq, GROUP_W), BF16),
        scratch_shapes=[pltpu.VMEM((4, seq + 2 * CONV_HALO, LANES), F32),
                        pltpu.VMEM((4, seq, LANES), F32),
                        pltpu.VMEM((seq, GROUP_W), F32),
                        pltpu.VMEM((nc, nd, SSD_CHUNK), F32),
                        pltpu.VMEM((nc, nd, SSD_CHUNK), F32),
                        pltpu.VMEM((nc, LANES, SSD_CHUNK), F32),
                        pltpu.VMEM((nc, SSD_CHUNK, LANES), F32),
                        pltpu.VMEM((nc, SSM_STATE, GROUP_W), BF16),
                        pltpu.VMEM((SSM_STATE, GROUP_W), F32),
                        pltpu.VMEM((SSM_STATE, GROUP_W), F32)],
        compiler_params=_params("parallel", "parallel"),
        name="ssd",
    )(zx4, zx4, bc4, bc4, dt_row, cwx, cwb, cwc, cbx, cbb, cbc, db_row, al_row, dskip, ng)
    return out.reshape(G, batch * seq, GROUP_W)


def _out_proj_kernel(attn_ref, ssd_ref, w_ref, x_ref, o_ref, lhs):
    for h in range(ATTN_HEADS):
        lhs[:, h * HEAD_DIM:(h + 1) * HEAD_DIM] = attn_ref[h].astype(BF16)
    for g in range(SSM_GROUPS):
        lhs[:, ATTN_W + g * GROUP_W:ATTN_W + (g + 1) * GROUP_W] = ssd_ref[g]
    o_ref[...] = x_ref[...] + jnp.dot(lhs[...], w_ref[...], preferred_element_type=F32)


def _out_proj(attn, ssd, w, x2d, tm=512):
    n, d = x2d.shape
    kk = w.shape[0]
    return pl.pallas_call(
        _out_proj_kernel,
        grid=(n // tm,),
        in_specs=[pl.BlockSpec((ATTN_HEADS, tm, HEAD_DIM), lambda i: (0, i, 0)),
                  pl.BlockSpec((SSM_GROUPS, tm, GROUP_W), lambda i: (0, i, 0)),
                  pl.BlockSpec((kk, d), lambda i: (0, 0), pipeline_mode=pl.Buffered(1)),
                  pl.BlockSpec((tm, d), lambda i: (i, 0))],
        out_specs=pl.BlockSpec((tm, d), lambda i: (i, 0)),
        out_shape=jax.ShapeDtypeStruct((n, d), F32),
        scratch_shapes=[pltpu.VMEM((tm, kk), BF16)],
        compiler_params=_params("parallel"),
        name="out_proj",
    )(attn, ssd, w, x2d)


def _mlp_kernel(x_ref, g_ref, wu_ref, wd_ref, o_ref, hm):
    def ffn():
        u = jnp.maximum(jnp.dot(hm[...], wu_ref[...], preferred_element_type=F32), 0.0)
        return jnp.dot((u * u).astype(BF16), wd_ref[...], preferred_element_type=F32)

    @pl.when(pl.program_id(1) == 0)
    def _():
        x = x_ref[...]
        ms = jnp.mean(x * x, axis=-1, keepdims=True)
        hm[...] = (x * lax.rsqrt(ms + EPS) * g_ref[...]).astype(hm.dtype)
        o_ref[...] = x + ffn()

    @pl.when(pl.program_id(1) > 0)
    def _():
        o_ref[...] += ffn()


def _mlp(x2d, g, wu, wd, tm=512, tf=1024):
    n, d = x2d.shape
    f = wu.shape[1]
    return pl.pallas_call(
        _mlp_kernel,
        grid=(n // tm, f // tf),
        in_specs=[pl.BlockSpec((tm, d), lambda i, j: (i, 0)),
                  pl.BlockSpec((1, d), lambda i, j: (0, 0)),
                  pl.BlockSpec((d, tf), lambda i, j: (0, j)),
                  pl.BlockSpec((tf, d), lambda i, j: (j, 0))],
        out_specs=pl.BlockSpec((tm, d), lambda i, j: (i, 0)),
        out_shape=jax.ShapeDtypeStruct((n, d), F32),
        scratch_shapes=[pltpu.VMEM((tm, d), BF16)],
        compiler_params=_params("parallel", "arbitrary"),
        name="mlp",
    )(x2d, g.reshape(1, d), wu, wd)


def kernel(x, norm_mix_g, w_in, q_norm_g, k_norm_g, rel_bias, conv_w, conv_b, dt_bias, a_log,
           d_skip, ssd_norm_g, w_out, norm_mlp_g, w_up, w_down):
    batch, seq, _ = x.shape
    n = batch * seq
    G, nh = SSM_GROUPS, HEADS_PER_GROUP
    nc = seq // SSD_CHUNK
    o_z = 3 * ATTN_W
    o_bc = o_z + 2 * SSM_W
    o_dt = o_bc + 2 * G * SSM_STATE
    x2d = x.reshape(n, D_MODEL)
    brow = _bias_rows(rel_bias)

    for layer in range(w_in.shape[0]):
        wi = w_in[layer].astype(BF16)
        h, dt_raw = _rmsnorm_dt(x2d, norm_mix_g[layer], wi[:, o_dt:])

        scale = 1.0 / math.sqrt(HEAD_DIM)
        gains = jnp.concatenate([jnp.tile(q_norm_g[layer].astype(F32) * scale, ATTN_HEADS),
                                 jnp.tile(k_norm_g[layer].astype(F32), ATTN_HEADS)]).reshape(1, 2 * ATTN_W)
        qk = _proj_qk(h, wi, gains)
        v = _proj_split(h, wi, 2 * ATTN_W, ATTN_W, HEAD_DIM, "proj_v", out_dtype=F32)
        zx = _proj_split(h, wi, o_z, o_dt - o_z, GROUP_W, "proj_zxbc")
        bcm = None

        attn = _attention(qk, v, brow, batch, seq)

        dt_row = (dt_raw.reshape(batch, nc, SSD_CHUNK, 2, G, nh)
                  .transpose(0, 4, 1, 3, 5, 2).reshape(batch, G, nc, 2 * nh, SSD_CHUNK))
        per_dir = lambda t: t.astype(F32).reshape(2, G, nh).transpose(1, 0, 2).reshape(G, 2 * nh)
        db, al = per_dir(dt_bias[layer]), per_dir(a_log[layer])
        cw, cbias = conv_w[layer].astype(F32), conv_b[layer].astype(F32)
        gn = G * SSM_STATE
        grp = lambda t, width: t.reshape(t.shape[0], G, width).transpose(1, 0, 2)
        ssd = _ssd(
            zx, bcm, dt_row,
            grp(cw[:, :SSM_W], GROUP_W), grp(cw[:, SSM_W:SSM_W + gn], SSM_STATE),
            grp(cw[:, SSM_W + gn:], SSM_STATE),
            grp(cbias[None, :SSM_W], GROUP_W), grp(cbias[None, SSM_W:SSM_W + gn], SSM_STATE),
            grp(cbias[None, SSM_W + gn:], SSM_STATE),
            db.reshape(G, 2 * nh, 1), al.reshape(G, 2 * nh, 1),
            jnp.repeat(d_skip[layer].astype(F32), SSM_HEAD_DIM).reshape(G, 1, GROUP_W),
            ssd_norm_g[layer].astype(F32).reshape(G, 1, GROUP_W),
            batch, seq)

        x2d = _out_proj(attn, ssd, w_out[layer].astype(BF16), x2d)
        x2d = _mlp(x2d, norm_mlp_g[layer], w_up[layer].astype(BF16), w_down[layer].astype(BF16))
    return x2d.reshape(batch, seq, D_MODEL)
```

```python
import functools
import math

import jax
import jax.numpy as jnp
from jax import lax
from jax.experimental import pallas as pl
from jax.experimental.pallas import tpu as pltpu

D_MODEL = 2048
ATTN_HEADS = 16
HEAD_DIM = 128
ATTN_W = ATTN_HEADS * HEAD_DIM
SSM_HEADS = 32
SSM_HEAD_DIM = 64
SSM_W = SSM_HEADS * SSM_HEAD_DIM
SSM_GROUPS = 8
HEADS_PER_GROUP = SSM_HEADS // SSM_GROUPS
GROUP_W = SSM_W // SSM_GROUPS
SSM_STATE = 128
CONV_WIDTH = 5
DILATIONS = (1, 4, 16)
HALF_WINDOW = 64
NUM_BUCKETS = 32
MAX_DISTANCE = 1024
NEG_INF = -1e30
EPS = 1e-6

LANES = 128
Q_SUB = 128
K_WIN = 256
SSD_CHUNK = 128
CONV_HALO = 8
VMEM_LIMIT = 56 * 1024 * 1024

F32 = jnp.float32
BF16 = jnp.bfloat16


def _params(*sem):
    return pltpu.CompilerParams(dimension_semantics=sem, vmem_limit_bytes=VMEM_LIMIT)


def _sigmoid(x):
    return 1.0 / (1.0 + jnp.exp2(x * (-1.0 / math.log(2.0))))


def _softplus(x):
    return jnp.maximum(x, 0.0) + jnp.log1p(jnp.exp(-jnp.abs(x)))


def _rmsnorm_dt_kernel(x_ref, g_ref, w_ref, h_ref, dt_ref):
    x = x_ref[...]
    ms = jnp.mean(x * x, axis=-1, keepdims=True)
    h = (x * lax.rsqrt(ms + EPS) * g_ref[...]).astype(h_ref.dtype)
    h_ref[...] = h
    dt_ref[...] = jnp.dot(h, w_ref[...], preferred_element_type=F32)


def _rmsnorm_dt(x2d, g, w_dt, tm=1024):
    n, d = x2d.shape
    m = w_dt.shape[1]
    return pl.pallas_call(
        _rmsnorm_dt_kernel,
        grid=(n // tm,),
        in_specs=[pl.BlockSpec((tm, d), lambda i: (i, 0)),
                  pl.BlockSpec((1, d), lambda i: (0, 0)),
                  pl.BlockSpec((d, m), lambda i: (0, 0))],
        out_specs=[pl.BlockSpec((tm, d), lambda i: (i, 0)),
                   pl.BlockSpec((tm, m), lambda i: (i, 0))],
        out_shape=[jax.ShapeDtypeStruct((n, d), BF16), jax.ShapeDtypeStruct((n, m), F32)],
        compiler_params=_params("parallel"),
        name="rmsnorm_dt",
    )(x2d, g.reshape(1, d), w_dt)


QK_SUB = 256


def _proj_qk_kernel(a_ref, w_ref, g_ref, o_ref, *, heads_per_tile, normed_tiles):
    per = QK_SUB // HEAD_DIM

    def slabs(normed):
        a = a_ref[...]
        for c in range(heads_per_tile // per):
            acc = jnp.dot(a, w_ref[:, c * QK_SUB:(c + 1) * QK_SUB], preferred_element_type=F32)
            for hh in range(per):
                h = c * per + hh
                s = acc[:, hh * HEAD_DIM:(hh + 1) * HEAD_DIM]
                if normed:
                    ms = jnp.mean(s * s, axis=-1, keepdims=True)
                    s = s * lax.rsqrt(ms + EPS) * g_ref[:, h * HEAD_DIM:(h + 1) * HEAD_DIM]
                o_ref[h] = s.astype(o_ref.dtype)

    @pl.when(pl.program_id(0) < normed_tiles)
    def _():
        slabs(True)

    @pl.when(pl.program_id(0) >= normed_tiles)
    def _():
        slabs(False)


def _proj_split_kernel(a_ref, w_ref, o_ref, *, width):
    a = a_ref[...]
    per = QK_SUB // width
    for s in range(w_ref.shape[1] // QK_SUB):
        acc = jnp.dot(a, w_ref[:, s * QK_SUB:(s + 1) * QK_SUB], preferred_element_type=F32)
        for c in range(per):
            o_ref[s * per + c] = acc[:, c * width:(c + 1) * width].astype(o_ref.dtype)


def _proj_qk(h, w, gains, tm=1024, tn=2048):
    n, k = h.shape
    m = 3 * ATTN_W
    hpt = tn // HEAD_DIM
    normed = 2 * ATTN_W // tn
    return pl.pallas_call(
        functools.partial(_proj_qk_kernel, heads_per_tile=hpt, normed_tiles=normed),
        grid=(m // tn, n // tm),
        in_specs=[pl.BlockSpec((tm, k), lambda j, i: (i, 0)),
                  pl.BlockSpec((k, tn), lambda j, i: (0, j)),
                  pl.BlockSpec((1, tn), lambda j, i: (0, jnp.minimum(j, normed - 1)))],
        out_specs=pl.BlockSpec((hpt, tm, HEAD_DIM), lambda j, i: (j, i, 0)),
        out_shape=jax.ShapeDtypeStruct((m // HEAD_DIM, n, HEAD_DIM), F32),
        compiler_params=_params("parallel", "arbitrary"),
        name="proj_qkv",
    )(h, w, gains)


def _proj_split(h, w, col0, m, width, name, out_dtype=BF16, tm=2048, tn=1024):
    n, k = h.shape
    cpt = tn // width
    j0 = col0 // tn
    return pl.pallas_call(
        functools.partial(_proj_split_kernel, width=width),
        grid=(m // tn, n // tm),
        in_specs=[pl.BlockSpec((tm, k), lambda j, i: (i, 0)),
                  pl.BlockSpec((k, tn), lambda j, i: (0, j0 + j))],
        out_specs=pl.BlockSpec((cpt, tm, width), lambda j, i: (j, i, 0)),
        out_shape=jax.ShapeDtypeStruct((m // width, n, width), out_dtype),
        compiler_params=_params("parallel", "arbitrary"),
        name=name,
    )(h, w)


def _t5_bucket(rel):
    nb = NUM_BUCKETS // 2
    max_exact = nb // 2
    ret = (rel > 0).astype(jnp.int32) * nb
    n = jnp.abs(rel)
    nf = jnp.maximum(n, 1).astype(jnp.float32)
    large = max_exact + (jnp.log(nf / max_exact) / math.log(MAX_DISTANCE / max_exact)
                         * (nb - max_exact)).astype(jnp.int32)
    large = jnp.minimum(large, nb - 1)
    return ret + jnp.where(n < max_exact, n, large)


def _bias_rows(rel_bias):
    period = K_WIN + Q_SUB
    m = jnp.arange(period)
    delta = jnp.where(m < K_WIN, m, m - period)
    rows = []
    for d in DILATIONS:
        for off in (0, -HALF_WINDOW, -2 * HALF_WINDOW):
            rel = delta + off
            valid = jnp.abs(rel) <= HALF_WINDOW
            b = rel_bias[_t5_bucket(rel * d)].astype(F32)
            rows.append(jnp.where(valid[:, None], b, NEG_INF))
    return jnp.stack(rows, axis=0).transpose(2, 0, 1)


def _attn_kernel(q_ref, k_ref, v_ref, brow_ref, o_ref, bias_s, tmp_s, xq4, xk4, xv4,
                 q16, k1, k4, k16, v1, v4, v16, acc_s, max_s, den_s, *, seq):
    period = K_WIN + Q_SUB
    n4, n16 = seq // 4, seq // 16

    @pl.when(pl.program_id(1) == 0)
    def _():
        for idx in range(9):
            row = jnp.broadcast_to(brow_ref[idx:idx + 1, :], (Q_SUB, period))
            tile = pltpu.roll(row, 0, 1, stride=1, stride_axis=0)
            d = DILATIONS[idx // 3]
            if d == 16:
                bias_s[idx] = tile[:, :K_WIN]
                continue
            for half in range(K_WIN // LANES):
                tmp_s[half] = tile[:, half * LANES:(half + 1) * LANES]
            groups = 16 // d
            for half in range(K_WIN // LANES):
                for g in range(groups):
                    n = Q_SUB // groups
                    bias_s[idx, g * n:(g + 1) * n, half * LANES:(half + 1) * LANES] = (
                        tmp_s[half, pl.ds(g, n, stride=groups), :])

    cp = 256

    def split4(src, dst):
        for r4 in range(4):
            for t0 in range(0, n4, cp):
                dst[r4, pl.ds(t0, cp), :] = src[pl.ds(r4 + 4 * t0, cp, stride=4), :]

    def split16(src4, dst16, dst4=None):
        for r4 in range(4):
            for a in range(4):
                dst16[4 * a + r4] = src4[r4, pl.ds(a, n16, stride=4), :].astype(BF16)
            if dst4 is not None:
                for t0 in range(0, n4, cp):
                    dst4[r4, pl.ds(t0, cp), :] = src4[r4, pl.ds(t0, cp), :].astype(BF16)

    def cast(src, dst):
        for t0 in range(0, seq, cp):
            dst[pl.ds(t0, cp), :] = src[pl.ds(t0, cp), :].astype(BF16)

    split4(q_ref, xq4)
    split16(xq4, q16)
    for src, x4, d1, d4, d16 in ((k_ref, xk4, k1, k4, k16), (v_ref, xv4, v1, v4, v16)):
        cast(src, d1)
        split4(src, x4)
        split16(x4, d16, d4)

    ones = jnp.ones((K_WIN, LANES), BF16)

    def sub_tile(q, k_s, v_s, length, s0, pi):
        if isinstance(s0, int):
            w0 = min(max(s0 - HALF_WINDOW, 0), length - K_WIN)
            place = 0 if s0 == 0 else (2 if s0 == length - Q_SUB else 1)
        else:
            w0 = pl.multiple_of(jnp.clip(s0 - HALF_WINDOW, 0, length - K_WIN), HALF_WINDOW)
            place = jnp.where(s0 == 0, 0, jnp.where(s0 == length - Q_SUB, 2, 1))
        k = k_s[pl.ds(w0, K_WIN), :]
        v = v_s[pl.ds(w0, K_WIN), :]
        s = lax.dot_general(q, k, (((1,), (1,)), ((), ())), preferred_element_type=F32)
        s = s + bias_s[3 * pi + place]
        m = jnp.max(s, axis=-1, keepdims=True)
        p = jnp.exp(s - m).astype(BF16)
        pv = jnp.dot(p, jnp.concatenate([v, ones], axis=1), preferred_element_type=F32)
        return pv[:, :LANES], jnp.broadcast_to(m, (Q_SUB, LANES)), pv[:, LANES:]

    stats = (acc_s, max_s, den_s)


    res = []
    for l0 in range(0, n16, Q_SUB // 4):
        for r4 in range(4):
            q = jnp.concatenate([q16[4 * a + r4, pl.ds(l0, Q_SUB // 4), :] for a in range(4)], axis=0)
            res.append((sub_tile(q, k4.at[r4], v4.at[r4], n4, 4 * l0, 1), r4, l0))
    for r3, r4, l0 in res:
        n = Q_SUB // 4
        for a in range(4):
            for val, dst in zip(r3, stats):
                dst[0, 4 * a + r4, pl.ds(l0, n), :] = val[a * n:(a + 1) * n]

    res = []
    for r in range(16):
        for l0 in range(0, n16, Q_SUB):
            res.append((sub_tile(q16[r, pl.ds(l0, Q_SUB), :], k16.at[r], v16.at[r], n16, l0, 2), r, l0))
    for r3, r, l0 in res:
        for val, dst in zip(r3, stats):
            dst[1, r, pl.ds(l0, Q_SUB), :] = val

    res = []
    for l0 in range(0, n16, Q_SUB // 16):
        q = jnp.concatenate([xq4[r % 4, pl.ds(4 * l0 + r // 4, Q_SUB // 16, stride=4), :] for r in range(16)],
                            axis=0).astype(BF16)
        res.append((sub_tile(q, k1, v1, seq, 16 * l0, 0), l0))
    for (aa, ma, da), l0 in res:
        n = Q_SUB // 16
        rows = lambda ref, pi: jnp.concatenate([ref[pi, r, pl.ds(l0, n), :] for r in range(16)], axis=0)
        mb, mc = rows(max_s, 0), rows(max_s, 1)
        mx = jnp.maximum(jnp.maximum(ma, mb), mc)
        ea, eb, ec = jnp.exp(ma - mx), jnp.exp(mb - mx), jnp.exp(mc - mx)
        num = ea * aa + eb * rows(acc_s, 0) + ec * rows(acc_s, 1)
        den = ea * da + eb * rows(den_s, 0) + ec * rows(den_s, 1)
        out = num / den
        for r in range(16):
            o_ref[pl.ds(r + 16 * l0, n, stride=16), :] = out[r * n:(r + 1) * n]


def _attention(qkv, brow, batch, seq):
    H = ATTN_HEADS
    n4, n16 = seq // 4, seq // 16
    qk4 = qkv.reshape(3 * H, batch, seq, HEAD_DIM)
    v4 = qk4
    in_specs = [pl.BlockSpec((None, None, seq, HEAD_DIM), lambda h, b: (h, b, 0, 0)),
                pl.BlockSpec((None, None, seq, HEAD_DIM), lambda h, b: (H + h, b, 0, 0)),
                pl.BlockSpec((None, None, seq, HEAD_DIM), lambda h, b: (2 * H + h, b, 0, 0))]
    in_specs.append(pl.BlockSpec((None, 9, K_WIN + Q_SUB), lambda h, b: (h, 0, 0)))
    kv_slabs = [pltpu.VMEM((seq, LANES), BF16), pltpu.VMEM((4, n4, LANES), BF16),
                pltpu.VMEM((16, n16, LANES), BF16)]
    out = pl.pallas_call(
        functools.partial(_attn_kernel, seq=seq),
        grid=(H, batch),
        in_specs=in_specs,
        out_specs=pl.BlockSpec((None, None, seq, HEAD_DIM), lambda h, b: (h, b, 0, 0)),
        out_shape=jax.ShapeDtypeStruct((H, batch, seq, HEAD_DIM), F32),
        scratch_shapes=[pltpu.VMEM((9, Q_SUB, K_WIN), F32),
                        pltpu.VMEM((K_WIN // LANES, Q_SUB, LANES), F32),
                        pltpu.VMEM((4, n4, LANES), F32),
                        pltpu.VMEM((4, n4, LANES), F32),
                        pltpu.VMEM((4, n4, LANES), F32),
                        pltpu.VMEM((16, n16, LANES), BF16)]
                       + kv_slabs + kv_slabs
                       + [pltpu.VMEM((2, 16, n16, LANES), F32)] * 3,
        compiler_params=_params("parallel", "arbitrary"),
        name="dilated_attention",
    )(qk4, qk4, v4, brow)
    return out.reshape(H, batch * seq, HEAD_DIM)


def _ssd_kernel(z_ref, x_ref, b_ref, c_ref, dtr_ref, cwx_ref, cwb_ref, cwc_ref,
                cbx_ref, cbb_ref, cbc_ref, dbr_ref, alr_ref, dsk_ref, ng_ref,
                o_ref, pad, cv, y_s, arg_s, diag_s, rows_s, cols_s, sbs, sf, sb, *, seq):
    T = SSD_CHUNK
    nc = seq // T
    hi = lax.Precision.HIGHEST
    halo = CONV_HALO
    half = T // 2

    pad[:, 0:halo, :] = jnp.zeros((4, halo, LANES), F32)
    pad[:, seq + halo:seq + 2 * halo, :] = jnp.zeros((4, halo, LANES), F32)

    for t0 in range(0, seq, T):
        xin = x_ref[pl.ds(t0, T), :].astype(F32)
        pad[0, pl.ds(t0 + halo, T), :] = xin[:, :LANES]
        pad[1, pl.ds(t0 + halo, T), :] = xin[:, LANES:]
        pad[2, pl.ds(t0 + halo, T), :] = b_ref[pl.ds(t0, T), :].astype(F32)
        pad[3, pl.ds(t0 + halo, T), :] = c_ref[pl.ds(t0, T), :].astype(F32)

    cws = (cwx_ref[:, :LANES], cwx_ref[:, LANES:], cwb_ref[...], cwc_ref[...])
    cbs = (cbx_ref[:, :LANES], cbx_ref[:, LANES:], cbb_ref[...], cbc_ref[...])

    def conv_chunk(ci):
        t0 = ci * T
        for s in range(4):
            ev = [pad[s, pl.ds(t0 + halo + 2 * k, half, stride=2), :] for k in (-1, 0, 1)]
            od = [pad[s, pl.ds(t0 + halo + 1 + 2 * k, half, stride=2), :] for k in (-1, 0, 1)]
            w = [cws[s][j:j + 1, :] for j in range(CONV_WIDTH)]
            out_e = cbs[s] + w[0] * ev[0] + w[1] * od[0] + w[2] * ev[1] + w[3] * od[1] + w[4] * ev[2]
            out_o = cbs[s] + w[0] * od[0] + w[1] * ev[1] + w[2] * od[1] + w[3] * ev[2] + w[4] * od[2]
            cv[s, pl.ds(t0, half, stride=2), :] = out_e * _sigmoid(out_e)
            cv[s, pl.ds(t0 + 1, half, stride=2), :] = out_o * _sigmoid(out_o)

    def load_x(t0):
        return jnp.concatenate([cv[0, pl.ds(t0, T), :], cv[1, pl.ds(t0, T), :]], axis=1)

    nh = HEADS_PER_GROUP
    nd = 2 * nh
    row = lax.broadcasted_iota(jnp.int32, (T, T), 0)
    col = lax.broadcasted_iota(jnp.int32, (T, T), 1)
    triu = (row <= col).astype(F32)
    lower = col < row
    upper = col > row

    dt = _softplus(dtr_ref[...] + dbr_ref[...])
    a = dt * (-jnp.exp(alr_ref[...]))
    cum = jnp.dot(a.reshape(nc * nd, T), triu, precision=hi,
                  preferred_element_type=F32).reshape(nc, nd, T)
    last = cum[:, :, T - 1:T]
    exc = cum - a
    fwd = lax.broadcasted_iota(jnp.int32, (nc, nd, T), 1) < nh
    base = jnp.where(fwd, cum, exc)
    log2e = 1.0 / math.log(2.0)
    log_dt = jnp.log(dt)
    arg_s[...] = jnp.where(fwd, cum - log_dt, exc + log_dt) * log2e
    log_sum = jnp.log(dt[:, 0:nh, :] + dt[:, nh:, :]) * log2e
    diag_s[...] = jnp.concatenate([log_sum, log_sum], axis=1)
    rows_s[:, 0:nd, :] = base * log2e
    rows_s[:, nd:2 * nd, :] = jnp.exp(jnp.where(fwd, cum, last - exc))
    rows_s[:, 2 * nd:3 * nd, :] = dt * jnp.exp(jnp.where(fwd, last - cum, exc))
    rows_s[:, 3 * nd:, :] = jnp.zeros((nc, LANES - 3 * nd, T), F32)

    def expand(cols, first):
        n = cols.shape[0]
        low = lax.broadcasted_iota(jnp.int32, (n, LANES), 1) < SSM_HEAD_DIM
        halves = []
        for j in range(GROUP_W // LANES):
            c0 = jnp.broadcast_to(cols[:, first + 2 * j:first + 2 * j + 1], (n, LANES))
            c1 = jnp.broadcast_to(cols[:, first + 2 * j + 1:first + 2 * j + 2], (n, LANES))
            halves.append(jnp.where(low, c0, c1))
        return jnp.concatenate(halves, axis=1)

    tn_dims = (((0,), (0,)), ((), ()))

    cid = lax.broadcasted_iota(jnp.int32, (LANES, 4 * GROUP_W), 0)
    lid = lax.broadcasted_iota(jnp.int32, (LANES, 4 * GROUP_W), 1)
    blk = lid // GROUP_W
    want = nd + (blk % 2) * nh + (blk // 2) * nd + (lid % GROUP_W) // SSM_HEAD_DIM
    spread = jnp.where(cid == want, 1.0, 0.0).astype(BF16)
    spread_f = spread[:, :3 * GROUP_W]
    spread_b = spread[:, 3 * GROUP_W:]

    sb[...] = jnp.zeros_like(sb)

    def state_back(ci):
        t0 = ci * T
        ct = rows_s[ci].T
        cols_s[ci] = ct
        s_prev = sb[...]
        sbs[ci] = s_prev.astype(BF16)
        wx = jnp.dot(ct.astype(BF16), spread_b, preferred_element_type=F32)
        xs = (load_x(t0) * wx).astype(BF16)
        sb[...] = expand(ct[0:1, :], nd + nh) * s_prev + lax.dot_general(
            cv[2, pl.ds(t0, T), :].astype(BF16), xs, tn_dims, preferred_element_type=F32)

    conv_chunk(nc - 1)
    for k in range(nc - 2, -1, -1):
        conv_chunk(k)
        state_back(k + 1)
    state_back(0)

    sf[...] = jnp.zeros_like(sf)
    low_half = lax.broadcasted_iota(jnp.int32, (T, LANES), 1) < SSM_HEAD_DIM

    def finish(ci):
        t0 = ci * T
        zz = z_ref[pl.ds(t0, T), :].astype(F32)
        y = y_s[pl.ds(t0, T), :] * (zz * _sigmoid(zz))
        ms = jnp.mean(y * y, axis=-1, keepdims=True)
        o_ref[pl.ds(t0, T), :] = (y * lax.rsqrt(ms + EPS) * ng_ref[...]).astype(o_ref.dtype)

    def chunk_fwd(ci):
        t0 = ci * T
        ct = cols_s[ci]
        arg_r = arg_s[ci]
        dt_r = diag_s[ci]
        x = load_x(t0)
        bk = cv[2, pl.ds(t0, T), :].astype(BF16)
        ck = cv[3, pl.ds(t0, T), :].astype(BF16)
        cb = lax.dot_general(ck, bk, (((1,), (1,)), ((), ())), preferred_element_type=F32)
        spreadv = jnp.dot(ct.astype(BF16), spread_f, preferred_element_type=F32)
        lhs = []
        for h in range(nh):
            hb = nh + h
            arg = jnp.where(lower, ct[:, h:h + 1] - arg_r[h:h + 1, :],
                            jnp.where(upper, arg_r[hb:hb + 1, :] - ct[:, hb:hb + 1], dt_r[h:h + 1, :]))
            lhs.append((cb * jnp.exp2(arg)).astype(BF16))
        s_prev = sf[...]
        y4 = jnp.dot(jnp.concatenate(lhs, axis=0), x.astype(BF16), preferred_element_type=F32)
        off = jnp.dot(ck, jnp.concatenate([s_prev.astype(BF16), sbs[ci]], axis=1),
                      preferred_element_type=F32)
        y = x * dsk_ref[...] + jnp.concatenate(
            [jnp.where(low_half, y4[2 * j * T:(2 * j + 1) * T, j * LANES:(j + 1) * LANES],
                       y4[(2 * j + 1) * T:(2 * j + 2) * T, j * LANES:(j + 1) * LANES])
             for j in range(GROUP_W // LANES)], axis=1)
        y = y + spreadv[:, :GROUP_W] * off[:, :GROUP_W] + spreadv[:, GROUP_W:2 * GROUP_W] * off[:, GROUP_W:]
        xs = (x * spreadv[:, 2 * GROUP_W:]).astype(BF16)
        sf[...] = expand(ct[T - 1:T, :], nd) * s_prev + lax.dot_general(
            bk, xs, tn_dims, preferred_element_type=F32)
        y_s[pl.ds(t0, T), :] = y

    chunk_fwd(0)
    for c in range(1, nc):
        chunk_fwd(c)
        finish(c - 1)
    finish(nc - 1)


def _ssd(zx, bcm, dt_row, cwx, cwb, cwc, cbx, cbb, cbc, db_row, al_row, dskip, ng, batch, seq):
    G = SSM_GROUPS
    nc = seq // SSD_CHUNK
    zx4 = zx.reshape(3 * G, batch, seq, GROUP_W)
    bc4 = zx4
    per_slab = GROUP_W // SSM_STATE
    nd = 2 * HEADS_PER_GROUP

    def per_group(shape):
        return pl.BlockSpec((None,) + shape, lambda b, g: (g,) + (0,) * len(shape))

    in_specs = [
        pl.BlockSpec((None, None, seq, GROUP_W), lambda b, g: (g, b, 0, 0)),
        pl.BlockSpec((None, None, seq, GROUP_W), lambda b, g: (G + g, b, 0, 0)),
        pl.BlockSpec((None, None, seq, SSM_STATE),
                     lambda b, g: (2 * G + g // per_slab, b, 0, g % per_slab)),
        pl.BlockSpec((None, None, seq, SSM_STATE),
                     lambda b, g: (2 * G + G // per_slab + g // per_slab, b, 0, g % per_slab)),
        pl.BlockSpec((None, None, nc, nd, SSD_CHUNK), lambda b, g: (b, g, 0, 0, 0)),
        per_group((CONV_WIDTH, GROUP_W)), per_group((CONV_WIDTH, SSM_STATE)),
        per_group((CONV_WIDTH, SSM_STATE)),
        per_group((1, GROUP_W)), per_group((1, SSM_STATE)), per_group((1, SSM_STATE)),
        per_group((nd, 1)), per_group((nd, 1)),
        per_group((1, GROUP_W)), per_group((1, GROUP_W)),
    ]
    out = pl.pallas_call(
        functools.partial(_ssd_kernel, seq=seq),
        grid=(batch, G),
        in_specs=in_specs,
        out_specs=pl.BlockSpec((None, None, seq, GROUP_W), lambda b, g: (g, b, 0, 0)),
        out_shape=jax.ShapeDtypeStruct((G, batch, seq, GROUP_W), BF16),
        scratch_shapes=[pltpu.VMEM((4, seq + 2 * CONV_HALO, LANES), F32),
                        pltpu.VMEM((4, seq, LANES), F32),
                        pltpu.VMEM((seq, GROUP_W), F32),
                        pltpu.VMEM((nc, nd, SSD_CHUNK), F32),
                        pltpu.VMEM((nc, nd, SSD_CHUNK), F32),
                        pltpu.VMEM((nc, LANES, SSD_CHUNK), F32),
                        pltpu.VMEM((nc, SSD_CHUNK, LANES), F32),
                        pltpu.VMEM((nc, SSM_STATE, GROUP_W), BF16),
                        pltpu.VMEM((SSM_STATE, GROUP_W), F32),
                        pltpu.VMEM((SSM_STATE, GROUP_W), F32)],
        compiler_params=_params("parallel", "parallel"),
        name="ssd",
    )(zx4, zx4, bc4, bc4, dt_row, cwx, cwb, cwc, cbx, cbb, cbc, db_row, al_row, dskip, ng)
    return out.reshape(G, batch * seq, GROUP_W)


def _out_proj_kernel(attn_ref, ssd_ref, w_ref, x_ref, o_ref, lhs):
    for h in range(ATTN_HEADS):
        lhs[:, h * HEAD_DIM:(h + 1) * HEAD_DIM] = attn_ref[h].astype(BF16)
    for g in range(SSM_GROUPS):
        lhs[:, ATTN_W + g * GROUP_W:ATTN_W + (g + 1) * GROUP_W] = ssd_ref[g]
    o_ref[...] = x_ref[...] + jnp.dot(lhs[...], w_ref[...], preferred_element_type=F32)


def _out_proj(attn, ssd, w, x2d, tm=512):
    n, d = x2d.shape
    kk = w.shape[0]
    return pl.pallas_call(
        _out_proj_kernel,
        grid=(n // tm,),
        in_specs=[pl.BlockSpec((ATTN_HEADS, tm, HEAD_DIM), lambda i: (0, i, 0)),
                  pl.BlockSpec((SSM_GROUPS, tm, GROUP_W), lambda i: (0, i, 0)),
                  pl.BlockSpec((kk, d), lambda i: (0, 0), pipeline_mode=pl.Buffered(1)),
                  pl.BlockSpec((tm, d), lambda i: (i, 0))],
        out_specs=pl.BlockSpec((tm, d), lambda i: (i, 0)),
        out_shape=jax.ShapeDtypeStruct((n, d), F32),
        scratch_shapes=[pltpu.VMEM((tm, kk), BF16)],
        compiler_params=_params("parallel"),
        name="out_proj",
    )(attn, ssd, w, x2d)


def _mlp_kernel(x_ref, g_ref, wu_ref, wd_ref, o_ref, hm):
    def ffn():
        u = jnp.maximum(jnp.dot(hm[...], wu_ref[...], preferred_element_type=F32), 0.0)
        return jnp.dot((u * u).astype(BF16), wd_ref[...], preferred_element_type=F32)

    @pl.when(pl.program_id(1) == 0)
    def _():
        x = x_ref[...]
        ms = jnp.mean(x * x, axis=-1, keepdims=True)
        hm[...] = (x * lax.rsqrt(ms + EPS) * g_ref[...]).astype(hm.dtype)
        o_ref[...] = x + ffn()

    @pl.when(pl.program_id(1) > 0)
    def _():
        o_ref[...] += ffn()


def _mlp(x2d, g, wu, wd, tm=512, tf=1024):
    n, d = x2d.shape
    f = wu.shape[1]
    return pl.pallas_call(
        _mlp_kernel,
        grid=(n // tm, f // tf),
        in_specs=[pl.BlockSpec((tm, d), lambda i, j: (i, 0)),
                  pl.BlockSpec((1, d), lambda i, j: (0, 0)),
                  pl.BlockSpec((d, tf), lambda i, j: (0, j)),
                  pl.BlockSpec((tf, d), lambda i, j: (j, 0))],
        out_specs=pl.BlockSpec((tm, d), lambda i, j: (i, 0)),
        out_shape=jax.ShapeDtypeStruct((n, d), F32),
        scratch_shapes=[pltpu.VMEM((tm, d), BF16)],
        compiler_params=_params("parallel", "arbitrary"),
        name="mlp",
    )(x2d, g.reshape(1, d), wu, wd)


def kernel(x, norm_mix_g, w_in, q_norm_g, k_norm_g, rel_bias, conv_w, conv_b, dt_bias, a_log,
           d_skip, ssd_norm_g, w_out, norm_mlp_g, w_up, w_down):
    batch, seq, _ = x.shape
    n = batch * seq
    G, nh = SSM_GROUPS, HEADS_PER_GROUP
    nc = seq // SSD_CHUNK
    o_z = 3 * ATTN_W
    o_bc = o_z + 2 * SSM_W
    o_dt = o_bc + 2 * G * SSM_STATE
    x2d = x.reshape(n, D_MODEL)
    brow = _bias_rows(rel_bias)

    for layer in range(w_in.shape[0]):
        wi = w_in[layer].astype(BF16)
        h, dt_raw = _rmsnorm_dt(x2d, norm_mix_g[layer], wi[:, o_dt:])

        scale = 1.0 / math.sqrt(HEAD_DIM)
        gains = jnp.concatenate([jnp.tile(q_norm_g[layer].astype(F32) * scale, ATTN_HEADS),
                                 jnp.tile(k_norm_g[layer].astype(F32), ATTN_HEADS)]).reshape(1, 2 * ATTN_W)
        qkv = _proj_qk(h, wi, gains)
        zx = _proj_split(h, wi, o_z, o_dt - o_z, GROUP_W, "proj_zxbc")
        bcm = None

        attn = _attention(qkv, brow, batch, seq)

        dt_row = (dt_raw.reshape(batch, nc, SSD_CHUNK, 2, G, nh)
                  .transpose(0, 4, 1, 3, 5, 2).reshape(batch, G, nc, 2 * nh, SSD_CHUNK))
        per_dir = lambda t: t.astype(F32).reshape(2, G, nh).transpose(1, 0, 2).reshape(G, 2 * nh)
        db, al = per_dir(dt_bias[layer]), per_dir(a_log[layer])
        cw, cbias = conv_w[layer].astype(F32), conv_b[layer].astype(F32)
        gn = G * SSM_STATE
        grp = lambda t, width: t.reshape(t.shape[0], G, width).transpose(1, 0, 2)
        ssd = _ssd(
            zx, bcm, dt_row,
            grp(cw[:, :SSM_W], GROUP_W), grp(cw[:, SSM_W:SSM_W + gn], SSM_STATE),
            grp(cw[:, SSM_W + gn:], SSM_STATE),
            grp(cbias[None, :SSM_W], GROUP_W), grp(cbias[None, SSM_W:SSM_W + gn], SSM_STATE),
            grp(cbias[None, SSM_W + gn:], SSM_STATE),
            db.reshape(G, 2 * nh, 1), al.reshape(G, 2 * nh, 1),
            jnp.repeat(d_skip[layer].astype(F32), SSM_HEAD_DIM).reshape(G, 1, GROUP_W),
            ssd_norm_g[layer].astype(F32).reshape(G, 1, GROUP_W),
            batch, seq)

        x2d = _out_proj(attn, ssd, w_out[layer].astype(BF16), x2d)
        x2d = _mlp(x2d, norm_mlp_g[layer], w_up[layer].astype(BF16), w_down[layer].astype(BF16))
    return x2d.reshape(batch, seq, D_MODEL)
```

```python
import functools
import math

import jax
import jax.numpy as jnp
from jax import lax
from jax.experimental import pallas as pl
from jax.experimental.pallas import tpu as pltpu

D_MODEL = 2048
ATTN_HEADS = 16
HEAD_DIM = 128
ATTN_W = ATTN_HEADS * HEAD_DIM
SSM_HEADS = 32
SSM_HEAD_DIM = 64
SSM_W = SSM_HEADS * SSM_HEAD_DIM
SSM_GROUPS = 8
HEADS_PER_GROUP = SSM_HEADS // SSM_GROUPS
GROUP_W = SSM_W // SSM_GROUPS
SSM_STATE = 128
CONV_WIDTH = 5
DILATIONS = (1, 4, 16)
HALF_WINDOW = 64
NUM_BUCKETS = 32
MAX_DISTANCE = 1024
NEG_INF = -1e30
EPS = 1e-6

LANES = 128
Q_SUB = 128
K_WIN = 256
SSD_CHUNK = 128
CONV_HALO = 8
VMEM_LIMIT = 56 * 1024 * 1024

F32 = jnp.float32
BF16 = jnp.bfloat16


def _params(*sem):
    return pltpu.CompilerParams(dimension_semantics=sem, vmem_limit_bytes=VMEM_LIMIT)


def _sigmoid(x):
    return 1.0 / (1.0 + jnp.exp2(x * (-1.0 / math.log(2.0))))


def _softplus(x):
    return jnp.maximum(x, 0.0) + jnp.log1p(jnp.exp(-jnp.abs(x)))


def _rmsnorm_dt_kernel(x_ref, g_ref, w_ref, h_ref, dt_ref):
    x = x_ref[...]
    ms = jnp.mean(x * x, axis=-1, keepdims=True)
    h = (x * lax.rsqrt(ms + EPS) * g_ref[...]).astype(h_ref.dtype)
    h_ref[...] = h
    dt_ref[...] = jnp.dot(h, w_ref[...], preferred_element_type=F32)


def _rmsnorm_dt(x2d, g, w_dt, tm=1024):
    n, d = x2d.shape
    m = w_dt.shape[1]
    return pl.pallas_call(
        _rmsnorm_dt_kernel,
        grid=(n // tm,),
        in_specs=[pl.BlockSpec((tm, d), lambda i: (i, 0)),
                  pl.BlockSpec((1, d), lambda i: (0, 0)),
                  pl.BlockSpec((d, m), lambda i: (0, 0))],
        out_specs=[pl.BlockSpec((tm, d), lambda i: (i, 0)),
                   pl.BlockSpec((tm, m), lambda i: (i, 0))],
        out_shape=[jax.ShapeDtypeStruct((n, d), BF16), jax.ShapeDtypeStruct((n, m), F32)],
        compiler_params=_params("parallel"),
        name="rmsnorm_dt",
    )(x2d, g.reshape(1, d), w_dt)


QK_SUB = 256


def _proj_qk_kernel(a_ref, w_ref, g_ref, o_ref, *, heads_per_tile, normed_tiles):
    per = QK_SUB // HEAD_DIM

    def slabs(normed):
        a = a_ref[...]
        for c in range(heads_per_tile // per):
            acc = jnp.dot(a, w_ref[:, c * QK_SUB:(c + 1) * QK_SUB], preferred_element_type=F32)
            for hh in range(per):
                h = c * per + hh
                s = acc[:, hh * HEAD_DIM:(hh + 1) * HEAD_DIM]
                if normed:
                    ms = jnp.mean(s * s, axis=-1, keepdims=True)
                    s = s * lax.rsqrt(ms + EPS) * g_ref[:, h * HEAD_DIM:(h + 1) * HEAD_DIM]
                o_ref[h] = s.astype(o_ref.dtype)

    @pl.when(pl.program_id(0) < normed_tiles)
    def _():
        slabs(True)

    @pl.when(pl.program_id(0) >= normed_tiles)
    def _():
        slabs(False)


def _proj_split_kernel(a_ref, w_ref, o_ref, *, width):
    a = a_ref[...]
    per = QK_SUB // width
    for s in range(w_ref.shape[1] // QK_SUB):
        acc = jnp.dot(a, w_ref[:, s * QK_SUB:(s + 1) * QK_SUB], preferred_element_type=F32)
        for c in range(per):
            o_ref[s * per + c] = acc[:, c * width:(c + 1) * width].astype(o_ref.dtype)


def _proj_qk(h, w, gains, tm=1024, tn=2048):
    n, k = h.shape
    m = 3 * ATTN_W
    hpt = tn // HEAD_DIM
    normed = 2 * ATTN_W // tn
    return pl.pallas_call(
        functools.partial(_proj_qk_kernel, heads_per_tile=hpt, normed_tiles=normed),
        grid=(m // tn, n // tm),
        in_specs=[pl.BlockSpec((tm, k), lambda j, i: (i, 0)),
                  pl.BlockSpec((k, tn), lambda j, i: (0, j)),
                  pl.BlockSpec((1, tn), lambda j, i: (0, jnp.minimum(j, normed - 1)))],
        out_specs=pl.BlockSpec((hpt, tm, HEAD_DIM), lambda j, i: (j, i, 0)),
        out_shape=jax.ShapeDtypeStruct((m // HEAD_DIM, n, HEAD_DIM), F32),
        compiler_params=_params("parallel", "arbitrary"),
        name="proj_qkv",
    )(h, w, gains)


def _proj_split(h, w, col0, m, width, name, out_dtype=BF16, tm=2048, tn=1024):
    n, k = h.shape
    cpt = tn // width
    j0 = col0 // tn
    return pl.pallas_call(
        functools.partial(_proj_split_kernel, width=width),
        grid=(m // tn, n // tm),
        in_specs=[pl.BlockSpec((tm, k), lambda j, i: (i, 0)),
                  pl.BlockSpec((k, tn), lambda j, i: (0, j0 + j))],
        out_specs=pl.BlockSpec((cpt, tm, width), lambda j, i: (j, i, 0)),
        out_shape=jax.ShapeDtypeStruct((m // width, n, width), out_dtype),
        compiler_params=_params("parallel", "arbitrary"),
        name=name,
    )(h, w)


def _t5_bucket(rel):
    nb = NUM_BUCKETS // 2
    max_exact = nb // 2
    ret = (rel > 0).astype(jnp.int32) * nb
    n = jnp.abs(rel)
    nf = jnp.maximum(n, 1).astype(jnp.float32)
    large = max_exact + (jnp.log(nf / max_exact) / math.log(MAX_DISTANCE / max_exact)
                         * (nb - max_exact)).astype(jnp.int32)
    large = jnp.minimum(large, nb - 1)
    return ret + jnp.where(n < max_exact, n, large)


def _bias_rows(rel_bias):
    period = K_WIN + Q_SUB
    m = jnp.arange(period)
    delta = jnp.where(m < K_WIN, m, m - period)
    rows = []
    for d in DILATIONS:
        for off in (0, -HALF_WINDOW, -2 * HALF_WINDOW):
            rel = delta + off
            valid = jnp.abs(rel) <= HALF_WINDOW
            b = rel_bias[_t5_bucket(rel * d)].astype(F32)
            rows.append(jnp.where(valid[:, None], b, NEG_INF))
    return jnp.stack(rows, axis=0).transpose(2, 0, 1)


def _attn_kernel(q_ref, k_ref, v_ref, brow_ref, o_ref, bias_s, tmp_s, xq4, xk4, xv4,
                 q16, k1, k4, k16, v1, v4, v16, acc_s, max_s, den_s, *, seq):
    period = K_WIN + Q_SUB
    n4, n16 = seq // 4, seq // 16

    @pl.when(pl.program_id(1) == 0)
    def _():
        for idx in range(9):
            row = jnp.broadcast_to(brow_ref[idx:idx + 1, :], (Q_SUB, period))
            tile = pltpu.roll(row, 0, 1, stride=1, stride_axis=0)
            d = DILATIONS[idx // 3]
            if d == 16:
                bias_s[idx] = tile[:, :K_WIN]
                continue
            for half in range(K_WIN // LANES):
                tmp_s[half] = tile[:, half * LANES:(half + 1) * LANES]
            groups = 16 // d
            for half in range(K_WIN // LANES):
                for g in range(groups):
                    n = Q_SUB // groups
                    bias_s[idx, g * n:(g + 1) * n, half * LANES:(half + 1) * LANES] = (
                        tmp_s[half, pl.ds(g, n, stride=groups), :])

    cp = 256

    def split4(src, dst):
        for r4 in range(4):
            for t0 in range(0, n4, cp):
                dst[r4, pl.ds(t0, cp), :] = src[pl.ds(r4 + 4 * t0, cp, stride=4), :]

    def split16(src4, dst16, dst4=None):
        for r4 in range(4):
            for a in range(4):
                dst16[4 * a + r4] = src4[r4, pl.ds(a, n16, stride=4), :].astype(BF16)
            if dst4 is not None:
                for t0 in range(0, n4, cp):
                    dst4[r4, pl.ds(t0, cp), :] = src4[r4, pl.ds(t0, cp), :].astype(BF16)

    def cast(src, dst):
        for t0 in range(0, seq, cp):
            dst[pl.ds(t0, cp), :] = src[pl.ds(t0, cp), :].astype(BF16)

    split4(q_ref, xq4)
    split16(xq4, q16)
    for src, x4, d1, d4, d16 in ((k_ref, xk4, k1, k4, k16), (v_ref, xv4, v1, v4, v16)):
        cast(src, d1)
        split4(src, x4)
        split16(x4, d16, d4)

    ones = jnp.ones((K_WIN, LANES), BF16)

    def sub_tile(q, k_s, v_s, length, s0, pi):
        if isinstance(s0, int):
            w0 = min(max(s0 - HALF_WINDOW, 0), length - K_WIN)
            place = 0 if s0 == 0 else (2 if s0 == length - Q_SUB else 1)
        else:
            w0 = pl.multiple_of(jnp.clip(s0 - HALF_WINDOW, 0, length - K_WIN), HALF_WINDOW)
            place = jnp.where(s0 == 0, 0, jnp.where(s0 == length - Q_SUB, 2, 1))
        k = k_s[pl.ds(w0, K_WIN), :]
        v = v_s[pl.ds(w0, K_WIN), :]
        s = lax.dot_general(q, k, (((1,), (1,)), ((), ())), preferred_element_type=F32)
        s = s + bias_s[3 * pi + place]
        m = jnp.max(s, axis=-1, keepdims=True)
        p = jnp.exp(s - m).astype(BF16)
        pv = jnp.dot(p, jnp.concatenate([v, ones], axis=1), preferred_element_type=F32)
        return pv[:, :LANES], jnp.broadcast_to(m, (Q_SUB, LANES)), pv[:, LANES:]

    stats = (acc_s, max_s, den_s)


    res = []
    for l0 in range(0, n16, Q_SUB // 4):
        for r4 in range(4):
            q = jnp.concatenate([q16[4 * a + r4, pl.ds(l0, Q_SUB // 4), :] for a in range(4)], axis=0)
            res.append((sub_tile(q, k4.at[r4], v4.at[r4], n4, 4 * l0, 1), r4, l0))
    for r3, r4, l0 in res:
        n = Q_SUB // 4
        for a in range(4):
            for val, dst in zip(r3, stats):
                dst[0, 4 * a + r4, pl.ds(l0, n), :] = val[a * n:(a + 1) * n]

    res = []
    for r in range(16):
        for l0 in range(0, n16, Q_SUB):
            res.append((sub_tile(q16[r, pl.ds(l0, Q_SUB), :], k16.at[r], v16.at[r], n16, l0, 2), r, l0))
    for r3, r, l0 in res:
        for val, dst in zip(r3, stats):
            dst[1, r, pl.ds(l0, Q_SUB), :] = val

    res = []
    for l0 in range(0, n16, Q_SUB // 16):
        q = jnp.concatenate([xq4[r % 4, pl.ds(4 * l0 + r // 4, Q_SUB // 16, stride=4), :] for r in range(16)],
                            axis=0).astype(BF16)
        res.append((sub_tile(q, k1, v1, seq, 16 * l0, 0), l0))
    for (aa, ma, da), l0 in res:
        n = Q_SUB // 16
        rows = lambda ref, pi: jnp.concatenate([ref[pi, r, pl.ds(l0, n), :] for r in range(16)], axis=0)
        mb, mc = rows(max_s, 0), rows(max_s, 1)
        mx = jnp.maximum(jnp.maximum(ma, mb), mc)
        ea, eb, ec = jnp.exp(ma - mx), jnp.exp(mb - mx), jnp.exp(mc - mx)
        num = ea * aa + eb * rows(acc_s, 0) + ec * rows(acc_s, 1)
        den = ea * da + eb * rows(den_s, 0) + ec * rows(den_s, 1)
        out = num / den
        for r in range(16):
            o_ref[pl.ds(r + 16 * l0, n, stride=16), :] = out[r * n:(r + 1) * n]


def _attention(qkv, brow, batch, seq):
    H = ATTN_HEADS
    n4, n16 = seq // 4, seq // 16
    qk4 = qkv.reshape(3 * H, batch, seq, HEAD_DIM)
    v4 = qk4
    in_specs = [pl.BlockSpec((None, None, seq, HEAD_DIM), lambda h, b: (h, b, 0, 0)),
                pl.BlockSpec((None, None, seq, HEAD_DIM), lambda h, b: (H + h, b, 0, 0)),
                pl.BlockSpec((None, None, seq, HEAD_DIM), lambda h, b: (2 * H + h, b, 0, 0))]
    in_specs.append(pl.BlockSpec((None, 9, K_WIN + Q_SUB), lambda h, b: (h, 0, 0)))
    kv_slabs = [pltpu.VMEM((seq, LANES), BF16), pltpu.VMEM((4, n4, LANES), BF16),
                pltpu.VMEM((16, n16, LANES), BF16)]
    out = pl.pallas_call(
        functools.partial(_attn_kernel, seq=seq),
        grid=(H, batch),
        in_specs=in_specs,
        out_specs=pl.BlockSpec((None, None, seq, HEAD_DIM), lambda h, b: (h, b, 0, 0)),
        out_shape=jax.ShapeDtypeStruct((H, batch, seq, HEAD_DIM), F32),
        scratch_shapes=[pltpu.VMEM((9, Q_SUB, K_WIN), F32),
                        pltpu.VMEM((K_WIN // LANES, Q_SUB, LANES), F32),
                        pltpu.VMEM((4, n4, LANES), F32),
                        pltpu.VMEM((4, n4, LANES), F32),
                        pltpu.VMEM((4, n4, LANES), F32),
                        pltpu.VMEM((16, n16, LANES), BF16)]
                       + kv_slabs + kv_slabs
                       + [pltpu.VMEM((2, 16, n16, LANES), F32)] * 3,
        compiler_params=_params("parallel", "arbitrary"),
        name="dilated_attention",
    )(qk4, qk4, v4, brow)
    return out.reshape(H, batch * seq, HEAD_DIM)


def _ssd_kernel(z_ref, x_ref, b_ref, c_ref, dtr_ref, cwx_ref, cwb_ref, cwc_ref,
                cbx_ref, cbb_ref, cbc_ref, dbr_ref, alr_ref, dsk_ref, ng_ref,
                o_ref, pad, cv, y_s, arg_s, diag_s, rows_s, cols_s, sbs, sf, sb, *, seq):
    T = SSD_CHUNK
    nc = seq // T
    hi = lax.Precision.HIGHEST
    halo = CONV_HALO
    half = T // 2

    pad[:, 0:halo, :] = jnp.zeros((4, halo, LANES), F32)
    pad[:, seq + halo:seq + 2 * halo, :] = jnp.zeros((4, halo, LANES), F32)

    for t0 in range(0, seq, T):
        xin = x_ref[pl.ds(t0, T), :].astype(F32)
        pad[0, pl.ds(t0 + halo, T), :] = xin[:, :LANES]
        pad[1, pl.ds(t0 + halo, T), :] = xin[:, LANES:]
        pad[2, pl.ds(t0 + halo, T), :] = b_ref[pl.ds(t0, T), :].astype(F32)
        pad[3, pl.ds(t0 + halo, T), :] = c_ref[pl.ds(t0, T), :].astype(F32)

    cws = (cwx_ref[:, :LANES], cwx_ref[:, LANES:], cwb_ref[...], cwc_ref[...])
    cbs = (cbx_ref[:, :LANES], cbx_ref[:, LANES:], cbb_ref[...], cbc_ref[...])

    def conv_chunk(ci):
        t0 = ci * T
        for s in range(4):
            ev = [pad[s, pl.ds(t0 + halo + 2 * k, half, stride=2), :] for k in (-1, 0, 1)]
            od = [pad[s, pl.ds(t0 + halo + 1 + 2 * k, half, stride=2), :] for k in (-1, 0, 1)]
            w = [cws[s][j:j + 1, :] for j in range(CONV_WIDTH)]
            out_e = cbs[s] + w[0] * ev[0] + w[1] * od[0] + w[2] * ev[1] + w[3] * od[1] + w[4] * ev[2]
            out_o = cbs[s] + w[0] * od[0] + w[1] * ev[1] + w[2] * od[1] + w[3] * ev[2] + w[4] * od[2]
            cv[s, pl.ds(t0, half, stride=2), :] = out_e * _sigmoid(out_e)
            cv[s, pl.ds(t0 + 1, half, stride=2), :] = out_o * _sigmoid(out_o)

    def load_x(t0):
        return jnp.concatenate([cv[0, pl.ds(t0, T), :], cv[1, pl.ds(t0, T), :]], axis=1)

    nh = HEADS_PER_GROUP
    nd = 2 * nh
    row = lax.broadcasted_iota(jnp.int32, (T, T), 0)
    col = lax.broadcasted_iota(jnp.int32, (T, T), 1)
    triu = (row <= col).astype(F32)
    lower = col < row
    upper = col > row

    dt = _softplus(dtr_ref[...] + dbr_ref[...])
    a = dt * (-jnp.exp(alr_ref[...]))
    cum = jnp.dot(a.reshape(nc * nd, T), triu, precision=hi,
                  preferred_element_type=F32).reshape(nc, nd, T)
    last = cum[:, :, T - 1:T]
    exc = cum - a
    fwd = lax.broadcasted_iota(jnp.int32, (nc, nd, T), 1) < nh
    base = jnp.where(fwd, cum, exc)
    log2e = 1.0 / math.log(2.0)
    log_dt = jnp.log(dt)
    arg_s[...] = jnp.where(fwd, cum - log_dt, exc + log_dt) * log2e
    log_sum = jnp.log(dt[:, 0:nh, :] + dt[:, nh:, :]) * log2e
    diag_s[...] = jnp.concatenate([log_sum, log_sum], axis=1)
    rows_s[:, 0:nd, :] = base * log2e
    rows_s[:, nd:2 * nd, :] = jnp.exp(jnp.where(fwd, cum, last - exc))
    rows_s[:, 2 * nd:3 * nd, :] = dt * jnp.exp(jnp.where(fwd, last - cum, exc))
    rows_s[:, 3 * nd:, :] = jnp.zeros((nc, LANES - 3 * nd, T), F32)

    def expand(cols, first):
        n = cols.shape[0]
        low = lax.broadcasted_iota(jnp.int32, (n, LANES), 1) < SSM_HEAD_DIM
        halves = []
        for j in range(GROUP_W // LANES):
            c0 = jnp.broadcast_to(cols[:, first + 2 * j:first + 2 * j + 1], (n, LANES))
            c1 = jnp.broadcast_to(cols[:, first + 2 * j + 1:first + 2 * j + 2], (n, LANES))
            halves.append(jnp.where(low, c0, c1))
        return jnp.concatenate(halves, axis=1)

    tn_dims = (((0,), (0,)), ((), ()))

    cid = lax.broadcasted_iota(jnp.int32, (LANES, 4 * GROUP_W), 0)
    lid = lax.broadcasted_iota(jnp.int32, (LANES, 4 * GROUP_W), 1)
    blk = lid // GROUP_W
    want = nd + (blk % 2) * nh + (blk // 2) * nd + (lid % GROUP_W) // SSM_HEAD_DIM
    spread = jnp.where(cid == want, 1.0, 0.0).astype(BF16)
    spread_f = spread[:, :3 * GROUP_W]
    spread_b = spread[:, 3 * GROUP_W:]

    sb[...] = jnp.zeros_like(sb)

    def state_back(ci):
        t0 = ci * T
        ct = rows_s[ci].T
        cols_s[ci] = ct
        s_prev = sb[...]
        sbs[ci] = s_prev.astype(BF16)
        wx = jnp.dot(ct.astype(BF16), spread_b, preferred_element_type=F32)
        xs = (load_x(t0) * wx).astype(BF16)
        sb[...] = expand(ct[0:1, :], nd + nh) * s_prev + lax.dot_general(
            cv[2, pl.ds(t0, T), :].astype(BF16), xs, tn_dims, preferred_element_type=F32)

    conv_chunk(nc - 1)
    for k in range(nc - 2, -1, -1):
        conv_chunk(k)
        state_back(k + 1)
    state_back(0)

    sf[...] = jnp.zeros_like(sf)
    low_half = lax.broadcasted_iota(jnp.int32, (T, LANES), 1) < SSM_HEAD_DIM

    def finish(ci):
        t0 = ci * T
        zz = z_ref[pl.ds(t0, T), :].astype(F32)
        y = y_s[pl.ds(t0, T), :] * (zz * _sigmoid(zz))
        ms = jnp.mean(y * y, axis=-1, keepdims=True)
        o_ref[pl.ds(t0, T), :] = (y * lax.rsqrt(ms + EPS) * ng_ref[...]).astype(o_ref.dtype)

    def chunk_fwd(ci):
        t0 = ci * T
        ct = cols_s[ci]
        arg_r = arg_s[ci]
        dt_r = diag_s[ci]
        x = load_x(t0)
        bk = cv[2, pl.ds(t0, T), :].astype(BF16)
        ck = cv[3, pl.ds(t0, T), :].astype(BF16)
        cb = lax.dot_general(ck, bk, (((1,), (1,)), ((), ())), preferred_element_type=F32)
        spreadv = jnp.dot(ct.astype(BF16), spread_f, preferred_element_type=F32)
        lhs = []
        for h in range(nh):
            hb = nh + h
            arg = jnp.where(lower, ct[:, h:h + 1] - arg_r[h:h + 1, :],
                            jnp.where(upper, arg_r[hb:hb + 1, :] - ct[:, hb:hb + 1], dt_r[h:h + 1, :]))
            lhs.append((cb * jnp.exp2(arg)).astype(BF16))
        s_prev = sf[...]
        y4 = jnp.dot(jnp.concatenate(lhs, axis=0), x.astype(BF16), preferred_element_type=F32)
        off = jnp.dot(ck, jnp.concatenate([s_prev.astype(BF16), sbs[ci]], axis=1),
                      preferred_element_type=F32)
        y = x * dsk_ref[...] + jnp.concatenate(
            [jnp.where(low_half, y4[2 * j * T:(2 * j + 1) * T, j * LANES:(j + 1) * LANES],
                       y4[(2 * j + 1) * T:(2 * j + 2) * T, j * LANES:(j + 1) * LANES])
             for j in range(GROUP_W // LANES)], axis=1)
        y = y + spreadv[:, :GROUP_W] * off[:, :GROUP_W] + spreadv[:, GROUP_W:2 * GROUP_W] * off[:, GROUP_W:]
        xs = (x * spreadv[:, 2 * GROUP_W:]).astype(BF16)
        sf[...] = expand(ct[T - 1:T, :], nd) * s_prev + lax.dot_general(
            bk, xs, tn_dims, preferred_element_type=F32)
        y_s[pl.ds(t0, T), :] = y

    chunk_fwd(0)
    for c in range(1, nc):
        chunk_fwd(c)
        finish(c - 1)
    finish(nc - 1)


def _ssd(zx, bcm, dt_row, cwx, cwb, cwc, cbx, cbb, cbc, db_row, al_row, dskip, ng, batch, seq):
    G = SSM_GROUPS
    nc = seq // SSD_CHUNK
    zx4 = zx.reshape(3 * G, batch, seq, GROUP_W)
    bc4 = zx4
    per_slab = GROUP_W // SSM_STATE
    nd = 2 * HEADS_PER_GROUP

    def per_group(shape):
        return pl.BlockSpec((None,) + shape, lambda g, b: (g,) + (0,) * len(shape))

    in_specs = [
        pl.BlockSpec((None, None, seq, GROUP_W), lambda g, b: (g, b, 0, 0)),
        pl.BlockSpec((None, None, seq, GROUP_W), lambda g, b: (G + g, b, 0, 0)),
        pl.BlockSpec((None, None, seq, SSM_STATE),
                     lambda g, b: (2 * G + g // per_slab, b, 0, g % per_slab)),
        pl.BlockSpec((None, None, seq, SSM_STATE),
                     lambda g, b: (2 * G + G // per_slab + g // per_slab, b, 0, g % per_slab)),
        pl.BlockSpec((None, None, nc, nd, SSD_CHUNK), lambda g, b: (b, g, 0, 0, 0)),
        per_group((CONV_WIDTH, GROUP_W)), per_group((CONV_WIDTH, SSM_STATE)),
        per_group((CONV_WIDTH, SSM_STATE)),
        per_group((1, GROUP_W)), per_group((1, SSM_STATE)), per_group((1, SSM_STATE)),
        per_group((nd, 1)), per_group((nd, 1)),
        per_group((1, GROUP_W)), per_group((1, GROUP_W)),
    ]
    out = pl.pallas_call(
        functools.partial(_ssd_kernel, seq=seq),
        grid=(G, batch),
        in_specs=in_specs,
        out_specs=pl.BlockSpec((None, None, seq, GROUP_W), lambda g, b: (g, b, 0, 0)),
        out_shape=jax.ShapeDtypeStruct((G, batch, seq, GROUP_W), BF16),
        scratch_shapes=[pltpu.VMEM((4, seq + 2 * CONV_HALO, LANES), F32),
                        pltpu.VMEM((4, seq, LANES), F32),
                        pltpu.VMEM((seq, GROUP_W), F32),
                        pltpu.VMEM((nc, nd, SSD_CHUNK), F32),
                        pltpu.VMEM((nc, nd, SSD_CHUNK), F32),
                        pltpu.VMEM((nc, LANES, SSD_CHUNK), F32),
                        pltpu.VMEM((nc, SSD_CHUNK, LANES), F32),
                        pltpu.VMEM((nc, SSM_STATE, GROUP_W), BF16),
                        pltpu.VMEM((SSM_STATE, GROUP_W), F32),
                        pltpu.VMEM((SSM_STATE, GROUP_W), F32)],
        compiler_params=_params("parallel", "parallel"),
        name="ssd",
    )(zx4, zx4, bc4, bc4, dt_row, cwx, cwb, cwc, cbx, cbb, cbc, db_row, al_row, dskip, ng)
    return out.reshape(G, batch * seq, GROUP_W)


def _out_proj_kernel(attn_ref, ssd_ref, w_ref, x_ref, o_ref, lhs):
    for h in range(ATTN_HEADS):
        lhs[:, h * HEAD_DIM:(h + 1) * HEAD_DIM] = attn_ref[h].astype(BF16)
    for g in range(SSM_GROUPS):
        lhs[:, ATTN_W + g * GROUP_W:ATTN_W + (g + 1) * GROUP_W] = ssd_ref[g]
    o_ref[...] = x_ref[...] + jnp.dot(lhs[...], w_ref[...], preferred_element_type=F32)


def _out_proj(attn, ssd, w, x2d, tm=512):
    n, d = x2d.shape
    kk = w.shape[0]
    return pl.pallas_call(
        _out_proj_kernel,
        grid=(n // tm,),
        in_specs=[pl.BlockSpec((ATTN_HEADS, tm, HEAD_DIM), lambda i: (0, i, 0)),
                  pl.BlockSpec((SSM_GROUPS, tm, GROUP_W), lambda i: (0, i, 0)),
                  pl.BlockSpec((kk, d), lambda i: (0, 0), pipeline_mode=pl.Buffered(1)),
                  pl.BlockSpec((tm, d), lambda i: (i, 0))],
        out_specs=pl.BlockSpec((tm, d), lambda i: (i, 0)),
        out_shape=jax.ShapeDtypeStruct((n, d), F32),
        scratch_shapes=[pltpu.VMEM((tm, kk), BF16)],
        compiler_params=_params("parallel"),
        name="out_proj",
    )(attn, ssd, w, x2d)


def _mlp_kernel(x_ref, g_ref, wu_ref, wd_ref, o_ref, hm):
    def ffn():
        u = jnp.maximum(jnp.dot(hm[...], wu_ref[...], preferred_element_type=F32), 0.0)
        return jnp.dot((u * u).astype(BF16), wd_ref[...], preferred_element_type=F32)

    @pl.when(pl.program_id(1) == 0)
    def _():
        x = x_ref[...]
        ms = jnp.mean(x * x, axis=-1, keepdims=True)
        hm[...] = (x * lax.rsqrt(ms + EPS) * g_ref[...]).astype(hm.dtype)
        o_ref[...] = x + ffn()

    @pl.when(pl.program_id(1) > 0)
    def _():
        o_ref[...] += ffn()


def _mlp(x2d, g, wu, wd, tm=512, tf=1024):
    n, d = x2d.shape
    f = wu.shape[1]
    return pl.pallas_call(
        _mlp_kernel,
        grid=(n // tm, f // tf),
        in_specs=[pl.BlockSpec((tm, d), lambda i, j: (i, 0)),
                  pl.BlockSpec((1, d), lambda i, j: (0, 0)),
                  pl.BlockSpec((d, tf), lambda i, j: (0, j)),
                  pl.BlockSpec((tf, d), lambda i, j: (j, 0))],
        out_specs=pl.BlockSpec((tm, d), lambda i, j: (i, 0)),
        out_shape=jax.ShapeDtypeStruct((n, d), F32),
        scratch_shapes=[pltpu.VMEM((tm, d), BF16)],
        compiler_params=_params("parallel", "arbitrary"),
        name="mlp",
    )(x2d, g.reshape(1, d), wu, wd)


def kernel(x, norm_mix_g, w_in, q_norm_g, k_norm_g, rel_bias, conv_w, conv_b, dt_bias, a_log,
           d_skip, ssd_norm_g, w_out, norm_mlp_g, w_up, w_down):
    batch, seq, _ = x.shape
    n = batch * seq
    G, nh = SSM_GROUPS, HEADS_PER_GROUP
    nc = seq // SSD_CHUNK
    o_z = 3 * ATTN_W
    o_bc = o_z + 2 * SSM_W
    o_dt = o_bc + 2 * G * SSM_STATE
    x2d = x.reshape(n, D_MODEL)
    brow = _bias_rows(rel_bias)

    for layer in range(w_in.shape[0]):
        wi = w_in[layer].astype(BF16)
        h, dt_raw = _rmsnorm_dt(x2d, norm_mix_g[layer], wi[:, o_dt:])

        scale = 1.0 / math.sqrt(HEAD_DIM)
        gains = jnp.concatenate([jnp.tile(q_norm_g[layer].astype(F32) * scale, ATTN_HEADS),
                                 jnp.tile(k_norm_g[layer].astype(F32), ATTN_HEADS)]).reshape(1, 2 * ATTN_W)
        qkv = _proj_qk(h, wi, gains)
        zx = _proj_split(h, wi, o_z, o_dt - o_z, GROUP_W, "proj_zxbc")
        bcm = None

        attn = _attention(qkv, brow, batch, seq)

        dt_row = (dt_raw.reshape(batch, nc, SSD_CHUNK, 2, G, nh)
                  .transpose(0, 4, 1, 3, 5, 2).reshape(batch, G, nc, 2 * nh, SSD_CHUNK))
        per_dir = lambda t: t.astype(F32).reshape(2, G, nh).transpose(1, 0, 2).reshape(G, 2 * nh)
        db, al = per_dir(dt_bias[layer]), per_dir(a_log[layer])
        cw, cbias = conv_w[layer].astype(F32), conv_b[layer].astype(F32)
        gn = G * SSM_STATE
        grp = lambda t, width: t.reshape(t.shape[0], G, width).transpose(1, 0, 2)
        ssd = _ssd(
            zx, bcm, dt_row,
            grp(cw[:, :SSM_W], GROUP_W), grp(cw[:, SSM_W:SSM_W + gn], SSM_STATE),
            grp(cw[:, SSM_W + gn:], SSM_STATE),
            grp(cbias[None, :SSM_W], GROUP_W), grp(cbias[None, SSM_W:SSM_W + gn], SSM_STATE),
            grp(cbias[None, SSM_W + gn:], SSM_STATE),
            db.reshape(G, 2 * nh, 1), al.reshape(G, 2 * nh, 1),
            jnp.repeat(d_skip[layer].astype(F32), SSM_HEAD_DIM).reshape(G, 1, GROUP_W),
            ssd_norm_g[layer].astype(F32).reshape(G, 1, GROUP_W),
            batch, seq)

        x2d = _out_proj(attn, ssd, w_out[layer].astype(BF16), x2d)
        x2d = _mlp(x2d, norm_mlp_g[layer], w_up[layer].astype(BF16), w_down[layer].astype(BF16))
    return x2d.reshape(batch, seq, D_MODEL)
```
